```python
import math
import jax
import jax.numpy as jnp
from jax import lax
import numpy as np

D_MODEL = 1024
BATCH = 8
SEQ = 2048
DEPTH = 2
DEC_BATCH = 32
DEC_SEQ = 1
PAST_LEN = 16384
PAGE_SIZE = 128

MIX = D_MODEL
HEAD_DIM = 64
W_A = MIX // 4
C_GRP = 16
G_A = W_A // C_GRP
N_A = 64
W_B = (MIX - W_A) // 2
H_B = W_B // HEAD_DIM
BRANCHES = ((128, 1), (512, 4), (2048, 16))
WIN_MAX = 2048
N_BUCKETS = 32
MAX_DIST = WIN_MAX
Q_BLK = 128
NEG_INF = -1e30
W_C = MIX - W_A - W_B
H_C = W_C // HEAD_DIM
R_W = 32
R_A = 32
R_G = 64
C_SHIFT = 3 * W_C + R_W + R_A + R_G
N_IN = W_A + 3 * W_B + C_SHIFT
GN_EPS = 64e-5
N_EXPERTS = 32
TOP_K = 4
D_FF = D_MODEL
SWIGLU_ALPHA = 1.702
SWIGLU_LIMIT = 7.0
MOE_BLK = 128
ALPHA = (2 * DEPTH) ** 0.25
BETA = (8 * DEPTH) ** -0.25
LN_EPS = 1e-5

kernel_name = "hymba_s5_longnet_rwkv7_moe_step"


def _layernorm(x, g, b):
    xf = x.astype(jnp.float32)
    mu = xf.mean(-1, keepdims=True)
    var = jnp.mean(jnp.square(xf - mu), -1, keepdims=True)
    return ((xf - mu) * lax.rsqrt(var + LN_EPS) * g + b).astype(x.dtype)


def _branch_offsets():
    return np.stack([np.arange(w // d + 1) * d for (w, d) in BRANCHES]).astype(np.int32)


def _t5_buckets(dist):
    max_exact = N_BUCKETS // 2
    d = np.maximum(dist, 1).astype(np.float32)
    large = max_exact + (np.log(d / max_exact) / np.log(MAX_DIST / max_exact)
                         * (N_BUCKETS - max_exact)).astype(np.int32)
    return np.where(dist < max_exact, dist, np.minimum(large, N_BUCKETS - 1)).astype(np.int32)


def _s5(u, h0, p):
    B, T, _ = u.shape
    f32 = jnp.float32
    lam = lax.complex(p["ssm_a_re"].astype(f32), p["ssm_a_im"].astype(f32))
    dt = jnp.exp(p["ssm_log_dt"].astype(f32))[:, None]
    lam_bar = jnp.exp(lam * dt)
    b_bar = ((lam_bar - 1.0) / lam)[..., None] * lax.complex(
        p["ssm_b_re"].astype(f32), p["ssm_b_im"].astype(f32))
    c = lax.complex(p["ssm_c_re"].astype(f32), p["ssm_c_im"].astype(f32))
    uf = u.astype(f32)
    bu = jnp.einsum("gnc,btgc->btgn", b_bar,
                    uf.reshape(B, T, G_A, C_GRP).astype(jnp.complex64))
    bu = bu.at[:, 0].add(lam_bar * h0)

    def combine(e1, e2):
        a1, b1 = e1
        a2, b2 = e2
        return a1 * a2, a2 * b1 + b2

    _, h = lax.associative_scan(combine, (jnp.broadcast_to(lam_bar, bu.shape), bu), axis=1)
    y = jnp.einsum("gcn,btgn->btgc", c, h).real.reshape(B, T, W_A) + p["ssm_d"].astype(f32) * uf
    z = jax.nn.gelu(y)
    out = z * jax.nn.sigmoid(z @ p["ssm_glu_w"].astype(f32) + p["ssm_glu_b"].astype(f32))
    return out.astype(u.dtype), h[:, -1]


def _dilated_attn(q, q_idx, k_all, v_all, bias):
    f32 = jnp.float32
    offs = jnp.asarray(_branch_offsets())
    qf = q.astype(f32) * HEAD_DIM ** -0.5
    outs, lses = [], []
    for br in range(len(BRANCHES)):
        idx = q_idx[:, None] - offs[br][None, :]
        valid = idx >= 0
        idx_c = jnp.maximum(idx, 0)
        kg = k_all[:, idx_c].astype(f32)
        vg = v_all[:, idx_c].astype(f32)
        s = jnp.einsum("bqhd,bqkhd->bhqk", qf, kg) + bias[br].astype(f32).T[None, :, None, :]
        s = jnp.where(valid[None, None], s, NEG_INF)
        lse = jax.nn.logsumexp(s, axis=-1)
        pr = jnp.exp(s - lse[..., None])
        outs.append(jnp.einsum("bhqk,bqkhd->bqhd", pr, vg))
        lses.append(lse)
    wts = jax.nn.softmax(jnp.stack(lses), axis=0)
    return jnp.einsum("nbhq,nbqhd->bqhd", wts, jnp.stack(outs))


def _attn_prompt(q, k, v, bias):
    B, S = q.shape[:2]
    n_blk = S // Q_BLK

    def blk(i):
        start = i * Q_BLK
        qb = lax.dynamic_slice_in_dim(q, start, Q_BLK, axis=1)
        return _dilated_attn(qb, start + jnp.arange(Q_BLK, dtype=jnp.int32), k, v, bias)

    o = lax.map(blk, jnp.arange(n_blk, dtype=jnp.int32))
    return o.transpose(1, 0, 2, 3, 4).reshape(B, S, H_B, HEAD_DIM)


def _rwkv7(pc, shift0, S0, p):
    B, T, _ = pc.shape
    f32 = jnp.float32
    prev = jnp.concatenate([shift0[:, None].astype(pc.dtype), pc[:, :-1]], axis=1)
    xs = pc + (prev - pc) * p["rwkv_mu"]
    r, k, v, w_in, a_in, g_in = jnp.split(
        xs, [W_C, 2 * W_C, 3 * W_C, 3 * W_C + R_W, 3 * W_C + R_W + R_A], axis=-1)
    w = -jax.nn.softplus(-(p["rwkv_w0"] + jnp.tanh(w_in) @ p["rwkv_w2"])) - 0.5
    a = jax.nn.sigmoid(p["rwkv_a0"] + a_in @ p["rwkv_a2"])
    g = jax.nn.sigmoid(g_in) @ p["rwkv_g2"]

    def heads(t):
        return t.astype(f32).reshape(B, T, H_C, HEAD_DIM)

    kk = heads(k * p["rwkv_k_k"])
    kk = kk / jnp.maximum(jnp.linalg.norm(kk, axis=-1, keepdims=True), 1e-12)
    k = k * (1 + (a - 1) * p["rwkv_k_a"])
    rh, kh, vh, ah = heads(r), heads(k), heads(v), heads(a)
    decay = jnp.exp(-jnp.exp(heads(w)))

    def step(S, inp):
        r_t, w_t, k_t, v_t, kk_t, a_t = inp
        sa = jnp.einsum("bhvk,bhk->bhv", S, -kk_t)
        S = (S * w_t[:, :, None, :] + sa[..., None] * (kk_t * a_t)[:, :, None, :]
             + v_t[..., None] * k_t[:, :, None, :])
        return S, jnp.einsum("bhvk,bhk->bhv", S, r_t)

    seq_first = lambda t: jnp.swapaxes(t, 0, 1)
    S_T, y = lax.scan(step, S0.astype(f32), tuple(seq_first(t) for t in (rh, decay, kh, vh, kk, ah)))
    y = seq_first(y)
    mu = y.mean(-1, keepdims=True)
    var = jnp.mean(jnp.square(y - mu), -1, keepdims=True)
    y = ((y - mu) * lax.rsqrt(var + GN_EPS)).reshape(B, T, W_C) * p["rwkv_ln_g"] + p["rwkv_ln_b"]
    bonus = jnp.sum(rh * kh * p["rwkv_r_k"], -1, keepdims=True) * vh
    y = (y + bonus.reshape(B, T, W_C)) * g
    return y.astype(pc.dtype), S_T, pc[:, -1]


def _moe(x, p):
    B, T, D = x.shape
    f32 = jnp.float32
    xt = x.reshape(B * T, D)
    n_tok = B * T
    logits = (xt @ p["moe_router_w"] + p["moe_router_b"]).astype(f32)
    top_val, top_idx = lax.top_k(logits, TOP_K)
    gates = jax.nn.softmax(top_val, axis=-1)
    n_asg = n_tok * TOP_K
    expert = top_idx.reshape(n_asg)
    token = jnp.arange(n_asg, dtype=jnp.int32) // TOP_K
    gate = gates.reshape(n_asg)
    order = jnp.argsort(expert)
    e_sorted = expert[order]
    counts = jnp.bincount(expert, length=N_EXPERTS)
    starts = jnp.cumsum(counts) - counts
    pcounts = (counts + MOE_BLK - 1) // MOE_BLK * MOE_BLK
    pends = jnp.cumsum(pcounts)
    pstarts = pends - pcounts
    dest = pstarts[e_sorted] + (jnp.arange(n_asg, dtype=jnp.int32) - starts[e_sorted])
    n_blk = -(-n_asg // MOE_BLK) + N_EXPERTS
    n_rows = n_blk * MOE_BLK
    row_token = jnp.zeros((n_rows,), jnp.int32).at[dest].set(token[order])
    row_gate = jnp.zeros((n_rows,), f32).at[dest].set(gate[order])
    blk_expert = jnp.minimum(
        jnp.searchsorted(pends, jnp.arange(n_blk, dtype=jnp.int32) * MOE_BLK, side="right"),
        N_EXPERTS - 1)

    def run_block(i):
        rows = lax.dynamic_slice_in_dim(row_token, i * MOE_BLK, MOE_BLK)
        e = blk_expert[i]
        h = xt[rows] @ p["moe_w_up"][e] + p["moe_b_up"][e]
        h_glu = jnp.minimum(h[:, :D_FF], SWIGLU_LIMIT)
        h_lin = jnp.clip(h[:, D_FF:], -SWIGLU_LIMIT, SWIGLU_LIMIT)
        act = h_glu * jax.nn.sigmoid(SWIGLU_ALPHA * h_glu) * (h_lin + 1)
        return act @ p["moe_w_down"][e] + p["moe_b_down"][e]

    y_rows = lax.map(run_block, jnp.arange(n_blk, dtype=jnp.int32)).reshape(n_rows, D)
    y = jax.ops.segment_sum(y_rows.astype(f32) * row_gate[:, None], row_token, num_segments=n_tok)
    return y.reshape(B, T, D).astype(x.dtype)


def _layer(x, kv_buf, h0, S0, shift0, p, bias):
    B, T, _ = x.shape
    proj = x @ p["w_in"]
    u, q, k, v, pc = jnp.split(proj, [W_A, W_A + W_B, W_A + 2 * W_B, W_A + 3 * W_B], axis=-1)
    q = q.reshape(B, T, H_B, HEAD_DIM)
    k = k.reshape(B, T, H_B, HEAD_DIM)
    v = v.reshape(B, T, H_B, HEAD_DIM)
    o_a, h_new = _s5(u, h0, p)
    if kv_buf is None:
        k_all, v_all = k, v
        o_b = _attn_prompt(q, k_all, v_all, bias)
    else:
        k_all = jnp.concatenate([kv_buf[0].astype(k.dtype), k], axis=1)
        v_all = jnp.concatenate([kv_buf[1].astype(v.dtype), v], axis=1)
        q_idx = kv_buf[0].shape[1] + jnp.arange(T, dtype=jnp.int32)
        o_b = _dilated_attn(q, q_idx, k_all, v_all, bias)
    keep = min(WIN_MAX, k_all.shape[1])
    o_c, S_new, shift_new = _rwkv7(pc, shift0, S0, p)
    mix = jnp.concatenate([o_a, o_b.astype(x.dtype).reshape(B, T, W_B), o_c], axis=-1) @ p["w_out"]
    x = _layernorm(ALPHA * x + mix, p["ln1_g"], p["ln1_b"])
    x = _layernorm(ALPHA * x + _moe(x, p), p["ln2_g"], p["ln2_b"])
    new_ssm = jnp.stack([h_new.real, h_new.imag], axis=-1)
    return x, (k_all[:, -keep:], v_all[:, -keep:], new_ssm, S_new, shift_new)


def setup_inputs(seed: int = 0) -> dict:
    key = jax.random.key(seed)
    keys = iter(jax.random.split(key, 48))
    f32 = jnp.float32

    def nrm(shape, scale):
        return jax.random.normal(next(keys), shape, f32) * scale

    def unif(shape, lo, hi):
        return jax.random.uniform(next(keys), shape, f32, lo, hi)

    l_buf = min(WIN_MAX, PAST_LEN)
    chan = jnp.arange(W_C, dtype=f32) / (W_C - 1)
    return {
        "x_prompt": nrm((BATCH, SEQ, D_MODEL), 1.0),
        "x_sample": nrm((DEC_BATCH, DEC_SEQ, D_MODEL), 1.0),
        "cache_attn_k": nrm((DEPTH, DEC_BATCH, l_buf, H_B, HEAD_DIM), 1.0),
        "cache_attn_v": nrm((DEPTH, DEC_BATCH, l_buf, H_B, HEAD_DIM), 1.0),
        "state_ssm": nrm((DEPTH, DEC_BATCH, G_A, N_A, 2), 0.5),
        "state_rwkv": nrm((DEPTH, DEC_BATCH, H_C, HEAD_DIM, HEAD_DIM), 0.5),
        "state_shift": nrm((DEPTH, DEC_BATCH, C_SHIFT), 1.0),
        "w_in": nrm((DEPTH, D_MODEL, N_IN), D_MODEL ** -0.5),
        "w_out": nrm((DEPTH, MIX, D_MODEL), BETA * MIX ** -0.5),
        "ln1_g": 1.0 + nrm((DEPTH, D_MODEL), 0.02),
        "ln1_b": nrm((DEPTH, D_MODEL), 0.01),
        "ln2_g": 1.0 + nrm((DEPTH, D_MODEL), 0.02),
        "ln2_b": nrm((DEPTH, D_MODEL), 0.01),
        "ssm_a_re": -0.5 + nrm((DEPTH, G_A, N_A), 0.01),
        "ssm_a_im": math.pi * jnp.arange(N_A, dtype=f32) + nrm((DEPTH, G_A, N_A), 0.01),
        "ssm_log_dt": unif((DEPTH, G_A), math.log(1e-3), math.log(1e-1)),
        "ssm_b_re": nrm((DEPTH, G_A, N_A, C_GRP), (2 * C_GRP) ** -0.5),
        "ssm_b_im": nrm((DEPTH, G_A, N_A, C_GRP), (2 * C_GRP) ** -0.5),
        "ssm_c_re": nrm((DEPTH, G_A, C_GRP, N_A), N_A ** -0.5),
        "ssm_c_im": nrm((DEPTH, G_A, C_GRP, N_A), N_A ** -0.5),
        "ssm_d": nrm((DEPTH, W_A), 1.0),
        "ssm_glu_w": nrm((DEPTH, W_A, W_A), W_A ** -0.5),
        "ssm_glu_b": nrm((DEPTH, W_A), 0.01),
        "rel_bias": nrm((N_BUCKETS, H_B), 0.1),
        "rwkv_mu": unif((DEPTH, C_SHIFT), 0.0, 1.0),
        "rwkv_w0": -6.5 + 5.0 * chan ** 0.9 + nrm((DEPTH, W_C), 0.1),
        "rwkv_w2": nrm((DEPTH, R_W, W_C), 0.1),
        "rwkv_a0": nrm((DEPTH, W_C), 0.01),
        "rwkv_a2": nrm((DEPTH, R_A, W_C), 0.1),
        "rwkv_g2": nrm((DEPTH, R_G, W_C), R_G ** -0.5),
        "rwkv_k_k": 0.85 + nrm((DEPTH, W_C), 0.02),
        "rwkv_k_a": 1.0 + nrm((DEPTH, W_C), 0.02),
        "rwkv_r_k": -0.04 + nrm((DEPTH, H_C, HEAD_DIM), 0.01),
        "rwkv_ln_g": 1.0 + nrm((DEPTH, W_C), 0.02),
        "rwkv_ln_b": nrm((DEPTH, W_C), 0.01),
        "moe_router_w": nrm((DEPTH, D_MODEL, N_EXPERTS), D_MODEL ** -0.5),
        "moe_router_b": nrm((DEPTH, N_EXPERTS), 0.01),
        "moe_w_up": nrm((DEPTH, N_EXPERTS, D_MODEL, 2 * D_FF), D_MODEL ** -0.5),
        "moe_b_up": nrm((DEPTH, N_EXPERTS, 2 * D_FF), 0.01),
        "moe_w_down": nrm((DEPTH, N_EXPERTS, D_FF, D_MODEL), BETA * D_FF ** -0.5),
        "moe_b_down": nrm((DEPTH, N_EXPERTS, D_MODEL), 0.01),
    }


def reference(x_prompt, x_sample, cache_attn_k, cache_attn_v, state_ssm, state_rwkv, state_shift,
              w_in, w_out, ln1_g, ln1_b, ln2_g, ln2_b,
              ssm_a_re, ssm_a_im, ssm_log_dt, ssm_b_re, ssm_b_im, ssm_c_re, ssm_c_im, ssm_d,
              ssm_glu_w, ssm_glu_b, rel_bias,
              rwkv_mu, rwkv_w0, rwkv_w2, rwkv_a0, rwkv_a2, rwkv_g2, rwkv_k_k, rwkv_k_a, rwkv_r_k,
              rwkv_ln_g, rwkv_ln_b,
              moe_router_w, moe_router_b, moe_w_up, moe_b_up, moe_w_down, moe_b_down):
    f32 = jnp.float32
    bias = rel_bias[jnp.asarray(_t5_buckets(_branch_offsets()))]
    xp, xs = x_prompt, x_sample
    bp = xp.shape[0]
    out_p = ([], [], [], [], [])
    out_s = ([], [], [], [], [])
    for l in range(DEPTH):
        p = {
            "w_in": w_in[l], "w_out": w_out[l],
            "ln1_g": ln1_g[l], "ln1_b": ln1_b[l], "ln2_g": ln2_g[l], "ln2_b": ln2_b[l],
            "ssm_a_re": ssm_a_re[l], "ssm_a_im": ssm_a_im[l], "ssm_log_dt": ssm_log_dt[l],
            "ssm_b_re": ssm_b_re[l], "ssm_b_im": ssm_b_im[l],
            "ssm_c_re": ssm_c_re[l], "ssm_c_im": ssm_c_im[l], "ssm_d": ssm_d[l],
            "ssm_glu_w": ssm_glu_w[l], "ssm_glu_b": ssm_glu_b[l],
            "rwkv_mu": rwkv_mu[l], "rwkv_w0": rwkv_w0[l], "rwkv_w2": rwkv_w2[l],
            "rwkv_a0": rwkv_a0[l], "rwkv_a2": rwkv_a2[l], "rwkv_g2": rwkv_g2[l],
            "rwkv_k_k": rwkv_k_k[l], "rwkv_k_a": rwkv_k_a[l], "rwkv_r_k": rwkv_r_k[l],
            "rwkv_ln_g": rwkv_ln_g[l], "rwkv_ln_b": rwkv_ln_b[l],
            "moe_router_w": moe_router_w[l], "moe_router_b": moe_router_b[l],
            "moe_w_up": moe_w_up[l], "moe_b_up": moe_b_up[l],
            "moe_w_down": moe_w_down[l], "moe_b_down": moe_b_down[l],
        }
        h0 = jnp.zeros((bp, G_A, N_A), jnp.complex64)
        S0 = jnp.zeros((bp, H_C, HEAD_DIM, HEAD_DIM), f32)
        sh0 = jnp.zeros((bp, C_SHIFT), xp.dtype)
        xp, st_p = _layer(xp, None, h0, S0, sh0, p, bias)
        h0s = lax.complex(state_ssm[l, ..., 0].astype(f32), state_ssm[l, ..., 1].astype(f32))
        xs, st_s = _layer(xs, (cache_attn_k[l], cache_attn_v[l]), h0s, state_rwkv[l], state_shift[l], p, bias)
        for lst, s in zip(out_p, st_p):
            lst.append(s)
        for lst, s in zip(out_s, st_s):
            lst.append(s)
    p_k, p_v, p_ssm, p_rwkv, p_shift = (jnp.stack(t) for t in out_p)
    s_k, s_v, s_ssm, s_rwkv, s_shift = (jnp.stack(t) for t in out_s)
    return (xp, xs, p_k, p_v, p_ssm, p_rwkv, p_shift, s_k, s_v, s_ssm, s_rwkv, s_shift)
```

```python
import functools
import math

import numpy as np
import jax
import jax.numpy as jnp
from jax import lax
from jax.experimental import pallas as pl
from jax.experimental.pallas import tpu as pltpu

F32 = jnp.float32
BF16 = jnp.bfloat16
HI = lax.Precision.HIGHEST

D_MODEL = 1024
DEPTH = 2
HEAD_DIM = 64
W_A = 256
C_GRP = 16
G_A = 16
N_A = 64
N_ST = G_A * N_A
W_B = 384
H_B = 6
BRANCHES = ((128, 1), (512, 4), (2048, 16))
WIN_MAX = 2048
N_BUCKETS = 32
MAX_DIST = WIN_MAX
NEG_INF = -1e30
W_C = 384
H_C = 6
R_W = 32
R_A = 32
R_G = 64
C_SHIFT = 3 * W_C + R_W + R_A + R_G
N_IN = W_A + 3 * W_B + C_SHIFT
GN_EPS = 64e-5
N_EXPERTS = 32
TOP_K = 4
D_FF = D_MODEL
SWIGLU_ALPHA = 1.702
SWIGLU_LIMIT = 7.0
ALPHA = (2 * DEPTH) ** 0.25
LN_EPS = 1e-5

LANES = 128
Q_TILE = 128
RWKV_CHUNK = 64
S5_CHUNK = 64
ROW_TILE = 512
MOE_ROWS = 256
COMBINE_TOK = 128
VMEM_LIMIT = 56 * 1024 * 1024


def _cp(sem, vmem=VMEM_LIMIT):
    return pltpu.CompilerParams(dimension_semantics=sem, vmem_limit_bytes=vmem)


def _dot(a, b, precision=None):
    return jnp.dot(a, b, preferred_element_type=F32, precision=precision)


def _dot_nt(a, b, precision=None):
    return lax.dot_general(a, b, (((1,), (1,)), ((), ())),
                           preferred_element_type=F32, precision=precision)


def _dot_tn(a, b, precision=None):
    return lax.dot_general(a, b, (((0,), (0,)), ((), ())),
                           preferred_element_type=F32, precision=precision)


def _layernorm_rows(x, g, b):
    mu = jnp.mean(x, axis=-1, keepdims=True)
    d = x - mu
    var = jnp.mean(d * d, axis=-1, keepdims=True)
    return d * lax.rsqrt(var + LN_EPS) * g + b


def _sigmoid(x):
    return 1.0 / (1.0 + jnp.exp(-x))


def _softplus(x):
    return jnp.maximum(x, 0.0) + jnp.log(1.0 + jnp.exp(-jnp.abs(x)))


def _gelu(x):
    c = math.sqrt(2.0 / math.pi)
    return 0.5 * x * (1.0 + jnp.tanh(c * (x + 0.044715 * (x * x * x))))


_PROJ_SPLITS = (0, W_A, W_A + W_B, W_A + 2 * W_B, W_A + 3 * W_B, N_IN)


def _proj_kernel(x_ref, w_ref, u_ref, q_ref, k_ref, v_ref, pc_ref):
    xb = x_ref[...].astype(BF16)
    outs = (u_ref, q_ref, k_ref, v_ref, pc_ref)
    for o_ref, lo, hi in zip(outs, _PROJ_SPLITS[:-1], _PROJ_SPLITS[1:]):
        o_ref[...] = _dot(xb, w_ref[:, lo:hi])


def _proj(x, w_bf16):
    n = x.shape[0]
    widths = [hi - lo for lo, hi in zip(_PROJ_SPLITS[:-1], _PROJ_SPLITS[1:])]
    return pl.pallas_call(
        _proj_kernel,
        out_shape=[jax.ShapeDtypeStruct((n, w), F32) for w in widths],
        grid=(n // ROW_TILE,),
        in_specs=[pl.BlockSpec((ROW_TILE, D_MODEL), lambda i: (i, 0)),
                  pl.BlockSpec((D_MODEL, N_IN), lambda i: (0, 0))],
        out_specs=[pl.BlockSpec((ROW_TILE, w), lambda i: (i, 0)) for w in widths],
        compiler_params=_cp(("parallel",)),
        name="proj",
    )(x, w_bf16)


def _s5_kernel(u_ref, lam_ref, bblk_ref, cblk_ref, d_ref, gw_ref, gb_ref,
               o_ref, ht_ref, utm, hs, ytm, h_scr, *, nb, lt):
    c = pl.program_id(0)

    @pl.when(c == 0)
    def _():
        h_scr[...] = jnp.zeros_like(h_scr)

    n_half = W_A // LANES
    for b in range(nb):
        for j in range(n_half):
            utm[j, pl.ds(b, lt, stride=nb), :] = u_ref[b, :, j * LANES:(j + 1) * LANES]
    u_all = jnp.concatenate([utm[j] for j in range(n_half)], axis=1)
    hs[...] = _dot(u_all.astype(BF16), bblk_ref[...])
    lr = jnp.broadcast_to(lam_ref[0:1, :], (nb, N_ST))
    li = jnp.broadcast_to(lam_ref[1:2, :], (nb, N_ST))

    def body(t, carry):
        hr, hi = carry
        row = pl.multiple_of(t * nb, nb)
        br = hs[pl.ds(row, nb), 0:N_ST]
        bi = hs[pl.ds(row, nb), N_ST:2 * N_ST]
        nr = lr * hr - li * hi + br
        ni = lr * hi + li * hr + bi
        hs[pl.ds(row, nb), 0:N_ST] = nr
        hs[pl.ds(row, nb), N_ST:2 * N_ST] = ni
        return nr, ni

    hr, hi = lax.fori_loop(0, lt, body, (h_scr[0], h_scr[1]), unroll=2)
    h_scr[0] = hr
    h_scr[1] = hi
    ht_ref[0] = hr
    ht_ref[1] = hi
    y = _dot(hs[...].astype(BF16), cblk_ref[...]) + d_ref[...] * u_all
    z = _gelu(y)
    gl = _dot(z.astype(BF16), gw_ref[...]) + gb_ref[...]
    res = z * _sigmoid(gl)
    for j in range(n_half):
        ytm[j] = res[:, j * LANES:(j + 1) * LANES]
    for b in range(nb):
        for j in range(n_half):
            o_ref[b, :, j * LANES:(j + 1) * LANES] = ytm[j, pl.ds(b, lt, stride=nb), :]


def _s5_params(a_re, a_im, log_dt, b_re, b_im, c_re, c_im):
    lam = lax.complex(a_re, a_im)
    dt = jnp.exp(log_dt)[:, None]
    lam_bar = jnp.exp(lam * dt)
    b_bar = ((lam_bar - 1.0) / lam)[..., None] * lax.complex(b_re, b_im)
    eye = jnp.eye(G_A, dtype=F32)

    def blk_in(m):
        return jnp.einsum("gnc,gh->gchn", m, eye).reshape(W_A, N_ST)

    def blk_out(m):
        return jnp.einsum("gcn,gh->gnhc", m, eye).reshape(N_ST, W_A)

    bblk = jnp.concatenate([blk_in(b_bar.real), blk_in(b_bar.imag)], axis=1)
    cblk = jnp.concatenate([blk_out(c_re), blk_out(-c_im)], axis=0)
    lam2 = jnp.stack([lam_bar.real.reshape(N_ST), lam_bar.imag.reshape(N_ST)])
    bblk3 = jnp.concatenate([blk_in(b_bar.real), blk_in(b_bar.imag - b_bar.real)], axis=1)
    cblk3 = jnp.concatenate([blk_out(c_re), blk_out(c_re + c_im)], axis=0)
    return (lam2, bblk, cblk), (lam2, bblk3.astype(BF16), cblk3.astype(BF16))


def _s5_prompt(u, lam2, bblk, cblk, d, gw, gb):
    nb, t, _ = u.shape
    lt = min(S5_CHUNK, t)
    kern = functools.partial(_s5_kernel, nb=nb, lt=lt)
    const = lambda shape: pl.BlockSpec(shape, lambda c: (0,) * len(shape))
    return pl.pallas_call(
        kern,
        out_shape=[jax.ShapeDtypeStruct((nb, t, W_A), F32),
                   jax.ShapeDtypeStruct((2, nb, N_ST), F32)],
        grid=(t // lt,),
        in_specs=[pl.BlockSpec((nb, lt, W_A), lambda c: (0, c, 0)),
                  const((2, N_ST)), const((W_A, 2 * N_ST)), const((2 * N_ST, W_A)),
                  const((1, W_A)), const((W_A, W_A)), const((1, W_A))],
        out_specs=[pl.BlockSpec((nb, lt, W_A), lambda c: (0, c, 0)),
                   const((2, nb, N_ST))],
        scratch_shapes=[pltpu.VMEM((W_A // LANES, nb * lt, LANES), F32),
                        pltpu.VMEM((nb * lt, 2 * N_ST), F32),
                        pltpu.VMEM((W_A // LANES, nb * lt, LANES), F32),
                        pltpu.VMEM((2, nb, N_ST), F32)],
        compiler_params=_cp(("arbitrary",)),
        name="s5_prompt",
    )(u, lam2, bblk.astype(BF16), cblk.astype(BF16), d.reshape(1, W_A),
      gw.astype(BF16), gb.reshape(1, W_A))


def _branch_offsets():
    return np.stack([np.arange(w // d + 1) * d for (w, d) in BRANCHES]).astype(np.int32)


def _t5_buckets(dist):
    max_exact = N_BUCKETS // 2
    d = np.maximum(dist, 1).astype(np.float32)
    large = max_exact + (np.log(d / max_exact) / np.log(MAX_DIST / max_exact)
                         * (N_BUCKETS - max_exact)).astype(np.int32)
    return np.where(dist < max_exact, dist, np.minimum(large, N_BUCKETS - 1)).astype(np.int32)


def _log_bias_table(rel_bias):
    offs = _branch_offsets()
    bias = rel_bias[jnp.asarray(_t5_buckets(offs))]
    table = jnp.full((WIN_MAX + 1, H_B), -jnp.inf, F32)
    for br in range(len(BRANCHES)):
        idx = jnp.asarray(offs[br])
        table = table.at[idx].set(jnp.logaddexp(table[idx], bias[br]))
    return jnp.maximum(table, NEG_INF)


def _bias_tiles(table, n_diff):
    i = np.arange(Q_TILE)[:, None]
    j = np.arange(Q_TILE)[None, :]
    dist = np.arange(n_diff)[:, None, None] * Q_TILE + (i - j)[None]
    ok = (dist >= 0) & (dist <= WIN_MAX)
    tiles = table[jnp.asarray(np.clip(dist, 0, WIN_MAX))]
    tiles = jnp.where(jnp.asarray(ok)[..., None], tiles, NEG_INF)
    return jnp.transpose(tiles, (3, 0, 1, 2))


def _attn_kernel(q_ref, k_ref, v_ref, bias_ref, o_ref, kb, vb, *, n_diff):
    qi = pl.program_id(2)

    @pl.when(qi == 0)
    def _():
        kb[...] = k_ref[0].astype(BF16)
        vb[...] = v_ref[0].astype(BF16)

    q = q_ref[0] * (HEAD_DIM ** -0.5)
    lane = lax.broadcasted_iota(jnp.int32, (Q_TILE, LANES), 1)
    out = jnp.zeros((Q_TILE, LANES), F32)
    for h in range(LANES // HEAD_DIM):
        msk = (lane >= HEAD_DIM * h) & (lane < HEAD_DIM * (h + 1))
        qh = jnp.where(msk, q, 0.0).astype(BF16)

        def body(kj, carry, qh=qh, h=h):
            m, l, acc = carry
            row = pl.multiple_of(kj * Q_TILE, Q_TILE)
            s = _dot_nt(qh, kb[pl.ds(row, Q_TILE), :]) + bias_ref[h, qi - kj]
            m_new = jnp.maximum(m, jnp.max(s, axis=-1, keepdims=True))
            p = jnp.exp(s - m_new)
            alpha = jnp.exp(m - m_new)
            l = alpha * l + jnp.sum(p, axis=-1, keepdims=True)
            acc = alpha * acc + _dot(p.astype(BF16), vb[pl.ds(row, Q_TILE), :])
            return m_new, l, acc

        init = (jnp.full((Q_TILE, 1), NEG_INF, F32), jnp.zeros((Q_TILE, 1), F32),
                jnp.zeros((Q_TILE, LANES), F32))
        _, l, acc = lax.fori_loop(jnp.maximum(qi - (n_diff - 1), 0), qi + 1, body, init)
        out = jnp.where(msk, acc / l, out)
    o_ref[0] = out


def _attn_prompt(q, k, v, tiles):
    nb, t, _ = q.shape
    n_diff = tiles.shape[1]
    hp = W_B // LANES
    per = LANES // HEAD_DIM
    kern = functools.partial(_attn_kernel, n_diff=n_diff)
    return pl.pallas_call(
        kern,
        out_shape=jax.ShapeDtypeStruct((nb, t, W_B), F32),
        grid=(nb, hp, t // Q_TILE),
        in_specs=[pl.BlockSpec((1, Q_TILE, LANES), lambda b, p, i: (b, i, p)),
                  pl.BlockSpec((1, t, LANES), lambda b, p, i: (b, 0, p)),
                  pl.BlockSpec((1, t, LANES), lambda b, p, i: (b, 0, p)),
                  pl.BlockSpec((per, n_diff, Q_TILE, Q_TILE), lambda b, p, i: (p, 0, 0, 0))],
        out_specs=pl.BlockSpec((1, Q_TILE, LANES), lambda b, p, i: (b, i, p)),
        scratch_shapes=[pltpu.VMEM((t, LANES), BF16), pltpu.VMEM((t, LANES), BF16)],
        compiler_params=_cp(("parallel", "parallel", "arbitrary")),
        name="attn_prompt",
    )(q, k, v, tiles)


def _rwkv_pre(pc, prev, prm):
    (mu, w0, w2p, a0, a2p, g2p, k_k, k_a, r_k, ln_g, ln_b, mseg) = prm
    xs = pc + (prev - pc) * mu
    r = xs[:, 0:W_C]
    k = xs[:, W_C:2 * W_C]
    v = xs[:, 2 * W_C:3 * W_C]
    tail = xs[:, 3 * W_C:C_SHIFT]
    w_raw = -_softplus(-(w0 + _dot(jnp.tanh(tail).astype(BF16), w2p))) - 0.5
    a = _sigmoid(a0 + _dot(tail.astype(BF16), a2p))
    g = _dot(_sigmoid(tail).astype(BF16), g2p)
    kk = k * k_k
    nrm = jnp.sqrt(_dot(kk * kk, mseg, HI))
    kk = kk / jnp.maximum(nrm, 1e-12)
    k2 = k * (1.0 + (a - 1.0) * k_a)
    logw = -jnp.exp(w_raw)
    return r, k2, v, kk, a, g, logw


def _rwkv_post(y, r, k2, v, g, prm):
    (mu, w0, w2p, a0, a2p, g2p, k_k, k_a, r_k, ln_g, ln_b, mseg) = prm
    mean = _dot(y, mseg, HI) * (1.0 / HEAD_DIM)
    d = y - mean
    var = _dot(d * d, mseg, HI) * (1.0 / HEAD_DIM)
    yn = d * lax.rsqrt(var + GN_EPS) * ln_g + ln_b
    bonus = _dot(r * k2 * r_k, mseg, HI) * v
    return (yn + bonus) * g


def _rwkv_kernel(pc_ref, mu_ref, w0_ref, w2_ref, a0_ref, a2_ref, g2_ref, kk_ref, ka_ref,
                 rk_ref, lng_ref, lnb_ref, mseg_ref, ltri_ref,
                 o_ref, s_ref, s_scr, prev_scr, y_scr, *, ch):
    c = pl.program_id(1)

    @pl.when(c == 0)
    def _():
        s_scr[...] = jnp.zeros_like(s_scr)
        prev_scr[...] = jnp.zeros_like(prev_scr)

    prm = (mu_ref[...], w0_ref[...], w2_ref[...], a0_ref[...], a2_ref[...], g2_ref[...],
           kk_ref[...], ka_ref[...], rk_ref[...], lng_ref[...], lnb_ref[...], mseg_ref[...])
    pc = pc_ref[0]
    rowi = lax.broadcasted_iota(jnp.int32, (ch, C_SHIFT), 0)
    prev = jnp.where(rowi == 0, prev_scr[...], pltpu.roll(pc, 1, axis=0))
    prev_scr[...] = pc[ch - 1:ch, :]
    r, k2, v, kk, a, g, logw = _rwkv_pre(pc, prev, prm)

    cs = _dot(ltri_ref[...], logw, HI)
    g_in = jnp.exp(cs)
    g_ex = jnp.exp(cs - logw)
    g_inv = jnp.exp(-cs)
    al = -kk * g_ex
    rt = r * g_in
    bh = kk * a * g_inv
    kh = k2 * g_inv
    g_end = g_in[ch - 1:ch, :]
    bc = bh * g_end
    kc = kh * g_end

    ti = lax.broadcasted_iota(jnp.int32, (ch, ch), 0)
    si = lax.broadcasted_iota(jnp.int32, (ch, ch), 1)
    strict = si < ti
    incl = si <= ti
    n_sq = int(math.log2(ch))
    for h in range(H_C):
        sl = slice(h * HEAD_DIM, (h + 1) * HEAD_DIM)
        al_h, rt_h, bh_h, kh_h, v_h = al[:, sl], rt[:, sl], bh[:, sl], kh[:, sl], v[:, sl]
        a_m = jnp.where(strict, _dot_nt(al_h, bh_h, HI), 0.0)
        b_m = jnp.where(strict, _dot_nt(al_h, kh_h, HI), 0.0)
        rb_m = jnp.where(incl, _dot_nt(rt_h, bh_h, HI), 0.0)
        rk_m = jnp.where(incl, _dot_nt(rt_h, kh_h, HI), 0.0)
        s0 = s_scr[h]
        x = _dot_nt(al_h, s0, HI) + _dot(b_m, v_h, HI)
        a_pow = a_m
        for it in range(n_sq):
            x = x + _dot(a_pow, x, HI)
            if it + 1 < n_sq:
                a_pow = _dot(a_pow, a_pow, HI)
        y_scr[:, sl] = _dot_nt(rt_h, s0, HI) + _dot(rb_m, x, HI) + _dot(rk_m, v_h, HI)
        s_new = s0 * g_end[:, sl] + _dot_tn(x, bc[:, sl], HI) + _dot_tn(v_h, kc[:, sl], HI)
        s_scr[h] = s_new
        s_ref[0, h] = s_new

    o_ref[0] = _rwkv_post(y_scr[...], r, k2, v, g, prm)


def _rwkv_params(p):
    pad = lambda w, lo: jnp.zeros((R_W + R_A + R_G, W_C), BF16).at[lo:lo + w.shape[0]].set(w.astype(BF16))
    seg = np.arange(W_C) // HEAD_DIM
    mseg = jnp.asarray((seg[:, None] == seg[None, :]).astype(np.float32))
    row = lambda t: t.reshape(1, -1)
    return (row(p["rwkv_mu"]), row(p["rwkv_w0"]), pad(p["rwkv_w2"], 0), row(p["rwkv_a0"]),
            pad(p["rwkv_a2"], R_W), pad(p["rwkv_g2"], R_W + R_A), row(p["rwkv_k_k"]),
            row(p["rwkv_k_a"]), row(p["rwkv_r_k"]), row(p["rwkv_ln_g"]), row(p["rwkv_ln_b"]), mseg)


def _rwkv_prompt(pc, prm):
    nb, t, _ = pc.shape
    ch = min(RWKV_CHUNK, t)
    ltri = jnp.asarray(np.tril(np.ones((ch, ch), np.float32)))
    kern = functools.partial(_rwkv_kernel, ch=ch)
    const = lambda a: pl.BlockSpec(a.shape, lambda b, c: (0,) * a.ndim)
    args = list(prm) + [ltri]
    return pl.pallas_call(
        kern,
        out_shape=[jax.ShapeDtypeStruct((nb, t, W_C), F32),
                   jax.ShapeDtypeStruct((nb, H_C, HEAD_DIM, HEAD_DIM), F32)],
        grid=(nb, t // ch),
        in_specs=[pl.BlockSpec((1, ch, C_SHIFT), lambda b, c: (b, c, 0))] + [const(a) for a in args],
        out_specs=[pl.BlockSpec((1, ch, W_C), lambda b, c: (b, c, 0)),
                   pl.BlockSpec((1, H_C, HEAD_DIM, HEAD_DIM), lambda b, c: (b, 0, 0, 0))],
        scratch_shapes=[pltpu.VMEM((H_C, HEAD_DIM, HEAD_DIM), F32),
                        pltpu.VMEM((1, C_SHIFT), F32),
                        pltpu.VMEM((ch, W_C), F32)],
        compiler_params=_cp(("parallel", "arbitrary")),
        name="rwkv_prompt",
    )(pc, *args)


def _route(x1, rw, rb):
    logits = _dot(x1.astype(BF16), rw) + rb
    lane = lax.broadcasted_iota(jnp.int32, logits.shape, 1).astype(F32)
    vals, idxs = [], []
    cur = logits
    for _ in range(TOP_K):
        m = jnp.max(cur, axis=-1, keepdims=True)
        idx = jnp.min(jnp.where(cur == m, lane, float(N_EXPERTS)), axis=-1, keepdims=True)
        vals.append(m)
        idxs.append(idx)
        cur = jnp.where(lane == idx, -jnp.inf, cur)
    ex = [jnp.exp(v - vals[0]) for v in vals]
    tot = ex[0] + ex[1] + ex[2] + ex[3]
    gates = jnp.concatenate([e / tot for e in ex], axis=-1)
    return jnp.concatenate(idxs, axis=-1).astype(jnp.int32), gates


def _outproj_kernel(x_ref, oa_ref, ob_ref, oc_ref, w_ref, g_ref, b_ref, rw_ref, rb_ref,
                    x1_ref, idx_ref, gate_ref):
    mix = (_dot(oa_ref[...].astype(BF16), w_ref[0:W_A, :])
           + _dot(ob_ref[...].astype(BF16), w_ref[W_A:W_A + W_B, :])
           + _dot(oc_ref[...].astype(BF16), w_ref[W_A + W_B:, :]))
    x1 = _layernorm_rows(ALPHA * x_ref[...] + mix, g_ref[...], b_ref[...])
    x1_ref[...] = x1
    idx, gates = _route(x1, rw_ref[...], rb_ref[...])
    idx_ref[...] = idx
    gate_ref[...] = gates


def _outproj(x, oa, ob, oc, w, g, b, rw, rb):
    n = x.shape[0]
    tm = min(ROW_TILE, n)
    rows = lambda w_: pl.BlockSpec((tm, w_), lambda i: (i, 0))
    const = lambda a: pl.BlockSpec(a.shape, lambda i: (0,) * a.ndim)
    consts = [w.astype(BF16), g.reshape(1, -1), b.reshape(1, -1), rw.astype(BF16), rb.reshape(1, -1)]
    return pl.pallas_call(
        _outproj_kernel,
        out_shape=[jax.ShapeDtypeStruct((n, D_MODEL), F32),
                   jax.ShapeDtypeStruct((n, TOP_K), jnp.int32),
                   jax.ShapeDtypeStruct((n, TOP_K), F32)],
        grid=(n // tm,),
        in_specs=[rows(D_MODEL), rows(W_A), rows(W_B), rows(W_C)] + [const(a) for a in consts],
        out_specs=[rows(D_MODEL), rows(TOP_K), rows(TOP_K)],
        compiler_params=_cp(("parallel",)),
        name="outproj",
    )(x, oa, ob, oc, *consts)


def _route_meta(top_idx, n_blk):
    n_tok = top_idx.shape[0]
    n_asg = n_tok * TOP_K
    expert = top_idx.reshape(n_asg)
    onehot = (expert[:, None] == jnp.arange(N_EXPERTS, dtype=jnp.int32)[None, :]).astype(jnp.int32)
    csum = jnp.cumsum(onehot, axis=0)
    rank = jnp.sum(csum * onehot, axis=1) - 1
    counts = csum[-1]
    pcounts = (counts + MOE_ROWS - 1) // MOE_ROWS * MOE_ROWS
    pends = jnp.cumsum(pcounts)
    pstarts = pends - pcounts
    dest = pstarts[expert] + rank
    token = jnp.arange(n_asg, dtype=jnp.int32) // TOP_K
    row_token = jnp.zeros((n_blk * MOE_ROWS,), jnp.int32).at[dest].set(token)
    blk_expert = jnp.minimum(
        jnp.searchsorted(pends, jnp.arange(n_blk, dtype=jnp.int32) * MOE_ROWS, side="right"),
        N_EXPERTS - 1).astype(jnp.int32)
    n_active = (pends[-1] // MOE_ROWS).astype(jnp.int32).reshape(1)
    return dest.astype(jnp.int32), row_token, blk_expert, n_active


def _swiglu(h):
    h_glu = jnp.minimum(h[:, :D_FF], SWIGLU_LIMIT)
    h_lin = jnp.clip(h[:, D_FF:], -SWIGLU_LIMIT, SWIGLU_LIMIT)
    return h_glu * _sigmoid(SWIGLU_ALPHA * h_glu) * (h_lin + 1.0)


def _moe_kernel(be_ref, na_ref, tok_ref, x_hbm, wup_ref, bup_ref, wdn_ref, bdn_ref,
                y_ref, xbuf, wup_b, wdn_b, sem):
    i = pl.program_id(0)
    active = i < na_ref[0]

    def row_copy(r, tok):
        return pltpu.make_async_copy(x_hbm.at[pl.ds(tok, 1), :], xbuf.at[pl.ds(r, 1), :], sem)

    @pl.when(active)
    def _():
        def issue(r, carry):
            row_copy(r, tok_ref[0, 0, r]).start()
            return carry

        lax.fori_loop(0, MOE_ROWS, issue, 0)
        changed = jnp.logical_or(i == 0, be_ref[i] != be_ref[jnp.maximum(i - 1, 0)])

        @pl.when(changed)
        def _():
            wup_b[...] = wup_ref[...].astype(BF16)
            wdn_b[...] = wdn_ref[...].astype(BF16)

        def drain(r, carry):
            row_copy(r, 0).wait()
            return carry

        lax.fori_loop(0, MOE_ROWS, drain, 0)
        h = _dot(xbuf[...].astype(BF16), wup_b[...]) + bup_ref[...]
        y_ref[...] = _dot(_swiglu(h).astype(BF16), wdn_b[...]) + bdn_ref[...]

    @pl.when(jnp.logical_not(active))
    def _():
        y_ref[...] = jnp.zeros_like(y_ref)


def _moe_rows(x1, row_token, blk_expert, n_active, w_up, b_up, w_dn, b_dn, layer):
    n_blk = blk_expert.shape[0]
    grid_spec = pltpu.PrefetchScalarGridSpec(
        num_scalar_prefetch=2,
        grid=(n_blk,),
        in_specs=[
            pl.BlockSpec((1, 1, MOE_ROWS), lambda i, be, na: (i, 0, 0), memory_space=pltpu.SMEM),
            pl.BlockSpec(memory_space=pl.ANY),
            pl.BlockSpec((None, None, D_MODEL, 2 * D_FF), lambda i, be, na: (layer, be[i], 0, 0)),
            pl.BlockSpec((None, None, 1, 2 * D_FF), lambda i, be, na: (layer, be[i], 0, 0)),
            pl.BlockSpec((None, None, D_FF, D_MODEL), lambda i, be, na: (layer, be[i], 0, 0)),
            pl.BlockSpec((None, None, 1, D_MODEL), lambda i, be, na: (layer, be[i], 0, 0)),
        ],
        out_specs=pl.BlockSpec((MOE_ROWS, D_MODEL), lambda i, be, na: (i, 0)),
        scratch_shapes=[pltpu.VMEM((MOE_ROWS, D_MODEL), F32),
                        pltpu.VMEM((D_MODEL, 2 * D_FF), BF16),
                        pltpu.VMEM((D_FF, D_MODEL), BF16),
                        pltpu.SemaphoreType.DMA],
    )
    return pl.pallas_call(
        _moe_kernel,
        out_shape=jax.ShapeDtypeStruct((n_blk * MOE_ROWS, D_MODEL), F32),
        grid_spec=grid_spec,
        compiler_params=_cp(("arbitrary",)),
        name="moe_rows",
    )(blk_expert, n_active, row_token.reshape(n_blk, 1, MOE_ROWS), x1,
      w_up, b_up.reshape(DEPTH, N_EXPERTS, 1, 2 * D_FF), w_dn, b_dn.reshape(DEPTH, N_EXPERTS, 1, D_MODEL))


def _combine_kernel(dest_ref, gate_ref, x1_ref, y_hbm, g_ref, b_ref, o_ref, ybuf, sem, *, tm):
    def row_copy(r, src):
        return pltpu.make_async_copy(y_hbm.at[pl.ds(src, 1), :], ybuf.at[pl.ds(r, 1), :], sem)

    def issue(r, carry):
        row_copy(r, dest_ref[0, 0, r]).start()
        return carry

    lax.fori_loop(0, TOP_K * tm, issue, 0)

    def drain(r, carry):
        row_copy(r, 0).wait()
        return carry

    lax.fori_loop(0, TOP_K * tm, drain, 0)
    gates = gate_ref[...]
    moe = jnp.zeros((tm, D_MODEL), F32)
    for k in range(TOP_K):
        moe = moe + gates[:, k:k + 1] * ybuf[k * tm:(k + 1) * tm, :]
    o_ref[...] = _layernorm_rows(ALPHA * x1_ref[...] + moe, g_ref[...], b_ref[...])


def _combine(dest, gates, x1, y_rows, g, b):
    n = x1.shape[0]
    tm = COMBINE_TOK
    nt = n // tm
    dest_t = dest.reshape(nt, tm, TOP_K).transpose(0, 2, 1).reshape(nt, 1, TOP_K * tm)
    kern = functools.partial(_combine_kernel, tm=tm)
    return pl.pallas_call(
        kern,
        out_shape=jax.ShapeDtypeStruct((n, D_MODEL), F32),
        grid=(nt,),
        in_specs=[pl.BlockSpec((1, 1, TOP_K * tm), lambda i: (i, 0, 0), memory_space=pltpu.SMEM),
                  pl.BlockSpec((tm, TOP_K), lambda i: (i, 0)),
                  pl.BlockSpec((tm, D_MODEL), lambda i: (i, 0)),
                  pl.BlockSpec(memory_space=pl.ANY),
                  pl.BlockSpec((1, D_MODEL), lambda i: (0, 0)),
                  pl.BlockSpec((1, D_MODEL), lambda i: (0, 0))],
        out_specs=pl.BlockSpec((tm, D_MODEL), lambda i: (i, 0)),
        scratch_shapes=[pltpu.VMEM((TOP_K * tm, D_MODEL), F32), pltpu.SemaphoreType.DMA],
        compiler_params=_cp(("arbitrary",)),
        name="moe_combine",
    )(dest_t, gates, x1, y_rows, g.reshape(1, -1), b.reshape(1, -1))


def _moe_prompt(x1, top_idx, gates, p, layer):
    n = x1.shape[0]
    n_blk = -(-n * TOP_K // MOE_ROWS) + N_EXPERTS
    dest, row_token, blk_expert, n_active = _route_meta(top_idx, n_blk)
    y_rows = _moe_rows(x1, row_token, blk_expert, n_active, p["moe_w_up"], p["moe_b_up"],
                       p["moe_w_down"], p["moe_b_down"], layer)
    return _combine(dest, gates, x1, y_rows, p["ln2_g"][layer], p["ln2_b"][layer])


def _sample_proj_kernel(x_ref, w_ref, h0_ref, lam_ref, bblk_ref, cblk_ref, d_ref, gw_ref, gb_ref,
                        q_ref, k_ref, v_ref, pc_ref, oa_ref, h_ref):
    xb = x_ref[...].astype(BF16)
    mm = lambda lo, hi: _dot(xb, w_ref[:, lo:hi])
    u = mm(_PROJ_SPLITS[0], _PROJ_SPLITS[1])
    q_ref[...] = mm(_PROJ_SPLITS[1], _PROJ_SPLITS[2])
    k_ref[...] = mm(_PROJ_SPLITS[2], _PROJ_SPLITS[3])
    v_ref[...] = mm(_PROJ_SPLITS[3], _PROJ_SPLITS[4])
    pc_ref[...] = mm(_PROJ_SPLITS[4], _PROJ_SPLITS[5])
    bu = _dot(u.astype(BF16), bblk_ref[...])
    bu_re = bu[:, 0:N_ST]
    bu_im = bu_re + bu[:, N_ST:2 * N_ST]
    lr, li = lam_ref[0:1, :], lam_ref[1:2, :]
    hr0, hi0 = h0_ref[0], h0_ref[1]
    hr = bu_re + (lr * hr0 - li * hi0)
    hi = bu_im + (lr * hi0 + li * hr0)
    h_ref[0] = hr
    h_ref[1] = hi
    y = (_dot((hr + hi).astype(BF16), cblk_ref[0:N_ST, :])
         - _dot(hi.astype(BF16), cblk_ref[N_ST:2 * N_ST, :])) + d_ref[...] * u
    z = _gelu(y)
    oa_ref[...] = z * _sigmoid(_dot(z.astype(BF16), gw_ref[...]) + gb_ref[...])


def _sample_proj(x, w_in, h0, lam2, bblk, cblk, d, gw, gb):
    n = x.shape[0]
    widths = (W_B, W_B, W_B, C_SHIFT, W_A)
    return pl.pallas_call(
        _sample_proj_kernel,
        out_shape=[jax.ShapeDtypeStruct((n, w), F32) for w in widths]
        + [jax.ShapeDtypeStruct((2, n, N_ST), F32)],
        compiler_params=_cp(None),
        name="sample_proj",
    )(x, w_in, h0, lam2, bblk, cblk, d.reshape(1, -1), gw.astype(BF16), gb.reshape(1, -1))


def _sample_mix_kernel(q_ref, kn_ref, vn_ref, pc_ref, sh_ref, kc_ref, vc_ref, s0_ref, lb_ref, lb0_ref,
                       mu_ref, w0_ref, w2_ref, a0_ref, a2_ref, g2_ref, kk_ref, ka_ref,
                       rk_ref, lng_ref, lnb_ref, mseg_ref,
                       ob_ref, oc_ref, s_ref):
    rows = 8
    lane = lax.broadcasted_iota(jnp.int32, (rows, W_B), 1)
    sub = lax.broadcasted_iota(jnp.int32, (rows, W_B), 0)
    hmask = (lane // HEAD_DIM) == sub
    rnd = lambda t: t.astype(BF16).astype(F32)
    q = q_ref[0] * (HEAD_DIM ** -0.5)
    qrows = jnp.where(hmask, jnp.broadcast_to(q, (rows, W_B)), 0.0).astype(BF16)
    kcb = kc_ref[0].astype(BF16)
    vcb = vc_ref[0].astype(BF16)
    knb = rnd(kn_ref[0])
    vnb = rnd(vn_ref[0])
    s_all = _dot_nt(qrows, kcb)
    s_new = jnp.sum(qrows.astype(F32) * knb, axis=-1, keepdims=True)
    outs, lses = [], []
    for br in range(len(BRANCHES)):
        s = s_all + lb_ref[br]
        s0 = s_new + lb0_ref[br]
        m = jnp.maximum(jnp.max(s, axis=-1, keepdims=True), s0)
        lse = m + jnp.log(jnp.sum(jnp.exp(s - m), axis=-1, keepdims=True) + jnp.exp(s0 - m))
        o = _dot(jnp.exp(s - lse).astype(BF16), vcb) + rnd(jnp.exp(s0 - lse)) * vnb
        outs.append(rnd(o))
        lses.append(lse)
    top = jnp.maximum(jnp.maximum(lses[0], lses[1]), lses[2])
    ex = [jnp.exp(t - top) for t in lses]
    tot = ex[0] + ex[1] + ex[2]
    o = rnd(ex[0] / tot) * outs[0] + rnd(ex[1] / tot) * outs[1] + rnd(ex[2] / tot) * outs[2]
    ob_ref[0] = jnp.sum(jnp.where(hmask, o, 0.0), axis=0, keepdims=True)

    prm = (mu_ref[...], w0_ref[...], w2_ref[...], a0_ref[...], a2_ref[...], g2_ref[...],
           kk_ref[...], ka_ref[...], rk_ref[...], lng_ref[...], lnb_ref[...], mseg_ref[...])
    pc = jnp.broadcast_to(pc_ref[0], (rows, C_SHIFT))
    prev = jnp.broadcast_to(sh_ref[0], (rows, C_SHIFT))
    r, k2, v, kk, a, g, logw = _rwkv_pre(pc, prev, prm)
    w = jnp.exp(logw)
    eye = (lax.broadcasted_iota(jnp.int32, (HEAD_DIM, HEAD_DIM), 0)
           == lax.broadcasted_iota(jnp.int32, (HEAD_DIM, HEAD_DIM), 1))
    ys = []
    for h in range(H_C):
        sl = slice(h * HEAD_DIM, (h + 1) * HEAD_DIM)
        row = lambda t: t[0:1, sl]
        col = lambda t: jnp.sum(jnp.where(eye, jnp.broadcast_to(row(t), (HEAD_DIM, HEAD_DIM)), 0.0),
                                axis=1, keepdims=True)
        s0 = s0_ref[0, h]
        sa = jnp.sum(rnd(s0) * rnd(-row(kk)), axis=1, keepdims=True)
        s1 = s0 * row(w) + sa * (row(kk) * row(a)) + col(v) * row(k2)
        s_ref[0, h] = s1
        y_col = jnp.sum(rnd(s1) * rnd(row(r)), axis=1, keepdims=True)
        ys.append(jnp.sum(jnp.where(eye, jnp.broadcast_to(y_col, (HEAD_DIM, HEAD_DIM)), 0.0),
                          axis=0, keepdims=True))
    y = jnp.broadcast_to(jnp.concatenate(ys, axis=1), (rows, W_C))
    oc_ref[0] = _rwkv_post(y, r, k2, v, g, prm)[0:1, :]


def _sample_mix(q, kn, vn, pc, shift0, k_cache, v_cache, s0, lb_rows, lb0, prm, layer):
    n = q.shape[0]
    l_buf = k_cache.shape[2]
    per_b = lambda w: pl.BlockSpec((1, 1, w), lambda b: (b, 0, 0))
    const = lambda a: pl.BlockSpec(a.shape, lambda b: (0,) * a.ndim)
    r3 = lambda t: t.reshape(n, 1, t.shape[-1])
    cache = pl.BlockSpec((None, 1, l_buf, W_B), lambda b: (layer, b, 0, 0))
    return pl.pallas_call(
        _sample_mix_kernel,
        out_shape=[jax.ShapeDtypeStruct((n, 1, W_B), F32),
                   jax.ShapeDtypeStruct((n, 1, W_C), F32),
                   jax.ShapeDtypeStruct((n, H_C, HEAD_DIM, HEAD_DIM), F32)],
        grid=(n,),
        in_specs=[per_b(W_B), per_b(W_B), per_b(W_B), per_b(C_SHIFT), per_b(C_SHIFT),
                  cache, cache,
                  pl.BlockSpec((1, H_C, HEAD_DIM, HEAD_DIM), lambda b: (b, 0, 0, 0)),
                  const(lb_rows), const(lb0)] + [const(a) for a in prm],
        out_specs=[per_b(W_B), per_b(W_C),
                   pl.BlockSpec((1, H_C, HEAD_DIM, HEAD_DIM), lambda b: (b, 0, 0, 0))],
        compiler_params=_cp(("parallel",)),
        name="sample_mix",
    )(r3(q), r3(kn), r3(vn), r3(pc), r3(shift0), k_cache, v_cache, s0, lb_rows, lb0, *prm)


def _sample_moe_kernel(x1_ref, gd_ref, wup_ref, bup_ref, wdn_ref, bdn_ref, g_ref, b_ref, o_ref, acc):
    e = pl.program_id(0)

    @pl.when(e == 0)
    def _():
        acc[...] = jnp.zeros_like(acc)

    x1 = x1_ref[...]
    h = _dot(x1.astype(BF16), wup_ref[...].astype(BF16)) + bup_ref[...]
    y = _dot(_swiglu(h).astype(BF16), wdn_ref[...].astype(BF16)) + bdn_ref[...]
    acc[...] += gd_ref[...] * y

    @pl.when(e == pl.num_programs(0) - 1)
    def _():
        o_ref[...] = _layernorm_rows(ALPHA * x1 + acc[...], g_ref[...], b_ref[...])


def _sample_moe(x1, top_idx, gates, p, layer):
    n = x1.shape[0]
    onehot = top_idx[:, :, None] == jnp.arange(N_EXPERTS, dtype=jnp.int32)[None, None, :]
    gd = jnp.sum(jnp.where(onehot, gates[:, :, None], 0.0), axis=1)
    gd = gd.T.reshape(N_EXPERTS, n, 1)
    return pl.pallas_call(
        _sample_moe_kernel,
        out_shape=jax.ShapeDtypeStruct((n, D_MODEL), F32),
        grid=(N_EXPERTS,),
        in_specs=[pl.BlockSpec((n, D_MODEL), lambda e: (0, 0)),
                  pl.BlockSpec((None, n, 1), lambda e: (e, 0, 0)),
                  pl.BlockSpec((None, None, D_MODEL, 2 * D_FF), lambda e: (layer, e, 0, 0)),
                  pl.BlockSpec((None, None, 1, 2 * D_FF), lambda e: (layer, e, 0, 0)),
                  pl.BlockSpec((None, None, D_FF, D_MODEL), lambda e: (layer, e, 0, 0)),
                  pl.BlockSpec((None, None, 1, D_MODEL), lambda e: (layer, e, 0, 0)),
                  pl.BlockSpec((1, D_MODEL), lambda e: (0, 0)),
                  pl.BlockSpec((1, D_MODEL), lambda e: (0, 0))],
        out_specs=pl.BlockSpec((n, D_MODEL), lambda e: (0, 0)),
        scratch_shapes=[pltpu.VMEM((n, D_MODEL), F32)],
        compiler_params=_cp(("arbitrary",)),
        name="sample_moe",
    )(x1, gd, p["moe_w_up"], p["moe_b_up"].reshape(DEPTH, N_EXPERTS, 1, 2 * D_FF),
      p["moe_w_down"], p["moe_b_down"].reshape(DEPTH, N_EXPERTS, 1, D_MODEL),
      p["ln2_g"][layer].reshape(1, -1), p["ln2_b"][layer].reshape(1, -1))


def _prompt_layer(x, nb, t, p, l, s5p, rwp, tiles):
    u, q, k, v, pc = _proj(x, p["w_in"][l].astype(BF16))
    shp = lambda a: a.reshape(nb, t, a.shape[-1])
    o_a, h_t = _s5_prompt(shp(u), *s5p, p["ssm_d"][l], p["ssm_glu_w"][l], p["ssm_glu_b"][l])
    o_b = _attn_prompt(shp(q), shp(k), shp(v), tiles)
    o_c, s_t = _rwkv_prompt(shp(pc), rwp)
    flat = lambda a: a.reshape(nb * t, a.shape[-1])
    x1, top_idx, gates = _outproj(x, flat(o_a), flat(o_b), flat(o_c), p["w_out"][l],
                                  p["ln1_g"][l], p["ln1_b"][l], p["moe_router_w"][l],
                                  p["moe_router_b"][l])
    x2 = _moe_prompt(x1, top_idx, gates, p, l)
    keep = min(WIN_MAX, t)
    ssm = jnp.transpose(h_t.reshape(2, nb, G_A, N_A), (1, 2, 3, 0))
    state = (shp(k)[:, t - keep:].reshape(nb, keep, H_B, HEAD_DIM),
             shp(v)[:, t - keep:].reshape(nb, keep, H_B, HEAD_DIM),
             ssm, s_t, shp(pc)[:, t - 1])
    return x2, state


def _sample_bias(rel_bias, l_buf):
    offs = _branch_offsets()
    bias = rel_bias[jnp.asarray(_t5_buckets(offs))]
    rows = []
    for br in range(len(BRANCHES)):
        m = np.arange(1, offs.shape[1])
        pos = l_buf - offs[br][m]
        ok = pos >= 0
        r = jnp.full((H_B, l_buf), NEG_INF, F32).at[:, jnp.asarray(pos[ok])].set(bias[br][jnp.asarray(m[ok])].T)
        rows.append(jnp.concatenate([r, jnp.zeros((8 - H_B, l_buf), F32)], axis=0))
    new = jnp.zeros((len(BRANCHES), 8, 1), F32).at[:, :H_B, 0].set(bias[:, 0])
    return jnp.stack(rows), new


def _sample_layer(x, p, l, s5p, rwp, sbias, k_cache, v_cache, st_ssm, st_rwkv, st_shift):
    n = x.shape[0]
    l_buf = k_cache.shape[2]
    lam2, bblk, cblk = s5p
    h0 = jnp.transpose(st_ssm.reshape(n, N_ST, 2), (2, 0, 1))
    q, kn, vn, pc, o_a, h1 = _sample_proj(x, p["w_in"][l].astype(BF16), h0, lam2, bblk, cblk, p["ssm_d"][l],
                                          p["ssm_glu_w"][l], p["ssm_glu_b"][l])
    lb_rows, lb0 = sbias
    o_b, o_c, s1 = _sample_mix(q, kn, vn, pc, st_shift, k_cache.reshape(DEPTH, n, l_buf, W_B),
                               v_cache.reshape(DEPTH, n, l_buf, W_B), st_rwkv, lb_rows, lb0, rwp, l)
    x1, top_idx, gates = _outproj(x, o_a, o_b.reshape(n, W_B), o_c.reshape(n, W_C), p["w_out"][l],
                                  p["ln1_g"][l], p["ln1_b"][l], p["moe_router_w"][l],
                                  p["moe_router_b"][l])
    x2 = _sample_moe(x1, top_idx, gates, p, l)
    ssm = jnp.transpose(h1.reshape(2, n, G_A, N_A), (1, 2, 3, 0))
    return x2, (kn, vn, ssm, s1, pc)


def kernel(x_prompt, x_sample, cache_attn_k, cache_attn_v, state_ssm, state_rwkv, state_shift, w_in, w_out, ln1_g, ln1_b, ln2_g, ln2_b, ssm_a_re, ssm_a_im, ssm_log_dt, ssm_b_re, ssm_b_im, ssm_c_re, ssm_c_im, ssm_d, ssm_glu_w, ssm_glu_b, rel_bias, rwkv_mu, rwkv_w0, rwkv_w2, rwkv_a0, rwkv_a2, rwkv_g2, rwkv_k_k, rwkv_k_a, rwkv_r_k, rwkv_ln_g, rwkv_ln_b, moe_router_w, moe_router_b, moe_w_up, moe_b_up, moe_w_down, moe_b_down):
    p = dict(w_in=w_in, w_out=w_out, ln1_g=ln1_g, ln1_b=ln1_b, ln2_g=ln2_g, ln2_b=ln2_b,
             ssm_d=ssm_d, ssm_glu_w=ssm_glu_w, ssm_glu_b=ssm_glu_b,
             moe_router_w=moe_router_w, moe_router_b=moe_router_b, moe_w_up=moe_w_up,
             moe_b_up=moe_b_up, moe_w_down=moe_w_down, moe_b_down=moe_b_down)
    nb, t, _ = x_prompt.shape
    ns = x_sample.shape[0]
    l_buf = cache_attn_k.shape[2]
    table = _log_bias_table(rel_bias)
    tiles = _bias_tiles(table, min(t, WIN_MAX) // Q_TILE + 1)
    sbias = _sample_bias(rel_bias, l_buf)
    xp = x_prompt.reshape(nb * t, D_MODEL)
    xs = x_sample.reshape(ns, D_MODEL)
    st_p, st_s = [], []
    for l in range(DEPTH):
        s5p, s5s = _s5_params(ssm_a_re[l], ssm_a_im[l], ssm_log_dt[l], ssm_b_re[l], ssm_b_im[l],
                              ssm_c_re[l], ssm_c_im[l])
        rwp = _rwkv_params(dict(rwkv_mu=rwkv_mu[l], rwkv_w0=rwkv_w0[l], rwkv_w2=rwkv_w2[l],
                                rwkv_a0=rwkv_a0[l], rwkv_a2=rwkv_a2[l], rwkv_g2=rwkv_g2[l],
                                rwkv_k_k=rwkv_k_k[l], rwkv_k_a=rwkv_k_a[l], rwkv_r_k=rwkv_r_k[l],
                                rwkv_ln_g=rwkv_ln_g[l], rwkv_ln_b=rwkv_ln_b[l]))
        xp, sp = _prompt_layer(xp, nb, t, p, l, s5p, rwp, tiles)
        xs, ss = _sample_layer(xs, p, l, s5s, rwp, sbias, cache_attn_k, cache_attn_v,
                               state_ssm[l], state_rwkv[l], state_shift[l])
        st_p.append(sp)
        st_s.append(ss)
    p_k, p_v, p_ssm, p_rwkv, p_shift = (jnp.stack(z) for z in zip(*st_p))
    kn, vn, s_ssm, s_rwkv, s_shift = (jnp.stack(z) for z in zip(*st_s))
    keep = min(WIN_MAX, l_buf + 1)
    drop = l_buf + 1 - keep
    s_k = jnp.concatenate([cache_attn_k[:, :, drop:], kn.reshape(DEPTH, ns, 1, H_B, HEAD_DIM)], axis=2)
    s_v = jnp.concatenate([cache_attn_v[:, :, drop:], vn.reshape(DEPTH, ns, 1, H_B, HEAD_DIM)], axis=2)
    return (xp.reshape(nb, t, D_MODEL), xs.reshape(ns, 1, D_MODEL),
            p_k, p_v, p_ssm, p_rwkv, p_shift, s_k, s_v, s_ssm, s_rwkv, s_shift)
```

```python
import functools
import math

import numpy as np
import jax
import jax.numpy as jnp
from jax import lax
from jax.experimental import pallas as pl
from jax.experimental.pallas import tpu as pltpu

F32 = jnp.float32
BF16 = jnp.bfloat16
HI = lax.Precision.HIGHEST

D_MODEL = 1024
DEPTH = 2
HEAD_DIM = 64
W_A = 256
C_GRP = 16
G_A = 16
N_A = 64
N_ST = G_A * N_A
W_B = 384
H_B = 6
BRANCHES = ((128, 1), (512, 4), (2048, 16))
WIN_MAX = 2048
N_BUCKETS = 32
MAX_DIST = WIN_MAX
NEG_INF = -1e30
W_C = 384
H_C = 6
R_W = 32
R_A = 32
R_G = 64
C_SHIFT = 3 * W_C + R_W + R_A + R_G
N_IN = W_A + 3 * W_B + C_SHIFT
GN_EPS = 64e-5
N_EXPERTS = 32
TOP_K = 4
D_FF = D_MODEL
SWIGLU_ALPHA = 1.702
SWIGLU_LIMIT = 7.0
ALPHA = (2 * DEPTH) ** 0.25
LN_EPS = 1e-5

LANES = 128
Q_TILE = 128
RWKV_CHUNK = 64
S5_CHUNK = 64
ROW_TILE = 512
MOE_ROWS = 256
COMBINE_TOK = 128
VMEM_LIMIT = 56 * 1024 * 1024


def _cp(sem, vmem=VMEM_LIMIT):
    return pltpu.CompilerParams(dimension_semantics=sem, vmem_limit_bytes=vmem)


def _dot(a, b, precision=None):
    return jnp.dot(a, b, preferred_element_type=F32, precision=precision)


def _dot_nt(a, b, precision=None):
    return lax.dot_general(a, b, (((1,), (1,)), ((), ())),
                           preferred_element_type=F32, precision=precision)


def _dot_tn(a, b, precision=None):
    return lax.dot_general(a, b, (((0,), (0,)), ((), ())),
                           preferred_element_type=F32, precision=precision)


def _layernorm_rows(x, g, b):
    mu = jnp.mean(x, axis=-1, keepdims=True)
    d = x - mu
    var = jnp.mean(d * d, axis=-1, keepdims=True)
    return d * lax.rsqrt(var + LN_EPS) * g + b


def _sigmoid(x):
    return 1.0 / (1.0 + jnp.exp(-x))


def _softplus(x):
    return jnp.maximum(x, 0.0) + jnp.log(1.0 + jnp.exp(-jnp.abs(x)))


def _gelu(x):
    c = math.sqrt(2.0 / math.pi)
    return 0.5 * x * (1.0 + jnp.tanh(c * (x + 0.044715 * (x * x * x))))


_PROJ_SPLITS = (0, W_A, W_A + W_B, W_A + 2 * W_B, W_A + 3 * W_B, N_IN)


def _proj_kernel(x_ref, w_ref, u_ref, q_ref, k_ref, v_ref, pc_ref):
    xb = x_ref[...].astype(BF16)
    outs = (u_ref, q_ref, k_ref, v_ref, pc_ref)
    for o_ref, lo, hi in zip(outs, _PROJ_SPLITS[:-1], _PROJ_SPLITS[1:]):
        o_ref[...] = _dot(xb, w_ref[:, lo:hi])


def _proj(x, w_bf16):
    n = x.shape[0]
    widths = [hi - lo for lo, hi in zip(_PROJ_SPLITS[:-1], _PROJ_SPLITS[1:])]
    return pl.pallas_call(
        _proj_kernel,
        out_shape=[jax.ShapeDtypeStruct((n, w), F32) for w in widths],
        grid=(n // ROW_TILE,),
        in_specs=[pl.BlockSpec((ROW_TILE, D_MODEL), lambda i: (i, 0)),
                  pl.BlockSpec((D_MODEL, N_IN), lambda i: (0, 0))],
        out_specs=[pl.BlockSpec((ROW_TILE, w), lambda i: (i, 0)) for w in widths],
        compiler_params=_cp(("parallel",)),
        name="proj",
    )(x, w_bf16)


def _s5_kernel(u_ref, lam_ref, bblk_ref, cblk_ref, d_ref, gw_ref, gb_ref,
               o_ref, ht_ref, utm, hs, ytm, h_scr, *, nb, lt):
    c = pl.program_id(0)

    @pl.when(c == 0)
    def _():
        h_scr[...] = jnp.zeros_like(h_scr)

    n_half = W_A // LANES
    for b in range(nb):
        for j in range(n_half):
            utm[j, pl.ds(b, lt, stride=nb), :] = u_ref[b, :, j * LANES:(j + 1) * LANES]
    u_all = jnp.concatenate([utm[j] for j in range(n_half)], axis=1)
    hs[...] = _dot(u_all.astype(BF16), bblk_ref[...])
    lr = jnp.broadcast_to(lam_ref[0:1, :], (nb, N_ST))
    li = jnp.broadcast_to(lam_ref[1:2, :], (nb, N_ST))

    def body(t, carry):
        hr, hi = carry
        row = pl.multiple_of(t * nb, nb)
        br = hs[pl.ds(row, nb), 0:N_ST]
        bi = hs[pl.ds(row, nb), N_ST:2 * N_ST]
        nr = lr * hr - li * hi + br
        ni = lr * hi + li * hr + bi
        hs[pl.ds(row, nb), 0:N_ST] = nr
        hs[pl.ds(row, nb), N_ST:2 * N_ST] = ni
        return nr, ni

    hr, hi = lax.fori_loop(0, lt, body, (h_scr[0], h_scr[1]), unroll=2)
    h_scr[0] = hr
    h_scr[1] = hi
    ht_ref[0] = hr
    ht_ref[1] = hi
    y = _dot(hs[...].astype(BF16), cblk_ref[...]) + d_ref[...] * u_all
    z = _gelu(y)
    gl = _dot(z.astype(BF16), gw_ref[...]) + gb_ref[...]
    res = z * _sigmoid(gl)
    for j in range(n_half):
        ytm[j] = res[:, j * LANES:(j + 1) * LANES]
    for b in range(nb):
        for j in range(n_half):
            o_ref[b, :, j * LANES:(j + 1) * LANES] = ytm[j, pl.ds(b, lt, stride=nb), :]


def _s5_params(a_re, a_im, log_dt, b_re, b_im, c_re, c_im):
    lam = lax.complex(a_re, a_im)
    dt = jnp.exp(log_dt)[:, None]
    lam_bar = jnp.exp(lam * dt)
    b_bar = ((lam_bar - 1.0) / lam)[..., None] * lax.complex(b_re, b_im)
    eye = jnp.eye(G_A, dtype=F32)

    def blk_in(m):
        return jnp.einsum("gnc,gh->gchn", m, eye).reshape(W_A, N_ST)

    def blk_out(m):
        return jnp.einsum("gcn,gh->gnhc", m, eye).reshape(N_ST, W_A)

    bblk = jnp.concatenate([blk_in(b_bar.real), blk_in(b_bar.imag)], axis=1)
    cblk = jnp.concatenate([blk_out(c_re), blk_out(-c_im)], axis=0)
    lam2 = jnp.stack([lam_bar.real.reshape(N_ST), lam_bar.imag.reshape(N_ST)])
    bblk3 = jnp.concatenate([blk_in(b_bar.real), blk_in(b_bar.imag - b_bar.real)], axis=1)
    cblk3 = jnp.concatenate([blk_out(c_re), blk_out(c_re + c_im)], axis=0)
    return (lam2, bblk, cblk), (lam2, bblk3.astype(BF16), cblk3.astype(BF16))


def _s5_prompt(u, lam2, bblk, cblk, d, gw, gb):
    nb, t, _ = u.shape
    lt = min(S5_CHUNK, t)
    kern = functools.partial(_s5_kernel, nb=nb, lt=lt)
    const = lambda shape: pl.BlockSpec(shape, lambda c: (0,) * len(shape))
    return pl.pallas_call(
        kern,
        out_shape=[jax.ShapeDtypeStruct((nb, t, W_A), F32),
                   jax.ShapeDtypeStruct((2, nb, N_ST), F32)],
        grid=(t // lt,),
        in_specs=[pl.BlockSpec((nb, lt, W_A), lambda c: (0, c, 0)),
                  const((2, N_ST)), const((W_A, 2 * N_ST)), const((2 * N_ST, W_A)),
                  const((1, W_A)), const((W_A, W_A)), const((1, W_A))],
        out_specs=[pl.BlockSpec((nb, lt, W_A), lambda c: (0, c, 0)),
                   const((2, nb, N_ST))],
        scratch_shapes=[pltpu.VMEM((W_A // LANES, nb * lt, LANES), F32),
                        pltpu.VMEM((nb * lt, 2 * N_ST), F32),
                        pltpu.VMEM((W_A // LANES, nb * lt, LANES), F32),
                        pltpu.VMEM((2, nb, N_ST), F32)],
        compiler_params=_cp(("arbitrary",)),
        name="s5_prompt",
    )(u, lam2, bblk.astype(BF16), cblk.astype(BF16), d.reshape(1, W_A),
      gw.astype(BF16), gb.reshape(1, W_A))


def _branch_offsets():
    return np.stack([np.arange(w // d + 1) * d for (w, d) in BRANCHES]).astype(np.int32)


def _t5_buckets(dist):
    max_exact = N_BUCKETS // 2
    d = np.maximum(dist, 1).astype(np.float32)
    large = max_exact + (np.log(d / max_exact) / np.log(MAX_DIST / max_exact)
                         * (N_BUCKETS - max_exact)).astype(np.int32)
    return np.where(dist < max_exact, dist, np.minimum(large, N_BUCKETS - 1)).astype(np.int32)


def _log_bias_table(rel_bias):
    offs = _branch_offsets()
    bias = rel_bias[jnp.asarray(_t5_buckets(offs))]
    table = jnp.full((WIN_MAX + 1, H_B), -jnp.inf, F32)
    for br in range(len(BRANCHES)):
        idx = jnp.asarray(offs[br])
        table = table.at[idx].set(jnp.logaddexp(table[idx], bias[br]))
    return jnp.maximum(table, NEG_INF)


def _bias_tiles(table, n_diff):
    pad = jnp.full((Q_TILE - 1, H_B), NEG_INF, F32)
    ext = jnp.concatenate([pad, table[:WIN_MAX + 1], pad], axis=0)
    last = ext.shape[0] - 1
    rev = ext[::-1].T
    rows = n_diff * Q_TILE
    assert rows <= last - (Q_TILE - 1) + 1
    starts = jnp.asarray((last - (Q_TILE - 1) - np.arange(rows)).astype(np.int32))
    window = lambda s: lax.dynamic_slice_in_dim(rev, s, Q_TILE, axis=1)
    full = jax.vmap(window)(starts)
    tiles = full.reshape(n_diff, Q_TILE, H_B // 2, 2, Q_TILE)
    return jnp.transpose(tiles, (2, 0, 3, 1, 4)).reshape(H_B // 2, n_diff, 2 * Q_TILE, Q_TILE)


def _attn_kernel(q_ref, k_ref, v_ref, bias_ref, o_ref, kb, vb, *, nq):
    kb[...] = k_ref[0].astype(BF16)
    vb[...] = v_ref[0].astype(BF16)
    lo = lax.broadcasted_iota(jnp.int32, (Q_TILE, LANES), 1) < HEAD_DIM
    for qi in range(nq):
        rows = slice(qi * Q_TILE, (qi + 1) * Q_TILE)
        q = q_ref[0, rows, :] * (HEAD_DIM ** -0.5)
        q2 = jnp.concatenate([jnp.where(lo, q, 0.0), jnp.where(lo, 0.0, q)], axis=0).astype(BF16)
        nk = (qi + 1) * Q_TILE
        bias = jnp.concatenate([bias_ref[qi - kj] for kj in range(qi + 1)], axis=1)
        s = _dot_nt(q2, kb[0:nk, :]) + bias
        m = jnp.max(s, axis=-1, keepdims=True)
        p = jnp.exp(s - m)
        l = jnp.sum(p, axis=-1, keepdims=True)
        o = _dot(p.astype(BF16), vb[0:nk, :]) / l
        o_ref[0, rows, :] = jnp.where(lo, o[0:Q_TILE], o[Q_TILE:2 * Q_TILE])


def _attn_prompt(q, k, v, tiles):
    nb, t, _ = q.shape
    n_diff = tiles.shape[1]
    hp = W_B // LANES
    kern = functools.partial(_attn_kernel, nq=t // Q_TILE)
    seq = pl.BlockSpec((1, t, LANES), lambda b, p: (b, 0, p))
    return pl.pallas_call(
        kern,
        out_shape=jax.ShapeDtypeStruct((nb, t, W_B), F32),
        grid=(nb, hp),
        in_specs=[seq, seq, seq,
                  pl.BlockSpec((None, n_diff, 2 * Q_TILE, Q_TILE), lambda b, p: (p, 0, 0, 0))],
        out_specs=seq,
        scratch_shapes=[pltpu.VMEM((t, LANES), BF16), pltpu.VMEM((t, LANES), BF16)],
        compiler_params=_cp(("parallel", "parallel")),
        name="attn_prompt",
    )(q, k, v, tiles)


def _split_dot(x, m_bf16, parts):
    acc = None
    rem = x
    for i in range(parts):
        piece = rem.astype(BF16)
        term = _dot(piece, m_bf16)
        acc = term if acc is None else acc + term
        if i + 1 < parts:
            rem = rem - piece.astype(F32)
    return acc


def _split_dot_left(m_bf16, x, parts):
    acc = None
    rem = x
    for i in range(parts):
        piece = rem.astype(BF16)
        term = _dot(m_bf16, piece)
        acc = term if acc is None else acc + term
        if i + 1 < parts:
            rem = rem - piece.astype(F32)
    return acc


def _rwkv_pre(pc, prev, prm, seg_sum):
    (mu, w0, w2p, a0, a2p, g2p, k_k, k_a, r_k, ln_g, ln_b, mseg) = prm
    xs = pc + (prev - pc) * mu
    r = xs[:, 0:W_C]
    k = xs[:, W_C:2 * W_C]
    v = xs[:, 2 * W_C:3 * W_C]
    tail = xs[:, 3 * W_C:C_SHIFT]
    w_raw = -_softplus(-(w0 + _dot(jnp.tanh(tail).astype(BF16), w2p))) - 0.5
    a = _sigmoid(a0 + _dot(tail.astype(BF16), a2p))
    g = _dot(_sigmoid(tail).astype(BF16), g2p)
    kk = k * k_k
    nrm = jnp.sqrt(seg_sum(kk * kk))
    kk = kk / jnp.maximum(nrm, 1e-12)
    k2 = k * (1.0 + (a - 1.0) * k_a)
    logw = -jnp.exp(w_raw)
    return r, k2, v, kk, a, g, logw


def _rwkv_post(y, r, k2, v, g, prm, seg_sum):
    (mu, w0, w2p, a0, a2p, g2p, k_k, k_a, r_k, ln_g, ln_b, mseg) = prm
    mean = seg_sum(y) * (1.0 / HEAD_DIM)
    d = y - mean
    var = seg_sum(d * d) * (1.0 / HEAD_DIM)
    yn = d * lax.rsqrt(var + GN_EPS) * ln_g + ln_b
    bonus = seg_sum(r * k2 * r_k) * v
    return (yn + bonus) * g


def _rwkv_kernel(pc_ref, mu_ref, w0_ref, w2_ref, a0_ref, a2_ref, g2_ref, kk_ref, ka_ref,
                 rk_ref, lng_ref, lnb_ref, mseg_ref, ltri_ref,
                 o_ref, s_ref, s_scr, prev_scr, yt_scr, *, ch):
    c = pl.program_id(1)

    @pl.when(c == 0)
    def _():
        s_scr[...] = jnp.zeros_like(s_scr)
        prev_scr[...] = jnp.zeros_like(prev_scr)
        yt_scr[...] = jnp.zeros_like(yt_scr)

    prm = (mu_ref[...], w0_ref[...], w2_ref[...], a0_ref[...], a2_ref[...], g2_ref[...],
           kk_ref[...], ka_ref[...], rk_ref[...], lng_ref[...], lnb_ref[...], None)
    mseg = mseg_ref[...]
    seg_sum = lambda t: _split_dot(t, mseg, 3)
    pc = pc_ref[0]
    rowi = lax.broadcasted_iota(jnp.int32, (ch, C_SHIFT), 0)
    prev = jnp.where(rowi == 0, prev_scr[...], pltpu.roll(pc, 1, axis=0))
    prev_scr[...] = pc[ch - 1:ch, :]
    r, k2, v, kk, a, g, logw = _rwkv_pre(pc, prev, prm, seg_sum)

    cs = _split_dot_left(ltri_ref[...], logw, 3)
    g_in = jnp.exp(cs)
    g_ex = jnp.exp(cs - logw)
    g_inv = jnp.exp(-cs)
    al = (-kk * g_ex).astype(BF16)
    rt = (r * g_in).astype(BF16)
    bh = kk * a * g_inv
    kh = k2 * g_inv
    g_end = g_in[ch - 1:ch, :]
    bc = (bh * g_end).astype(BF16)
    kc = (kh * g_end).astype(BF16)
    bh = bh.astype(BF16)
    kh = kh.astype(BF16)
    vb = v.astype(BF16)

    si = lax.broadcasted_iota(jnp.int32, (ch, ch), 0)
    ti = lax.broadcasted_iota(jnp.int32, (ch, ch), 1)
    strict = si < ti
    incl = si <= ti
    n_sq = int(math.log2(ch))
    heads = range(H_C)
    sls = [slice(h * HEAD_DIM, (h + 1) * HEAD_DIM) for h in heads]
    s0 = [s_scr[h] for h in heads]
    s0b = [t.astype(BF16) for t in s0]
    bk = [jnp.concatenate([bh[:, sl], kh[:, sl]], axis=0) for sl in sls]
    ga = [_dot_nt(bk[h], al[:, sls[h]]) for h in heads]
    gr = [_dot_nt(bk[h], rt[:, sls[h]]) for h in heads]
    p0 = [_dot_nt(s0b[h], al[:, sls[h]]) for h in heads]
    rs = [_dot_nt(s0b[h], rt[:, sls[h]]) for h in heads]
    a_t = [jnp.where(strict, ga[h][0:ch], 0.0) for h in heads]
    b_t = [jnp.where(strict, ga[h][ch:2 * ch], 0.0).astype(BF16) for h in heads]
    rb_t = [jnp.where(incl, gr[h][0:ch], 0.0).astype(BF16) for h in heads]
    rk_t = [jnp.where(incl, gr[h][ch:2 * ch], 0.0).astype(BF16) for h in heads]
    bv = [_dot_tn(vb[:, sls[h]], b_t[h]) for h in heads]
    rv = [_dot_tn(vb[:, sls[h]], rk_t[h]) for h in heads]
    sv = [_dot_tn(vb[:, sls[h]], kc[:, sls[h]]) for h in heads]
    x_t = [p0[h] + bv[h] for h in heads]
    for it in range(n_sq):
        a_b = [a_t[h].astype(BF16) for h in heads]
        if it + 1 < n_sq:
            both = [_dot(jnp.concatenate([x_t[h], a_t[h]], axis=0).astype(BF16), a_b[h]) for h in heads]
            x_t = [x_t[h] + both[h][0:HEAD_DIM] for h in heads]
            a_t = [both[h][HEAD_DIM:HEAD_DIM + ch] for h in heads]
        else:
            step = [_dot(x_t[h].astype(BF16), a_b[h]) for h in heads]
            x_t = [x_t[h] + step[h] for h in heads]
    x_b = [t.astype(BF16) for t in x_t]
    yx = [_dot(x_b[h], rb_t[h]) for h in heads]
    sx = [_dot(x_b[h], bc[:, sls[h]]) for h in heads]
    for h in heads:
        s_new = s0[h] * g_end[:, sls[h]] + sx[h] + sv[h]
        yt_scr[h * HEAD_DIM:(h + 1) * HEAD_DIM, 0:ch] = rs[h] + yx[h] + rv[h]
        s_scr[h] = s_new
        s_ref[0, h] = s_new

    y = jnp.transpose(yt_scr[...])[0:ch, :]
    o_ref[0] = _rwkv_post(y, r, k2, v, g, prm, seg_sum)


def _rwkv_params(p):
    pad = lambda w, lo: jnp.zeros((R_W + R_A + R_G, W_C), BF16).at[lo:lo + w.shape[0]].set(w.astype(BF16))
    seg = np.arange(W_C) // HEAD_DIM
    mseg = jnp.asarray((seg[:, None] == seg[None, :]).astype(np.float32))
    row = lambda t: t.reshape(1, -1)
    return (row(p["rwkv_mu"]), row(p["rwkv_w0"]), pad(p["rwkv_w2"], 0), row(p["rwkv_a0"]),
            pad(p["rwkv_a2"], R_W), pad(p["rwkv_g2"], R_W + R_A), row(p["rwkv_k_k"]),
            row(p["rwkv_k_a"]), row(p["rwkv_r_k"]), row(p["rwkv_ln_g"]), row(p["rwkv_ln_b"]), mseg)


def _rwkv_prompt(pc, prm):
    nb, t, _ = pc.shape
    ch = min(RWKV_CHUNK, t)
    ltri = jnp.asarray(np.tril(np.ones((ch, ch), np.float32))).astype(BF16)
    kern = functools.partial(_rwkv_kernel, ch=ch)
    const = lambda a: pl.BlockSpec(a.shape, lambda b, c: (0,) * a.ndim)
    args = list(prm[:-1]) + [prm[-1].astype(BF16), ltri]
    return pl.pallas_call(
        kern,
        out_shape=[jax.ShapeDtypeStruct((nb, t, W_C), F32),
                   jax.ShapeDtypeStruct((nb, H_C, HEAD_DIM, HEAD_DIM), F32)],
        grid=(nb, t // ch),
        in_specs=[pl.BlockSpec((1, ch, C_SHIFT), lambda b, c: (b, c, 0))] + [const(a) for a in args],
        out_specs=[pl.BlockSpec((1, ch, W_C), lambda b, c: (b, c, 0)),
                   pl.BlockSpec((1, H_C, HEAD_DIM, HEAD_DIM), lambda b, c: (b, 0, 0, 0))],
        scratch_shapes=[pltpu.VMEM((H_C, HEAD_DIM, HEAD_DIM), F32),
                        pltpu.VMEM((1, C_SHIFT), F32),
                        pltpu.VMEM((W_C, LANES), F32)],
        compiler_params=_cp(("parallel", "arbitrary")),
        name="rwkv_prompt",
    )(pc, *args)


def _route(x1, rw, rb):
    logits = _dot(x1.astype(BF16), rw) + rb
    lane = lax.broadcasted_iota(jnp.int32, logits.shape, 1).astype(F32)
    vals, idxs = [], []
    cur = logits
    for _ in range(TOP_K):
        m = jnp.max(cur, axis=-1, keepdims=True)
        idx = jnp.min(jnp.where(cur == m, lane, float(N_EXPERTS)), axis=-1, keepdims=True)
        vals.append(m)
        idxs.append(idx)
        cur = jnp.where(lane == idx, -jnp.inf, cur)
    ex = [jnp.exp(v - vals[0]) for v in vals]
    tot = ex[0] + ex[1] + ex[2] + ex[3]
    gates = jnp.concatenate([e / tot for e in ex], axis=-1)
    return jnp.concatenate(idxs, axis=-1).astype(jnp.int32), gates


def _outproj_kernel(x_ref, oa_ref, ob_ref, oc_ref, w_ref, g_ref, b_ref, rw_ref, rb_ref,
                    x1_ref, idx_ref, gate_ref):
    mix = (_dot(oa_ref[...].astype(BF16), w_ref[0:W_A, :])
           + _dot(ob_ref[...].astype(BF16), w_ref[W_A:W_A + W_B, :])
           + _dot(oc_ref[...].astype(BF16), w_ref[W_A + W_B:, :]))
    x1 = _layernorm_rows(ALPHA * x_ref[...] + mix, g_ref[...], b_ref[...])
    x1_ref[...] = x1
    idx, gates = _route(x1, rw_ref[...], rb_ref[...])
    idx_ref[...] = idx
    gate_ref[...] = gates


def _outproj(x, oa, ob, oc, w, g, b, rw, rb):
    n = x.shape[0]
    tm = min(ROW_TILE, n)
    rows = lambda w_: pl.BlockSpec((tm, w_), lambda i: (i, 0))
    const = lambda a: pl.BlockSpec(a.shape, lambda i: (0,) * a.ndim)
    consts = [w.astype(BF16), g.reshape(1, -1), b.reshape(1, -1), rw.astype(BF16), rb.reshape(1, -1)]
    return pl.pallas_call(
        _outproj_kernel,
        out_shape=[jax.ShapeDtypeStruct((n, D_MODEL), F32),
                   jax.ShapeDtypeStruct((n, TOP_K), jnp.int32),
                   jax.ShapeDtypeStruct((n, TOP_K), F32)],
        grid=(n // tm,),
        in_specs=[rows(D_MODEL), rows(W_A), rows(W_B), rows(W_C)] + [const(a) for a in consts],
        out_specs=[rows(D_MODEL), rows(TOP_K), rows(TOP_K)],
        compiler_params=_cp(("parallel",)),
        name="outproj",
    )(x, oa, ob, oc, *consts)


def _route_meta(top_idx, n_blk):
    n_tok = top_idx.shape[0]
    n_asg = n_tok * TOP_K
    expert = top_idx.reshape(n_asg)
    onehot = (expert[:, None] == jnp.arange(N_EXPERTS, dtype=jnp.int32)[None, :]).astype(jnp.int32)
    csum = jnp.cumsum(onehot, axis=0)
    rank = jnp.sum(csum * onehot, axis=1) - 1
    counts = csum[-1]
    pcounts = (counts + MOE_ROWS - 1) // MOE_ROWS * MOE_ROWS
    pends = jnp.cumsum(pcounts)
    pstarts = pends - pcounts
    dest = pstarts[expert] + rank
    token = jnp.arange(n_asg, dtype=jnp.int32) // TOP_K
    row_token = jnp.zeros((n_blk * MOE_ROWS,), jnp.int32).at[dest].set(token)
    blk_expert = jnp.minimum(
        jnp.searchsorted(pends, jnp.arange(n_blk, dtype=jnp.int32) * MOE_ROWS, side="right"),
        N_EXPERTS - 1).astype(jnp.int32)
    n_active = (pends[-1] // MOE_ROWS).astype(jnp.int32).reshape(1)
    return dest.astype(jnp.int32), row_token, blk_expert, n_active


def _swiglu(h):
    h_glu = jnp.minimum(h[:, :D_FF], SWIGLU_LIMIT)
    h_lin = jnp.clip(h[:, D_FF:], -SWIGLU_LIMIT, SWIGLU_LIMIT)
    return h_glu * _sigmoid(SWIGLU_ALPHA * h_glu) * (h_lin + 1.0)


def _moe_kernel(be_ref, na_ref, tok_ref, tokn_ref, x_hbm, wup_ref, bup_ref, wdn_ref, bdn_ref,
                y_ref, xbuf, wup_b, wdn_b, sem):
    i = pl.program_id(0)
    n_act = na_ref[0]
    active = i < n_act
    slot = i % 2

    def gather_block(toks, s):
        def issue(r, carry):
            pltpu.make_async_copy(x_hbm.at[pl.ds(toks[0, 0, r], 1), :],
                                  xbuf.at[s, pl.ds(r, 1), :], sem.at[s]).start()
            return carry

        lax.fori_loop(0, MOE_ROWS, issue, 0, unroll=8)

    @pl.when(jnp.logical_and(i == 0, active))
    def _():
        gather_block(tok_ref, 0)

    @pl.when(i + 1 < n_act)
    def _():
        gather_block(tokn_ref, 1 - slot)

    @pl.when(active)
    def _():
        changed = jnp.logical_or(i == 0, be_ref[i] != be_ref[jnp.maximum(i - 1, 0)])

        @pl.when(changed)
        def _():
            wup_b[...] = wup_ref[...].astype(BF16)
            wdn_b[...] = wdn_ref[...].astype(BF16)

        pltpu.make_async_copy(x_hbm.at[pl.ds(0, MOE_ROWS), :], xbuf.at[slot], sem.at[slot]).wait()
        h = _dot(xbuf[slot].astype(BF16), wup_b[...]) + bup_ref[...]
        y_ref[...] = _dot(_swiglu(h).astype(BF16), wdn_b[...]) + bdn_ref[...]

    @pl.when(jnp.logical_not(active))
    def _():
        y_ref[...] = jnp.zeros_like(y_ref)


def _moe_rows(x1, row_token, blk_expert, n_active, w_up, b_up, w_dn, b_dn, layer):
    n_blk = blk_expert.shape[0]
    grid_spec = pltpu.PrefetchScalarGridSpec(
        num_scalar_prefetch=2,
        grid=(n_blk,),
        in_specs=[
            pl.BlockSpec((1, 1, MOE_ROWS), lambda i, be, na: (i, 0, 0), memory_space=pltpu.SMEM),
            pl.BlockSpec((1, 1, MOE_ROWS), lambda i, be, na: (jnp.minimum(i + 1, n_blk - 1), 0, 0),
                         memory_space=pltpu.SMEM),
            pl.BlockSpec(memory_space=pl.ANY),
            pl.BlockSpec((None, None, D_MODEL, 2 * D_FF), lambda i, be, na: (layer, be[i], 0, 0)),
            pl.BlockSpec((None, None, 1, 2 * D_FF), lambda i, be, na: (layer, be[i], 0, 0)),
            pl.BlockSpec((None, None, D_FF, D_MODEL), lambda i, be, na: (layer, be[i], 0, 0)),
            pl.BlockSpec((None, None, 1, D_MODEL), lambda i, be, na: (layer, be[i], 0, 0)),
        ],
        out_specs=pl.BlockSpec((MOE_ROWS, D_MODEL), lambda i, be, na: (i, 0)),
        scratch_shapes=[pltpu.VMEM((2, MOE_ROWS, D_MODEL), F32),
                        pltpu.VMEM((D_MODEL, 2 * D_FF), BF16),
                        pltpu.VMEM((D_FF, D_MODEL), BF16),
                        pltpu.SemaphoreType.DMA((2,))],
    )
    toks = row_token.reshape(n_blk, 1, MOE_ROWS)
    return pl.pallas_call(
        _moe_kernel,
        out_shape=jax.ShapeDtypeStruct((n_blk * MOE_ROWS, D_MODEL), F32),
        grid_spec=grid_spec,
        compiler_params=_cp(("arbitrary",)),
        name="moe_rows",
    )(blk_expert, n_active, toks, toks, x1,
      w_up, b_up.reshape(DEPTH, N_EXPERTS, 1, 2 * D_FF), w_dn, b_dn.reshape(DEPTH, N_EXPERTS, 1, D_MODEL))


def _combine_kernel(dest_ref, destn_ref, gate_ref, x1_ref, y_hbm, g_ref, b_ref, o_ref, ybuf, sem, *, tm):
    i = pl.program_id(0)
    slot = i % 2
    n_rows = TOP_K * tm

    def gather_tile(dests, s):
        def issue(r, carry):
            pltpu.make_async_copy(y_hbm.at[pl.ds(dests[0, 0, r], 1), :],
                                  ybuf.at[s, pl.ds(r, 1), :], sem.at[s]).start()
            return carry

        lax.fori_loop(0, n_rows, issue, 0, unroll=8)

    @pl.when(i == 0)
    def _():
        gather_tile(dest_ref, 0)

    @pl.when(i + 1 < pl.num_programs(0))
    def _():
        gather_tile(destn_ref, 1 - slot)

    pltpu.make_async_copy(y_hbm.at[pl.ds(0, n_rows), :], ybuf.at[slot], sem.at[slot]).wait()
    gates = gate_ref[...]
    moe = jnp.zeros((tm, D_MODEL), F32)
    for k in range(TOP_K):
        moe = moe + gates[:, k:k + 1] * ybuf[slot, k * tm:(k + 1) * tm, :]
    o_ref[...] = _layernorm_rows(ALPHA * x1_ref[...] + moe, g_ref[...], b_ref[...])


def _combine(dest, gates, x1, y_rows, g, b):
    n = x1.shape[0]
    tm = COMBINE_TOK
    nt = n // tm
    dest_t = dest.reshape(nt, tm, TOP_K).transpose(0, 2, 1).reshape(nt, 1, TOP_K * tm)
    kern = functools.partial(_combine_kernel, tm=tm)
    return pl.pallas_call(
        kern,
        out_shape=jax.ShapeDtypeStruct((n, D_MODEL), F32),
        grid=(nt,),
        in_specs=[pl.BlockSpec((1, 1, TOP_K * tm), lambda i: (i, 0, 0), memory_space=pltpu.SMEM),
                  pl.BlockSpec((1, 1, TOP_K * tm), lambda i: (jnp.minimum(i + 1, nt - 1), 0, 0),
                               memory_space=pltpu.SMEM),
                  pl.BlockSpec((tm, TOP_K), lambda i: (i, 0)),
                  pl.BlockSpec((tm, D_MODEL), lambda i: (i, 0)),
                  pl.BlockSpec(memory_space=pl.ANY),
                  pl.BlockSpec((1, D_MODEL), lambda i: (0, 0)),
                  pl.BlockSpec((1, D_MODEL), lambda i: (0, 0))],
        out_specs=pl.BlockSpec((tm, D_MODEL), lambda i: (i, 0)),
        scratch_shapes=[pltpu.VMEM((2, TOP_K * tm, D_MODEL), F32), pltpu.SemaphoreType.DMA((2,))],
        compiler_params=_cp(("arbitrary",)),
        name="moe_combine",
    )(dest_t, dest_t, gates, x1, y_rows, g.reshape(1, -1), b.reshape(1, -1))


def _moe_prompt(x1, top_idx, gates, p, layer):
    n = x1.shape[0]
    n_blk = -(-n * TOP_K // MOE_ROWS) + N_EXPERTS
    dest, row_token, blk_expert, n_active = _route_meta(top_idx, n_blk)
    y_rows = _moe_rows(x1, row_token, blk_expert, n_active, p["moe_w_up"], p["moe_b_up"],
                       p["moe_w_down"], p["moe_b_down"], layer)
    return _combine(dest, gates, x1, y_rows, p["ln2_g"][layer], p["ln2_b"][layer])


def _sample_proj_kernel(x_ref, w_ref, h0_ref, lam_ref, bblk_ref, cblk_ref, d_ref, gw_ref, gb_ref,
                        q_ref, k_ref, v_ref, pc_ref, oa_ref, h_ref):
    xb = x_ref[...].astype(BF16)
    mm = lambda lo, hi: _dot(xb, w_ref[:, lo:hi])
    u = mm(_PROJ_SPLITS[0], _PROJ_SPLITS[1])
    q_ref[...] = mm(_PROJ_SPLITS[1], _PROJ_SPLITS[2])
    k_ref[...] = mm(_PROJ_SPLITS[2], _PROJ_SPLITS[3])
    v_ref[...] = mm(_PROJ_SPLITS[3], _PROJ_SPLITS[4])
    pc_ref[...] = mm(_PROJ_SPLITS[4], _PROJ_SPLITS[5])
    bu = _dot(u.astype(BF16), bblk_ref[...])
    bu_re = bu[:, 0:N_ST]
    bu_im = bu_re + bu[:, N_ST:2 * N_ST]
    lr, li = lam_ref[0:1, :], lam_ref[1:2, :]
    hr0, hi0 = h0_ref[0], h0_ref[1]
    hr = bu_re + (lr * hr0 - li * hi0)
    hi = bu_im + (lr * hi0 + li * hr0)
    h_ref[0] = hr
    h_ref[1] = hi
    y = (_dot((hr + hi).astype(BF16), cblk_ref[0:N_ST, :])
         - _dot(hi.astype(BF16), cblk_ref[N_ST:2 * N_ST, :])) + d_ref[...] * u
    z = _gelu(y)
    oa_ref[...] = z * _sigmoid(_dot(z.astype(BF16), gw_ref[...]) + gb_ref[...])


def _sample_proj(x, w_in, h0, lam2, bblk, cblk, d, gw, gb):
    n = x.shape[0]
    widths = (W_B, W_B, W_B, C_SHIFT, W_A)
    return pl.pallas_call(
        _sample_proj_kernel,
        out_shape=[jax.ShapeDtypeStruct((n, w), F32) for w in widths]
        + [jax.ShapeDtypeStruct((2, n, N_ST), F32)],
        compiler_params=_cp(None),
        name="sample_proj",
    )(x, w_in, h0, lam2, bblk, cblk, d.reshape(1, -1), gw.astype(BF16), gb.reshape(1, -1))


def _sample_mix_kernel(q_ref, kn_ref, vn_ref, pc_ref, sh_ref, kc_ref, vc_ref, s0_ref, lb_ref, lb0_ref,
                       mu_ref, w0_ref, w2_ref, a0_ref, a2_ref, g2_ref, kk_ref, ka_ref,
                       rk_ref, lng_ref, lnb_ref, mseg_ref,
                       ob_ref, oc_ref, s_ref):
    rows = 8
    lane = lax.broadcasted_iota(jnp.int32, (rows, W_B), 1)
    sub = lax.broadcasted_iota(jnp.int32, (rows, W_B), 0)
    hmask = (lane // HEAD_DIM) == sub
    rnd = lambda t: t.astype(BF16).astype(F32)
    q = q_ref[0] * (HEAD_DIM ** -0.5)
    qrows = jnp.where(hmask, jnp.broadcast_to(q, (rows, W_B)), 0.0).astype(BF16)
    kcb = kc_ref[0].astype(BF16)
    vcb = vc_ref[0].astype(BF16)
    knb = rnd(kn_ref[0])
    vnb = rnd(vn_ref[0])
    s_all = _dot_nt(qrows, kcb)
    s_new = jnp.sum(qrows.astype(F32) * knb, axis=-1, keepdims=True)
    outs, lses = [], []
    for br in range(len(BRANCHES)):
        s = s_all + lb_ref[br]
        s0 = s_new + lb0_ref[br]
        m = jnp.maximum(jnp.max(s, axis=-1, keepdims=True), s0)
        lse = m + jnp.log(jnp.sum(jnp.exp(s - m), axis=-1, keepdims=True) + jnp.exp(s0 - m))
        o = _dot(jnp.exp(s - lse).astype(BF16), vcb) + rnd(jnp.exp(s0 - lse)) * vnb
        outs.append(rnd(o))
        lses.append(lse)
    top = jnp.maximum(jnp.maximum(lses[0], lses[1]), lses[2])
    ex = [jnp.exp(t - top) for t in lses]
    tot = ex[0] + ex[1] + ex[2]
    o = rnd(ex[0] / tot) * outs[0] + rnd(ex[1] / tot) * outs[1] + rnd(ex[2] / tot) * outs[2]
    ob_ref[0] = jnp.sum(jnp.where(hmask, o, 0.0), axis=0, keepdims=True)

    prm = (mu_ref[...], w0_ref[...], w2_ref[...], a0_ref[...], a2_ref[...], g2_ref[...],
           kk_ref[...], ka_ref[...], rk_ref[...], lng_ref[...], lnb_ref[...], mseg_ref[...])
    pc = jnp.broadcast_to(pc_ref[0], (rows, C_SHIFT))
    prev = jnp.broadcast_to(sh_ref[0], (rows, C_SHIFT))
    seg_sum = lambda t: _dot(t, mseg_ref[...], HI)
    r, k2, v, kk, a, g, logw = _rwkv_pre(pc, prev, prm, seg_sum)
    w = jnp.exp(logw)
    eye = (lax.broadcasted_iota(jnp.int32, (HEAD_DIM, HEAD_DIM), 0)
           == lax.broadcasted_iota(jnp.int32, (HEAD_DIM, HEAD_DIM), 1))
    ys = []
    for h in range(H_C):
        sl = slice(h * HEAD_DIM, (h + 1) * HEAD_DIM)
        row = lambda t: t[0:1, sl]
        col = lambda t: jnp.sum(jnp.where(eye, jnp.broadcast_to(row(t), (HEAD_DIM, HEAD_DIM)), 0.0),
                                axis=1, keepdims=True)
        s0 = s0_ref[0, h]
        sa = jnp.sum(rnd(s0) * rnd(-row(kk)), axis=1, keepdims=True)
        s1 = s0 * row(w) + sa * (row(kk) * row(a)) + col(v) * row(k2)
        s_ref[0, h] = s1
        y_col = jnp.sum(rnd(s1) * rnd(row(r)), axis=1, keepdims=True)
        ys.append(jnp.sum(jnp.where(eye, jnp.broadcast_to(y_col, (HEAD_DIM, HEAD_DIM)), 0.0),
                          axis=0, keepdims=True))
    y = jnp.broadcast_to(jnp.concatenate(ys, axis=1), (rows, W_C))
    oc_ref[0] = _rwkv_post(y, r, k2, v, g, prm, seg_sum)[0:1, :]


def _sample_mix(q, kn, vn, pc, shift0, k_cache, v_cache, s0, lb_rows, lb0, prm, layer):
    n = q.shape[0]
    l_buf = k_cache.shape[2]
    per_b = lambda w: pl.BlockSpec((1, 1, w), lambda b: (b, 0, 0))
    const = lambda a: pl.BlockSpec(a.shape, lambda b: (0,) * a.ndim)
    r3 = lambda t: t.reshape(n, 1, t.shape[-1])
    cache = pl.BlockSpec((None, 1, l_buf, W_B), lambda b: (layer, b, 0, 0))
    return pl.pallas_call(
        _sample_mix_kernel,
        out_shape=[jax.ShapeDtypeStruct((n, 1, W_B), F32),
                   jax.ShapeDtypeStruct((n, 1, W_C), F32),
                   jax.ShapeDtypeStruct((n, H_C, HEAD_DIM, HEAD_DIM), F32)],
        grid=(n,),
        in_specs=[per_b(W_B), per_b(W_B), per_b(W_B), per_b(C_SHIFT), per_b(C_SHIFT),
                  cache, cache,
                  pl.BlockSpec((1, H_C, HEAD_DIM, HEAD_DIM), lambda b: (b, 0, 0, 0)),
                  const(lb_rows), const(lb0)] + [const(a) for a in prm],
        out_specs=[per_b(W_B), per_b(W_C),
                   pl.BlockSpec((1, H_C, HEAD_DIM, HEAD_DIM), lambda b: (b, 0, 0, 0))],
        compiler_params=_cp(("parallel",)),
        name="sample_mix",
    )(r3(q), r3(kn), r3(vn), r3(pc), r3(shift0), k_cache, v_cache, s0, lb_rows, lb0, *prm)


def _sample_moe_kernel(x1_ref, gd_ref, wup_ref, bup_ref, wdn_ref, bdn_ref, g_ref, b_ref, o_ref, acc):
    e = pl.program_id(0)

    @pl.when(e == 0)
    def _():
        acc[...] = jnp.zeros_like(acc)

    x1 = x1_ref[...]
    h = _dot(x1.astype(BF16), wup_ref[...].astype(BF16)) + bup_ref[...]
    y = _dot(_swiglu(h).astype(BF16), wdn_ref[...].astype(BF16)) + bdn_ref[...]
    acc[...] += gd_ref[...] * y

    @pl.when(e == pl.num_programs(0) - 1)
    def _():
        o_ref[...] = _layernorm_rows(ALPHA * x1 + acc[...], g_ref[...], b_ref[...])


def _sample_moe(x1, top_idx, gates, p, layer):
    n = x1.shape[0]
    onehot = top_idx[:, :, None] == jnp.arange(N_EXPERTS, dtype=jnp.int32)[None, None, :]
    gd = jnp.sum(jnp.where(onehot, gates[:, :, None], 0.0), axis=1)
    gd = gd.T.reshape(N_EXPERTS, n, 1)
    return pl.pallas_call(
        _sample_moe_kernel,
        out_shape=jax.ShapeDtypeStruct((n, D_MODEL), F32),
        grid=(N_EXPERTS,),
        in_specs=[pl.BlockSpec((n, D_MODEL), lambda e: (0, 0)),
                  pl.BlockSpec((None, n, 1), lambda e: (e, 0, 0)),
                  pl.BlockSpec((None, None, D_MODEL, 2 * D_FF), lambda e: (layer, e, 0, 0)),
                  pl.BlockSpec((None, None, 1, 2 * D_FF), lambda e: (layer, e, 0, 0)),
                  pl.BlockSpec((None, None, D_FF, D_MODEL), lambda e: (layer, e, 0, 0)),
                  pl.BlockSpec((None, None, 1, D_MODEL), lambda e: (layer, e, 0, 0)),
                  pl.BlockSpec((1, D_MODEL), lambda e: (0, 0)),
                  pl.BlockSpec((1, D_MODEL), lambda e: (0, 0))],
        out_specs=pl.BlockSpec((n, D_MODEL), lambda e: (0, 0)),
        scratch_shapes=[pltpu.VMEM((n, D_MODEL), F32)],
        compiler_params=_cp(("arbitrary",)),
        name="sample_moe",
    )(x1, gd, p["moe_w_up"], p["moe_b_up"].reshape(DEPTH, N_EXPERTS, 1, 2 * D_FF),
      p["moe_w_down"], p["moe_b_down"].reshape(DEPTH, N_EXPERTS, 1, D_MODEL),
      p["ln2_g"][layer].reshape(1, -1), p["ln2_b"][layer].reshape(1, -1))


def _prompt_layer(x, nb, t, p, l, s5p, rwp, tiles):
    u, q, k, v, pc = _proj(x, p["w_in"][l].astype(BF16))
    shp = lambda a: a.reshape(nb, t, a.shape[-1])
    o_a, h_t = _s5_prompt(shp(u), *s5p, p["ssm_d"][l], p["ssm_glu_w"][l], p["ssm_glu_b"][l])
    o_b = _attn_prompt(shp(q), shp(k), shp(v), tiles)
    o_c, s_t = _rwkv_prompt(shp(pc), rwp)
    flat = lambda a: a.reshape(nb * t, a.shape[-1])
    x1, top_idx, gates = _outproj(x, flat(o_a), flat(o_b), flat(o_c), p["w_out"][l],
                                  p["ln1_g"][l], p["ln1_b"][l], p["moe_router_w"][l],
                                  p["moe_router_b"][l])
    x2 = _moe_prompt(x1, top_idx, gates, p, l)
    keep = min(WIN_MAX, t)
    ssm = jnp.transpose(h_t.reshape(2, nb, G_A, N_A), (1, 2, 3, 0))
    state = (shp(k)[:, t - keep:].reshape(nb, keep, H_B, HEAD_DIM),
             shp(v)[:, t - keep:].reshape(nb, keep, H_B, HEAD_DIM),
             ssm, s_t, shp(pc)[:, t - 1])
    return x2, state


def _sample_bias(rel_bias, l_buf):
    offs = _branch_offsets()
    bias = rel_bias[jnp.asarray(_t5_buckets(offs))]
    rows = []
    for br in range(len(BRANCHES)):
        m = np.arange(1, offs.shape[1])
        pos = l_buf - offs[br][m]
        ok = pos >= 0
        r = jnp.full((H_B, l_buf), NEG_INF, F32).at[:, jnp.asarray(pos[ok])].set(bias[br][jnp.asarray(m[ok])].T)
        rows.append(jnp.concatenate([r, jnp.zeros((8 - H_B, l_buf), F32)], axis=0))
    new = jnp.zeros((len(BRANCHES), 8, 1), F32).at[:, :H_B, 0].set(bias[:, 0])
    return jnp.stack(rows), new


def _sample_layer(x, p, l, s5p, rwp, sbias, k_cache, v_cache, st_ssm, st_rwkv, st_shift):
    n = x.shape[0]
    l_buf = k_cache.shape[2]
    lam2, bblk, cblk = s5p
    h0 = jnp.transpose(st_ssm.reshape(n, N_ST, 2), (2, 0, 1))
    q, kn, vn, pc, o_a, h1 = _sample_proj(x, p["w_in"][l].astype(BF16), h0, lam2, bblk, cblk, p["ssm_d"][l],
                                          p["ssm_glu_w"][l], p["ssm_glu_b"][l])
    lb_rows, lb0 = sbias
    o_b, o_c, s1 = _sample_mix(q, kn, vn, pc, st_shift, k_cache.reshape(DEPTH, n, l_buf, W_B),
                               v_cache.reshape(DEPTH, n, l_buf, W_B), st_rwkv, lb_rows, lb0, rwp, l)
    x1, top_idx, gates = _outproj(x, o_a, o_b.reshape(n, W_B), o_c.reshape(n, W_C), p["w_out"][l],
                                  p["ln1_g"][l], p["ln1_b"][l], p["moe_router_w"][l],
                                  p["moe_router_b"][l])
    x2 = _sample_moe(x1, top_idx, gates, p, l)
    ssm = jnp.transpose(h1.reshape(2, n, G_A, N_A), (1, 2, 3, 0))
    return x2, (kn, vn, ssm, s1, pc)


def kernel(x_prompt, x_sample, cache_attn_k, cache_attn_v, state_ssm, state_rwkv, state_shift, w_in, w_out, ln1_g, ln1_b, ln2_g, ln2_b, ssm_a_re, ssm_a_im, ssm_log_dt, ssm_b_re, ssm_b_im, ssm_c_re, ssm_c_im, ssm_d, ssm_glu_w, ssm_glu_b, rel_bias, rwkv_mu, rwkv_w0, rwkv_w2, rwkv_a0, rwkv_a2, rwkv_g2, rwkv_k_k, rwkv_k_a, rwkv_r_k, rwkv_ln_g, rwkv_ln_b, moe_router_w, moe_router_b, moe_w_up, moe_b_up, moe_w_down, moe_b_down):
    p = dict(w_in=w_in, w_out=w_out, ln1_g=ln1_g, ln1_b=ln1_b, ln2_g=ln2_g, ln2_b=ln2_b,
             ssm_d=ssm_d, ssm_glu_w=ssm_glu_w, ssm_glu_b=ssm_glu_b,
             moe_router_w=moe_router_w, moe_router_b=moe_router_b, moe_w_up=moe_w_up,
             moe_b_up=moe_b_up, moe_w_down=moe_w_down, moe_b_down=moe_b_down)
    nb, t, _ = x_prompt.shape
    ns = x_sample.shape[0]
    l_buf = cache_attn_k.shape[2]
    table = _log_bias_table(rel_bias)
    tiles = _bias_tiles(table, min(t, WIN_MAX) // Q_TILE + 1)
    sbias = _sample_bias(rel_bias, l_buf)
    xp = x_prompt.reshape(nb * t, D_MODEL)
    xs = x_sample.reshape(ns, D_MODEL)
    st_p, st_s = [], []
    for l in range(DEPTH):
        s5p, s5s = _s5_params(ssm_a_re[l], ssm_a_im[l], ssm_log_dt[l], ssm_b_re[l], ssm_b_im[l],
                              ssm_c_re[l], ssm_c_im[l])
        rwp = _rwkv_params(dict(rwkv_mu=rwkv_mu[l], rwkv_w0=rwkv_w0[l], rwkv_w2=rwkv_w2[l],
                                rwkv_a0=rwkv_a0[l], rwkv_a2=rwkv_a2[l], rwkv_g2=rwkv_g2[l],
                                rwkv_k_k=rwkv_k_k[l], rwkv_k_a=rwkv_k_a[l], rwkv_r_k=rwkv_r_k[l],
                                rwkv_ln_g=rwkv_ln_g[l], rwkv_ln_b=rwkv_ln_b[l]))
        xp, sp = _prompt_layer(xp, nb, t, p, l, s5p, rwp, tiles)
        xs, ss = _sample_layer(xs, p, l, s5s, rwp, sbias, cache_attn_k, cache_attn_v,
                               state_ssm[l], state_rwkv[l], state_shift[l])
        st_p.append(sp)
        st_s.append(ss)
    p_k, p_v, p_ssm, p_rwkv, p_shift = (jnp.stack(z) for z in zip(*st_p))
    kn, vn, s_ssm, s_rwkv, s_shift = (jnp.stack(z) for z in zip(*st_s))
    keep = min(WIN_MAX, l_buf + 1)
    drop = l_buf + 1 - keep
    s_k = jnp.concatenate([cache_attn_k[:, :, drop:], kn.reshape(DEPTH, ns, 1, H_B, HEAD_DIM)], axis=2)
    s_v = jnp.concatenate([cache_attn_v[:, :, drop:], vn.reshape(DEPTH, ns, 1, H_B, HEAD_DIM)], axis=2)
    return (xp.reshape(nb, t, D_MODEL), xs.reshape(ns, 1, D_MODEL),
            p_k, p_v, p_ssm, p_rwkv, p_shift, s_k, s_v, s_ssm, s_rwkv, s_shift)
```

```python
import functools
import math

import numpy as np
import jax
import jax.numpy as jnp
from jax import lax
from jax.experimental import pallas as pl
from jax.experimental.pallas import tpu as pltpu

F32 = jnp.float32
BF16 = jnp.bfloat16
HI = lax.Precision.HIGHEST

D_MODEL = 1024
DEPTH = 2
HEAD_DIM = 64
W_A = 256
C_GRP = 16
G_A = 16
N_A = 64
N_ST = G_A * N_A
W_B = 384
H_B = 6
BRANCHES = ((128, 1), (512, 4), (2048, 16))
WIN_MAX = 2048
N_BUCKETS = 32
MAX_DIST = WIN_MAX
NEG_INF = -1e30
W_C = 384
H_C = 6
R_W = 32
R_A = 32
R_G = 64
C_SHIFT = 3 * W_C + R_W + R_A + R_G
N_IN = W_A + 3 * W_B + C_SHIFT
GN_EPS = 64e-5
N_EXPERTS = 32
TOP_K = 4
D_FF = D_MODEL
SWIGLU_ALPHA = 1.702
SWIGLU_LIMIT = 7.0
ALPHA = (2 * DEPTH) ** 0.25
LN_EPS = 1e-5

LANES = 128
Q_TILE = 128
RWKV_CHUNK = 64
S5_CHUNK = 64
ROW_TILE = 512
MOE_ROWS = 256
COMBINE_TOK = 128
VMEM_LIMIT = 56 * 1024 * 1024


def _cp(sem, vmem=VMEM_LIMIT):
    return pltpu.CompilerParams(dimension_semantics=sem, vmem_limit_bytes=vmem)


def _dot(a, b, precision=None):
    return jnp.dot(a, b, preferred_element_type=F32, precision=precision)


def _dot_nt(a, b, precision=None):
    return lax.dot_general(a, b, (((1,), (1,)), ((), ())),
                           preferred_element_type=F32, precision=precision)


def _dot_tn(a, b, precision=None):
    return lax.dot_general(a, b, (((0,), (0,)), ((), ())),
                           preferred_element_type=F32, precision=precision)


def _layernorm_rows(x, g, b):
    mu = jnp.mean(x, axis=-1, keepdims=True)
    d = x - mu
    var = jnp.mean(d * d, axis=-1, keepdims=True)
    return d * lax.rsqrt(var + LN_EPS) * g + b


def _sigmoid(x):
    return 1.0 / (1.0 + jnp.exp(-x))


def _softplus(x):
    return jnp.maximum(x, 0.0) + jnp.log(1.0 + jnp.exp(-jnp.abs(x)))


def _gelu(x):
    c = math.sqrt(2.0 / math.pi)
    return 0.5 * x * (1.0 + jnp.tanh(c * (x + 0.044715 * (x * x * x))))


_PROJ_SPLITS = (0, W_A, W_A + W_B, W_A + 2 * W_B, W_A + 3 * W_B, N_IN)


def _proj_kernel(x_ref, w_ref, *refs):
    u_ref, q_ref, k_ref, v_ref, pc_ref = refs[-5:]
    xb = x_ref[...].astype(BF16)
    outs = ((u_ref, False), (q_ref, False), (k_ref, True), (v_ref, True), (pc_ref, False))
    for (o_ref, time_minor), lo, hi in zip(outs, _PROJ_SPLITS[:-1], _PROJ_SPLITS[1:]):
        y = _dot(xb, w_ref[:, lo:hi])
        o_ref[...] = jnp.transpose(y) if time_minor else y


def _proj(x, w_bf16, nb, layer, kv_prev):
    n = x.shape[0]
    t = n // nb
    per_seq = t // ROW_TILE
    widths = [hi - lo for lo, hi in zip(_PROJ_SPLITS[:-1], _PROJ_SPLITS[1:])]
    rows = lambda w: pl.BlockSpec((ROW_TILE, w), lambda i: (i, 0))
    stacked = pl.BlockSpec((None, None, W_B, ROW_TILE), lambda i: (layer, i // per_seq, 0, i % per_seq))
    flat = lambda w: jax.ShapeDtypeStruct((n, w), F32)
    kv = jax.ShapeDtypeStruct((DEPTH, nb, W_B, t), F32)
    prev = list(kv_prev)
    return pl.pallas_call(
        _proj_kernel,
        out_shape=[flat(widths[0]), flat(widths[1]), kv, kv, flat(widths[4])],
        grid=(n // ROW_TILE,),
        in_specs=[pl.BlockSpec((ROW_TILE, D_MODEL), lambda i: (i, 0)),
                  pl.BlockSpec((D_MODEL, N_IN), lambda i: (0, 0))]
        + [pl.BlockSpec(memory_space=pl.ANY)] * len(prev),
        out_specs=[rows(widths[0]), rows(widths[1]), stacked, stacked, rows(widths[4])],
        input_output_aliases={2: 2, 3: 3},
        compiler_params=_cp(("parallel",)),
        name="proj",
    )(x, w_bf16, *prev)


def _s5_kernel(u_ref, lam_ref, bblk_ref, cblk_ref, d_ref, gw_ref, gb_ref,
               o_ref, ht_ref, utm, hs, ytm, h_scr, *, nb, lt):
    c = pl.program_id(0)

    @pl.when(c == 0)
    def _():
        h_scr[...] = jnp.zeros_like(h_scr)

    n_half = W_A // LANES
    for b in range(nb):
        for j in range(n_half):
            utm[j, pl.ds(b, lt, stride=nb), :] = u_ref[b, :, j * LANES:(j + 1) * LANES]
    u_all = jnp.concatenate([utm[j] for j in range(n_half)], axis=1)
    hs[...] = _dot(u_all.astype(BF16), bblk_ref[...])
    lr = jnp.broadcast_to(lam_ref[0:1, :], (nb, N_ST))
    li = jnp.broadcast_to(lam_ref[1:2, :], (nb, N_ST))

    def body(t, carry):
        hr, hi = carry
        row = pl.multiple_of(t * nb, nb)
        br = hs[pl.ds(row, nb), 0:N_ST]
        bi = hs[pl.ds(row, nb), N_ST:2 * N_ST]
        nr = lr * hr - li * hi + br
        ni = lr * hi + li * hr + bi
        hs[pl.ds(row, nb), 0:N_ST] = nr
        hs[pl.ds(row, nb), N_ST:2 * N_ST] = ni
        return nr, ni

    hr, hi = lax.fori_loop(0, lt, body, (h_scr[0], h_scr[1]), unroll=2)
    h_scr[0] = hr
    h_scr[1] = hi
    ht_ref[0] = hr
    ht_ref[1] = hi
    y = _dot(hs[...].astype(BF16), cblk_ref[...]) + d_ref[...] * u_all
    z = _gelu(y)
    gl = _dot(z.astype(BF16), gw_ref[...]) + gb_ref[...]
    res = z * _sigmoid(gl)
    for j in range(n_half):
        ytm[j] = res[:, j * LANES:(j + 1) * LANES]
    for b in range(nb):
        for j in range(n_half):
            o_ref[b, :, j * LANES:(j + 1) * LANES] = ytm[j, pl.ds(b, lt, stride=nb), :]


def _s5_params(a_re, a_im, log_dt, b_re, b_im, c_re, c_im):
    lam = lax.complex(a_re, a_im)
    dt = jnp.exp(log_dt)[:, None]
    lam_bar = jnp.exp(lam * dt)
    b_bar = ((lam_bar - 1.0) / lam)[..., None] * lax.complex(b_re, b_im)
    eye = jnp.eye(G_A, dtype=F32)

    def blk_in(m):
        return jnp.einsum("gnc,gh->gchn", m, eye).reshape(W_A, N_ST)

    def blk_out(m):
        return jnp.einsum("gcn,gh->gnhc", m, eye).reshape(N_ST, W_A)

    bblk = jnp.concatenate([blk_in(b_bar.real), blk_in(b_bar.imag)], axis=1)
    cblk = jnp.concatenate([blk_out(c_re), blk_out(-c_im)], axis=0)
    lam2 = jnp.stack([lam_bar.real.reshape(N_ST), lam_bar.imag.reshape(N_ST)])
    bblk3 = jnp.concatenate([blk_in(b_bar.real), blk_in(b_bar.imag - b_bar.real)], axis=1)
    cblk3 = jnp.concatenate([blk_out(c_re), blk_out(c_re + c_im)], axis=0)
    return (lam2, bblk, cblk), (lam2, bblk3.astype(BF16), cblk3.astype(BF16))


def _s5_prompt(u, lam2, bblk, cblk, d, gw, gb):
    nb, t, _ = u.shape
    lt = min(S5_CHUNK, t)
    kern = functools.partial(_s5_kernel, nb=nb, lt=lt)
    const = lambda shape: pl.BlockSpec(shape, lambda c: (0,) * len(shape))
    return pl.pallas_call(
        kern,
        out_shape=[jax.ShapeDtypeStruct((nb, t, W_A), F32),
                   jax.ShapeDtypeStruct((2, nb, N_ST), F32)],
        grid=(t // lt,),
        in_specs=[pl.BlockSpec((nb, lt, W_A), lambda c: (0, c, 0)),
                  const((2, N_ST)), const((W_A, 2 * N_ST)), const((2 * N_ST, W_A)),
                  const((1, W_A)), const((W_A, W_A)), const((1, W_A))],
        out_specs=[pl.BlockSpec((nb, lt, W_A), lambda c: (0, c, 0)),
                   const((2, nb, N_ST))],
        scratch_shapes=[pltpu.VMEM((W_A // LANES, nb * lt, LANES), F32),
                        pltpu.VMEM((nb * lt, 2 * N_ST), F32),
                        pltpu.VMEM((W_A // LANES, nb * lt, LANES), F32),
                        pltpu.VMEM((2, nb, N_ST), F32)],
        compiler_params=_cp(("arbitrary",)),
        name="s5_prompt",
    )(u, lam2, bblk.astype(BF16), cblk.astype(BF16), d.reshape(1, W_A),
      gw.astype(BF16), gb.reshape(1, W_A))


def _branch_offsets():
    return np.stack([np.arange(w // d + 1) * d for (w, d) in BRANCHES]).astype(np.int32)


def _t5_buckets(dist):
    max_exact = N_BUCKETS // 2
    d = np.maximum(dist, 1).astype(np.float32)
    large = max_exact + (np.log(d / max_exact) / np.log(MAX_DIST / max_exact)
                         * (N_BUCKETS - max_exact)).astype(np.int32)
    return np.where(dist < max_exact, dist, np.minimum(large, N_BUCKETS - 1)).astype(np.int32)


def _log_bias_table(rel_bias):
    offs = _branch_offsets()
    bias = rel_bias[jnp.asarray(_t5_buckets(offs))]
    table = jnp.full((WIN_MAX + 1, H_B), -jnp.inf, F32)
    for br in range(len(BRANCHES)):
        idx = jnp.asarray(offs[br])
        table = table.at[idx].set(jnp.logaddexp(table[idx], bias[br]))
    return jnp.maximum(table, NEG_INF)


def _bias_tiles(table, n_diff):
    per = 2 * Q_TILE
    m = np.arange(per)
    dist = np.arange(n_diff)[:, None] * Q_TILE + np.where(m < Q_TILE, -m, per - m)[None, :]
    ok = (dist >= 0) & (dist <= WIN_MAX) & (m != Q_TILE)[None, :]
    v = jnp.where(jnp.asarray(ok)[..., None], table[jnp.asarray(np.clip(dist, 0, WIN_MAX))], NEG_INF)
    v = jnp.transpose(v, (2, 0, 1))
    flat = jnp.broadcast_to(v[:, :, None, :], (H_B, n_diff, Q_TILE, per)).reshape(H_B, n_diff, Q_TILE * per)
    tiles = flat[:, :, :Q_TILE * (per - 1)].reshape(H_B, n_diff, Q_TILE, per - 1)[..., :Q_TILE]
    tiles = tiles.reshape(H_B // 2, 2, n_diff, Q_TILE, Q_TILE)
    return jnp.transpose(tiles, (0, 2, 1, 3, 4)).reshape(H_B // 2, n_diff, 2 * Q_TILE, Q_TILE)


def _attn_kernel(q_ref, k_ref, v_ref, bias_ref, o_ref, kb, vb, *, nq):
    kb[...] = k_ref[...].astype(BF16)
    vb[...] = v_ref[...].astype(BF16)
    lo = lax.broadcasted_iota(jnp.int32, (Q_TILE, LANES), 1) < HEAD_DIM
    for qi in range(nq):
        rows = slice(qi * Q_TILE, (qi + 1) * Q_TILE)
        q = q_ref[0, rows, :] * (HEAD_DIM ** -0.5)
        q2 = jnp.concatenate([jnp.where(lo, q, 0.0), jnp.where(lo, 0.0, q)], axis=0).astype(BF16)
        nk = (qi + 1) * Q_TILE
        bias = jnp.concatenate([bias_ref[qi - kj] for kj in range(qi + 1)], axis=1)
        s = _dot(q2, kb[:, 0:nk]) + bias
        m = jnp.max(s, axis=-1, keepdims=True)
        p = jnp.exp(s - m)
        l = jnp.sum(p, axis=-1, keepdims=True)
        o = _dot_nt(p.astype(BF16), vb[:, 0:nk]) / l
        o_ref[0, rows, :] = jnp.where(lo, o[0:Q_TILE], o[Q_TILE:2 * Q_TILE])


def _attn_prompt(q, k_all, v_all, tiles, layer):
    nb, t, _ = q.shape
    n_diff = tiles.shape[1]
    hp = W_B // LANES
    kern = functools.partial(_attn_kernel, nq=t // Q_TILE)
    seq = pl.BlockSpec((1, t, LANES), lambda b, p: (b, 0, p))
    kv = pl.BlockSpec((None, None, LANES, t), lambda b, p: (layer, b, p, 0))
    return pl.pallas_call(
        kern,
        out_shape=jax.ShapeDtypeStruct((nb, t, W_B), F32),
        grid=(nb, hp),
        in_specs=[seq, kv, kv,
                  pl.BlockSpec((None, n_diff, 2 * Q_TILE, Q_TILE), lambda b, p: (p, 0, 0, 0))],
        out_specs=seq,
        scratch_shapes=[pltpu.VMEM((LANES, t), BF16), pltpu.VMEM((LANES, t), BF16)],
        compiler_params=_cp(("parallel", "parallel")),
        name="attn_prompt",
    )(q, k_all, v_all, tiles)


def _split_dot(x, m_bf16, parts):
    acc = None
    rem = x
    for i in range(parts):
        piece = rem.astype(BF16)
        term = _dot(piece, m_bf16)
        acc = term if acc is None else acc + term
        if i + 1 < parts:
            rem = rem - piece.astype(F32)
    return acc


def _split_dot_left(m_bf16, x, parts):
    acc = None
    rem = x
    for i in range(parts):
        piece = rem.astype(BF16)
        term = _dot(m_bf16, piece)
        acc = term if acc is None else acc + term
        if i + 1 < parts:
            rem = rem - piece.astype(F32)
    return acc


def _rwkv_pre(pc, prev, prm, seg_sum):
    (mu, w0, w2p, a0, a2p, g2p, k_k, k_a, r_k, ln_g, ln_b, mseg) = prm
    xs = pc + (prev - pc) * mu
    r = xs[:, 0:W_C]
    k = xs[:, W_C:2 * W_C]
    v = xs[:, 2 * W_C:3 * W_C]
    tail = xs[:, 3 * W_C:C_SHIFT]
    w_raw = -_softplus(-(w0 + _dot(jnp.tanh(tail).astype(BF16), w2p))) - 0.5
    a = _sigmoid(a0 + _dot(tail.astype(BF16), a2p))
    g = _dot(_sigmoid(tail).astype(BF16), g2p)
    kk = k * k_k
    nrm = jnp.sqrt(seg_sum(kk * kk))
    kk = kk / jnp.maximum(nrm, 1e-12)
    k2 = k * (1.0 + (a - 1.0) * k_a)
    logw = -jnp.exp(w_raw)
    return r, k2, v, kk, a, g, logw


def _rwkv_post(y, r, k2, v, g, prm, seg_sum):
    (mu, w0, w2p, a0, a2p, g2p, k_k, k_a, r_k, ln_g, ln_b, mseg) = prm
    mean = seg_sum(y) * (1.0 / HEAD_DIM)
    d = y - mean
    var = seg_sum(d * d) * (1.0 / HEAD_DIM)
    yn = d * lax.rsqrt(var + GN_EPS) * ln_g + ln_b
    bonus = seg_sum(r * k2 * r_k) * v
    return (yn + bonus) * g


def _rwkv_kernel(pc_ref, mu_ref, w0_ref, w2_ref, a0_ref, a2_ref, g2_ref, kk_ref, ka_ref,
                 rk_ref, lng_ref, lnb_ref, mseg_ref, ltri_ref,
                 o_ref, s_ref, s_scr, prev_scr, yt_scr, *, ch):
    c = pl.program_id(1)

    @pl.when(c == 0)
    def _():
        s_scr[...] = jnp.zeros_like(s_scr)
        prev_scr[...] = jnp.zeros_like(prev_scr)
        yt_scr[...] = jnp.zeros_like(yt_scr)

    prm = (mu_ref[...], w0_ref[...], w2_ref[...], a0_ref[...], a2_ref[...], g2_ref[...],
           kk_ref[...], ka_ref[...], rk_ref[...], lng_ref[...], lnb_ref[...], None)
    mseg = mseg_ref[...]
    seg_sum = lambda t: _split_dot(t, mseg, 3)
    pc = pc_ref[0]
    rowi = lax.broadcasted_iota(jnp.int32, (ch, C_SHIFT), 0)
    prev = jnp.where(rowi == 0, prev_scr[...], pltpu.roll(pc, 1, axis=0))
    prev_scr[...] = pc[ch - 1:ch, :]
    r, k2, v, kk, a, g, logw = _rwkv_pre(pc, prev, prm, seg_sum)

    cs = _split_dot_left(ltri_ref[...], logw, 3)
    g_in = jnp.exp(cs)
    g_ex = jnp.exp(cs - logw)
    g_inv = jnp.exp(-cs)
    al = (-kk * g_ex).astype(BF16)
    rt = (r * g_in).astype(BF16)
    bh = kk * a * g_inv
    kh = k2 * g_inv
    g_end = g_in[ch - 1:ch, :]
    bc = (bh * g_end).astype(BF16)
    kc = (kh * g_end).astype(BF16)
    bh = bh.astype(BF16)
    kh = kh.astype(BF16)
    vb = v.astype(BF16)

    si = lax.broadcasted_iota(jnp.int32, (ch, ch), 0)
    ti = lax.broadcasted_iota(jnp.int32, (ch, ch), 1)
    strict = si < ti
    incl = si <= ti
    n_sq = int(math.log2(ch))
    heads = range(H_C)
    sls = [slice(h * HEAD_DIM, (h + 1) * HEAD_DIM) for h in heads]
    s0 = [s_scr[h] for h in heads]
    s0b = [t.astype(BF16) for t in s0]
    bk = [jnp.concatenate([bh[:, sl], kh[:, sl]], axis=0) for sl in sls]
    ga = [_dot_nt(bk[h], al[:, sls[h]]) for h in heads]
    gr = [_dot_nt(bk[h], rt[:, sls[h]]) for h in heads]
    p0 = [_dot_nt(s0b[h], al[:, sls[h]]) for h in heads]
    rs = [_dot_nt(s0b[h], rt[:, sls[h]]) for h in heads]
    a_t = [jnp.where(strict, ga[h][0:ch], 0.0) for h in heads]
    b_t = [jnp.where(strict, ga[h][ch:2 * ch], 0.0).astype(BF16) for h in heads]
    rb_t = [jnp.where(incl, gr[h][0:ch], 0.0).astype(BF16) for h in heads]
    rk_t = [jnp.where(incl, gr[h][ch:2 * ch], 0.0).astype(BF16) for h in heads]
    bv = [_dot_tn(vb[:, sls[h]], b_t[h]) for h in heads]
    rv = [_dot_tn(vb[:, sls[h]], rk_t[h]) for h in heads]
    sv = [_dot_tn(vb[:, sls[h]], kc[:, sls[h]]) for h in heads]
    x_t = [p0[h] + bv[h] for h in heads]
    for it in range(n_sq):
        a_b = [a_t[h].astype(BF16) for h in heads]
        if it + 1 < n_sq:
            both = [_dot(jnp.concatenate([x_t[h], a_t[h]], axis=0).astype(BF16), a_b[h]) for h in heads]
            x_t = [x_t[h] + both[h][0:HEAD_DIM] for h in heads]
            a_t = [both[h][HEAD_DIM:HEAD_DIM + ch] for h in heads]
        else:
            step = [_dot(x_t[h].astype(BF16), a_b[h]) for h in heads]
            x_t = [x_t[h] + step[h] for h in heads]
    x_b = [t.astype(BF16) for t in x_t]
    yx = [_dot(x_b[h], rb_t[h]) for h in heads]
    sx = [_dot(x_b[h], bc[:, sls[h]]) for h in heads]
    for h in heads:
        s_new = s0[h] * g_end[:, sls[h]] + sx[h] + sv[h]
        yt_scr[h * HEAD_DIM:(h + 1) * HEAD_DIM, 0:ch] = rs[h] + yx[h] + rv[h]
        s_scr[h] = s_new
        s_ref[0, h] = s_new

    y = jnp.transpose(yt_scr[...])[0:ch, :]
    o_ref[0] = _rwkv_post(y, r, k2, v, g, prm, seg_sum)


def _rwkv_params(p):
    pad = lambda w, lo: jnp.zeros((R_W + R_A + R_G, W_C), BF16).at[lo:lo + w.shape[0]].set(w.astype(BF16))
    seg = np.arange(W_C) // HEAD_DIM
    mseg = jnp.asarray((seg[:, None] == seg[None, :]).astype(np.float32))
    row = lambda t: t.reshape(1, -1)
    return (row(p["rwkv_mu"]), row(p["rwkv_w0"]), pad(p["rwkv_w2"], 0), row(p["rwkv_a0"]),
            pad(p["rwkv_a2"], R_W), pad(p["rwkv_g2"], R_W + R_A), row(p["rwkv_k_k"]),
            row(p["rwkv_k_a"]), row(p["rwkv_r_k"]), row(p["rwkv_ln_g"]), row(p["rwkv_ln_b"]), mseg)


def _rwkv_prompt(pc, prm):
    nb, t, _ = pc.shape
    ch = min(RWKV_CHUNK, t)
    ltri = jnp.asarray(np.tril(np.ones((ch, ch), np.float32))).astype(BF16)
    kern = functools.partial(_rwkv_kernel, ch=ch)
    const = lambda a: pl.BlockSpec(a.shape, lambda b, c: (0,) * a.ndim)
    args = list(prm[:-1]) + [prm[-1].astype(BF16), ltri]
    return pl.pallas_call(
        kern,
        out_shape=[jax.ShapeDtypeStruct((nb, t, W_C), F32),
                   jax.ShapeDtypeStruct((nb, H_C, HEAD_DIM, HEAD_DIM), F32)],
        grid=(nb, t // ch),
        in_specs=[pl.BlockSpec((1, ch, C_SHIFT), lambda b, c: (b, c, 0))] + [const(a) for a in args],
        out_specs=[pl.BlockSpec((1, ch, W_C), lambda b, c: (b, c, 0)),
                   pl.BlockSpec((1, H_C, HEAD_DIM, HEAD_DIM), lambda b, c: (b, 0, 0, 0))],
        scratch_shapes=[pltpu.VMEM((H_C, HEAD_DIM, HEAD_DIM), F32),
                        pltpu.VMEM((1, C_SHIFT), F32),
                        pltpu.VMEM((W_C, LANES), F32)],
        compiler_params=_cp(("parallel", "arbitrary")),
        name="rwkv_prompt",
    )(pc, *args)


def _route(x1, rw, rb):
    logits = _dot(x1.astype(BF16), rw) + rb
    lane = lax.broadcasted_iota(jnp.int32, logits.shape, 1).astype(F32)
    vals, idxs = [], []
    cur = logits
    for _ in range(TOP_K):
        m = jnp.max(cur, axis=-1, keepdims=True)
        idx = jnp.min(jnp.where(cur == m, lane, float(N_EXPERTS)), axis=-1, keepdims=True)
        vals.append(m)
        idxs.append(idx)
        cur = jnp.where(lane == idx, -jnp.inf, cur)
    ex = [jnp.exp(v - vals[0]) for v in vals]
    tot = ex[0] + ex[1] + ex[2] + ex[3]
    gates = jnp.concatenate([e / tot for e in ex], axis=-1)
    return jnp.concatenate(idxs, axis=-1).astype(jnp.int32), gates


def _outproj_kernel(x_ref, oa_ref, ob_ref, oc_ref, w_ref, g_ref, b_ref, rw_ref, rb_ref,
                    x1_ref, idx_ref, gate_ref):
    mix = (_dot(oa_ref[...].astype(BF16), w_ref[0:W_A, :])
           + _dot(ob_ref[...].astype(BF16), w_ref[W_A:W_A + W_B, :])
           + _dot(oc_ref[...].astype(BF16), w_ref[W_A + W_B:, :]))
    x1 = _layernorm_rows(ALPHA * x_ref[...] + mix, g_ref[...], b_ref[...])
    x1_ref[...] = x1
    idx, gates = _route(x1, rw_ref[...], rb_ref[...])
    idx_ref[...] = idx
    gate_ref[...] = gates


def _outproj(x, oa, ob, oc, w, g, b, rw, rb):
    n = x.shape[0]
    tm = min(ROW_TILE, n)
    rows = lambda w_: pl.BlockSpec((tm, w_), lambda i: (i, 0))
    const = lambda a: pl.BlockSpec(a.shape, lambda i: (0,) * a.ndim)
    consts = [w.astype(BF16), g.reshape(1, -1), b.reshape(1, -1), rw.astype(BF16), rb.reshape(1, -1)]
    return pl.pallas_call(
        _outproj_kernel,
        out_shape=[jax.ShapeDtypeStruct((n, D_MODEL), F32),
                   jax.ShapeDtypeStruct((n, TOP_K), jnp.int32),
                   jax.ShapeDtypeStruct((n, TOP_K), F32)],
        grid=(n // tm,),
        in_specs=[rows(D_MODEL), rows(W_A), rows(W_B), rows(W_C)] + [const(a) for a in consts],
        out_specs=[rows(D_MODEL), rows(TOP_K), rows(TOP_K)],
        compiler_params=_cp(("parallel",)),
        name="outproj",
    )(x, oa, ob, oc, *consts)


def _dest_kernel(idx_ref, ltri_ref, utri_ref, dest_ref, cnt_ref, cnt_scr, run_scr, start_scr):
    ph = pl.program_id(0)
    i = pl.program_id(1)
    idx = idx_ref[...]
    lane = lax.broadcasted_iota(jnp.int32, (idx.shape[0], N_EXPERTS), 1)
    hot = [idx[:, k:k + 1] == lane for k in range(TOP_K)]
    multi = jnp.zeros(lane.shape, F32)
    for k in range(TOP_K):
        multi = multi + jnp.where(hot[k], 1.0, 0.0)
    tile_cnt = jnp.sum(multi, axis=0, keepdims=True)

    @pl.when(jnp.logical_and(ph == 0, i == 0))
    def _():
        cnt_scr[...] = jnp.zeros_like(cnt_scr)

    @pl.when(ph == 0)
    def _():
        cnt_scr[...] += tile_cnt

    @pl.when(jnp.logical_and(ph == 1, i == 0))
    def _():
        cnt = cnt_scr[...]
        padded = jnp.floor((cnt + (MOE_ROWS - 1)) * (1.0 / MOE_ROWS)) * MOE_ROWS
        start_scr[...] = _dot(jnp.broadcast_to(padded, (8, N_EXPERTS)), utri_ref[...], HI)[0:1, :]
        run_scr[...] = jnp.zeros_like(run_scr)
        cnt_ref[...] = cnt

    @pl.when(ph == 1)
    def _():
        before = _dot(ltri_ref[...], multi.astype(BF16))
        base = start_scr[...] + run_scr[...] + before
        cols = [jnp.sum(jnp.where(hot[k], base, 0.0), axis=1, keepdims=True) for k in range(TOP_K)]
        dest_ref[...] = jnp.concatenate(cols, axis=1).astype(jnp.int32)
        run_scr[...] += tile_cnt


def _route_dest(top_idx):
    n = top_idx.shape[0]
    tile = min(ROW_TILE, n)
    nt = n // tile
    ltri = jnp.asarray(np.tril(np.ones((tile, tile), np.float32), -1)).astype(BF16)
    utri = jnp.asarray(np.triu(np.ones((N_EXPERTS, N_EXPERTS), np.float32), 1))
    return pl.pallas_call(
        _dest_kernel,
        out_shape=[jax.ShapeDtypeStruct((n, TOP_K), jnp.int32),
                   jax.ShapeDtypeStruct((1, N_EXPERTS), F32)],
        grid=(2, nt),
        in_specs=[pl.BlockSpec((tile, TOP_K), lambda ph, i: (i, 0)),
                  pl.BlockSpec((tile, tile), lambda ph, i: (0, 0)),
                  pl.BlockSpec((N_EXPERTS, N_EXPERTS), lambda ph, i: (0, 0))],
        out_specs=[pl.BlockSpec((tile, TOP_K), lambda ph, i: (i * ph, 0)),
                   pl.BlockSpec((1, N_EXPERTS), lambda ph, i: (0, 0))],
        scratch_shapes=[pltpu.VMEM((1, N_EXPERTS), F32)] * 3,
        compiler_params=_cp(("arbitrary", "arbitrary")),
        name="moe_dest",
    )(top_idx, ltri, utri)


def _block_experts(counts, n_blk):
    cnt = counts.reshape(N_EXPERTS).astype(jnp.int32)
    pends = jnp.cumsum((cnt + MOE_ROWS - 1) // MOE_ROWS * MOE_ROWS)
    blk_expert = jnp.minimum(
        jnp.searchsorted(pends, jnp.arange(n_blk, dtype=jnp.int32) * MOE_ROWS, side="right"),
        N_EXPERTS - 1).astype(jnp.int32)
    return blk_expert, (pends[-1] // MOE_ROWS).astype(jnp.int32).reshape(1)


def _dispatch_kernel(dest_ref, x1_ref, rows_in, rows_hbm, sem, *, tm):
    del rows_in
    for r in range(tm):
        for k in range(TOP_K):
            pltpu.make_async_copy(x1_ref.at[pl.ds(r, 1), :],
                                  rows_hbm.at[pl.ds(dest_ref[0, 0, r * TOP_K + k], 1), :], sem).start()
    for k in range(TOP_K):
        pltpu.make_async_copy(x1_ref, rows_hbm.at[pl.ds(0, tm), :], sem).wait()


def _dispatch(x1, dest, n_rows):
    n = x1.shape[0]
    tm = COMBINE_TOK
    nt = n // tm
    kern = functools.partial(_dispatch_kernel, tm=tm)
    return pl.pallas_call(
        kern,
        out_shape=jax.ShapeDtypeStruct((n_rows, D_MODEL), F32),
        grid=(nt,),
        in_specs=[pl.BlockSpec((1, 1, TOP_K * tm), lambda i: (i, 0, 0), memory_space=pltpu.SMEM),
                  pl.BlockSpec((tm, D_MODEL), lambda i: (i, 0)),
                  pl.BlockSpec(memory_space=pl.ANY)],
        out_specs=pl.BlockSpec(memory_space=pl.ANY),
        scratch_shapes=[pltpu.SemaphoreType.DMA],
        input_output_aliases={2: 0},
        compiler_params=_cp(("arbitrary",)),
        name="moe_dispatch",
    )(dest.reshape(nt, 1, TOP_K * tm), x1, jnp.zeros((n_rows, D_MODEL), F32))


def _swiglu(h):
    h_glu = jnp.minimum(h[:, :D_FF], SWIGLU_LIMIT)
    h_lin = jnp.clip(h[:, D_FF:], -SWIGLU_LIMIT, SWIGLU_LIMIT)
    return h_glu * _sigmoid(SWIGLU_ALPHA * h_glu) * (h_lin + 1.0)


def _moe_kernel(be_ref, na_ref, x_ref, wup_ref, bup_ref, wdn_ref, bdn_ref, y_ref, wup_b, wdn_b):
    i = pl.program_id(0)
    active = i < na_ref[0]

    @pl.when(active)
    def _():
        changed = jnp.logical_or(i == 0, be_ref[i] != be_ref[jnp.maximum(i - 1, 0)])

        @pl.when(changed)
        def _():
            wup_b[...] = wup_ref[...].astype(BF16)
            wdn_b[...] = wdn_ref[...].astype(BF16)

        h = _dot(x_ref[...].astype(BF16), wup_b[...]) + bup_ref[...]
        y_ref[...] = _dot(_swiglu(h).astype(BF16), wdn_b[...]) + bdn_ref[...]

    @pl.when(jnp.logical_not(active))
    def _():
        y_ref[...] = jnp.zeros_like(y_ref)


def _moe_rows(rows, blk_expert, n_active, w_up, b_up, w_dn, b_dn, layer):
    n_blk = blk_expert.shape[0]
    grid_spec = pltpu.PrefetchScalarGridSpec(
        num_scalar_prefetch=2,
        grid=(n_blk,),
        in_specs=[
            pl.BlockSpec((MOE_ROWS, D_MODEL), lambda i, be, na: (jnp.minimum(i, jnp.maximum(na[0] - 1, 0)), 0)),
            pl.BlockSpec((None, None, D_MODEL, 2 * D_FF), lambda i, be, na: (layer, be[i], 0, 0)),
            pl.BlockSpec((None, None, 1, 2 * D_FF), lambda i, be, na: (layer, be[i], 0, 0)),
            pl.BlockSpec((None, None, D_FF, D_MODEL), lambda i, be, na: (layer, be[i], 0, 0)),
            pl.BlockSpec((None, None, 1, D_MODEL), lambda i, be, na: (layer, be[i], 0, 0)),
        ],
        out_specs=pl.BlockSpec((MOE_ROWS, D_MODEL), lambda i, be, na: (i, 0)),
        scratch_shapes=[pltpu.VMEM((D_MODEL, 2 * D_FF), BF16),
                        pltpu.VMEM((D_FF, D_MODEL), BF16)],
    )
    return pl.pallas_call(
        _moe_kernel,
        out_shape=jax.ShapeDtypeStruct((n_blk * MOE_ROWS, D_MODEL), F32),
        grid_spec=grid_spec,
        compiler_params=_cp(("arbitrary",)),
        name="moe_rows",
    )(blk_expert, n_active, rows,
      w_up, b_up.reshape(DEPTH, N_EXPERTS, 1, 2 * D_FF), w_dn, b_dn.reshape(DEPTH, N_EXPERTS, 1, D_MODEL))


def _combine_kernel(dest_ref, destn_ref, gate_ref, x1_ref, y_hbm, g_ref, b_ref, o_ref, ybuf, sem, *, tm):
    i = pl.program_id(0)
    slot = i % 2
    n_rows = TOP_K * tm

    def gather_tile(dests, s):
        for r in range(n_rows):
            pltpu.make_async_copy(y_hbm.at[pl.ds(dests[0, 0, r], 1), :],
                                  ybuf.at[s, pl.ds(r, 1), :], sem.at[s]).start()

    @pl.when(i == 0)
    def _():
        gather_tile(dest_ref, 0)

    for s in range(2):
        @pl.when(jnp.logical_and(i + 1 < pl.num_programs(0), slot == 1 - s))
        def _(s=s):
            gather_tile(destn_ref, s)

    for s in range(2):
        @pl.when(slot == s)
        def _(s=s):
            pltpu.make_async_copy(y_hbm.at[pl.ds(0, n_rows), :], ybuf.at[s], sem.at[s]).wait()
            gates = gate_ref[...]
            moe = jnp.zeros((tm, D_MODEL), F32)
            for k in range(TOP_K):
                moe = moe + gates[:, k:k + 1] * ybuf[s, k * tm:(k + 1) * tm, :]
            o_ref[...] = _layernorm_rows(ALPHA * x1_ref[...] + moe, g_ref[...], b_ref[...])


def _combine(dest, gates, x1, y_rows, g, b):
    n = x1.shape[0]
    tm = COMBINE_TOK
    nt = n // tm
    dest_t = dest.reshape(nt, tm, TOP_K).transpose(0, 2, 1).reshape(nt, 1, TOP_K * tm)
    kern = functools.partial(_combine_kernel, tm=tm)
    return pl.pallas_call(
        kern,
        out_shape=jax.ShapeDtypeStruct((n, D_MODEL), F32),
        grid=(nt,),
        in_specs=[pl.BlockSpec((1, 1, TOP_K * tm), lambda i: (i, 0, 0), memory_space=pltpu.SMEM),
                  pl.BlockSpec((1, 1, TOP_K * tm), lambda i: (jnp.minimum(i + 1, nt - 1), 0, 0),
                               memory_space=pltpu.SMEM),
                  pl.BlockSpec((tm, TOP_K), lambda i: (i, 0)),
                  pl.BlockSpec((tm, D_MODEL), lambda i: (i, 0)),
                  pl.BlockSpec(memory_space=pl.ANY),
                  pl.BlockSpec((1, D_MODEL), lambda i: (0, 0)),
                  pl.BlockSpec((1, D_MODEL), lambda i: (0, 0))],
        out_specs=pl.BlockSpec((tm, D_MODEL), lambda i: (i, 0)),
        scratch_shapes=[pltpu.VMEM((2, TOP_K * tm, D_MODEL), F32), pltpu.SemaphoreType.DMA((2,))],
        compiler_params=_cp(("arbitrary",)),
        name="moe_combine",
    )(dest_t, dest_t, gates, x1, y_rows, g.reshape(1, -1), b.reshape(1, -1))


def _moe_prompt(x1, top_idx, gates, p, layer):
    n = x1.shape[0]
    n_blk = -(-n * TOP_K // MOE_ROWS) + N_EXPERTS
    dest, counts = _route_dest(top_idx)
    blk_expert, n_active = _block_experts(counts, n_blk)
    rows = _dispatch(x1, dest, n_blk * MOE_ROWS)
    y_rows = _moe_rows(rows, blk_expert, n_active, p["moe_w_up"], p["moe_b_up"],
                       p["moe_w_down"], p["moe_b_down"], layer)
    return _combine(dest, gates, x1, y_rows, p["ln2_g"][layer], p["ln2_b"][layer])


def _sample_proj_kernel(x_ref, w_ref, h0_ref, lam_ref, bblk_ref, cblk_ref, d_ref, gw_ref, gb_ref,
                        q_ref, k_ref, v_ref, pc_ref, oa_ref, h_ref):
    xb = x_ref[...].astype(BF16)
    mm = lambda lo, hi: _dot(xb, w_ref[:, lo:hi])
    u = mm(_PROJ_SPLITS[0], _PROJ_SPLITS[1])
    q_ref[...] = mm(_PROJ_SPLITS[1], _PROJ_SPLITS[2])
    k_ref[...] = mm(_PROJ_SPLITS[2], _PROJ_SPLITS[3])
    v_ref[...] = mm(_PROJ_SPLITS[3], _PROJ_SPLITS[4])
    pc_ref[...] = mm(_PROJ_SPLITS[4], _PROJ_SPLITS[5])
    bu = _dot(u.astype(BF16), bblk_ref[...])
    bu_re = bu[:, 0:N_ST]
    bu_im = bu_re + bu[:, N_ST:2 * N_ST]
    lr, li = lam_ref[0:1, :], lam_ref[1:2, :]
    hr0, hi0 = h0_ref[0], h0_ref[1]
    hr = bu_re + (lr * hr0 - li * hi0)
    hi = bu_im + (lr * hi0 + li * hr0)
    h_ref[0] = hr
    h_ref[1] = hi
    y = (_dot((hr + hi).astype(BF16), cblk_ref[0:N_ST, :])
         - _dot(hi.astype(BF16), cblk_ref[N_ST:2 * N_ST, :])) + d_ref[...] * u
    z = _gelu(y)
    oa_ref[...] = z * _sigmoid(_dot(z.astype(BF16), gw_ref[...]) + gb_ref[...])


def _sample_proj(x, w_in, h0, lam2, bblk, cblk, d, gw, gb):
    n = x.shape[0]
    widths = (W_B, W_B, W_B, C_SHIFT, W_A)
    return pl.pallas_call(
        _sample_proj_kernel,
        out_shape=[jax.ShapeDtypeStruct((n, w), F32) for w in widths]
        + [jax.ShapeDtypeStruct((2, n, N_ST), F32)],
        compiler_params=_cp(None),
        name="sample_proj",
    )(x, w_in, h0, lam2, bblk, cblk, d.reshape(1, -1), gw.astype(BF16), gb.reshape(1, -1))


def _sample_mix_kernel(q_ref, kn_ref, vn_ref, pc_ref, sh_ref, kc_ref, vc_ref, s0_ref, lb_ref, lb0_ref,
                       mu_ref, w0_ref, w2_ref, a0_ref, a2_ref, g2_ref, kk_ref, ka_ref,
                       rk_ref, lng_ref, lnb_ref, mseg_ref,
                       ob_ref, oc_ref, s_ref):
    rows = 8
    lane = lax.broadcasted_iota(jnp.int32, (rows, W_B), 1)
    sub = lax.broadcasted_iota(jnp.int32, (rows, W_B), 0)
    hmask = (lane // HEAD_DIM) == sub
    rnd = lambda t: t.astype(BF16).astype(F32)
    q = q_ref[0] * (HEAD_DIM ** -0.5)
    qrows = jnp.where(hmask, jnp.broadcast_to(q, (rows, W_B)), 0.0).astype(BF16)
    kcb = kc_ref[0].astype(BF16)
    vcb = vc_ref[0].astype(BF16)
    knb = rnd(kn_ref[0])
    vnb = rnd(vn_ref[0])
    s_all = _dot(qrows, kcb)
    s_new = jnp.sum(qrows.astype(F32) * knb, axis=-1, keepdims=True)
    outs, lses = [], []
    for br in range(len(BRANCHES)):
        s = s_all + lb_ref[br]
        s0 = s_new + lb0_ref[br]
        m = jnp.maximum(jnp.max(s, axis=-1, keepdims=True), s0)
        lse = m + jnp.log(jnp.sum(jnp.exp(s - m), axis=-1, keepdims=True) + jnp.exp(s0 - m))
        o = _dot_nt(jnp.exp(s - lse).astype(BF16), vcb) + rnd(jnp.exp(s0 - lse)) * vnb
        outs.append(rnd(o))
        lses.append(lse)
    top = jnp.maximum(jnp.maximum(lses[0], lses[1]), lses[2])
    ex = [jnp.exp(t - top) for t in lses]
    tot = ex[0] + ex[1] + ex[2]
    o = rnd(ex[0] / tot) * outs[0] + rnd(ex[1] / tot) * outs[1] + rnd(ex[2] / tot) * outs[2]
    ob_ref[0] = jnp.sum(jnp.where(hmask, o, 0.0), axis=0, keepdims=True)

    prm = (mu_ref[...], w0_ref[...], w2_ref[...], a0_ref[...], a2_ref[...], g2_ref[...],
           kk_ref[...], ka_ref[...], rk_ref[...], lng_ref[...], lnb_ref[...], mseg_ref[...])
    pc = jnp.broadcast_to(pc_ref[0], (rows, C_SHIFT))
    prev = jnp.broadcast_to(sh_ref[0], (rows, C_SHIFT))
    seg_sum = lambda t: _dot(t, mseg_ref[...], HI)
    r, k2, v, kk, a, g, logw = _rwkv_pre(pc, prev, prm, seg_sum)
    w = jnp.exp(logw)
    eye = (lax.broadcasted_iota(jnp.int32, (HEAD_DIM, HEAD_DIM), 0)
           == lax.broadcasted_iota(jnp.int32, (HEAD_DIM, HEAD_DIM), 1))
    ys = []
    for h in range(H_C):
        sl = slice(h * HEAD_DIM, (h + 1) * HEAD_DIM)
        row = lambda t: t[0:1, sl]
        col = lambda t: jnp.sum(jnp.where(eye, jnp.broadcast_to(row(t), (HEAD_DIM, HEAD_DIM)), 0.0),
                                axis=1, keepdims=True)
        s0 = s0_ref[0, h]
        sa = jnp.sum(rnd(s0) * rnd(-row(kk)), axis=1, keepdims=True)
        s1 = s0 * row(w) + sa * (row(kk) * row(a)) + col(v) * row(k2)
        s_ref[0, h] = s1
        y_col = jnp.sum(rnd(s1) * rnd(row(r)), axis=1, keepdims=True)
        ys.append(jnp.sum(jnp.where(eye, jnp.broadcast_to(y_col, (HEAD_DIM, HEAD_DIM)), 0.0),
                          axis=0, keepdims=True))
    y = jnp.broadcast_to(jnp.concatenate(ys, axis=1), (rows, W_C))
    oc_ref[0] = _rwkv_post(y, r, k2, v, g, prm, seg_sum)[0:1, :]


def _sample_mix(q, kn, vn, pc, shift0, k_cache, v_cache, s0, lb_rows, lb0, prm, layer):
    n = q.shape[0]
    l_buf = k_cache.shape[3]
    per_b = lambda w: pl.BlockSpec((1, 1, w), lambda b: (b, 0, 0))
    const = lambda a: pl.BlockSpec(a.shape, lambda b: (0,) * a.ndim)
    r3 = lambda t: t.reshape(n, 1, t.shape[-1])
    cache = pl.BlockSpec((None, 1, W_B, l_buf), lambda b: (layer, b, 0, 0))
    return pl.pallas_call(
        _sample_mix_kernel,
        out_shape=[jax.ShapeDtypeStruct((n, 1, W_B), F32),
                   jax.ShapeDtypeStruct((n, 1, W_C), F32),
                   jax.ShapeDtypeStruct((n, H_C, HEAD_DIM, HEAD_DIM), F32)],
        grid=(n,),
        in_specs=[per_b(W_B), per_b(W_B), per_b(W_B), per_b(C_SHIFT), per_b(C_SHIFT),
                  cache, cache,
                  pl.BlockSpec((1, H_C, HEAD_DIM, HEAD_DIM), lambda b: (b, 0, 0, 0)),
                  const(lb_rows), const(lb0)] + [const(a) for a in prm],
        out_specs=[per_b(W_B), per_b(W_C),
                   pl.BlockSpec((1, H_C, HEAD_DIM, HEAD_DIM), lambda b: (b, 0, 0, 0))],
        compiler_params=_cp(("parallel",)),
        name="sample_mix",
    )(r3(q), r3(kn), r3(vn), r3(pc), r3(shift0), k_cache, v_cache, s0, lb_rows, lb0, *prm)


def _sample_moe_kernel(x1_ref, gd_ref, wup_ref, bup_ref, wdn_ref, bdn_ref, g_ref, b_ref, o_ref, acc):
    e = pl.program_id(0)

    @pl.when(e == 0)
    def _():
        acc[...] = jnp.zeros_like(acc)

    x1 = x1_ref[...]
    h = _dot(x1.astype(BF16), wup_ref[...].astype(BF16)) + bup_ref[...]
    y = _dot(_swiglu(h).astype(BF16), wdn_ref[...].astype(BF16)) + bdn_ref[...]
    acc[...] += gd_ref[...] * y

    @pl.when(e == pl.num_programs(0) - 1)
    def _():
        o_ref[...] = _layernorm_rows(ALPHA * x1 + acc[...], g_ref[...], b_ref[...])


def _sample_moe(x1, top_idx, gates, p, layer):
    n = x1.shape[0]
    onehot = top_idx[:, :, None] == jnp.arange(N_EXPERTS, dtype=jnp.int32)[None, None, :]
    gd = jnp.sum(jnp.where(onehot, gates[:, :, None], 0.0), axis=1)
    gd = gd.T.reshape(N_EXPERTS, n, 1)
    return pl.pallas_call(
        _sample_moe_kernel,
        out_shape=jax.ShapeDtypeStruct((n, D_MODEL), F32),
        grid=(N_EXPERTS,),
        in_specs=[pl.BlockSpec((n, D_MODEL), lambda e: (0, 0)),
                  pl.BlockSpec((None, n, 1), lambda e: (e, 0, 0)),
                  pl.BlockSpec((None, None, D_MODEL, 2 * D_FF), lambda e: (layer, e, 0, 0)),
                  pl.BlockSpec((None, None, 1, 2 * D_FF), lambda e: (layer, e, 0, 0)),
                  pl.BlockSpec((None, None, D_FF, D_MODEL), lambda e: (layer, e, 0, 0)),
                  pl.BlockSpec((None, None, 1, D_MODEL), lambda e: (layer, e, 0, 0)),
                  pl.BlockSpec((1, D_MODEL), lambda e: (0, 0)),
                  pl.BlockSpec((1, D_MODEL), lambda e: (0, 0))],
        out_specs=pl.BlockSpec((n, D_MODEL), lambda e: (0, 0)),
        scratch_shapes=[pltpu.VMEM((n, D_MODEL), F32)],
        compiler_params=_cp(("arbitrary",)),
        name="sample_moe",
    )(x1, gd, p["moe_w_up"], p["moe_b_up"].reshape(DEPTH, N_EXPERTS, 1, 2 * D_FF),
      p["moe_w_down"], p["moe_b_down"].reshape(DEPTH, N_EXPERTS, 1, D_MODEL),
      p["ln2_g"][layer].reshape(1, -1), p["ln2_b"][layer].reshape(1, -1))


def _prompt_layer(x, nb, t, p, l, s5p, rwp, tiles, kv_prev):
    u, q, k_all, v_all, pc = _proj(x, p["w_in"][l].astype(BF16), nb, l, kv_prev)
    shp = lambda a: a.reshape(nb, t, a.shape[-1])
    o_a, h_t = _s5_prompt(shp(u), *s5p, p["ssm_d"][l], p["ssm_glu_w"][l], p["ssm_glu_b"][l])
    o_b = _attn_prompt(shp(q), k_all, v_all, tiles, l)
    o_c, s_t = _rwkv_prompt(shp(pc), rwp)
    flat = lambda a: a.reshape(nb * t, a.shape[-1])
    x1, top_idx, gates = _outproj(x, flat(o_a), flat(o_b), flat(o_c), p["w_out"][l],
                                  p["ln1_g"][l], p["ln1_b"][l], p["moe_router_w"][l],
                                  p["moe_router_b"][l])
    x2 = _moe_prompt(x1, top_idx, gates, p, l)
    ssm = jnp.transpose(h_t.reshape(2, nb, G_A, N_A), (1, 2, 3, 0))
    return x2, (ssm, s_t, shp(pc)[:, t - 1]), (k_all, v_all)


def _sample_bias(rel_bias, l_buf):
    offs = _branch_offsets()
    bias = rel_bias[jnp.asarray(_t5_buckets(offs))]
    rows = []
    for br in range(len(BRANCHES)):
        m = np.arange(1, offs.shape[1])
        pos = l_buf - offs[br][m]
        ok = pos >= 0
        r = jnp.full((H_B, l_buf), NEG_INF, F32).at[:, jnp.asarray(pos[ok])].set(bias[br][jnp.asarray(m[ok])].T)
        rows.append(jnp.concatenate([r, jnp.zeros((8 - H_B, l_buf), F32)], axis=0))
    new = jnp.zeros((len(BRANCHES), 8, 1), F32).at[:, :H_B, 0].set(bias[:, 0])
    return jnp.stack(rows), new


def _sample_layer(x, p, l, s5p, rwp, sbias, k_cache, v_cache, st_ssm, st_rwkv, st_shift):
    n = x.shape[0]
    lam2, bblk, cblk = s5p
    h0 = jnp.transpose(st_ssm.reshape(n, N_ST, 2), (2, 0, 1))
    q, kn, vn, pc, o_a, h1 = _sample_proj(x, p["w_in"][l].astype(BF16), h0, lam2, bblk, cblk, p["ssm_d"][l],
                                          p["ssm_glu_w"][l], p["ssm_glu_b"][l])
    lb_rows, lb0 = sbias
    o_b, o_c, s1 = _sample_mix(q, kn, vn, pc, st_shift, k_cache, v_cache, st_rwkv, lb_rows, lb0, rwp, l)
    x1, top_idx, gates = _outproj(x, o_a, o_b.reshape(n, W_B), o_c.reshape(n, W_C), p["w_out"][l],
                                  p["ln1_g"][l], p["ln1_b"][l], p["moe_router_w"][l],
                                  p["moe_router_b"][l])
    x2 = _sample_moe(x1, top_idx, gates, p, l)
    ssm = jnp.transpose(h1.reshape(2, n, G_A, N_A), (1, 2, 3, 0))
    return x2, (kn, vn, ssm, s1, pc)


def _shift_kernel(c_ref, new_ref, o_ref):
    x = c_ref[...]
    length = x.shape[1]
    lane = lax.broadcasted_iota(jnp.int32, x.shape, 1)
    o_ref[...] = jnp.where(lane == length - 1, new_ref[...], pltpu.roll(x, length - 1, axis=1))


def _shift_cache(cache_t, new):
    rows, length = cache_t.shape
    blk = min(ROW_TILE, rows)
    return pl.pallas_call(
        _shift_kernel,
        out_shape=jax.ShapeDtypeStruct((rows, length), F32),
        grid=(rows // blk,),
        in_specs=[pl.BlockSpec((blk, length), lambda i: (i, 0)),
                  pl.BlockSpec((blk, 1), lambda i: (i, 0))],
        out_specs=pl.BlockSpec((blk, length), lambda i: (i, 0)),
        compiler_params=_cp(("parallel",)),
        name="shift_cache",
    )(cache_t, new)


def kernel(x_prompt, x_sample, cache_attn_k, cache_attn_v, state_ssm, state_rwkv, state_shift, w_in, w_out, ln1_g, ln1_b, ln2_g, ln2_b, ssm_a_re, ssm_a_im, ssm_log_dt, ssm_b_re, ssm_b_im, ssm_c_re, ssm_c_im, ssm_d, ssm_glu_w, ssm_glu_b, rel_bias, rwkv_mu, rwkv_w0, rwkv_w2, rwkv_a0, rwkv_a2, rwkv_g2, rwkv_k_k, rwkv_k_a, rwkv_r_k, rwkv_ln_g, rwkv_ln_b, moe_router_w, moe_router_b, moe_w_up, moe_b_up, moe_w_down, moe_b_down):
    p = dict(w_in=w_in, w_out=w_out, ln1_g=ln1_g, ln1_b=ln1_b, ln2_g=ln2_g, ln2_b=ln2_b,
             ssm_d=ssm_d, ssm_glu_w=ssm_glu_w, ssm_glu_b=ssm_glu_b,
             moe_router_w=moe_router_w, moe_router_b=moe_router_b, moe_w_up=moe_w_up,
             moe_b_up=moe_b_up, moe_w_down=moe_w_down, moe_b_down=moe_b_down)
    nb, t, _ = x_prompt.shape
    ns = x_sample.shape[0]
    l_buf = cache_attn_k.shape[2]
    table = _log_bias_table(rel_bias)
    tiles = _bias_tiles(table, min(t, WIN_MAX) // Q_TILE + 1)
    sbias = _sample_bias(rel_bias, l_buf)
    xp = x_prompt.reshape(nb * t, D_MODEL)
    xs = x_sample.reshape(ns, D_MODEL)
    st_p, st_s = [], []
    time_minor = lambda c: jnp.transpose(c, (0, 1, 3, 4, 2)).reshape(DEPTH, c.shape[1], W_B, c.shape[2])
    time_major = lambda c, n: jnp.transpose(c.reshape(DEPTH, n, H_B, HEAD_DIM, c.shape[-1]), (0, 1, 4, 2, 3))
    kc_t, vc_t = time_minor(cache_attn_k), time_minor(cache_attn_v)
    kv = (jnp.zeros((DEPTH, nb, W_B, t), F32), jnp.zeros((DEPTH, nb, W_B, t), F32))
    for l in range(DEPTH):
        s5p, s5s = _s5_params(ssm_a_re[l], ssm_a_im[l], ssm_log_dt[l], ssm_b_re[l], ssm_b_im[l],
                              ssm_c_re[l], ssm_c_im[l])
        rwp = _rwkv_params(dict(rwkv_mu=rwkv_mu[l], rwkv_w0=rwkv_w0[l], rwkv_w2=rwkv_w2[l],
                                rwkv_a0=rwkv_a0[l], rwkv_a2=rwkv_a2[l], rwkv_g2=rwkv_g2[l],
                                rwkv_k_k=rwkv_k_k[l], rwkv_k_a=rwkv_k_a[l], rwkv_r_k=rwkv_r_k[l],
                                rwkv_ln_g=rwkv_ln_g[l], rwkv_ln_b=rwkv_ln_b[l]))
        xp, sp, kv = _prompt_layer(xp, nb, t, p, l, s5p, rwp, tiles, kv)
        xs, ss = _sample_layer(xs, p, l, s5s, rwp, sbias, kc_t, vc_t,
                               state_ssm[l], state_rwkv[l], state_shift[l])
        st_p.append(sp)
        st_s.append(ss)
    p_ssm, p_rwkv, p_shift = (jnp.stack(z) for z in zip(*st_p))
    kn, vn, s_ssm, s_rwkv, s_shift = (jnp.stack(z) for z in zip(*st_s))
    keep_p = min(WIN_MAX, t)
    p_k, p_v = (time_major(a[..., t - keep_p:], nb) for a in kv)
    keep = min(WIN_MAX, l_buf + 1)
    drop = l_buf + 1 - keep
    if drop == 1:
        advance = lambda c_t, new: _shift_cache(c_t.reshape(-1, l_buf), new.reshape(-1, 1)).reshape(c_t.shape)
    else:
        advance = lambda c_t, new: jnp.concatenate([c_t[..., drop:], new[..., None]], axis=-1)
    s_k = time_major(advance(kc_t, kn), ns)
    s_v = time_major(advance(vc_t, vn), ns)
    return (xp.reshape(nb, t, D_MODEL), xs.reshape(ns, 1, D_MODEL),
            p_k, p_v, p_ssm, p_rwkv, p_shift, s_k, s_v, s_ssm, s_rwkv, s_shift)
```

```python
import functools
import math

import numpy as np
import jax
import jax.numpy as jnp
from jax import lax
from jax.experimental import pallas as pl
from jax.experimental.pallas import tpu as pltpu

F32 = jnp.float32
BF16 = jnp.bfloat16
HI = lax.Precision.HIGHEST

D_MODEL = 1024
DEPTH = 2
HEAD_DIM = 64
W_A = 256
C_GRP = 16
G_A = 16
N_A = 64
N_ST = G_A * N_A
W_B = 384
H_B = 6
BRANCHES = ((128, 1), (512, 4), (2048, 16))
WIN_MAX = 2048
N_BUCKETS = 32
MAX_DIST = WIN_MAX
NEG_INF = -1e30
W_C = 384
H_C = 6
R_W = 32
R_A = 32
R_G = 64
C_SHIFT = 3 * W_C + R_W + R_A + R_G
N_IN = W_A + 3 * W_B + C_SHIFT
GN_EPS = 64e-5
N_EXPERTS = 32
TOP_K = 4
D_FF = D_MODEL
SWIGLU_ALPHA = 1.702
SWIGLU_LIMIT = 7.0
ALPHA = (2 * DEPTH) ** 0.25
LN_EPS = 1e-5

LANES = 128
Q_TILE = 128
RWKV_CHUNK = 64
RWKV_SEQS = 2
S5_CHUNK = 64
ROW_TILE = 512
MOE_ROWS = 256
COMBINE_TOK = 128
VMEM_LIMIT = 56 * 1024 * 1024


def _cp(sem, vmem=VMEM_LIMIT):
    return pltpu.CompilerParams(dimension_semantics=sem, vmem_limit_bytes=vmem)


def _dot(a, b, precision=None):
    return jnp.dot(a, b, preferred_element_type=F32, precision=precision)


def _dot_nt(a, b, precision=None):
    return lax.dot_general(a, b, (((1,), (1,)), ((), ())),
                           preferred_element_type=F32, precision=precision)


def _dot_tn(a, b, precision=None):
    return lax.dot_general(a, b, (((0,), (0,)), ((), ())),
                           preferred_element_type=F32, precision=precision)


def _layernorm_rows(x, g, b):
    mu = jnp.mean(x, axis=-1, keepdims=True)
    d = x - mu
    var = jnp.mean(d * d, axis=-1, keepdims=True)
    return d * lax.rsqrt(var + LN_EPS) * g + b


def _sigmoid(x):
    return 1.0 / (1.0 + jnp.exp(-x))


def _softplus(x):
    return jnp.maximum(x, 0.0) + jnp.log(1.0 + jnp.exp(-jnp.abs(x)))


def _gelu(x):
    c = math.sqrt(2.0 / math.pi)
    return 0.5 * x * (1.0 + jnp.tanh(c * (x + 0.044715 * (x * x * x))))


_PROJ_SPLITS = (0, W_A, W_A + W_B, W_A + 2 * W_B, W_A + 3 * W_B, N_IN)


def _proj_kernel(x_ref, w_ref, *refs):
    u_ref, q_ref, k_ref, v_ref, pc_ref = refs[-5:]
    xb = x_ref[...].astype(BF16)
    outs = ((u_ref, False), (q_ref, False), (k_ref, True), (v_ref, True), (pc_ref, False))
    for (o_ref, time_minor), lo, hi in zip(outs, _PROJ_SPLITS[:-1], _PROJ_SPLITS[1:]):
        y = _dot(xb, w_ref[:, lo:hi])
        o_ref[...] = jnp.transpose(y) if time_minor else y


def _proj(x, w_bf16, nb, layer, kv_prev):
    n = x.shape[0]
    t = n // nb
    per_seq = t // ROW_TILE
    widths = [hi - lo for lo, hi in zip(_PROJ_SPLITS[:-1], _PROJ_SPLITS[1:])]
    rows = lambda w: pl.BlockSpec((ROW_TILE, w), lambda i: (i, 0))
    stacked = pl.BlockSpec((None, None, W_B, ROW_TILE), lambda i: (layer, i // per_seq, 0, i % per_seq))
    flat = lambda w: jax.ShapeDtypeStruct((n, w), F32)
    kv = jax.ShapeDtypeStruct((DEPTH, nb, W_B, t), F32)
    prev = list(kv_prev)
    return pl.pallas_call(
        _proj_kernel,
        out_shape=[flat(widths[0]), flat(widths[1]), kv, kv, flat(widths[4])],
        grid=(n // ROW_TILE,),
        in_specs=[pl.BlockSpec((ROW_TILE, D_MODEL), lambda i: (i, 0)),
                  pl.BlockSpec((D_MODEL, N_IN), lambda i: (0, 0))]
        + [pl.BlockSpec(memory_space=pl.ANY)] * len(prev),
        out_specs=[rows(widths[0]), rows(widths[1]), stacked, stacked, rows(widths[4])],
        input_output_aliases={2: 2, 3: 3},
        compiler_params=_cp(("parallel",)),
        name="proj",
    )(x, w_bf16, *prev)


def _s5_kernel(u_ref, lam_ref, bblk_ref, cblk_ref, d_ref, gw_ref, gb_ref,
               o_ref, ht_ref, utm, hs, ytm, h_scr, *, nb, lt):
    c = pl.program_id(0)

    @pl.when(c == 0)
    def _():
        h_scr[...] = jnp.zeros_like(h_scr)

    n_half = W_A // LANES
    for b in range(nb):
        for j in range(n_half):
            utm[j, pl.ds(b, lt, stride=nb), :] = u_ref[b, :, j * LANES:(j + 1) * LANES]
    u_all = jnp.concatenate([utm[j] for j in range(n_half)], axis=1)
    hs[...] = _dot(u_all.astype(BF16), bblk_ref[...])
    lr = jnp.broadcast_to(lam_ref[0:1, :], (nb, N_ST))
    li = jnp.broadcast_to(lam_ref[1:2, :], (nb, N_ST))

    def body(t, carry):
        hr, hi = carry
        row = pl.multiple_of(t * nb, nb)
        br = hs[pl.ds(row, nb), 0:N_ST]
        bi = hs[pl.ds(row, nb), N_ST:2 * N_ST]
        nr = lr * hr - li * hi + br
        ni = lr * hi + li * hr + bi
        hs[pl.ds(row, nb), 0:N_ST] = nr
        hs[pl.ds(row, nb), N_ST:2 * N_ST] = ni
        return nr, ni

    hr, hi = lax.fori_loop(0, lt, body, (h_scr[0], h_scr[1]), unroll=2)
    h_scr[0] = hr
    h_scr[1] = hi
    ht_ref[0] = hr
    ht_ref[1] = hi
    y = _dot(hs[...].astype(BF16), cblk_ref[...]) + d_ref[...] * u_all
    z = _gelu(y)
    gl = _dot(z.astype(BF16), gw_ref[...]) + gb_ref[...]
    res = z * _sigmoid(gl)
    for j in range(n_half):
        ytm[j] = res[:, j * LANES:(j + 1) * LANES]
    for b in range(nb):
        for j in range(n_half):
            o_ref[b, :, j * LANES:(j + 1) * LANES] = ytm[j, pl.ds(b, lt, stride=nb), :]


def _s5_params(a_re, a_im, log_dt, b_re, b_im, c_re, c_im):
    lam = lax.complex(a_re, a_im)
    dt = jnp.exp(log_dt)[:, None]
    lam_bar = jnp.exp(lam * dt)
    b_bar = ((lam_bar - 1.0) / lam)[..., None] * lax.complex(b_re, b_im)
    eye = jnp.eye(G_A, dtype=F32)

    def blk_in(m):
        return jnp.einsum("gnc,gh->gchn", m, eye).reshape(W_A, N_ST)

    def blk_out(m):
        return jnp.einsum("gcn,gh->gnhc", m, eye).reshape(N_ST, W_A)

    bblk = jnp.concatenate([blk_in(b_bar.real), blk_in(b_bar.imag)], axis=1)
    cblk = jnp.concatenate([blk_out(c_re), blk_out(-c_im)], axis=0)
    lam2 = jnp.stack([lam_bar.real.reshape(N_ST), lam_bar.imag.reshape(N_ST)])
    bblk3 = jnp.concatenate([blk_in(b_bar.real), blk_in(b_bar.imag - b_bar.real)], axis=1)
    cblk3 = jnp.concatenate([blk_out(c_re), blk_out(c_re + c_im)], axis=0)
    return (lam2, bblk, cblk), (lam2, bblk3.astype(BF16), cblk3.astype(BF16))


def _s5_prompt(u, lam2, bblk, cblk, d, gw, gb):
    nb, t, _ = u.shape
    lt = min(S5_CHUNK, t)
    kern = functools.partial(_s5_kernel, nb=nb, lt=lt)
    const = lambda shape: pl.BlockSpec(shape, lambda c: (0,) * len(shape))
    return pl.pallas_call(
        kern,
        out_shape=[jax.ShapeDtypeStruct((nb, t, W_A), F32),
                   jax.ShapeDtypeStruct((2, nb, N_ST), F32)],
        grid=(t // lt,),
        in_specs=[pl.BlockSpec((nb, lt, W_A), lambda c: (0, c, 0)),
                  const((2, N_ST)), const((W_A, 2 * N_ST)), const((2 * N_ST, W_A)),
                  const((1, W_A)), const((W_A, W_A)), const((1, W_A))],
        out_specs=[pl.BlockSpec((nb, lt, W_A), lambda c: (0, c, 0)),
                   const((2, nb, N_ST))],
        scratch_shapes=[pltpu.VMEM((W_A // LANES, nb * lt, LANES), F32),
                        pltpu.VMEM((nb * lt, 2 * N_ST), F32),
                        pltpu.VMEM((W_A // LANES, nb * lt, LANES), F32),
                        pltpu.VMEM((2, nb, N_ST), F32)],
        compiler_params=_cp(("arbitrary",)),
        name="s5_prompt",
    )(u, lam2, bblk.astype(BF16), cblk.astype(BF16), d.reshape(1, W_A),
      gw.astype(BF16), gb.reshape(1, W_A))


def _branch_offsets():
    return np.stack([np.arange(w // d + 1) * d for (w, d) in BRANCHES]).astype(np.int32)


def _t5_buckets(dist):
    max_exact = N_BUCKETS // 2
    d = np.maximum(dist, 1).astype(np.float32)
    large = max_exact + (np.log(d / max_exact) / np.log(MAX_DIST / max_exact)
                         * (N_BUCKETS - max_exact)).astype(np.int32)
    return np.where(dist < max_exact, dist, np.minimum(large, N_BUCKETS - 1)).astype(np.int32)


def _log_bias_table(rel_bias):
    offs = _branch_offsets()
    bias = rel_bias[jnp.asarray(_t5_buckets(offs))]
    table = jnp.full((WIN_MAX + 1, H_B), -jnp.inf, F32)
    for br in range(len(BRANCHES)):
        idx = jnp.asarray(offs[br])
        table = table.at[idx].set(jnp.logaddexp(table[idx], bias[br]))
    return jnp.maximum(table, NEG_INF)


def _bias_tiles(table, n_diff):
    per = 2 * Q_TILE
    m = np.arange(per)
    dist = np.arange(n_diff)[:, None] * Q_TILE + np.where(m < Q_TILE, -m, per - m)[None, :]
    ok = (dist >= 0) & (dist <= WIN_MAX) & (m != Q_TILE)[None, :]
    v = jnp.where(jnp.asarray(ok)[..., None], table[jnp.asarray(np.clip(dist, 0, WIN_MAX))], NEG_INF)
    v = jnp.transpose(v, (2, 0, 1))
    flat = jnp.broadcast_to(v[:, :, None, :], (H_B, n_diff, Q_TILE, per)).reshape(H_B, n_diff, Q_TILE * per)
    tiles = flat[:, :, :Q_TILE * (per - 1)].reshape(H_B, n_diff, Q_TILE, per - 1)[..., :Q_TILE]
    tiles = tiles.reshape(H_B // 2, 2, n_diff, Q_TILE, Q_TILE)
    return jnp.transpose(tiles, (0, 2, 1, 3, 4)).reshape(H_B // 2, n_diff, 2 * Q_TILE, Q_TILE)


def _attn_kernel(q_ref, k_ref, v_ref, bias_ref, o_ref, kb, vb, *, nq):
    kb[...] = k_ref[...].astype(BF16)
    vb[...] = v_ref[...].astype(BF16)
    lo = lax.broadcasted_iota(jnp.int32, (Q_TILE, LANES), 1) < HEAD_DIM
    for qi in range(nq):
        rows = slice(qi * Q_TILE, (qi + 1) * Q_TILE)
        q = q_ref[0, rows, :] * (HEAD_DIM ** -0.5)
        q2 = jnp.concatenate([jnp.where(lo, q, 0.0), jnp.where(lo, 0.0, q)], axis=0).astype(BF16)
        nk = (qi + 1) * Q_TILE
        bias = jnp.concatenate([bias_ref[qi - kj] for kj in range(qi + 1)], axis=1)
        s = _dot(q2, kb[:, 0:nk]) + bias
        m = jnp.max(s, axis=-1, keepdims=True)
        p = jnp.exp(s - m)
        l = jnp.sum(p, axis=-1, keepdims=True)
        o = _dot_nt(p.astype(BF16), vb[:, 0:nk]) / l
        o_ref[0, rows, :] = jnp.where(lo, o[0:Q_TILE], o[Q_TILE:2 * Q_TILE])


def _attn_prompt(q, k_all, v_all, tiles, layer):
    nb, t, _ = q.shape
    n_diff = tiles.shape[1]
    hp = W_B // LANES
    kern = functools.partial(_attn_kernel, nq=t // Q_TILE)
    seq = pl.BlockSpec((1, t, LANES), lambda b, p: (b, 0, p))
    kv = pl.BlockSpec((None, None, LANES, t), lambda b, p: (layer, b, p, 0))
    return pl.pallas_call(
        kern,
        out_shape=jax.ShapeDtypeStruct((nb, t, W_B), F32),
        grid=(nb, hp),
        in_specs=[seq, kv, kv,
                  pl.BlockSpec((None, n_diff, 2 * Q_TILE, Q_TILE), lambda b, p: (p, 0, 0, 0))],
        out_specs=seq,
        scratch_shapes=[pltpu.VMEM((LANES, t), BF16), pltpu.VMEM((LANES, t), BF16)],
        compiler_params=_cp(("parallel", "parallel")),
        name="attn_prompt",
    )(q, k_all, v_all, tiles)


def _split_dot(x, m_bf16, parts):
    acc = None
    rem = x
    for i in range(parts):
        piece = rem.astype(BF16)
        term = _dot(piece, m_bf16)
        acc = term if acc is None else acc + term
        if i + 1 < parts:
            rem = rem - piece.astype(F32)
    return acc


def _split_dot_left(m_bf16, x, parts):
    acc = None
    rem = x
    for i in range(parts):
        piece = rem.astype(BF16)
        term = _dot(m_bf16, piece)
        acc = term if acc is None else acc + term
        if i + 1 < parts:
            rem = rem - piece.astype(F32)
    return acc


def _rwkv_pre(pc, prev, prm, seg_sum):
    (mu, w0, w2p, a0, a2p, g2p, k_k, k_a, r_k, ln_g, ln_b, mseg) = prm
    xs = pc + (prev - pc) * mu
    r = xs[:, 0:W_C]
    k = xs[:, W_C:2 * W_C]
    v = xs[:, 2 * W_C:3 * W_C]
    tail = xs[:, 3 * W_C:C_SHIFT]
    w_raw = -_softplus(-(w0 + _dot(jnp.tanh(tail).astype(BF16), w2p))) - 0.5
    a = _sigmoid(a0 + _dot(tail.astype(BF16), a2p))
    g = _dot(_sigmoid(tail).astype(BF16), g2p)
    kk = k * k_k
    nrm = jnp.sqrt(seg_sum(kk * kk))
    kk = kk / jnp.maximum(nrm, 1e-12)
    k2 = k * (1.0 + (a - 1.0) * k_a)
    logw = -jnp.exp(w_raw)
    return r, k2, v, kk, a, g, logw


def _rwkv_post(y, r, k2, v, g, prm, seg_sum):
    (mu, w0, w2p, a0, a2p, g2p, k_k, k_a, r_k, ln_g, ln_b, mseg) = prm
    mean = seg_sum(y) * (1.0 / HEAD_DIM)
    d = y - mean
    var = seg_sum(d * d) * (1.0 / HEAD_DIM)
    yn = d * lax.rsqrt(var + GN_EPS) * ln_g + ln_b
    bonus = seg_sum(r * k2 * r_k) * v
    return (yn + bonus) * g


def _rwkv_kernel(pc_ref, mu_ref, w0_ref, w2_ref, a0_ref, a2_ref, g2_ref, kk_ref, ka_ref,
                 rk_ref, lng_ref, lnb_ref, mseg_ref, ltri_ref,
                 o_ref, s_ref, s_scr, prev_scr, yt_scr, *, ch, nseq):
    c = pl.program_id(1)
    rows = nseq * ch

    @pl.when(c == 0)
    def _():
        s_scr[...] = jnp.zeros_like(s_scr)
        prev_scr[...] = jnp.zeros_like(prev_scr)
        yt_scr[...] = jnp.zeros_like(yt_scr)

    prm = (mu_ref[...], w0_ref[...], w2_ref[...], a0_ref[...], a2_ref[...], g2_ref[...],
           kk_ref[...], ka_ref[...], rk_ref[...], lng_ref[...], lnb_ref[...], None)
    mseg = mseg_ref[...]
    seg_sum = lambda t: _split_dot(t, mseg, 3)
    pc = pc_ref[...].reshape(rows, C_SHIFT)
    rowi = lax.broadcasted_iota(jnp.int32, (rows, C_SHIFT), 0)
    prev = pltpu.roll(pc, 1, axis=0)
    for b in range(nseq):
        prev = jnp.where(rowi == b * ch, prev_scr[b:b + 1, :], prev)
        prev_scr[b:b + 1, :] = pc[(b + 1) * ch - 1:(b + 1) * ch, :]
    r, k2, v, kk, a, g, logw = _rwkv_pre(pc, prev, prm, seg_sum)

    cs = _split_dot_left(ltri_ref[...], logw, 3)
    g_in = jnp.exp(cs)
    g_ex = jnp.exp(cs - logw)
    g_inv = jnp.exp(-cs)
    al = (-kk * g_ex).astype(BF16)
    rt = (r * g_in).astype(BF16)
    bh = kk * a * g_inv
    kh = k2 * g_inv
    vb = v.astype(BF16)

    si = lax.broadcasted_iota(jnp.int32, (ch, ch), 0)
    ti = lax.broadcasted_iota(jnp.int32, (ch, ch), 1)
    strict = si < ti
    incl = si <= ti
    n_sq = int(math.log2(ch))
    units = [(b, h) for b in range(nseq) for h in range(H_C)]
    idx = range(len(units))
    cut = lambda t, u: t[units[u][0] * ch:(units[u][0] + 1) * ch,
                         units[u][1] * HEAD_DIM:(units[u][1] + 1) * HEAD_DIM]
    g_end = [cut(g_in, u)[ch - 1:ch, :] for u in idx]
    al_u = [cut(al, u) for u in idx]
    rt_u = [cut(rt, u) for u in idx]
    v_u = [cut(vb, u) for u in idx]
    bh_u = [cut(bh, u) for u in idx]
    kh_u = [cut(kh, u) for u in idx]
    bc = [(bh_u[u] * g_end[u]).astype(BF16) for u in idx]
    kc = [(kh_u[u] * g_end[u]).astype(BF16) for u in idx]
    s0 = [s_scr[b, h] for (b, h) in units]
    s0b = [t.astype(BF16) for t in s0]
    bk = [jnp.concatenate([bh_u[u], kh_u[u]], axis=0).astype(BF16) for u in idx]
    ga = [_dot_nt(bk[u], al_u[u]) for u in idx]
    gr = [_dot_nt(bk[u], rt_u[u]) for u in idx]
    p0 = [_dot_nt(s0b[u], al_u[u]) for u in idx]
    rs = [_dot_nt(s0b[u], rt_u[u]) for u in idx]
    a_t = [jnp.where(strict, ga[u][0:ch], 0.0) for u in idx]
    b_t = [jnp.where(strict, ga[u][ch:2 * ch], 0.0).astype(BF16) for u in idx]
    rb_t = [jnp.where(incl, gr[u][0:ch], 0.0).astype(BF16) for u in idx]
    rk_t = [jnp.where(incl, gr[u][ch:2 * ch], 0.0).astype(BF16) for u in idx]
    bv = [_dot_tn(v_u[u], b_t[u]) for u in idx]
    rv = [_dot_tn(v_u[u], rk_t[u]) for u in idx]
    sv = [_dot_tn(v_u[u], kc[u]) for u in idx]
    x_t = [p0[u] + bv[u] for u in idx]
    for it in range(n_sq):
        a_b = [a_t[u].astype(BF16) for u in idx]
        if it + 1 < n_sq:
            both = [_dot(jnp.concatenate([x_t[u], a_t[u]], axis=0).astype(BF16), a_b[u]) for u in idx]
            x_t = [x_t[u] + both[u][0:HEAD_DIM] for u in idx]
            a_t = [both[u][HEAD_DIM:HEAD_DIM + ch] for u in idx]
        else:
            step = [_dot(x_t[u].astype(BF16), a_b[u]) for u in idx]
            x_t = [x_t[u] + step[u] for u in idx]
    x_b = [t.astype(BF16) for t in x_t]
    yx = [_dot(x_b[u], rb_t[u]) for u in idx]
    sx = [_dot(x_b[u], bc[u]) for u in idx]
    for u, (b, h) in enumerate(units):
        s_new = s0[u] * g_end[u] + sx[u] + sv[u]
        yt_scr[h * HEAD_DIM:(h + 1) * HEAD_DIM, b * ch:(b + 1) * ch] = rs[u] + yx[u] + rv[u]
        s_scr[b, h] = s_new
        s_ref[b, h] = s_new

    y = jnp.transpose(yt_scr[...])[0:rows, :]
    o_ref[...] = _rwkv_post(y, r, k2, v, g, prm, seg_sum).reshape(nseq, ch, W_C)


def _rwkv_params(p):
    pad = lambda w, lo: jnp.zeros((R_W + R_A + R_G, W_C), BF16).at[lo:lo + w.shape[0]].set(w.astype(BF16))
    seg = np.arange(W_C) // HEAD_DIM
    mseg = jnp.asarray((seg[:, None] == seg[None, :]).astype(np.float32))
    row = lambda t: t.reshape(1, -1)
    return (row(p["rwkv_mu"]), row(p["rwkv_w0"]), pad(p["rwkv_w2"], 0), row(p["rwkv_a0"]),
            pad(p["rwkv_a2"], R_W), pad(p["rwkv_g2"], R_W + R_A), row(p["rwkv_k_k"]),
            row(p["rwkv_k_a"]), row(p["rwkv_r_k"]), row(p["rwkv_ln_g"]), row(p["rwkv_ln_b"]), mseg)


def _rwkv_prompt(pc, prm):
    nb, t, _ = pc.shape
    ch = min(RWKV_CHUNK, t)
    nseq = max(1, min(RWKV_SEQS * LANES // ch, nb))
    assert nb % nseq == 0
    ltri = np.kron(np.eye(nseq, dtype=np.float32), np.tril(np.ones((ch, ch), np.float32)))
    ltri = jnp.asarray(ltri).astype(BF16)
    kern = functools.partial(_rwkv_kernel, ch=ch, nseq=nseq)
    const = lambda a: pl.BlockSpec(a.shape, lambda b, c: (0,) * a.ndim)
    args = list(prm[:-1]) + [prm[-1].astype(BF16), ltri]
    return pl.pallas_call(
        kern,
        out_shape=[jax.ShapeDtypeStruct((nb, t, W_C), F32),
                   jax.ShapeDtypeStruct((nb, H_C, HEAD_DIM, HEAD_DIM), F32)],
        grid=(nb // nseq, t // ch),
        in_specs=[pl.BlockSpec((nseq, ch, C_SHIFT), lambda b, c: (b, c, 0))] + [const(a) for a in args],
        out_specs=[pl.BlockSpec((nseq, ch, W_C), lambda b, c: (b, c, 0)),
                   pl.BlockSpec((nseq, H_C, HEAD_DIM, HEAD_DIM), lambda b, c: (b, 0, 0, 0))],
        scratch_shapes=[pltpu.VMEM((nseq, H_C, HEAD_DIM, HEAD_DIM), F32),
                        pltpu.VMEM((nseq, C_SHIFT), F32),
                        pltpu.VMEM((W_C, -(-nseq * ch // LANES) * LANES), F32)],
        compiler_params=_cp(("parallel", "arbitrary")),
        name="rwkv_prompt",
    )(pc, *args)


def _route(x1, rw, rb):
    logits = _dot(x1.astype(BF16), rw) + rb
    lane = lax.broadcasted_iota(jnp.int32, logits.shape, 1).astype(F32)
    vals, idxs = [], []
    cur = logits
    for _ in range(TOP_K):
        m = jnp.max(cur, axis=-1, keepdims=True)
        idx = jnp.min(jnp.where(cur == m, lane, float(N_EXPERTS)), axis=-1, keepdims=True)
        vals.append(m)
        idxs.append(idx)
        cur = jnp.where(lane == idx, -jnp.inf, cur)
    ex = [jnp.exp(v - vals[0]) for v in vals]
    tot = ex[0] + ex[1] + ex[2] + ex[3]
    gates = jnp.concatenate([e / tot for e in ex], axis=-1)
    return jnp.concatenate(idxs, axis=-1).astype(jnp.int32), gates


def _outproj_kernel(x_ref, oa_ref, ob_ref, oc_ref, w_ref, g_ref, b_ref, rw_ref, rb_ref,
                    x1_ref, idx_ref, gate_ref):
    mix = (_dot(oa_ref[...].astype(BF16), w_ref[0:W_A, :])
           + _dot(ob_ref[...].astype(BF16), w_ref[W_A:W_A + W_B, :])
           + _dot(oc_ref[...].astype(BF16), w_ref[W_A + W_B:, :]))
    x1 = _layernorm_rows(ALPHA * x_ref[...] + mix, g_ref[...], b_ref[...])
    x1_ref[...] = x1
    idx, gates = _route(x1, rw_ref[...], rb_ref[...])
    idx_ref[...] = idx
    gate_ref[...] = gates


def _outproj(x, oa, ob, oc, w, g, b, rw, rb):
    n = x.shape[0]
    tm = min(ROW_TILE, n)
    rows = lambda w_: pl.BlockSpec((tm, w_), lambda i: (i, 0))
    const = lambda a: pl.BlockSpec(a.shape, lambda i: (0,) * a.ndim)
    consts = [w.astype(BF16), g.reshape(1, -1), b.reshape(1, -1), rw.astype(BF16), rb.reshape(1, -1)]
    return pl.pallas_call(
        _outproj_kernel,
        out_shape=[jax.ShapeDtypeStruct((n, D_MODEL), F32),
                   jax.ShapeDtypeStruct((n, TOP_K), jnp.int32),
                   jax.ShapeDtypeStruct((n, TOP_K), F32)],
        grid=(n // tm,),
        in_specs=[rows(D_MODEL), rows(W_A), rows(W_B), rows(W_C)] + [const(a) for a in consts],
        out_specs=[rows(D_MODEL), rows(TOP_K), rows(TOP_K)],
        compiler_params=_cp(("parallel",)),
        name="outproj",
    )(x, oa, ob, oc, *consts)


def _dest_kernel(idx_ref, ltri_ref, utri_ref, dest_ref, cnt_ref, cnt_scr, run_scr, start_scr):
    ph = pl.program_id(0)
    i = pl.program_id(1)
    idx = idx_ref[...]
    lane = lax.broadcasted_iota(jnp.int32, (idx.shape[0], N_EXPERTS), 1)
    hot = [idx[:, k:k + 1] == lane for k in range(TOP_K)]
    multi = jnp.zeros(lane.shape, F32)
    for k in range(TOP_K):
        multi = multi + jnp.where(hot[k], 1.0, 0.0)
    tile_cnt = jnp.sum(multi, axis=0, keepdims=True)

    @pl.when(jnp.logical_and(ph == 0, i == 0))
    def _():
        cnt_scr[...] = jnp.zeros_like(cnt_scr)

    @pl.when(ph == 0)
    def _():
        cnt_scr[...] += tile_cnt

    @pl.when(jnp.logical_and(ph == 1, i == 0))
    def _():
        cnt = cnt_scr[...]
        padded = jnp.floor((cnt + (MOE_ROWS - 1)) * (1.0 / MOE_ROWS)) * MOE_ROWS
        start_scr[...] = _dot(jnp.broadcast_to(padded, (8, N_EXPERTS)), utri_ref[...], HI)[0:1, :]
        run_scr[...] = jnp.zeros_like(run_scr)
        cnt_ref[...] = cnt

    @pl.when(ph == 1)
    def _():
        before = _dot(ltri_ref[...], multi.astype(BF16))
        base = start_scr[...] + run_scr[...] + before
        cols = [jnp.sum(jnp.where(hot[k], base, 0.0), axis=1, keepdims=True) for k in range(TOP_K)]
        dest_ref[...] = jnp.concatenate(cols, axis=1).astype(jnp.int32)
        run_scr[...] += tile_cnt


def _route_dest(top_idx):
    n = top_idx.shape[0]
    tile = min(ROW_TILE, n)
    nt = n // tile
    ltri = jnp.asarray(np.tril(np.ones((tile, tile), np.float32), -1)).astype(BF16)
    utri = jnp.asarray(np.triu(np.ones((N_EXPERTS, N_EXPERTS), np.float32), 1))
    return pl.pallas_call(
        _dest_kernel,
        out_shape=[jax.ShapeDtypeStruct((n, TOP_K), jnp.int32),
                   jax.ShapeDtypeStruct((1, N_EXPERTS), F32)],
        grid=(2, nt),
        in_specs=[pl.BlockSpec((tile, TOP_K), lambda ph, i: (i, 0)),
                  pl.BlockSpec((tile, tile), lambda ph, i: (0, 0)),
                  pl.BlockSpec((N_EXPERTS, N_EXPERTS), lambda ph, i: (0, 0))],
        out_specs=[pl.BlockSpec((tile, TOP_K), lambda ph, i: (i * ph, 0)),
                   pl.BlockSpec((1, N_EXPERTS), lambda ph, i: (0, 0))],
        scratch_shapes=[pltpu.VMEM((1, N_EXPERTS), F32)] * 3,
        compiler_params=_cp(("arbitrary", "arbitrary")),
        name="moe_dest",
    )(top_idx, ltri, utri)


def _block_experts(counts, n_blk):
    cnt = counts.reshape(N_EXPERTS).astype(jnp.int32)
    pends = jnp.cumsum((cnt + MOE_ROWS - 1) // MOE_ROWS * MOE_ROWS)
    starts = jnp.arange(n_blk, dtype=jnp.int32)[:, None] * MOE_ROWS
    blk_expert = jnp.minimum(jnp.sum((pends[None, :] <= starts).astype(jnp.int32), axis=1), N_EXPERTS - 1)
    last_blk = jnp.maximum(pends // MOE_ROWS - 1, 0).astype(jnp.int32)
    return blk_expert, (pends[-1] // MOE_ROWS).astype(jnp.int32).reshape(1), last_blk


def _dispatch_kernel(last_ref, na_ref, dest_ref, x1_ref, rows_hbm, zbuf, sem, zsem, *, tm, n_blk):
    @pl.when(pl.program_id(0) == 0)
    def _():
        zbuf[...] = jnp.zeros_like(zbuf)
        n_tail = n_blk - na_ref[0]

        def clear(blk):
            return pltpu.make_async_copy(zbuf, rows_hbm.at[pl.ds(blk * MOE_ROWS, MOE_ROWS), :], zsem)

        def start(j, carry):
            clear(jnp.where(j < N_EXPERTS, last_ref[jnp.minimum(j, N_EXPERTS - 1)],
                            na_ref[0] + j - N_EXPERTS)).start()
            return carry

        def wait(j, carry):
            clear(0).wait()
            return carry

        lax.fori_loop(0, N_EXPERTS + n_tail, start, 0)
        lax.fori_loop(0, N_EXPERTS + n_tail, wait, 0)

    for r in range(tm):
        for k in range(TOP_K):
            pltpu.make_async_copy(x1_ref.at[pl.ds(r, 1), :],
                                  rows_hbm.at[pl.ds(dest_ref[0, 0, r * TOP_K + k], 1), :], sem).start()
    for k in range(TOP_K):
        pltpu.make_async_copy(x1_ref, rows_hbm.at[pl.ds(0, tm), :], sem).wait()


def _dispatch(x1, dest, last_blk, n_active, n_blk):
    n = x1.shape[0]
    tm = COMBINE_TOK
    nt = n // tm
    kern = functools.partial(_dispatch_kernel, tm=tm, n_blk=n_blk)
    grid_spec = pltpu.PrefetchScalarGridSpec(
        num_scalar_prefetch=2,
        grid=(nt,),
        in_specs=[pl.BlockSpec((1, 1, TOP_K * tm), lambda i, lb, na: (i, 0, 0), memory_space=pltpu.SMEM),
                  pl.BlockSpec((tm, D_MODEL), lambda i, lb, na: (i, 0))],
        out_specs=pl.BlockSpec(memory_space=pl.ANY),
        scratch_shapes=[pltpu.VMEM((MOE_ROWS, D_MODEL), F32), pltpu.SemaphoreType.DMA,
                        pltpu.SemaphoreType.DMA],
    )
    return pl.pallas_call(
        kern,
        out_shape=jax.ShapeDtypeStruct((n_blk * MOE_ROWS, D_MODEL), F32),
        grid_spec=grid_spec,
        compiler_params=_cp(("arbitrary",)),
        name="moe_dispatch",
    )(last_blk, n_active, dest.reshape(nt, 1, TOP_K * tm), x1)


def _swiglu(h):
    h_glu = jnp.minimum(h[:, :D_FF], SWIGLU_LIMIT)
    h_lin = jnp.clip(h[:, D_FF:], -SWIGLU_LIMIT, SWIGLU_LIMIT)
    return h_glu * _sigmoid(SWIGLU_ALPHA * h_glu) * (h_lin + 1.0)


def _moe_kernel(be_ref, na_ref, x_ref, wup_ref, bup_ref, wdn_ref, bdn_ref, y_ref, wup_b, wdn_b):
    i = pl.program_id(0)
    active = i < na_ref[0]

    @pl.when(active)
    def _():
        changed = jnp.logical_or(i == 0, be_ref[i] != be_ref[jnp.maximum(i - 1, 0)])

        @pl.when(changed)
        def _():
            wup_b[...] = wup_ref[...].astype(BF16)
            wdn_b[...] = wdn_ref[...].astype(BF16)

        h = _dot(x_ref[...].astype(BF16), wup_b[...]) + bup_ref[...]
        y_ref[...] = _dot(_swiglu(h).astype(BF16), wdn_b[...]) + bdn_ref[...]

    @pl.when(jnp.logical_not(active))
    def _():
        y_ref[...] = jnp.zeros_like(y_ref)


def _moe_rows(rows, blk_expert, n_active, w_up, b_up, w_dn, b_dn, layer):
    n_blk = blk_expert.shape[0]
    grid_spec = pltpu.PrefetchScalarGridSpec(
        num_scalar_prefetch=2,
        grid=(n_blk,),
        in_specs=[
            pl.BlockSpec((MOE_ROWS, D_MODEL), lambda i, be, na: (jnp.minimum(i, jnp.maximum(na[0] - 1, 0)), 0)),
            pl.BlockSpec((None, None, D_MODEL, 2 * D_FF), lambda i, be, na: (layer, be[i], 0, 0)),
            pl.BlockSpec((None, None, 1, 2 * D_FF), lambda i, be, na: (layer, be[i], 0, 0)),
            pl.BlockSpec((None, None, D_FF, D_MODEL), lambda i, be, na: (layer, be[i], 0, 0)),
            pl.BlockSpec((None, None, 1, D_MODEL), lambda i, be, na: (layer, be[i], 0, 0)),
        ],
        out_specs=pl.BlockSpec((MOE_ROWS, D_MODEL), lambda i, be, na: (i, 0)),
        scratch_shapes=[pltpu.VMEM((D_MODEL, 2 * D_FF), BF16),
                        pltpu.VMEM((D_FF, D_MODEL), BF16)],
    )
    return pl.pallas_call(
        _moe_kernel,
        out_shape=jax.ShapeDtypeStruct((n_blk * MOE_ROWS, D_MODEL), F32),
        grid_spec=grid_spec,
        compiler_params=_cp(("arbitrary",)),
        name="moe_rows",
    )(blk_expert, n_active, rows,
      w_up, b_up.reshape(DEPTH, N_EXPERTS, 1, 2 * D_FF), w_dn, b_dn.reshape(DEPTH, N_EXPERTS, 1, D_MODEL))


def _combine_kernel(dest_ref, destn_ref, gate_ref, x1_ref, y_hbm, g_ref, b_ref, o_ref, ybuf, sem, *, tm):
    i = pl.program_id(0)
    slot = i % 2
    n_rows = TOP_K * tm

    def gather_tile(dests, s):
        for r in range(n_rows):
            pltpu.make_async_copy(y_hbm.at[pl.ds(dests[0, 0, r], 1), :],
                                  ybuf.at[s, pl.ds(r, 1), :], sem.at[s]).start()

    @pl.when(i == 0)
    def _():
        gather_tile(dest_ref, 0)

    for s in range(2):
        @pl.when(jnp.logical_and(i + 1 < pl.num_programs(0), slot == 1 - s))
        def _(s=s):
            gather_tile(destn_ref, s)

    for s in range(2):
        @pl.when(slot == s)
        def _(s=s):
            pltpu.make_async_copy(y_hbm.at[pl.ds(0, n_rows), :], ybuf.at[s], sem.at[s]).wait()
            gates = gate_ref[...]
            moe = jnp.zeros((tm, D_MODEL), F32)
            for k in range(TOP_K):
                moe = moe + gates[:, k:k + 1] * ybuf[s, k * tm:(k + 1) * tm, :]
            o_ref[...] = _layernorm_rows(ALPHA * x1_ref[...] + moe, g_ref[...], b_ref[...])


def _combine(dest, gates, x1, y_rows, g, b):
    n = x1.shape[0]
    tm = COMBINE_TOK
    nt = n // tm
    dest_t = dest.reshape(nt, tm, TOP_K).transpose(0, 2, 1).reshape(nt, 1, TOP_K * tm)
    kern = functools.partial(_combine_kernel, tm=tm)
    return pl.pallas_call(
        kern,
        out_shape=jax.ShapeDtypeStruct((n, D_MODEL), F32),
        grid=(nt,),
        in_specs=[pl.BlockSpec((1, 1, TOP_K * tm), lambda i: (i, 0, 0), memory_space=pltpu.SMEM),
                  pl.BlockSpec((1, 1, TOP_K * tm), lambda i: (jnp.minimum(i + 1, nt - 1), 0, 0),
                               memory_space=pltpu.SMEM),
                  pl.BlockSpec((tm, TOP_K), lambda i: (i, 0)),
                  pl.BlockSpec((tm, D_MODEL), lambda i: (i, 0)),
                  pl.BlockSpec(memory_space=pl.ANY),
                  pl.BlockSpec((1, D_MODEL), lambda i: (0, 0)),
                  pl.BlockSpec((1, D_MODEL), lambda i: (0, 0))],
        out_specs=pl.BlockSpec((tm, D_MODEL), lambda i: (i, 0)),
        scratch_shapes=[pltpu.VMEM((2, TOP_K * tm, D_MODEL), F32), pltpu.SemaphoreType.DMA((2,))],
        compiler_params=_cp(("arbitrary",)),
        name="moe_combine",
    )(dest_t, dest_t, gates, x1, y_rows, g.reshape(1, -1), b.reshape(1, -1))


def _moe_prompt(x1, top_idx, gates, p, layer):
    n = x1.shape[0]
    n_blk = -(-n * TOP_K // MOE_ROWS) + N_EXPERTS
    dest, counts = _route_dest(top_idx)
    blk_expert, n_active, last_blk = _block_experts(counts, n_blk)
    rows = _dispatch(x1, dest, last_blk, n_active, n_blk)
    y_rows = _moe_rows(rows, blk_expert, n_active, p["moe_w_up"], p["moe_b_up"],
                       p["moe_w_down"], p["moe_b_down"], layer)
    return _combine(dest, gates, x1, y_rows, p["ln2_g"][layer], p["ln2_b"][layer])


def _sample_proj_kernel(x_ref, w_ref, h0_ref, lam_ref, bblk_ref, cblk_ref, d_ref, gw_ref, gb_ref,
                        q_ref, k_ref, v_ref, pc_ref, oa_ref, h_ref):
    xb = x_ref[...].astype(BF16)
    mm = lambda lo, hi: _dot(xb, w_ref[:, lo:hi])
    u = mm(_PROJ_SPLITS[0], _PROJ_SPLITS[1])
    q_ref[...] = mm(_PROJ_SPLITS[1], _PROJ_SPLITS[2])
    k_ref[...] = mm(_PROJ_SPLITS[2], _PROJ_SPLITS[3])
    v_ref[...] = mm(_PROJ_SPLITS[3], _PROJ_SPLITS[4])
    pc_ref[...] = mm(_PROJ_SPLITS[4], _PROJ_SPLITS[5])
    bu = _dot(u.astype(BF16), bblk_ref[...])
    bu_re = bu[:, 0:N_ST]
    bu_im = bu_re + bu[:, N_ST:2 * N_ST]
    lr, li = lam_ref[0:1, :], lam_ref[1:2, :]
    hr0, hi0 = h0_ref[0], h0_ref[1]
    hr = bu_re + (lr * hr0 - li * hi0)
    hi = bu_im + (lr * hi0 + li * hr0)
    h_ref[0] = hr
    h_ref[1] = hi
    y = (_dot((hr + hi).astype(BF16), cblk_ref[0:N_ST, :])
         - _dot(hi.astype(BF16), cblk_ref[N_ST:2 * N_ST, :])) + d_ref[...] * u
    z = _gelu(y)
    oa_ref[...] = z * _sigmoid(_dot(z.astype(BF16), gw_ref[...]) + gb_ref[...])


def _sample_proj(x, w_in, h0, lam2, bblk, cblk, d, gw, gb):
    n = x.shape[0]
    widths = (W_B, W_B, W_B, C_SHIFT, W_A)
    return pl.pallas_call(
        _sample_proj_kernel,
        out_shape=[jax.ShapeDtypeStruct((n, w), F32) for w in widths]
        + [jax.ShapeDtypeStruct((2, n, N_ST), F32)],
        compiler_params=_cp(None),
        name="sample_proj",
    )(x, w_in, h0, lam2, bblk, cblk, d.reshape(1, -1), gw.astype(BF16), gb.reshape(1, -1))


def _sample_mix_kernel(q_ref, kn_ref, vn_ref, pc_ref, sh_ref, kc_ref, vc_ref, s0_ref, lb_ref, lb0_ref,
                       mu_ref, w0_ref, w2_ref, a0_ref, a2_ref, g2_ref, kk_ref, ka_ref,
                       rk_ref, lng_ref, lnb_ref, mseg_ref,
                       ob_ref, oc_ref, s_ref):
    rows = 8
    lane = lax.broadcasted_iota(jnp.int32, (rows, W_B), 1)
    sub = lax.broadcasted_iota(jnp.int32, (rows, W_B), 0)
    hmask = (lane // HEAD_DIM) == sub
    rnd = lambda t: t.astype(BF16).astype(F32)
    q = q_ref[0] * (HEAD_DIM ** -0.5)
    qrows = jnp.where(hmask, jnp.broadcast_to(q, (rows, W_B)), 0.0).astype(BF16)
    kcb = kc_ref[0].astype(BF16)
    vcb = vc_ref[0].astype(BF16)
    knb = rnd(kn_ref[0])
    vnb = rnd(vn_ref[0])
    s_all = _dot(qrows, kcb)
    s_new = jnp.sum(qrows.astype(F32) * knb, axis=-1, keepdims=True)
    outs, lses = [], []
    for br in range(len(BRANCHES)):
        s = s_all + lb_ref[br]
        s0 = s_new + lb0_ref[br]
        m = jnp.maximum(jnp.max(s, axis=-1, keepdims=True), s0)
        lse = m + jnp.log(jnp.sum(jnp.exp(s - m), axis=-1, keepdims=True) + jnp.exp(s0 - m))
        o = _dot_nt(jnp.exp(s - lse).astype(BF16), vcb) + rnd(jnp.exp(s0 - lse)) * vnb
        outs.append(rnd(o))
        lses.append(lse)
    top = jnp.maximum(jnp.maximum(lses[0], lses[1]), lses[2])
    ex = [jnp.exp(t - top) for t in lses]
    tot = ex[0] + ex[1] + ex[2]
    o = rnd(ex[0] / tot) * outs[0] + rnd(ex[1] / tot) * outs[1] + rnd(ex[2] / tot) * outs[2]
    ob_ref[0] = jnp.sum(jnp.where(hmask, o, 0.0), axis=0, keepdims=True)

    prm = (mu_ref[...], w0_ref[...], w2_ref[...], a0_ref[...], a2_ref[...], g2_ref[...],
           kk_ref[...], ka_ref[...], rk_ref[...], lng_ref[...], lnb_ref[...], mseg_ref[...])
    pc = jnp.broadcast_to(pc_ref[0], (rows, C_SHIFT))
    prev = jnp.broadcast_to(sh_ref[0], (rows, C_SHIFT))
    seg_sum = lambda t: _dot(t, mseg_ref[...], HI)
    r, k2, v, kk, a, g, logw = _rwkv_pre(pc, prev, prm, seg_sum)
    w = jnp.exp(logw)
    eye = (lax.broadcasted_iota(jnp.int32, (HEAD_DIM, HEAD_DIM), 0)
           == lax.broadcasted_iota(jnp.int32, (HEAD_DIM, HEAD_DIM), 1))
    ys = []
    for h in range(H_C):
        sl = slice(h * HEAD_DIM, (h + 1) * HEAD_DIM)
        row = lambda t: t[0:1, sl]
        col = lambda t: jnp.sum(jnp.where(eye, jnp.broadcast_to(row(t), (HEAD_DIM, HEAD_DIM)), 0.0),
                                axis=1, keepdims=True)
        s0 = s0_ref[0, h]
        sa = jnp.sum(rnd(s0) * rnd(-row(kk)), axis=1, keepdims=True)
        s1 = s0 * row(w) + sa * (row(kk) * row(a)) + col(v) * row(k2)
        s_ref[0, h] = s1
        y_col = jnp.sum(rnd(s1) * rnd(row(r)), axis=1, keepdims=True)
        ys.append(jnp.sum(jnp.where(eye, jnp.broadcast_to(y_col, (HEAD_DIM, HEAD_DIM)), 0.0),
                          axis=0, keepdims=True))
    y = jnp.broadcast_to(jnp.concatenate(ys, axis=1), (rows, W_C))
    oc_ref[0] = _rwkv_post(y, r, k2, v, g, prm, seg_sum)[0:1, :]


def _sample_mix(q, kn, vn, pc, shift0, k_cache, v_cache, s0, lb_rows, lb0, prm, layer):
    n = q.shape[0]
    l_buf = k_cache.shape[3]
    per_b = lambda w: pl.BlockSpec((1, 1, w), lambda b: (b, 0, 0))
    const = lambda a: pl.BlockSpec(a.shape, lambda b: (0,) * a.ndim)
    r3 = lambda t: t.reshape(n, 1, t.shape[-1])
    cache = pl.BlockSpec((None, 1, W_B, l_buf), lambda b: (layer, b, 0, 0))
    return pl.pallas_call(
        _sample_mix_kernel,
        out_shape=[jax.ShapeDtypeStruct((n, 1, W_B), F32),
                   jax.ShapeDtypeStruct((n, 1, W_C), F32),
                   jax.ShapeDtypeStruct((n, H_C, HEAD_DIM, HEAD_DIM), F32)],
        grid=(n,),
        in_specs=[per_b(W_B), per_b(W_B), per_b(W_B), per_b(C_SHIFT), per_b(C_SHIFT),
                  cache, cache,
                  pl.BlockSpec((1, H_C, HEAD_DIM, HEAD_DIM), lambda b: (b, 0, 0, 0)),
                  const(lb_rows), const(lb0)] + [const(a) for a in prm],
        out_specs=[per_b(W_B), per_b(W_C),
                   pl.BlockSpec((1, H_C, HEAD_DIM, HEAD_DIM), lambda b: (b, 0, 0, 0))],
        compiler_params=_cp(("parallel",)),
        name="sample_mix",
    )(r3(q), r3(kn), r3(vn), r3(pc), r3(shift0), k_cache, v_cache, s0, lb_rows, lb0, *prm)


def _sample_moe_kernel(x1_ref, gd_ref, wup_ref, bup_ref, wdn_ref, bdn_ref, g_ref, b_ref, o_ref, acc):
    e = pl.program_id(0)

    @pl.when(e == 0)
    def _():
        acc[...] = jnp.zeros_like(acc)

    x1 = x1_ref[...]
    h = _dot(x1.astype(BF16), wup_ref[...].astype(BF16)) + bup_ref[...]
    y = _dot(_swiglu(h).astype(BF16), wdn_ref[...].astype(BF16)) + bdn_ref[...]
    acc[...] += gd_ref[...] * y

    @pl.when(e == pl.num_programs(0) - 1)
    def _():
        o_ref[...] = _layernorm_rows(ALPHA * x1 + acc[...], g_ref[...], b_ref[...])


def _sample_moe(x1, top_idx, gates, p, layer):
    n = x1.shape[0]
    onehot = top_idx[:, :, None] == jnp.arange(N_EXPERTS, dtype=jnp.int32)[None, None, :]
    gd = jnp.sum(jnp.where(onehot, gates[:, :, None], 0.0), axis=1)
    gd = gd.T.reshape(N_EXPERTS, n, 1)
    return pl.pallas_call(
        _sample_moe_kernel,
        out_shape=jax.ShapeDtypeStruct((n, D_MODEL), F32),
        grid=(N_EXPERTS,),
        in_specs=[pl.BlockSpec((n, D_MODEL), lambda e: (0, 0)),
                  pl.BlockSpec((None, n, 1), lambda e: (e, 0, 0)),
                  pl.BlockSpec((None, None, D_MODEL, 2 * D_FF), lambda e: (layer, e, 0, 0)),
                  pl.BlockSpec((None, None, 1, 2 * D_FF), lambda e: (layer, e, 0, 0)),
                  pl.BlockSpec((None, None, D_FF, D_MODEL), lambda e: (layer, e, 0, 0)),
                  pl.BlockSpec((None, None, 1, D_MODEL), lambda e: (layer, e, 0, 0)),
                  pl.BlockSpec((1, D_MODEL), lambda e: (0, 0)),
                  pl.BlockSpec((1, D_MODEL), lambda e: (0, 0))],
        out_specs=pl.BlockSpec((n, D_MODEL), lambda e: (0, 0)),
        scratch_shapes=[pltpu.VMEM((n, D_MODEL), F32)],
        compiler_params=_cp(("arbitrary",)),
        name="sample_moe",
    )(x1, gd, p["moe_w_up"], p["moe_b_up"].reshape(DEPTH, N_EXPERTS, 1, 2 * D_FF),
      p["moe_w_down"], p["moe_b_down"].reshape(DEPTH, N_EXPERTS, 1, D_MODEL),
      p["ln2_g"][layer].reshape(1, -1), p["ln2_b"][layer].reshape(1, -1))


def _prompt_layer(x, nb, t, p, l, s5p, rwp, tiles, kv_prev):
    u, q, k_all, v_all, pc = _proj(x, p["w_in"][l].astype(BF16), nb, l, kv_prev)
    shp = lambda a: a.reshape(nb, t, a.shape[-1])
    o_a, h_t = _s5_prompt(shp(u), *s5p, p["ssm_d"][l], p["ssm_glu_w"][l], p["ssm_glu_b"][l])
    o_b = _attn_prompt(shp(q), k_all, v_all, tiles, l)
    o_c, s_t = _rwkv_prompt(shp(pc), rwp)
    flat = lambda a: a.reshape(nb * t, a.shape[-1])
    x1, top_idx, gates = _outproj(x, flat(o_a), flat(o_b), flat(o_c), p["w_out"][l],
                                  p["ln1_g"][l], p["ln1_b"][l], p["moe_router_w"][l],
                                  p["moe_router_b"][l])
    x2 = _moe_prompt(x1, top_idx, gates, p, l)
    ssm = jnp.transpose(h_t.reshape(2, nb, G_A, N_A), (1, 2, 3, 0))
    return x2, (ssm, s_t, shp(pc)[:, t - 1]), (k_all, v_all)


def _sample_bias(rel_bias, l_buf):
    offs = _branch_offsets()
    bias = rel_bias[jnp.asarray(_t5_buckets(offs))]
    rows = []
    for br in range(len(BRANCHES)):
        m = np.arange(1, offs.shape[1])
        pos = l_buf - offs[br][m]
        ok = pos >= 0
        r = jnp.full((H_B, l_buf), NEG_INF, F32).at[:, jnp.asarray(pos[ok])].set(bias[br][jnp.asarray(m[ok])].T)
        rows.append(jnp.concatenate([r, jnp.zeros((8 - H_B, l_buf), F32)], axis=0))
    new = jnp.zeros((len(BRANCHES), 8, 1), F32).at[:, :H_B, 0].set(bias[:, 0])
    return jnp.stack(rows), new


def _sample_layer(x, p, l, s5p, rwp, sbias, k_cache, v_cache, st_ssm, st_rwkv, st_shift):
    n = x.shape[0]
    lam2, bblk, cblk = s5p
    h0 = jnp.transpose(st_ssm.reshape(n, N_ST, 2), (2, 0, 1))
    q, kn, vn, pc, o_a, h1 = _sample_proj(x, p["w_in"][l].astype(BF16), h0, lam2, bblk, cblk, p["ssm_d"][l],
                                          p["ssm_glu_w"][l], p["ssm_glu_b"][l])
    lb_rows, lb0 = sbias
    o_b, o_c, s1 = _sample_mix(q, kn, vn, pc, st_shift, k_cache, v_cache, st_rwkv, lb_rows, lb0, rwp, l)
    x1, top_idx, gates = _outproj(x, o_a, o_b.reshape(n, W_B), o_c.reshape(n, W_C), p["w_out"][l],
                                  p["ln1_g"][l], p["ln1_b"][l], p["moe_router_w"][l],
                                  p["moe_router_b"][l])
    x2 = _sample_moe(x1, top_idx, gates, p, l)
    ssm = jnp.transpose(h1.reshape(2, n, G_A, N_A), (1, 2, 3, 0))
    return x2, (kn, vn, ssm, s1, pc)


def _shift_kernel(c_ref, new_ref, o_ref):
    x = c_ref[...]
    length = x.shape[1]
    lane = lax.broadcasted_iota(jnp.int32, x.shape, 1)
    o_ref[...] = jnp.where(lane == length - 1, new_ref[...], pltpu.roll(x, length - 1, axis=1))


def _shift_cache(cache_t, new):
    rows, length = cache_t.shape
    blk = min(ROW_TILE, rows)
    return pl.pallas_call(
        _shift_kernel,
        out_shape=jax.ShapeDtypeStruct((rows, length), F32),
        grid=(rows // blk,),
        in_specs=[pl.BlockSpec((blk, length), lambda i: (i, 0)),
                  pl.BlockSpec((blk, 1), lambda i: (i, 0))],
        out_specs=pl.BlockSpec((blk, length), lambda i: (i, 0)),
        compiler_params=_cp(("parallel",)),
        name="shift_cache",
    )(cache_t, new)


def kernel(x_prompt, x_sample, cache_attn_k, cache_attn_v, state_ssm, state_rwkv, state_shift, w_in, w_out, ln1_g, ln1_b, ln2_g, ln2_b, ssm_a_re, ssm_a_im, ssm_log_dt, ssm_b_re, ssm_b_im, ssm_c_re, ssm_c_im, ssm_d, ssm_glu_w, ssm_glu_b, rel_bias, rwkv_mu, rwkv_w0, rwkv_w2, rwkv_a0, rwkv_a2, rwkv_g2, rwkv_k_k, rwkv_k_a, rwkv_r_k, rwkv_ln_g, rwkv_ln_b, moe_router_w, moe_router_b, moe_w_up, moe_b_up, moe_w_down, moe_b_down):
    p = dict(w_in=w_in, w_out=w_out, ln1_g=ln1_g, ln1_b=ln1_b, ln2_g=ln2_g, ln2_b=ln2_b,
             ssm_d=ssm_d, ssm_glu_w=ssm_glu_w, ssm_glu_b=ssm_glu_b,
             moe_router_w=moe_router_w, moe_router_b=moe_router_b, moe_w_up=moe_w_up,
             moe_b_up=moe_b_up, moe_w_down=moe_w_down, moe_b_down=moe_b_down)
    nb, t, _ = x_prompt.shape
    ns = x_sample.shape[0]
    l_buf = cache_attn_k.shape[2]
    table = _log_bias_table(rel_bias)
    tiles = _bias_tiles(table, min(t, WIN_MAX) // Q_TILE + 1)
    sbias = _sample_bias(rel_bias, l_buf)
    xp = x_prompt.reshape(nb * t, D_MODEL)
    xs = x_sample.reshape(ns, D_MODEL)
    st_p, st_s = [], []
    time_minor = lambda c: jnp.transpose(c, (0, 1, 3, 4, 2)).reshape(DEPTH, c.shape[1], W_B, c.shape[2])
    time_major = lambda c, n: jnp.transpose(c.reshape(DEPTH, n, H_B, HEAD_DIM, c.shape[-1]), (0, 1, 4, 2, 3))
    kc_t, vc_t = time_minor(cache_attn_k), time_minor(cache_attn_v)
    kv = (jnp.zeros((DEPTH, nb, W_B, t), F32), jnp.zeros((DEPTH, nb, W_B, t), F32))
    for l in range(DEPTH):
        s5p, s5s = _s5_params(ssm_a_re[l], ssm_a_im[l], ssm_log_dt[l], ssm_b_re[l], ssm_b_im[l],
                              ssm_c_re[l], ssm_c_im[l])
        rwp = _rwkv_params(dict(rwkv_mu=rwkv_mu[l], rwkv_w0=rwkv_w0[l], rwkv_w2=rwkv_w2[l],
                                rwkv_a0=rwkv_a0[l], rwkv_a2=rwkv_a2[l], rwkv_g2=rwkv_g2[l],
                                rwkv_k_k=rwkv_k_k[l], rwkv_k_a=rwkv_k_a[l], rwkv_r_k=rwkv_r_k[l],
                                rwkv_ln_g=rwkv_ln_g[l], rwkv_ln_b=rwkv_ln_b[l]))
        xp, sp, kv = _prompt_layer(xp, nb, t, p, l, s5p, rwp, tiles, kv)
        xs, ss = _sample_layer(xs, p, l, s5s, rwp, sbias, kc_t, vc_t,
                               state_ssm[l], state_rwkv[l], state_shift[l])
        st_p.append(sp)
        st_s.append(ss)
    p_ssm, p_rwkv, p_shift = (jnp.stack(z) for z in zip(*st_p))
    kn, vn, s_ssm, s_rwkv, s_shift = (jnp.stack(z) for z in zip(*st_s))
    keep_p = min(WIN_MAX, t)
    p_k, p_v = (time_major(a[..., t - keep_p:], nb) for a in kv)
    keep = min(WIN_MAX, l_buf + 1)
    drop = l_buf + 1 - keep
    if drop == 1:
        advance = lambda c_t, new: _shift_cache(c_t.reshape(-1, l_buf), new.reshape(-1, 1)).reshape(c_t.shape)
    else:
        advance = lambda c_t, new: jnp.concatenate([c_t[..., drop:], new[..., None]], axis=-1)
    s_k = time_major(advance(kc_t, kn), ns)
    s_v = time_major(advance(vc_t, vn), ns)
    return (xp.reshape(nb, t, D_MODEL), xs.reshape(ns, 1, D_MODEL),
            p_k, p_v, p_ssm, p_rwkv, p_shift, s_k, s_v, s_ssm, s_rwkv, s_shift)
```

```python
import functools
import math

import numpy as np
import jax
import jax.numpy as jnp
from jax import lax
from jax.experimental import pallas as pl
from jax.experimental.pallas import tpu as pltpu

F32 = jnp.float32
BF16 = jnp.bfloat16
HI = lax.Precision.HIGHEST

D_MODEL = 1024
DEPTH = 2
HEAD_DIM = 64
W_A = 256
C_GRP = 16
G_A = 16
N_A = 64
N_ST = G_A * N_A
W_B = 384
H_B = 6
BRANCHES = ((128, 1), (512, 4), (2048, 16))
WIN_MAX = 2048
N_BUCKETS = 32
MAX_DIST = WIN_MAX
NEG_INF = -1e30
W_C = 384
H_C = 6
R_W = 32
R_A = 32
R_G = 64
C_SHIFT = 3 * W_C + R_W + R_A + R_G
N_IN = W_A + 3 * W_B + C_SHIFT
GN_EPS = 64e-5
N_EXPERTS = 32
TOP_K = 4
D_FF = D_MODEL
SWIGLU_ALPHA = 1.702
SWIGLU_LIMIT = 7.0
ALPHA = (2 * DEPTH) ** 0.25
LN_EPS = 1e-5

LANES = 128
Q_TILE = 128
RWKV_CHUNK = 64
RWKV_SEQS = 2
S5_CHUNK = 64
ROW_TILE = 512
MOE_ROWS = 256
COMBINE_TOK = 128
VMEM_LIMIT = 56 * 1024 * 1024


def _cp(sem, vmem=VMEM_LIMIT):
    return pltpu.CompilerParams(dimension_semantics=sem, vmem_limit_bytes=vmem)


def _dot(a, b, precision=None):
    return jnp.dot(a, b, preferred_element_type=F32, precision=precision)


def _dot_nt(a, b, precision=None):
    return lax.dot_general(a, b, (((1,), (1,)), ((), ())),
                           preferred_element_type=F32, precision=precision)


def _dot_tn(a, b, precision=None):
    return lax.dot_general(a, b, (((0,), (0,)), ((), ())),
                           preferred_element_type=F32, precision=precision)


def _layernorm_rows(x, g, b):
    mu = jnp.mean(x, axis=-1, keepdims=True)
    d = x - mu
    var = jnp.mean(d * d, axis=-1, keepdims=True)
    return d * lax.rsqrt(var + LN_EPS) * g + b


def _sigmoid(x):
    return 1.0 / (1.0 + jnp.exp(-x))


def _softplus(x):
    return jnp.maximum(x, 0.0) + jnp.log(1.0 + jnp.exp(-jnp.abs(x)))


def _gelu(x):
    c = math.sqrt(2.0 / math.pi)
    return 0.5 * x * (1.0 + jnp.tanh(c * (x + 0.044715 * (x * x * x))))


_PROJ_SPLITS = (0, W_A, W_A + W_B, W_A + 2 * W_B, W_A + 3 * W_B, N_IN)


def _proj_kernel(x_ref, w_ref, *refs):
    u_ref, q_ref, k_ref, v_ref, pc_ref = refs[-5:]
    xb = x_ref[...].astype(BF16)
    outs = ((u_ref, False), (q_ref, False), (k_ref, True), (v_ref, True), (pc_ref, False))
    for (o_ref, time_minor), lo, hi in zip(outs, _PROJ_SPLITS[:-1], _PROJ_SPLITS[1:]):
        y = _dot(xb, w_ref[:, lo:hi])
        o_ref[...] = jnp.transpose(y) if time_minor else y


def _proj(x, w_bf16, nb, t, layer, kv_prev):
    n = nb * t
    per_seq = t // ROW_TILE
    widths = [hi - lo for lo, hi in zip(_PROJ_SPLITS[:-1], _PROJ_SPLITS[1:])]
    rows = lambda w: pl.BlockSpec((ROW_TILE, w), lambda i: (i, 0))
    stacked = pl.BlockSpec((None, None, W_B, ROW_TILE), lambda i: (layer, i // per_seq, 0, i % per_seq))
    flat = lambda w: jax.ShapeDtypeStruct((n, w), F32)
    kv = jax.ShapeDtypeStruct((DEPTH, nb, W_B, t), F32)
    prev = list(kv_prev)
    return pl.pallas_call(
        _proj_kernel,
        out_shape=[flat(widths[0]), flat(widths[1]), kv, kv, flat(widths[4])],
        grid=(n // ROW_TILE,),
        in_specs=[pl.BlockSpec((ROW_TILE, D_MODEL), lambda i: (i, 0)),
                  pl.BlockSpec((D_MODEL, N_IN), lambda i: (0, 0))]
        + [pl.BlockSpec(memory_space=pl.ANY)] * len(prev),
        out_specs=[rows(widths[0]), rows(widths[1]), stacked, stacked, rows(widths[4])],
        input_output_aliases={2: 2, 3: 3},
        compiler_params=_cp(("parallel",)),
        name="proj",
    )(x, w_bf16, *prev)


def _s5_kernel(u_ref, lam_ref, bblk_ref, cblk_ref, d_ref, gw_ref, gb_ref,
               o_ref, ht_ref, utm, hs, ytm, h_scr, *, nb, lt):
    c = pl.program_id(0)

    @pl.when(c == 0)
    def _():
        h_scr[...] = jnp.zeros_like(h_scr)

    n_half = W_A // LANES
    for b in range(nb):
        for j in range(n_half):
            utm[j, pl.ds(b, lt, stride=nb), :] = u_ref[b, :, j * LANES:(j + 1) * LANES]
    u_all = jnp.concatenate([utm[j] for j in range(n_half)], axis=1)
    hs[...] = _dot(u_all.astype(BF16), bblk_ref[...])
    lr = jnp.broadcast_to(lam_ref[0:1, :], (nb, N_ST))
    li = jnp.broadcast_to(lam_ref[1:2, :], (nb, N_ST))

    def body(t, carry):
        hr, hi = carry
        row = pl.multiple_of(t * nb, nb)
        br = hs[pl.ds(row, nb), 0:N_ST]
        bi = hs[pl.ds(row, nb), N_ST:2 * N_ST]
        nr = lr * hr - li * hi + br
        ni = lr * hi + li * hr + bi
        hs[pl.ds(row, nb), 0:N_ST] = nr
        hs[pl.ds(row, nb), N_ST:2 * N_ST] = ni
        return nr, ni

    hr, hi = lax.fori_loop(0, lt, body, (h_scr[0], h_scr[1]), unroll=2)
    h_scr[0] = hr
    h_scr[1] = hi
    ht_ref[0] = hr
    ht_ref[1] = hi
    y = _dot(hs[...].astype(BF16), cblk_ref[...]) + d_ref[...] * u_all
    z = _gelu(y)
    gl = _dot(z.astype(BF16), gw_ref[...]) + gb_ref[...]
    res = z * _sigmoid(gl)
    for j in range(n_half):
        ytm[j] = res[:, j * LANES:(j + 1) * LANES]
    for b in range(nb):
        for j in range(n_half):
            o_ref[b, :, j * LANES:(j + 1) * LANES] = ytm[j, pl.ds(b, lt, stride=nb), :]


def _s5_params(a_re, a_im, log_dt, b_re, b_im, c_re, c_im):
    lam = lax.complex(a_re, a_im)
    dt = jnp.exp(log_dt)[:, None]
    lam_bar = jnp.exp(lam * dt)
    b_bar = ((lam_bar - 1.0) / lam)[..., None] * lax.complex(b_re, b_im)
    eye = jnp.eye(G_A, dtype=F32)

    def blk_in(m):
        return jnp.einsum("gnc,gh->gchn", m, eye).reshape(W_A, N_ST)

    def blk_out(m):
        return jnp.einsum("gcn,gh->gnhc", m, eye).reshape(N_ST, W_A)

    bblk = jnp.concatenate([blk_in(b_bar.real), blk_in(b_bar.imag)], axis=1)
    cblk = jnp.concatenate([blk_out(c_re), blk_out(-c_im)], axis=0)
    lam2 = jnp.stack([lam_bar.real.reshape(N_ST), lam_bar.imag.reshape(N_ST)])
    bblk3 = jnp.concatenate([blk_in(b_bar.real), blk_in(b_bar.imag - b_bar.real)], axis=1)
    cblk3 = jnp.concatenate([blk_out(c_re), blk_out(c_re + c_im)], axis=0)
    return (lam2, bblk, cblk), (lam2, bblk3.astype(BF16), cblk3.astype(BF16))


def _s5_prompt(u, lam2, bblk, cblk, d, gw, gb):
    nb, t, _ = u.shape
    lt = min(S5_CHUNK, t)
    kern = functools.partial(_s5_kernel, nb=nb, lt=lt)
    const = lambda shape: pl.BlockSpec(shape, lambda c: (0,) * len(shape))
    return pl.pallas_call(
        kern,
        out_shape=[jax.ShapeDtypeStruct((nb, t, W_A), F32),
                   jax.ShapeDtypeStruct((2, nb, N_ST), F32)],
        grid=(t // lt,),
        in_specs=[pl.BlockSpec((nb, lt, W_A), lambda c: (0, c, 0)),
                  const((2, N_ST)), const((W_A, 2 * N_ST)), const((2 * N_ST, W_A)),
                  const((1, W_A)), const((W_A, W_A)), const((1, W_A))],
        out_specs=[pl.BlockSpec((nb, lt, W_A), lambda c: (0, c, 0)),
                   const((2, nb, N_ST))],
        scratch_shapes=[pltpu.VMEM((W_A // LANES, nb * lt, LANES), F32),
                        pltpu.VMEM((nb * lt, 2 * N_ST), F32),
                        pltpu.VMEM((W_A // LANES, nb * lt, LANES), F32),
                        pltpu.VMEM((2, nb, N_ST), F32)],
        compiler_params=_cp(("arbitrary",)),
        name="s5_prompt",
    )(u, lam2, bblk.astype(BF16), cblk.astype(BF16), d.reshape(1, W_A),
      gw.astype(BF16), gb.reshape(1, W_A))


def _branch_offsets():
    return np.stack([np.arange(w // d + 1) * d for (w, d) in BRANCHES]).astype(np.int32)


def _t5_buckets(dist):
    max_exact = N_BUCKETS // 2
    d = np.maximum(dist, 1).astype(np.float32)
    large = max_exact + (np.log(d / max_exact) / np.log(MAX_DIST / max_exact)
                         * (N_BUCKETS - max_exact)).astype(np.int32)
    return np.where(dist < max_exact, dist, np.minimum(large, N_BUCKETS - 1)).astype(np.int32)


def _log_bias_table(rel_bias):
    offs = _branch_offsets()
    bias = rel_bias[jnp.asarray(_t5_buckets(offs))]
    table = jnp.full((WIN_MAX + 1, H_B), -jnp.inf, F32)
    for br in range(len(BRANCHES)):
        idx = jnp.asarray(offs[br])
        table = table.at[idx].set(jnp.logaddexp(table[idx], bias[br]))
    return jnp.maximum(table, NEG_INF)


def _bias_tiles(table, n_diff):
    per = 2 * Q_TILE
    m = np.arange(per)
    dist = np.arange(n_diff)[:, None] * Q_TILE + np.where(m < Q_TILE, -m, per - m)[None, :]
    ok = (dist >= 0) & (dist <= WIN_MAX) & (m != Q_TILE)[None, :]
    v = jnp.where(jnp.asarray(ok)[..., None], table[jnp.asarray(np.clip(dist, 0, WIN_MAX))], NEG_INF)
    v = jnp.transpose(v, (2, 0, 1))
    flat = jnp.broadcast_to(v[:, :, None, :], (H_B, n_diff, Q_TILE, per)).reshape(H_B, n_diff, Q_TILE * per)
    tiles = flat[:, :, :Q_TILE * (per - 1)].reshape(H_B, n_diff, Q_TILE, per - 1)[..., :Q_TILE]
    tiles = tiles.reshape(H_B // 2, 2, n_diff, Q_TILE, Q_TILE)
    return jnp.transpose(tiles, (0, 2, 1, 3, 4)).reshape(H_B // 2, n_diff, 2 * Q_TILE, Q_TILE)


def _attn_kernel(q_ref, k_ref, v_ref, bias_ref, o_ref, kb, vb, *, nq):
    kb[...] = k_ref[...].astype(BF16)
    vb[...] = v_ref[...].astype(BF16)
    lo = lax.broadcasted_iota(jnp.int32, (Q_TILE, LANES), 1) < HEAD_DIM
    for qi in range(nq):
        rows = slice(qi * Q_TILE, (qi + 1) * Q_TILE)
        q = q_ref[0, rows, :] * (HEAD_DIM ** -0.5)
        q2 = jnp.concatenate([jnp.where(lo, q, 0.0), jnp.where(lo, 0.0, q)], axis=0).astype(BF16)
        nk = (qi + 1) * Q_TILE
        bias = jnp.concatenate([bias_ref[qi - kj] for kj in range(qi + 1)], axis=1)
        s = _dot(q2, kb[:, 0:nk]) + bias
        m = jnp.max(s, axis=-1, keepdims=True)
        p = jnp.exp(s - m)
        l = jnp.sum(p, axis=-1, keepdims=True)
        o = _dot_nt(p.astype(BF16), vb[:, 0:nk]) / l
        o_ref[0, rows, :] = jnp.where(lo, o[0:Q_TILE], o[Q_TILE:2 * Q_TILE])


def _attn_prompt(q, k_all, v_all, tiles, layer):
    nb, t, _ = q.shape
    n_diff = tiles.shape[1]
    hp = W_B // LANES
    kern = functools.partial(_attn_kernel, nq=t // Q_TILE)
    seq = pl.BlockSpec((1, t, LANES), lambda b, p: (b, 0, p))
    kv = pl.BlockSpec((None, None, LANES, t), lambda b, p: (layer, b, p, 0))
    return pl.pallas_call(
        kern,
        out_shape=jax.ShapeDtypeStruct((nb, t, W_B), F32),
        grid=(nb, hp),
        in_specs=[seq, kv, kv,
                  pl.BlockSpec((None, n_diff, 2 * Q_TILE, Q_TILE), lambda b, p: (p, 0, 0, 0))],
        out_specs=seq,
        scratch_shapes=[pltpu.VMEM((LANES, t), BF16), pltpu.VMEM((LANES, t), BF16)],
        compiler_params=_cp(("parallel", "parallel")),
        name="attn_prompt",
    )(q, k_all, v_all, tiles)


def _split_dot(x, m_bf16, parts):
    acc = None
    rem = x
    for i in range(parts):
        piece = rem.astype(BF16)
        term = _dot(piece, m_bf16)
        acc = term if acc is None else acc + term
        if i + 1 < parts:
            rem = rem - piece.astype(F32)
    return acc


def _split_dot_left(m_bf16, x, parts):
    acc = None
    rem = x
    for i in range(parts):
        piece = rem.astype(BF16)
        term = _dot(m_bf16, piece)
        acc = term if acc is None else acc + term
        if i + 1 < parts:
            rem = rem - piece.astype(F32)
    return acc


def _rwkv_pre(pc, prev, prm, seg_sum):
    (mu, w0, w2p, a0, a2p, g2p, k_k, k_a, r_k, ln_g, ln_b, mseg) = prm
    xs = pc + (prev - pc) * mu
    r = xs[:, 0:W_C]
    k = xs[:, W_C:2 * W_C]
    v = xs[:, 2 * W_C:3 * W_C]
    tail = xs[:, 3 * W_C:C_SHIFT]
    w_raw = -_softplus(-(w0 + _dot(jnp.tanh(tail).astype(BF16), w2p))) - 0.5
    a = _sigmoid(a0 + _dot(tail.astype(BF16), a2p))
    g = _dot(_sigmoid(tail).astype(BF16), g2p)
    kk = k * k_k
    nrm = jnp.sqrt(seg_sum(kk * kk))
    kk = kk / jnp.maximum(nrm, 1e-12)
    k2 = k * (1.0 + (a - 1.0) * k_a)
    logw = -jnp.exp(w_raw)
    return r, k2, v, kk, a, g, logw


def _rwkv_post(y, r, k2, v, g, prm, seg_sum):
    (mu, w0, w2p, a0, a2p, g2p, k_k, k_a, r_k, ln_g, ln_b, mseg) = prm
    mean = seg_sum(y) * (1.0 / HEAD_DIM)
    d = y - mean
    var = seg_sum(d * d) * (1.0 / HEAD_DIM)
    yn = d * lax.rsqrt(var + GN_EPS) * ln_g + ln_b
    bonus = seg_sum(r * k2 * r_k) * v
    return (yn + bonus) * g


def _rwkv_kernel(pc_ref, mu_ref, w0_ref, w2_ref, a0_ref, a2_ref, g2_ref, kk_ref, ka_ref,
                 rk_ref, lng_ref, lnb_ref, mseg_ref, ltri_ref,
                 o_ref, s_ref, s_scr, prev_scr, yt_scr, *, ch, nseq):
    c = pl.program_id(1)
    rows = nseq * ch

    @pl.when(c == 0)
    def _():
        s_scr[...] = jnp.zeros_like(s_scr)
        prev_scr[...] = jnp.zeros_like(prev_scr)
        yt_scr[...] = jnp.zeros_like(yt_scr)

    prm = (mu_ref[...], w0_ref[...], w2_ref[...], a0_ref[...], a2_ref[...], g2_ref[...],
           kk_ref[...], ka_ref[...], rk_ref[...], lng_ref[...], lnb_ref[...], None)
    mseg = mseg_ref[...]
    seg_sum = lambda t: _split_dot(t, mseg, 3)
    pc = pc_ref[...].reshape(rows, C_SHIFT)
    rowi = lax.broadcasted_iota(jnp.int32, (rows, C_SHIFT), 0)
    prev = pltpu.roll(pc, 1, axis=0)
    for b in range(nseq):
        prev = jnp.where(rowi == b * ch, prev_scr[b:b + 1, :], prev)
        prev_scr[b:b + 1, :] = pc[(b + 1) * ch - 1:(b + 1) * ch, :]
    r, k2, v, kk, a, g, logw = _rwkv_pre(pc, prev, prm, seg_sum)

    cs = _split_dot_left(ltri_ref[...], logw, 3)
    g_in = jnp.exp(cs)
    g_ex = jnp.exp(cs - logw)
    g_inv = jnp.exp(-cs)
    al = (-kk * g_ex).astype(BF16)
    rt = (r * g_in).astype(BF16)
    bh = kk * a * g_inv
    kh = k2 * g_inv
    vb = v.astype(BF16)

    si = lax.broadcasted_iota(jnp.int32, (ch, ch), 0)
    ti = lax.broadcasted_iota(jnp.int32, (ch, ch), 1)
    strict = si < ti
    incl = si <= ti
    n_sq = int(math.log2(ch))
    units = [(b, h) for b in range(nseq) for h in range(H_C)]
    idx = range(len(units))
    cut = lambda t, u: t[units[u][0] * ch:(units[u][0] + 1) * ch,
                         units[u][1] * HEAD_DIM:(units[u][1] + 1) * HEAD_DIM]
    g_end = [cut(g_in, u)[ch - 1:ch, :] for u in idx]
    al_u = [cut(al, u) for u in idx]
    rt_u = [cut(rt, u) for u in idx]
    v_u = [cut(vb, u) for u in idx]
    bh_u = [cut(bh, u) for u in idx]
    kh_u = [cut(kh, u) for u in idx]
    bc = [(bh_u[u] * g_end[u]).astype(BF16) for u in idx]
    kc = [(kh_u[u] * g_end[u]).astype(BF16) for u in idx]
    s0 = [s_scr[b, h] for (b, h) in units]
    s0b = [t.astype(BF16) for t in s0]
    bk = [jnp.concatenate([bh_u[u], kh_u[u]], axis=0).astype(BF16) for u in idx]
    ga = [_dot_nt(bk[u], al_u[u]) for u in idx]
    gr = [_dot_nt(bk[u], rt_u[u]) for u in idx]
    p0 = [_dot_nt(s0b[u], al_u[u]) for u in idx]
    rs = [_dot_nt(s0b[u], rt_u[u]) for u in idx]
    a_t = [jnp.where(strict, ga[u][0:ch], 0.0) for u in idx]
    b_t = [jnp.where(strict, ga[u][ch:2 * ch], 0.0).astype(BF16) for u in idx]
    rb_t = [jnp.where(incl, gr[u][0:ch], 0.0).astype(BF16) for u in idx]
    rk_t = [jnp.where(incl, gr[u][ch:2 * ch], 0.0).astype(BF16) for u in idx]
    bv = [_dot_tn(v_u[u], b_t[u]) for u in idx]
    rv = [_dot_tn(v_u[u], rk_t[u]) for u in idx]
    sv = [_dot_tn(v_u[u], kc[u]) for u in idx]
    x_t = [p0[u] + bv[u] for u in idx]
    for it in range(n_sq):
        a_b = [a_t[u].astype(BF16) for u in idx]
        if it + 1 < n_sq:
            both = [_dot(jnp.concatenate([x_t[u], a_t[u]], axis=0).astype(BF16), a_b[u]) for u in idx]
            x_t = [x_t[u] + both[u][0:HEAD_DIM] for u in idx]
            a_t = [both[u][HEAD_DIM:HEAD_DIM + ch] for u in idx]
        else:
            step = [_dot(x_t[u].astype(BF16), a_b[u]) for u in idx]
            x_t = [x_t[u] + step[u] for u in idx]
    x_b = [t.astype(BF16) for t in x_t]
    yx = [_dot(x_b[u], rb_t[u]) for u in idx]
    sx = [_dot(x_b[u], bc[u]) for u in idx]
    for u, (b, h) in enumerate(units):
        s_new = s0[u] * g_end[u] + sx[u] + sv[u]
        yt_scr[h * HEAD_DIM:(h + 1) * HEAD_DIM, b * ch:(b + 1) * ch] = rs[u] + yx[u] + rv[u]
        s_scr[b, h] = s_new
        s_ref[b, h] = s_new

    y = jnp.transpose(yt_scr[...])[0:rows, :]
    o_ref[...] = _rwkv_post(y, r, k2, v, g, prm, seg_sum).reshape(nseq, ch, W_C)


def _rwkv_params(p):
    pad = lambda w, lo: jnp.zeros((R_W + R_A + R_G, W_C), BF16).at[lo:lo + w.shape[0]].set(w.astype(BF16))
    seg = np.arange(W_C) // HEAD_DIM
    mseg = jnp.asarray((seg[:, None] == seg[None, :]).astype(np.float32))
    row = lambda t: t.reshape(1, -1)
    return (row(p["rwkv_mu"]), row(p["rwkv_w0"]), pad(p["rwkv_w2"], 0), row(p["rwkv_a0"]),
            pad(p["rwkv_a2"], R_W), pad(p["rwkv_g2"], R_W + R_A), row(p["rwkv_k_k"]),
            row(p["rwkv_k_a"]), row(p["rwkv_r_k"]), row(p["rwkv_ln_g"]), row(p["rwkv_ln_b"]), mseg)


def _rwkv_prompt(pc, prm):
    nb, t, _ = pc.shape
    ch = min(RWKV_CHUNK, t)
    nseq = max(1, min(RWKV_SEQS * LANES // ch, nb))
    assert nb % nseq == 0
    ltri = np.kron(np.eye(nseq, dtype=np.float32), np.tril(np.ones((ch, ch), np.float32)))
    ltri = jnp.asarray(ltri).astype(BF16)
    kern = functools.partial(_rwkv_kernel, ch=ch, nseq=nseq)
    const = lambda a: pl.BlockSpec(a.shape, lambda b, c: (0,) * a.ndim)
    args = list(prm[:-1]) + [prm[-1].astype(BF16), ltri]
    return pl.pallas_call(
        kern,
        out_shape=[jax.ShapeDtypeStruct((nb, t, W_C), F32),
                   jax.ShapeDtypeStruct((nb, H_C, HEAD_DIM, HEAD_DIM), F32)],
        grid=(nb // nseq, t // ch),
        in_specs=[pl.BlockSpec((nseq, ch, C_SHIFT), lambda b, c: (b, c, 0))] + [const(a) for a in args],
        out_specs=[pl.BlockSpec((nseq, ch, W_C), lambda b, c: (b, c, 0)),
                   pl.BlockSpec((nseq, H_C, HEAD_DIM, HEAD_DIM), lambda b, c: (b, 0, 0, 0))],
        scratch_shapes=[pltpu.VMEM((nseq, H_C, HEAD_DIM, HEAD_DIM), F32),
                        pltpu.VMEM((nseq, C_SHIFT), F32),
                        pltpu.VMEM((W_C, -(-nseq * ch // LANES) * LANES), F32)],
        compiler_params=_cp(("parallel", "arbitrary")),
        name="rwkv_prompt",
    )(pc, *args)


def _route(x1, rw, rb):
    logits = _dot(x1.astype(BF16), rw) + rb
    lane = lax.broadcasted_iota(jnp.int32, logits.shape, 1).astype(F32)
    vals, idxs = [], []
    cur = logits
    for _ in range(TOP_K):
        m = jnp.max(cur, axis=-1, keepdims=True)
        idx = jnp.min(jnp.where(cur == m, lane, float(N_EXPERTS)), axis=-1, keepdims=True)
        vals.append(m)
        idxs.append(idx)
        cur = jnp.where(lane == idx, -jnp.inf, cur)
    ex = [jnp.exp(v - vals[0]) for v in vals]
    tot = ex[0] + ex[1] + ex[2] + ex[3]
    gates = jnp.concatenate([e / tot for e in ex], axis=-1)
    return jnp.concatenate(idxs, axis=-1).astype(jnp.int32), gates


def _outproj_kernel(x_ref, oa_ref, ob_ref, oc_ref, w_ref, g_ref, b_ref, rw_ref, rb_ref, *refs):
    x1_ref, idx_ref, gate_ref = refs[-3:]
    mix = (_dot(oa_ref[...].astype(BF16), w_ref[0:W_A, :])
           + _dot(ob_ref[...].astype(BF16), w_ref[W_A:W_A + W_B, :])
           + _dot(oc_ref[...].astype(BF16), w_ref[W_A + W_B:, :]))
    x1 = _layernorm_rows(ALPHA * x_ref[...] + mix, g_ref[...], b_ref[...])
    x1_ref[...] = x1
    idx, gates = _route(x1, rw_ref[...], rb_ref[...])
    idx_ref[...] = idx
    gate_ref[...] = gates


def _outproj(x, oa, ob, oc, w, g, b, rw, rb, bufs, row0):
    n = oa.shape[0]
    tm = min(ROW_TILE, n)
    assert row0 % tm == 0
    rows = lambda w_: pl.BlockSpec((tm, w_), lambda i: (i, 0))
    shifted = lambda w_: pl.BlockSpec((tm, w_), lambda i: (i + row0 // tm, 0))
    const = lambda a: pl.BlockSpec(a.shape, lambda i: (0,) * a.ndim)
    consts = [w.astype(BF16), g.reshape(1, -1), b.reshape(1, -1), rw.astype(BF16), rb.reshape(1, -1)]
    n_in = 4 + len(consts)
    return pl.pallas_call(
        _outproj_kernel,
        out_shape=[jax.ShapeDtypeStruct(a.shape, a.dtype) for a in bufs],
        grid=(n // tm,),
        in_specs=[rows(D_MODEL), rows(W_A), rows(W_B), rows(W_C)] + [const(a) for a in consts]
        + [pl.BlockSpec(memory_space=pl.ANY)] * len(bufs),
        out_specs=[shifted(D_MODEL), shifted(TOP_K), shifted(TOP_K)],
        input_output_aliases={n_in + j: j for j in range(len(bufs))},
        compiler_params=_cp(("parallel",)),
        name="outproj",
    )(x, oa, ob, oc, *consts, *bufs)


def _dest_kernel(idx_ref, ltri_ref, utri_ref, dest_ref, cnt_ref, cnt_scr, run_scr, start_scr):
    ph = pl.program_id(0)
    i = pl.program_id(1)
    idx = idx_ref[...]
    lane = lax.broadcasted_iota(jnp.int32, (idx.shape[0], N_EXPERTS), 1)
    hot = [idx[:, k:k + 1] == lane for k in range(TOP_K)]
    multi = jnp.zeros(lane.shape, F32)
    for k in range(TOP_K):
        multi = multi + jnp.where(hot[k], 1.0, 0.0)
    tile_cnt = jnp.sum(multi, axis=0, keepdims=True)

    @pl.when(jnp.logical_and(ph == 0, i == 0))
    def _():
        cnt_scr[...] = jnp.zeros_like(cnt_scr)

    @pl.when(ph == 0)
    def _():
        cnt_scr[...] += tile_cnt

    @pl.when(jnp.logical_and(ph == 1, i == 0))
    def _():
        cnt = cnt_scr[...]
        padded = jnp.floor((cnt + (MOE_ROWS - 1)) * (1.0 / MOE_ROWS)) * MOE_ROWS
        start_scr[...] = _dot(jnp.broadcast_to(padded, (8, N_EXPERTS)), utri_ref[...], HI)[0:1, :]
        run_scr[...] = jnp.zeros_like(run_scr)
        cnt_ref[...] = cnt

    @pl.when(ph == 1)
    def _():
        before = _dot(ltri_ref[...], multi.astype(BF16))
        base = start_scr[...] + run_scr[...] + before
        cols = [jnp.sum(jnp.where(hot[k], base, 0.0), axis=1, keepdims=True) for k in range(TOP_K)]
        dest_ref[...] = jnp.concatenate(cols, axis=1).astype(jnp.int32)
        run_scr[...] += tile_cnt


def _route_dest(top_idx):
    n = top_idx.shape[0]
    tile = max(c for c in range(COMBINE_TOK, ROW_TILE + 1, COMBINE_TOK) if n % c == 0)
    nt = n // tile
    ltri = jnp.asarray(np.tril(np.ones((tile, tile), np.float32), -1)).astype(BF16)
    utri = jnp.asarray(np.triu(np.ones((N_EXPERTS, N_EXPERTS), np.float32), 1))
    return pl.pallas_call(
        _dest_kernel,
        out_shape=[jax.ShapeDtypeStruct((n, TOP_K), jnp.int32),
                   jax.ShapeDtypeStruct((1, N_EXPERTS), F32)],
        grid=(2, nt),
        in_specs=[pl.BlockSpec((tile, TOP_K), lambda ph, i: (i, 0)),
                  pl.BlockSpec((tile, tile), lambda ph, i: (0, 0)),
                  pl.BlockSpec((N_EXPERTS, N_EXPERTS), lambda ph, i: (0, 0))],
        out_specs=[pl.BlockSpec((tile, TOP_K), lambda ph, i: (i * ph, 0)),
                   pl.BlockSpec((1, N_EXPERTS), lambda ph, i: (0, 0))],
        scratch_shapes=[pltpu.VMEM((1, N_EXPERTS), F32)] * 3,
        compiler_params=_cp(("arbitrary", "arbitrary")),
        name="moe_dest",
    )(top_idx, ltri, utri)


def _block_experts(counts, n_blk):
    cnt = counts.reshape(N_EXPERTS).astype(jnp.int32)
    pends = jnp.cumsum((cnt + MOE_ROWS - 1) // MOE_ROWS * MOE_ROWS)
    starts = jnp.arange(n_blk, dtype=jnp.int32)[:, None] * MOE_ROWS
    blk_expert = jnp.minimum(jnp.sum((pends[None, :] <= starts).astype(jnp.int32), axis=1), N_EXPERTS - 1)
    last_blk = jnp.maximum(pends // MOE_ROWS - 1, 0).astype(jnp.int32)
    return blk_expert, (pends[-1] // MOE_ROWS).astype(jnp.int32).reshape(1), last_blk


def _dispatch_kernel(last_ref, na_ref, dest_ref, x1_ref, rows_hbm, zbuf, sem, zsem, *, tm, n_blk):
    @pl.when(pl.program_id(0) == 0)
    def _():
        zbuf[...] = jnp.zeros_like(zbuf)
        n_tail = n_blk - na_ref[0]

        def clear(blk):
            return pltpu.make_async_copy(zbuf, rows_hbm.at[pl.ds(blk * MOE_ROWS, MOE_ROWS), :], zsem)

        def start(j, carry):
            clear(jnp.where(j < N_EXPERTS, last_ref[jnp.minimum(j, N_EXPERTS - 1)],
                            na_ref[0] + j - N_EXPERTS)).start()
            return carry

        def wait(j, carry):
            clear(0).wait()
            return carry

        lax.fori_loop(0, N_EXPERTS + n_tail, start, 0)
        lax.fori_loop(0, N_EXPERTS + n_tail, wait, 0)

    for r in range(tm):
        for k in range(TOP_K):
            pltpu.make_async_copy(x1_ref.at[pl.ds(r, 1), :],
                                  rows_hbm.at[pl.ds(dest_ref[0, 0, r * TOP_K + k], 1), :], sem).start()
    for k in range(TOP_K):
        pltpu.make_async_copy(x1_ref, rows_hbm.at[pl.ds(0, tm), :], sem).wait()


def _dispatch(x1, dest, last_blk, n_active, n_blk):
    n = x1.shape[0]
    tm = COMBINE_TOK
    nt = n // tm
    kern = functools.partial(_dispatch_kernel, tm=tm, n_blk=n_blk)
    grid_spec = pltpu.PrefetchScalarGridSpec(
        num_scalar_prefetch=2,
        grid=(nt,),
        in_specs=[pl.BlockSpec((1, 1, TOP_K * tm), lambda i, lb, na: (i, 0, 0), memory_space=pltpu.SMEM),
                  pl.BlockSpec((tm, D_MODEL), lambda i, lb, na: (i, 0))],
        out_specs=pl.BlockSpec(memory_space=pl.ANY),
        scratch_shapes=[pltpu.VMEM((MOE_ROWS, D_MODEL), F32), pltpu.SemaphoreType.DMA,
                        pltpu.SemaphoreType.DMA],
    )
    return pl.pallas_call(
        kern,
        out_shape=jax.ShapeDtypeStruct((n_blk * MOE_ROWS, D_MODEL), F32),
        grid_spec=grid_spec,
        compiler_params=_cp(("arbitrary",)),
        name="moe_dispatch",
    )(last_blk, n_active, dest.reshape(nt, 1, TOP_K * tm), x1)


def _swiglu(h):
    h_glu = jnp.minimum(h[:, :D_FF], SWIGLU_LIMIT)
    h_lin = jnp.clip(h[:, D_FF:], -SWIGLU_LIMIT, SWIGLU_LIMIT)
    return h_glu * _sigmoid(SWIGLU_ALPHA * h_glu) * (h_lin + 1.0)


def _moe_kernel(be_ref, na_ref, x_ref, wup_ref, bup_ref, wdn_ref, bdn_ref, y_ref, wup_b, wdn_b):
    i = pl.program_id(0)
    active = i < na_ref[0]

    @pl.when(active)
    def _():
        changed = jnp.logical_or(i == 0, be_ref[i] != be_ref[jnp.maximum(i - 1, 0)])

        @pl.when(changed)
        def _():
            wup_b[...] = wup_ref[...].astype(BF16)
            wdn_b[...] = wdn_ref[...].astype(BF16)

        h = _dot(x_ref[...].astype(BF16), wup_b[...]) + bup_ref[...]
        y_ref[...] = _dot(_swiglu(h).astype(BF16), wdn_b[...]) + bdn_ref[...]

    @pl.when(jnp.logical_not(active))
    def _():
        y_ref[...] = jnp.zeros_like(y_ref)


def _moe_rows(rows, blk_expert, n_active, w_up, b_up, w_dn, b_dn, layer):
    n_blk = blk_expert.shape[0]
    grid_spec = pltpu.PrefetchScalarGridSpec(
        num_scalar_prefetch=2,
        grid=(n_blk,),
        in_specs=[
            pl.BlockSpec((MOE_ROWS, D_MODEL), lambda i, be, na: (jnp.minimum(i, jnp.maximum(na[0] - 1, 0)), 0)),
            pl.BlockSpec((None, None, D_MODEL, 2 * D_FF), lambda i, be, na: (layer, be[i], 0, 0)),
            pl.BlockSpec((None, None, 1, 2 * D_FF), lambda i, be, na: (layer, be[i], 0, 0)),
            pl.BlockSpec((None, None, D_FF, D_MODEL), lambda i, be, na: (layer, be[i], 0, 0)),
            pl.BlockSpec((None, None, 1, D_MODEL), lambda i, be, na: (layer, be[i], 0, 0)),
        ],
        out_specs=pl.BlockSpec((MOE_ROWS, D_MODEL), lambda i, be, na: (i, 0)),
        scratch_shapes=[pltpu.VMEM((D_MODEL, 2 * D_FF), BF16),
                        pltpu.VMEM((D_FF, D_MODEL), BF16)],
    )
    return pl.pallas_call(
        _moe_kernel,
        out_shape=jax.ShapeDtypeStruct((n_blk * MOE_ROWS, D_MODEL), F32),
        grid_spec=grid_spec,
        compiler_params=_cp(("arbitrary",)),
        name="moe_rows",
    )(blk_expert, n_active, rows,
      w_up, b_up.reshape(DEPTH, N_EXPERTS, 1, 2 * D_FF), w_dn, b_dn.reshape(DEPTH, N_EXPERTS, 1, D_MODEL))


def _combine_kernel(dest_ref, destn_ref, gate_ref, x1_ref, y_hbm, g_ref, b_ref, o_ref, ybuf, sem, *, tm):
    i = pl.program_id(0)
    slot = i % 2
    n_rows = TOP_K * tm

    def gather_tile(dests, s):
        for r in range(n_rows):
            pltpu.make_async_copy(y_hbm.at[pl.ds(dests[0, 0, r], 1), :],
                                  ybuf.at[s, pl.ds(r, 1), :], sem.at[s]).start()

    @pl.when(i == 0)
    def _():
        gather_tile(dest_ref, 0)

    for s in range(2):
        @pl.when(jnp.logical_and(i + 1 < pl.num_programs(0), slot == 1 - s))
        def _(s=s):
            gather_tile(destn_ref, s)

    for s in range(2):
        @pl.when(slot == s)
        def _(s=s):
            pltpu.make_async_copy(y_hbm.at[pl.ds(0, n_rows), :], ybuf.at[s], sem.at[s]).wait()
            gates = gate_ref[...]
            moe = jnp.zeros((tm, D_MODEL), F32)
            for k in range(TOP_K):
                moe = moe + gates[:, k:k + 1] * ybuf[s, k * tm:(k + 1) * tm, :]
            o_ref[...] = _layernorm_rows(ALPHA * x1_ref[...] + moe, g_ref[...], b_ref[...])


def _combine(dest, gates, x1, y_rows, g, b):
    n = x1.shape[0]
    tm = COMBINE_TOK
    nt = n // tm
    dest_t = dest.reshape(nt, tm, TOP_K).transpose(0, 2, 1).reshape(nt, 1, TOP_K * tm)
    kern = functools.partial(_combine_kernel, tm=tm)
    return pl.pallas_call(
        kern,
        out_shape=jax.ShapeDtypeStruct((n, D_MODEL), F32),
        grid=(nt,),
        in_specs=[pl.BlockSpec((1, 1, TOP_K * tm), lambda i: (i, 0, 0), memory_space=pltpu.SMEM),
                  pl.BlockSpec((1, 1, TOP_K * tm), lambda i: (jnp.minimum(i + 1, nt - 1), 0, 0),
                               memory_space=pltpu.SMEM),
                  pl.BlockSpec((tm, TOP_K), lambda i: (i, 0)),
                  pl.BlockSpec((tm, D_MODEL), lambda i: (i, 0)),
                  pl.BlockSpec(memory_space=pl.ANY),
                  pl.BlockSpec((1, D_MODEL), lambda i: (0, 0)),
                  pl.BlockSpec((1, D_MODEL), lambda i: (0, 0))],
        out_specs=pl.BlockSpec((tm, D_MODEL), lambda i: (i, 0)),
        scratch_shapes=[pltpu.VMEM((2, TOP_K * tm, D_MODEL), F32), pltpu.SemaphoreType.DMA((2,))],
        compiler_params=_cp(("arbitrary",)),
        name="moe_combine",
    )(dest_t, dest_t, gates, x1, y_rows, g.reshape(1, -1), b.reshape(1, -1))


def _moe(x1, top_idx, gates, p, layer):
    n = x1.shape[0]
    n_blk = -(-n * TOP_K // MOE_ROWS) + N_EXPERTS
    dest, counts = _route_dest(top_idx)
    blk_expert, n_active, last_blk = _block_experts(counts, n_blk)
    rows = _dispatch(x1, dest, last_blk, n_active, n_blk)
    y_rows = _moe_rows(rows, blk_expert, n_active, p["moe_w_up"], p["moe_b_up"],
                       p["moe_w_down"], p["moe_b_down"], layer)
    return _combine(dest, gates, x1, y_rows, p["ln2_g"][layer], p["ln2_b"][layer])


def _sample_proj_kernel(x_ref, w_ref, h0_ref, lam_ref, bblk_ref, cblk_ref, d_ref, gw_ref, gb_ref,
                        q_ref, k_ref, v_ref, pc_ref, oa_ref, h_ref):
    xb = x_ref[...].astype(BF16)
    mm = lambda lo, hi: _dot(xb, w_ref[:, lo:hi])
    u = mm(_PROJ_SPLITS[0], _PROJ_SPLITS[1])
    q_ref[...] = mm(_PROJ_SPLITS[1], _PROJ_SPLITS[2])
    k_ref[...] = mm(_PROJ_SPLITS[2], _PROJ_SPLITS[3])
    v_ref[...] = mm(_PROJ_SPLITS[3], _PROJ_SPLITS[4])
    pc_ref[...] = mm(_PROJ_SPLITS[4], _PROJ_SPLITS[5])
    bu = _dot(u.astype(BF16), bblk_ref[...])
    bu_re = bu[:, 0:N_ST]
    bu_im = bu_re + bu[:, N_ST:2 * N_ST]
    lr, li = lam_ref[0:1, :], lam_ref[1:2, :]
    hr0, hi0 = h0_ref[0], h0_ref[1]
    hr = bu_re + (lr * hr0 - li * hi0)
    hi = bu_im + (lr * hi0 + li * hr0)
    h_ref[0] = hr
    h_ref[1] = hi
    y = (_dot((hr + hi).astype(BF16), cblk_ref[0:N_ST, :])
         - _dot(hi.astype(BF16), cblk_ref[N_ST:2 * N_ST, :])) + d_ref[...] * u
    z = _gelu(y)
    oa_ref[...] = z * _sigmoid(_dot(z.astype(BF16), gw_ref[...]) + gb_ref[...])


def _sample_proj(x, w_in, h0, lam2, bblk, cblk, d, gw, gb):
    n = x.shape[0]
    widths = (W_B, W_B, W_B, C_SHIFT, W_A)
    return pl.pallas_call(
        _sample_proj_kernel,
        out_shape=[jax.ShapeDtypeStruct((n, w), F32) for w in widths]
        + [jax.ShapeDtypeStruct((2, n, N_ST), F32)],
        compiler_params=_cp(None),
        name="sample_proj",
    )(x, w_in, h0, lam2, bblk, cblk, d.reshape(1, -1), gw.astype(BF16), gb.reshape(1, -1))


def _sample_mix_kernel(q_ref, kn_ref, vn_ref, pc_ref, sh_ref, kc_ref, vc_ref, s0_ref, lb_ref, lb0_ref,
                       mu_ref, w0_ref, w2_ref, a0_ref, a2_ref, g2_ref, kk_ref, ka_ref,
                       rk_ref, lng_ref, lnb_ref, mseg_ref,
                       ob_ref, oc_ref, s_ref):
    rows = 8
    lane = lax.broadcasted_iota(jnp.int32, (rows, W_B), 1)
    sub = lax.broadcasted_iota(jnp.int32, (rows, W_B), 0)
    hmask = (lane // HEAD_DIM) == sub
    rnd = lambda t: t.astype(BF16).astype(F32)
    q = q_ref[0] * (HEAD_DIM ** -0.5)
    qrows = jnp.where(hmask, jnp.broadcast_to(q, (rows, W_B)), 0.0).astype(BF16)
    kcb = kc_ref[0].astype(BF16)
    vcb = vc_ref[0].astype(BF16)
    knb = rnd(kn_ref[0])
    vnb = rnd(vn_ref[0])
    s_all = _dot(qrows, kcb)
    s_new = jnp.sum(qrows.astype(F32) * knb, axis=-1, keepdims=True)
    outs, lses = [], []
    for br in range(len(BRANCHES)):
        s = s_all + lb_ref[br]
        s0 = s_new + lb0_ref[br]
        m = jnp.maximum(jnp.max(s, axis=-1, keepdims=True), s0)
        lse = m + jnp.log(jnp.sum(jnp.exp(s - m), axis=-1, keepdims=True) + jnp.exp(s0 - m))
        o = _dot_nt(jnp.exp(s - lse).astype(BF16), vcb) + rnd(jnp.exp(s0 - lse)) * vnb
        outs.append(rnd(o))
        lses.append(lse)
    top = jnp.maximum(jnp.maximum(lses[0], lses[1]), lses[2])
    ex = [jnp.exp(t - top) for t in lses]
    tot = ex[0] + ex[1] + ex[2]
    o = rnd(ex[0] / tot) * outs[0] + rnd(ex[1] / tot) * outs[1] + rnd(ex[2] / tot) * outs[2]
    ob_ref[0] = jnp.sum(jnp.where(hmask, o, 0.0), axis=0, keepdims=True)

    prm = (mu_ref[...], w0_ref[...], w2_ref[...], a0_ref[...], a2_ref[...], g2_ref[...],
           kk_ref[...], ka_ref[...], rk_ref[...], lng_ref[...], lnb_ref[...], mseg_ref[...])
    pc = jnp.broadcast_to(pc_ref[0], (rows, C_SHIFT))
    prev = jnp.broadcast_to(sh_ref[0], (rows, C_SHIFT))
    seg_sum = lambda t: _dot(t, mseg_ref[...], HI)
    r, k2, v, kk, a, g, logw = _rwkv_pre(pc, prev, prm, seg_sum)
    w = jnp.exp(logw)
    eye = (lax.broadcasted_iota(jnp.int32, (HEAD_DIM, HEAD_DIM), 0)
           == lax.broadcasted_iota(jnp.int32, (HEAD_DIM, HEAD_DIM), 1))
    ys = []
    for h in range(H_C):
        sl = slice(h * HEAD_DIM, (h + 1) * HEAD_DIM)
        row = lambda t: t[0:1, sl]
        col = lambda t: jnp.sum(jnp.where(eye, jnp.broadcast_to(row(t), (HEAD_DIM, HEAD_DIM)), 0.0),
                                axis=1, keepdims=True)
        s0 = s0_ref[0, h]
        sa = jnp.sum(rnd(s0) * rnd(-row(kk)), axis=1, keepdims=True)
        s1 = s0 * row(w) + sa * (row(kk) * row(a)) + col(v) * row(k2)
        s_ref[0, h] = s1
        y_col = jnp.sum(rnd(s1) * rnd(row(r)), axis=1, keepdims=True)
        ys.append(jnp.sum(jnp.where(eye, jnp.broadcast_to(y_col, (HEAD_DIM, HEAD_DIM)), 0.0),
                          axis=0, keepdims=True))
    y = jnp.broadcast_to(jnp.concatenate(ys, axis=1), (rows, W_C))
    oc_ref[0] = _rwkv_post(y, r, k2, v, g, prm, seg_sum)[0:1, :]


def _sample_mix(q, kn, vn, pc, shift0, k_cache, v_cache, s0, lb_rows, lb0, prm, layer):
    n = q.shape[0]
    l_buf = k_cache.shape[3]
    per_b = lambda w: pl.BlockSpec((1, 1, w), lambda b: (b, 0, 0))
    const = lambda a: pl.BlockSpec(a.shape, lambda b: (0,) * a.ndim)
    r3 = lambda t: t.reshape(n, 1, t.shape[-1])
    cache = pl.BlockSpec((None, 1, W_B, l_buf), lambda b: (layer, b, 0, 0))
    return pl.pallas_call(
        _sample_mix_kernel,
        out_shape=[jax.ShapeDtypeStruct((n, 1, W_B), F32),
                   jax.ShapeDtypeStruct((n, 1, W_C), F32),
                   jax.ShapeDtypeStruct((n, H_C, HEAD_DIM, HEAD_DIM), F32)],
        grid=(n,),
        in_specs=[per_b(W_B), per_b(W_B), per_b(W_B), per_b(C_SHIFT), per_b(C_SHIFT),
                  cache, cache,
                  pl.BlockSpec((1, H_C, HEAD_DIM, HEAD_DIM), lambda b: (b, 0, 0, 0)),
                  const(lb_rows), const(lb0)] + [const(a) for a in prm],
        out_specs=[per_b(W_B), per_b(W_C),
                   pl.BlockSpec((1, H_C, HEAD_DIM, HEAD_DIM), lambda b: (b, 0, 0, 0))],
        compiler_params=_cp(("parallel",)),
        name="sample_mix",
    )(r3(q), r3(kn), r3(vn), r3(pc), r3(shift0), k_cache, v_cache, s0, lb_rows, lb0, *prm)


def _prompt_mixers(x, nb, t, p, l, s5p, rwp, tiles, kv_prev):
    u, q, k_all, v_all, pc = _proj(x, p["w_in"][l].astype(BF16), nb, t, l, kv_prev)
    shp = lambda a: a.reshape(nb, t, a.shape[-1])
    o_a, h_t = _s5_prompt(shp(u), *s5p, p["ssm_d"][l], p["ssm_glu_w"][l], p["ssm_glu_b"][l])
    o_b = _attn_prompt(shp(q), k_all, v_all, tiles, l)
    o_c, s_t = _rwkv_prompt(shp(pc), rwp)
    flat = lambda a: a.reshape(nb * t, a.shape[-1])
    ssm = jnp.transpose(h_t.reshape(2, nb, G_A, N_A), (1, 2, 3, 0))
    return (flat(o_a), flat(o_b), flat(o_c)), (ssm, s_t, shp(pc)[:, t - 1]), (k_all, v_all)


def _sample_bias(rel_bias, l_buf):
    offs = _branch_offsets()
    bias = rel_bias[jnp.asarray(_t5_buckets(offs))]
    rows = []
    for br in range(len(BRANCHES)):
        m = np.arange(1, offs.shape[1])
        pos = l_buf - offs[br][m]
        ok = pos >= 0
        r = jnp.full((H_B, l_buf), NEG_INF, F32).at[:, jnp.asarray(pos[ok])].set(bias[br][jnp.asarray(m[ok])].T)
        rows.append(jnp.concatenate([r, jnp.zeros((8 - H_B, l_buf), F32)], axis=0))
    new = jnp.zeros((len(BRANCHES), 8, 1), F32).at[:, :H_B, 0].set(bias[:, 0])
    return jnp.stack(rows), new


def _sample_mixers(x, p, l, s5p, rwp, sbias, k_cache, v_cache, st_ssm, st_rwkv, st_shift):
    n = x.shape[0]
    lam2, bblk, cblk = s5p
    h0 = jnp.transpose(st_ssm.reshape(n, N_ST, 2), (2, 0, 1))
    q, kn, vn, pc, o_a, h1 = _sample_proj(x, p["w_in"][l].astype(BF16), h0, lam2, bblk, cblk, p["ssm_d"][l],
                                          p["ssm_glu_w"][l], p["ssm_glu_b"][l])
    lb_rows, lb0 = sbias
    o_b, o_c, s1 = _sample_mix(q, kn, vn, pc, st_shift, k_cache, v_cache, st_rwkv, lb_rows, lb0, rwp, l)
    ssm = jnp.transpose(h1.reshape(2, n, G_A, N_A), (1, 2, 3, 0))
    return (o_a, o_b.reshape(n, W_B), o_c.reshape(n, W_C)), (kn, vn, ssm, s1, pc)


def _shift_kernel(c_ref, new_ref, o_ref):
    x = c_ref[...]
    length = x.shape[1]
    lane = lax.broadcasted_iota(jnp.int32, x.shape, 1)
    o_ref[...] = jnp.where(lane == length - 1, new_ref[...], pltpu.roll(x, length - 1, axis=1))


def _shift_cache(cache_t, new):
    rows, length = cache_t.shape
    blk = min(ROW_TILE, rows)
    return pl.pallas_call(
        _shift_kernel,
        out_shape=jax.ShapeDtypeStruct((rows, length), F32),
        grid=(rows // blk,),
        in_specs=[pl.BlockSpec((blk, length), lambda i: (i, 0)),
                  pl.BlockSpec((blk, 1), lambda i: (i, 0))],
        out_specs=pl.BlockSpec((blk, length), lambda i: (i, 0)),
        compiler_params=_cp(("parallel",)),
        name="shift_cache",
    )(cache_t, new)


def kernel(x_prompt, x_sample, cache_attn_k, cache_attn_v, state_ssm, state_rwkv, state_shift, w_in, w_out, ln1_g, ln1_b, ln2_g, ln2_b, ssm_a_re, ssm_a_im, ssm_log_dt, ssm_b_re, ssm_b_im, ssm_c_re, ssm_c_im, ssm_d, ssm_glu_w, ssm_glu_b, rel_bias, rwkv_mu, rwkv_w0, rwkv_w2, rwkv_a0, rwkv_a2, rwkv_g2, rwkv_k_k, rwkv_k_a, rwkv_r_k, rwkv_ln_g, rwkv_ln_b, moe_router_w, moe_router_b, moe_w_up, moe_b_up, moe_w_down, moe_b_down):
    p = dict(w_in=w_in, w_out=w_out, ln1_g=ln1_g, ln1_b=ln1_b, ln2_g=ln2_g, ln2_b=ln2_b,
             ssm_d=ssm_d, ssm_glu_w=ssm_glu_w, ssm_glu_b=ssm_glu_b,
             moe_router_w=moe_router_w, moe_router_b=moe_router_b, moe_w_up=moe_w_up,
             moe_b_up=moe_b_up, moe_w_down=moe_w_down, moe_b_down=moe_b_down)
    nb, t, _ = x_prompt.shape
    ns = x_sample.shape[0]
    l_buf = cache_attn_k.shape[2]
    table = _log_bias_table(rel_bias)
    tiles = _bias_tiles(table, min(t, WIN_MAX) // Q_TILE + 1)
    sbias = _sample_bias(rel_bias, l_buf)
    n_p = nb * t
    n_all = n_p + -(-ns // COMBINE_TOK) * COMBINE_TOK
    xp = x_prompt.reshape(n_p, D_MODEL)
    xs = x_sample.reshape(ns, D_MODEL)
    st_p, st_s = [], []
    time_minor = lambda c: jnp.transpose(c, (0, 1, 3, 4, 2)).reshape(DEPTH, c.shape[1], W_B, c.shape[2])
    time_major = lambda c, n: jnp.transpose(c.reshape(DEPTH, n, H_B, HEAD_DIM, c.shape[-1]), (0, 1, 4, 2, 3))
    kc_t, vc_t = time_minor(cache_attn_k), time_minor(cache_attn_v)
    kv = (jnp.zeros((DEPTH, nb, W_B, t), F32), jnp.zeros((DEPTH, nb, W_B, t), F32))
    for l in range(DEPTH):
        s5p, s5s = _s5_params(ssm_a_re[l], ssm_a_im[l], ssm_log_dt[l], ssm_b_re[l], ssm_b_im[l],
                              ssm_c_re[l], ssm_c_im[l])
        rwp = _rwkv_params(dict(rwkv_mu=rwkv_mu[l], rwkv_w0=rwkv_w0[l], rwkv_w2=rwkv_w2[l],
                                rwkv_a0=rwkv_a0[l], rwkv_a2=rwkv_a2[l], rwkv_g2=rwkv_g2[l],
                                rwkv_k_k=rwkv_k_k[l], rwkv_k_a=rwkv_k_a[l], rwkv_r_k=rwkv_r_k[l],
                                rwkv_ln_g=rwkv_ln_g[l], rwkv_ln_b=rwkv_ln_b[l]))
        mix_p, sp, kv = _prompt_mixers(xp, nb, t, p, l, s5p, rwp, tiles, kv)
        mix_s, ss = _sample_mixers(xs, p, l, s5s, rwp, sbias, kc_t, vc_t,
                                   state_ssm[l], state_rwkv[l], state_shift[l])
        st_p.append(sp)
        st_s.append(ss)
        bufs = (jnp.zeros((n_all, D_MODEL), F32),
                jnp.broadcast_to(jnp.arange(TOP_K, dtype=jnp.int32), (n_all, TOP_K)),
                jnp.zeros((n_all, TOP_K), F32))
        head = (p["w_out"][l], p["ln1_g"][l], p["ln1_b"][l], p["moe_router_w"][l], p["moe_router_b"][l])
        bufs = _outproj(xp, *mix_p, *head, bufs, 0)
        bufs = _outproj(xs, *mix_s, *head, bufs, n_p)
        xp = _moe(*bufs, p, l)
        xs = xp[n_p:n_p + ns]
    p_ssm, p_rwkv, p_shift = (jnp.stack(z) for z in zip(*st_p))
    kn, vn, s_ssm, s_rwkv, s_shift = (jnp.stack(z) for z in zip(*st_s))
    keep_p = min(WIN_MAX, t)
    p_k, p_v = (time_major(a[..., t - keep_p:], nb) for a in kv)
    keep = min(WIN_MAX, l_buf + 1)
    drop = l_buf + 1 - keep
    if drop == 1:
        advance = lambda c_t, new: _shift_cache(c_t.reshape(-1, l_buf), new.reshape(-1, 1)).reshape(c_t.shape)
    else:
        advance = lambda c_t, new: jnp.concatenate([c_t[..., drop:], new[..., None]], axis=-1)
    s_k = time_major(advance(kc_t, kn), ns)
    s_v = time_major(advance(vc_t, vn), ns)
    return (xp[:n_p].reshape(nb, t, D_MODEL), xs.reshape(ns, 1, D_MODEL),
            p_k, p_v, p_ssm, p_rwkv, p_shift, s_k, s_v, s_ssm, s_rwkv, s_shift)
```

```python
import functools
import math

import numpy as np
import jax
import jax.numpy as jnp
from jax import lax
from jax.experimental import pallas as pl
from jax.experimental.pallas import tpu as pltpu

F32 = jnp.float32
BF16 = jnp.bfloat16
HI = lax.Precision.HIGHEST

D_MODEL = 1024
DEPTH = 2
HEAD_DIM = 64
W_A = 256
C_GRP = 16
G_A = 16
N_A = 64
N_ST = G_A * N_A
W_B = 384
H_B = 6
BRANCHES = ((128, 1), (512, 4), (2048, 16))
WIN_MAX = 2048
N_BUCKETS = 32
MAX_DIST = WIN_MAX
NEG_INF = -1e30
W_C = 384
H_C = 6
R_W = 32
R_A = 32
R_G = 64
C_SHIFT = 3 * W_C + R_W + R_A + R_G
N_IN = W_A + 3 * W_B + C_SHIFT
GN_EPS = 64e-5
N_EXPERTS = 32
TOP_K = 4
D_FF = D_MODEL
SWIGLU_ALPHA = 1.702
SWIGLU_LIMIT = 7.0
ALPHA = (2 * DEPTH) ** 0.25
LN_EPS = 1e-5

LANES = 128
Q_TILE = 128
RWKV_CHUNK = 64
RWKV_SEQS = 2
S5_CHUNK = 64
ROW_TILE = 512
MOE_ROWS = 256
COMBINE_TOK = 128
VMEM_LIMIT = 56 * 1024 * 1024


def _cp(sem, vmem=VMEM_LIMIT):
    return pltpu.CompilerParams(dimension_semantics=sem, vmem_limit_bytes=vmem)


def _dot(a, b, precision=None):
    return jnp.dot(a, b, preferred_element_type=F32, precision=precision)


def _dot_nt(a, b, precision=None):
    return lax.dot_general(a, b, (((1,), (1,)), ((), ())),
                           preferred_element_type=F32, precision=precision)


def _dot_tn(a, b, precision=None):
    return lax.dot_general(a, b, (((0,), (0,)), ((), ())),
                           preferred_element_type=F32, precision=precision)


def _layernorm_rows(x, g, b):
    mu = jnp.mean(x, axis=-1, keepdims=True)
    d = x - mu
    var = jnp.mean(d * d, axis=-1, keepdims=True)
    return d * lax.rsqrt(var + LN_EPS) * g + b


def _sigmoid(x):
    return 1.0 / (1.0 + jnp.exp(-x))


def _softplus(x):
    return jnp.maximum(x, 0.0) + jnp.log(1.0 + jnp.exp(-jnp.abs(x)))


def _gelu(x):
    c = math.sqrt(2.0 / math.pi)
    return 0.5 * x * (1.0 + jnp.tanh(c * (x + 0.044715 * (x * x * x))))


_PROJ_SPLITS = (0, W_A, W_A + W_B, W_A + 2 * W_B, W_A + 3 * W_B, N_IN)


def _proj_kernel(x_ref, w_ref, *refs):
    u_ref, q_ref, k_ref, v_ref, pc_ref = refs[-5:]
    xb = x_ref[...].astype(BF16)
    outs = ((u_ref, False), (q_ref, False), (k_ref, True), (v_ref, True), (pc_ref, False))
    for (o_ref, time_minor), lo, hi in zip(outs, _PROJ_SPLITS[:-1], _PROJ_SPLITS[1:]):
        y = _dot(xb, w_ref[:, lo:hi])
        o_ref[...] = jnp.transpose(y) if time_minor else y


def _proj(x, w_bf16, nb, t, layer, kv_prev):
    n = nb * t
    per_seq = t // ROW_TILE
    widths = [hi - lo for lo, hi in zip(_PROJ_SPLITS[:-1], _PROJ_SPLITS[1:])]
    rows = lambda w: pl.BlockSpec((ROW_TILE, w), lambda i: (i, 0))
    stacked = pl.BlockSpec((None, None, W_B, ROW_TILE), lambda i: (layer, i // per_seq, 0, i % per_seq))
    flat = lambda w: jax.ShapeDtypeStruct((n, w), F32)
    kv = jax.ShapeDtypeStruct((DEPTH, nb, W_B, t), F32)
    prev = list(kv_prev)
    return pl.pallas_call(
        _proj_kernel,
        out_shape=[flat(widths[0]), flat(widths[1]), kv, kv, flat(widths[4])],
        grid=(n // ROW_TILE,),
        in_specs=[pl.BlockSpec((ROW_TILE, D_MODEL), lambda i: (i, 0)),
                  pl.BlockSpec((D_MODEL, N_IN), lambda i: (0, 0))]
        + [pl.BlockSpec(memory_space=pl.ANY)] * len(prev),
        out_specs=[rows(widths[0]), rows(widths[1]), stacked, stacked, rows(widths[4])],
        input_output_aliases={2: 2, 3: 3},
        compiler_params=_cp(("parallel",)),
        name="proj",
    )(x, w_bf16, *prev)


def _s5_kernel(u_ref, lam_ref, bblk_ref, cblk_ref, d_ref, gw_ref, gb_ref,
               o_ref, ht_ref, utm, hs, ytm, h_scr, *, nb, lt):
    c = pl.program_id(0)

    @pl.when(c == 0)
    def _():
        h_scr[...] = jnp.zeros_like(h_scr)

    n_half = W_A // LANES
    for b in range(nb):
        for j in range(n_half):
            utm[j, pl.ds(b, lt, stride=nb), :] = u_ref[b, :, j * LANES:(j + 1) * LANES]
    u_all = jnp.concatenate([utm[j] for j in range(n_half)], axis=1)
    hs[...] = _dot(u_all.astype(BF16), bblk_ref[...])
    lr = jnp.broadcast_to(lam_ref[0:1, :], (nb, N_ST))
    li = jnp.broadcast_to(lam_ref[1:2, :], (nb, N_ST))

    def body(t, carry):
        hr, hi = carry
        row = pl.multiple_of(t * nb, nb)
        br = hs[pl.ds(row, nb), 0:N_ST]
        bi = hs[pl.ds(row, nb), N_ST:2 * N_ST]
        nr = lr * hr - li * hi + br
        ni = lr * hi + li * hr + bi
        hs[pl.ds(row, nb), 0:N_ST] = nr
        hs[pl.ds(row, nb), N_ST:2 * N_ST] = ni
        return nr, ni

    hr, hi = lax.fori_loop(0, lt, body, (h_scr[0], h_scr[1]), unroll=2)
    h_scr[0] = hr
    h_scr[1] = hi
    ht_ref[0] = hr
    ht_ref[1] = hi
    y = _dot(hs[...].astype(BF16), cblk_ref[...]) + d_ref[...] * u_all
    z = _gelu(y)
    gl = _dot(z.astype(BF16), gw_ref[...]) + gb_ref[...]
    res = z * _sigmoid(gl)
    for j in range(n_half):
        ytm[j] = res[:, j * LANES:(j + 1) * LANES]
    for b in range(nb):
        for j in range(n_half):
            o_ref[b, :, j * LANES:(j + 1) * LANES] = ytm[j, pl.ds(b, lt, stride=nb), :]


def _s5_params(a_re, a_im, log_dt, b_re, b_im, c_re, c_im):
    lam = lax.complex(a_re, a_im)
    dt = jnp.exp(log_dt)[:, None]
    lam_bar = jnp.exp(lam * dt)
    b_bar = ((lam_bar - 1.0) / lam)[..., None] * lax.complex(b_re, b_im)
    eye = jnp.eye(G_A, dtype=F32)

    def blk_in(m):
        return jnp.einsum("gnc,gh->gchn", m, eye).reshape(W_A, N_ST)

    def blk_out(m):
        return jnp.einsum("gcn,gh->gnhc", m, eye).reshape(N_ST, W_A)

    bblk = jnp.concatenate([blk_in(b_bar.real), blk_in(b_bar.imag)], axis=1)
    cblk = jnp.concatenate([blk_out(c_re), blk_out(-c_im)], axis=0)
    lam2 = jnp.stack([lam_bar.real.reshape(N_ST), lam_bar.imag.reshape(N_ST)])
    bblk3 = jnp.concatenate([blk_in(b_bar.real), blk_in(b_bar.imag - b_bar.real)], axis=1)
    cblk3 = jnp.concatenate([blk_out(c_re), blk_out(c_re + c_im)], axis=0)
    return (lam2, bblk, cblk), (lam2, bblk3.astype(BF16), cblk3.astype(BF16))


def _s5_prompt(u, lam2, bblk, cblk, d, gw, gb):
    nb, t, _ = u.shape
    lt = min(S5_CHUNK, t)
    kern = functools.partial(_s5_kernel, nb=nb, lt=lt)
    const = lambda shape: pl.BlockSpec(shape, lambda c: (0,) * len(shape))
    return pl.pallas_call(
        kern,
        out_shape=[jax.ShapeDtypeStruct((nb, t, W_A), F32),
                   jax.ShapeDtypeStruct((2, nb, N_ST), F32)],
        grid=(t // lt,),
        in_specs=[pl.BlockSpec((nb, lt, W_A), lambda c: (0, c, 0)),
                  const((2, N_ST)), const((W_A, 2 * N_ST)), const((2 * N_ST, W_A)),
                  const((1, W_A)), const((W_A, W_A)), const((1, W_A))],
        out_specs=[pl.BlockSpec((nb, lt, W_A), lambda c: (0, c, 0)),
                   const((2, nb, N_ST))],
        scratch_shapes=[pltpu.VMEM((W_A // LANES, nb * lt, LANES), F32),
                        pltpu.VMEM((nb * lt, 2 * N_ST), F32),
                        pltpu.VMEM((W_A // LANES, nb * lt, LANES), F32),
                        pltpu.VMEM((2, nb, N_ST), F32)],
        compiler_params=_cp(("arbitrary",)),
        name="s5_prompt",
    )(u, lam2, bblk.astype(BF16), cblk.astype(BF16), d.reshape(1, W_A),
      gw.astype(BF16), gb.reshape(1, W_A))


def _branch_offsets():
    return np.stack([np.arange(w // d + 1) * d for (w, d) in BRANCHES]).astype(np.int32)


def _t5_buckets(dist):
    max_exact = N_BUCKETS // 2
    d = np.maximum(dist, 1).astype(np.float32)
    large = max_exact + (np.log(d / max_exact) / np.log(MAX_DIST / max_exact)
                         * (N_BUCKETS - max_exact)).astype(np.int32)
    return np.where(dist < max_exact, dist, np.minimum(large, N_BUCKETS - 1)).astype(np.int32)


def _log_bias_table(rel_bias):
    offs = _branch_offsets()
    bias = rel_bias[jnp.asarray(_t5_buckets(offs))]
    table = jnp.full((WIN_MAX + 1, H_B), -jnp.inf, F32)
    for br in range(len(BRANCHES)):
        idx = jnp.asarray(offs[br])
        table = table.at[idx].set(jnp.logaddexp(table[idx], bias[br]))
    return jnp.maximum(table, NEG_INF)


def _bias_tiles(table, n_diff):
    per = 2 * Q_TILE
    m = np.arange(per)
    dist = np.arange(n_diff)[:, None] * Q_TILE + np.where(m < Q_TILE, -m, per - m)[None, :]
    ok = (dist >= 0) & (dist <= WIN_MAX) & (m != Q_TILE)[None, :]
    v = jnp.where(jnp.asarray(ok)[..., None], table[jnp.asarray(np.clip(dist, 0, WIN_MAX))], NEG_INF)
    v = jnp.transpose(v, (2, 0, 1))
    flat = jnp.broadcast_to(v[:, :, None, :], (H_B, n_diff, Q_TILE, per)).reshape(H_B, n_diff, Q_TILE * per)
    tiles = flat[:, :, :Q_TILE * (per - 1)].reshape(H_B, n_diff, Q_TILE, per - 1)[..., :Q_TILE]
    tiles = tiles.reshape(H_B // 2, 2, n_diff, Q_TILE, Q_TILE)
    return jnp.transpose(tiles, (0, 2, 1, 3, 4)).reshape(H_B // 2, n_diff, 2 * Q_TILE, Q_TILE)


def _attn_kernel(q_ref, k_ref, v_ref, bias_ref, o_ref, kb, vb, *, nq):
    kb[...] = k_ref[...].astype(BF16)
    vb[...] = v_ref[...].astype(BF16)
    lo = lax.broadcasted_iota(jnp.int32, (Q_TILE, LANES), 1) < HEAD_DIM
    for qi in range(nq):
        rows = slice(qi * Q_TILE, (qi + 1) * Q_TILE)
        q = q_ref[0, rows, :] * (HEAD_DIM ** -0.5)
        q2 = jnp.concatenate([jnp.where(lo, q, 0.0), jnp.where(lo, 0.0, q)], axis=0).astype(BF16)
        nk = (qi + 1) * Q_TILE
        bias = jnp.concatenate([bias_ref[qi - kj] for kj in range(qi + 1)], axis=1)
        s = _dot(q2, kb[:, 0:nk]) + bias
        m = jnp.max(s, axis=-1, keepdims=True)
        p = jnp.exp(s - m)
        l = jnp.sum(p, axis=-1, keepdims=True)
        o = _dot_nt(p.astype(BF16), vb[:, 0:nk]) / l
        o_ref[0, rows, :] = jnp.where(lo, o[0:Q_TILE], o[Q_TILE:2 * Q_TILE])


def _attn_prompt(q, k_all, v_all, tiles, layer):
    nb, t, _ = q.shape
    n_diff = tiles.shape[1]
    hp = W_B // LANES
    kern = functools.partial(_attn_kernel, nq=t // Q_TILE)
    seq = pl.BlockSpec((1, t, LANES), lambda b, p: (b, 0, p))
    kv = pl.BlockSpec((None, None, LANES, t), lambda b, p: (layer, b, p, 0))
    return pl.pallas_call(
        kern,
        out_shape=jax.ShapeDtypeStruct((nb, t, W_B), F32),
        grid=(nb, hp),
        in_specs=[seq, kv, kv,
                  pl.BlockSpec((None, n_diff, 2 * Q_TILE, Q_TILE), lambda b, p: (p, 0, 0, 0))],
        out_specs=seq,
        scratch_shapes=[pltpu.VMEM((LANES, t), BF16), pltpu.VMEM((LANES, t), BF16)],
        compiler_params=_cp(("parallel", "parallel")),
        name="attn_prompt",
    )(q, k_all, v_all, tiles)


def _split_dot(x, m_bf16, parts):
    acc = None
    rem = x
    for i in range(parts):
        piece = rem.astype(BF16)
        term = _dot(piece, m_bf16)
        acc = term if acc is None else acc + term
        if i + 1 < parts:
            rem = rem - piece.astype(F32)
    return acc


def _split_dot_left(m_bf16, x, parts):
    acc = None
    rem = x
    for i in range(parts):
        piece = rem.astype(BF16)
        term = _dot(m_bf16, piece)
        acc = term if acc is None else acc + term
        if i + 1 < parts:
            rem = rem - piece.astype(F32)
    return acc


def _rwkv_pre(pc, prev, prm, seg_sum):
    (mu, w0, w2p, a0, a2p, g2p, k_k, k_a, r_k, ln_g, ln_b, mseg) = prm
    xs = pc + (prev - pc) * mu
    r = xs[:, 0:W_C]
    k = xs[:, W_C:2 * W_C]
    v = xs[:, 2 * W_C:3 * W_C]
    tail = xs[:, 3 * W_C:C_SHIFT]
    w_raw = -_softplus(-(w0 + _dot(jnp.tanh(tail).astype(BF16), w2p))) - 0.5
    a = _sigmoid(a0 + _dot(tail.astype(BF16), a2p))
    g = _dot(_sigmoid(tail).astype(BF16), g2p)
    kk = k * k_k
    nrm = jnp.sqrt(seg_sum(kk * kk))
    kk = kk / jnp.maximum(nrm, 1e-12)
    k2 = k * (1.0 + (a - 1.0) * k_a)
    logw = -jnp.exp(w_raw)
    return r, k2, v, kk, a, g, logw


def _rwkv_post(y, r, k2, v, g, prm, seg_sum):
    (mu, w0, w2p, a0, a2p, g2p, k_k, k_a, r_k, ln_g, ln_b, mseg) = prm
    mean = seg_sum(y) * (1.0 / HEAD_DIM)
    d = y - mean
    var = seg_sum(d * d) * (1.0 / HEAD_DIM)
    yn = d * lax.rsqrt(var + GN_EPS) * ln_g + ln_b
    bonus = seg_sum(r * k2 * r_k) * v
    return (yn + bonus) * g


def _rwkv_kernel(pc_ref, mu_ref, w0_ref, w2_ref, a0_ref, a2_ref, g2_ref, kk_ref, ka_ref,
                 rk_ref, lng_ref, lnb_ref, mseg_ref, ltri_ref,
                 o_ref, s_ref, s_scr, prev_scr, yt_scr, *, ch, nseq):
    c = pl.program_id(1)
    rows = nseq * ch

    @pl.when(c == 0)
    def _():
        s_scr[...] = jnp.zeros_like(s_scr)
        prev_scr[...] = jnp.zeros_like(prev_scr)
        yt_scr[...] = jnp.zeros_like(yt_scr)

    prm = (mu_ref[...], w0_ref[...], w2_ref[...], a0_ref[...], a2_ref[...], g2_ref[...],
           kk_ref[...], ka_ref[...], rk_ref[...], lng_ref[...], lnb_ref[...], None)
    mseg = mseg_ref[...]
    seg_sum = lambda t: _split_dot(t, mseg, 2)
    pc = pc_ref[...].reshape(rows, C_SHIFT)
    rowi = lax.broadcasted_iota(jnp.int32, (rows, C_SHIFT), 0)
    prev = pltpu.roll(pc, 1, axis=0)
    for b in range(nseq):
        prev = jnp.where(rowi == b * ch, prev_scr[b:b + 1, :], prev)
        prev_scr[b:b + 1, :] = pc[(b + 1) * ch - 1:(b + 1) * ch, :]
    r, k2, v, kk, a, g, logw = _rwkv_pre(pc, prev, prm, seg_sum)

    cs = _split_dot_left(ltri_ref[...], logw, 3)
    g_in = jnp.exp(cs)
    g_ex = jnp.exp(cs - logw)
    g_inv = jnp.exp(-cs)
    al = (-kk * g_ex).astype(BF16)
    rt = (r * g_in).astype(BF16)
    bh = kk * a * g_inv
    kh = k2 * g_inv
    vb = v.astype(BF16)

    si = lax.broadcasted_iota(jnp.int32, (2 * ch, 2 * ch), 0) % ch
    ti = lax.broadcasted_iota(jnp.int32, (2 * ch, 2 * ch), 1)
    keep = si + jnp.where(ti < ch, 0, ch - 1) < ti
    n_sq = int(math.log2(ch))
    units = [(b, h) for b in range(nseq) for h in range(H_C)]
    idx = range(len(units))
    cut = lambda t, u: t[units[u][0] * ch:(units[u][0] + 1) * ch,
                         units[u][1] * HEAD_DIM:(units[u][1] + 1) * HEAD_DIM]
    g_end = [cut(g_in, u)[ch - 1:ch, :] for u in idx]
    al_u = [cut(al, u) for u in idx]
    rt_u = [cut(rt, u) for u in idx]
    v_u = [cut(vb, u) for u in idx]
    bh_u = [cut(bh, u) for u in idx]
    kh_u = [cut(kh, u) for u in idx]
    bc = [bh_u[u] * g_end[u] for u in idx]
    kc = [kh_u[u] * g_end[u] for u in idx]
    s0 = [s_scr[b, h] for (b, h) in units]
    s0b = [t.astype(BF16) for t in s0]
    bk = [jnp.concatenate([bh_u[u], kh_u[u]], axis=0).astype(BF16) for u in idx]
    ar = [jnp.concatenate([al_u[u], rt_u[u]], axis=0) for u in idx]
    gm = [jnp.where(keep, _dot_nt(bk[u], ar[u]), 0.0) for u in idx]
    ps = [_dot_nt(s0b[u], ar[u]) for u in idx]
    a_t = [gm[u][0:ch, 0:ch] for u in idx]
    vv = [_dot_tn(v_u[u], jnp.concatenate([gm[u][ch:2 * ch, :], kc[u]], axis=1).astype(BF16))
          for u in idx]
    x_t = [ps[u][:, 0:ch] + vv[u][:, 0:ch] for u in idx]
    for it in range(n_sq):
        a_b = [a_t[u].astype(BF16) for u in idx]
        if it + 1 < n_sq:
            both = [_dot(jnp.concatenate([x_t[u], a_t[u]], axis=0).astype(BF16), a_b[u]) for u in idx]
            x_t = [x_t[u] + both[u][0:HEAD_DIM] for u in idx]
            a_t = [both[u][HEAD_DIM:HEAD_DIM + ch] for u in idx]
        else:
            step = [_dot(x_t[u].astype(BF16), a_b[u]) for u in idx]
            x_t = [x_t[u] + step[u] for u in idx]
    x_b = [t.astype(BF16) for t in x_t]
    xx = [_dot(x_b[u], jnp.concatenate([gm[u][0:ch, :], bc[u]], axis=1).astype(BF16))
          for u in idx]
    for u, (b, h) in enumerate(units):
        s_new = s0[u] * g_end[u] + xx[u][:, 2 * ch:] + vv[u][:, 2 * ch:]
        y_sum = ps[u] + xx[u][:, 0:2 * ch] + vv[u][:, 0:2 * ch]
        yt_scr[h * HEAD_DIM:(h + 1) * HEAD_DIM, b * ch:(b + 1) * ch] = y_sum[:, ch:2 * ch]
        s_scr[b, h] = s_new
        s_ref[b, h] = s_new

    y = jnp.transpose(yt_scr[...])[0:rows, :]
    o_ref[...] = _rwkv_post(y, r, k2, v, g, prm, seg_sum).reshape(nseq, ch, W_C)


def _rwkv_params(p):
    pad = lambda w, lo: jnp.zeros((R_W + R_A + R_G, W_C), BF16).at[lo:lo + w.shape[0]].set(w.astype(BF16))
    seg = np.arange(W_C) // HEAD_DIM
    mseg = jnp.asarray((seg[:, None] == seg[None, :]).astype(np.float32))
    row = lambda t: t.reshape(1, -1)
    return (row(p["rwkv_mu"]), row(p["rwkv_w0"]), pad(p["rwkv_w2"], 0), row(p["rwkv_a0"]),
            pad(p["rwkv_a2"], R_W), pad(p["rwkv_g2"], R_W + R_A), row(p["rwkv_k_k"]),
            row(p["rwkv_k_a"]), row(p["rwkv_r_k"]), row(p["rwkv_ln_g"]), row(p["rwkv_ln_b"]), mseg)


def _rwkv_prompt(pc, prm):
    nb, t, _ = pc.shape
    ch = min(RWKV_CHUNK, t)
    nseq = max(1, min(RWKV_SEQS * LANES // ch, nb))
    assert nb % nseq == 0
    ltri = np.kron(np.eye(nseq, dtype=np.float32), np.tril(np.ones((ch, ch), np.float32)))
    ltri = jnp.asarray(ltri).astype(BF16)
    kern = functools.partial(_rwkv_kernel, ch=ch, nseq=nseq)
    const = lambda a: pl.BlockSpec(a.shape, lambda b, c: (0,) * a.ndim)
    args = list(prm[:-1]) + [prm[-1].astype(BF16), ltri]
    return pl.pallas_call(
        kern,
        out_shape=[jax.ShapeDtypeStruct((nb, t, W_C), F32),
                   jax.ShapeDtypeStruct((nb, H_C, HEAD_DIM, HEAD_DIM), F32)],
        grid=(nb // nseq, t // ch),
        in_specs=[pl.BlockSpec((nseq, ch, C_SHIFT), lambda b, c: (b, c, 0))] + [const(a) for a in args],
        out_specs=[pl.BlockSpec((nseq, ch, W_C), lambda b, c: (b, c, 0)),
                   pl.BlockSpec((nseq, H_C, HEAD_DIM, HEAD_DIM), lambda b, c: (b, 0, 0, 0))],
        scratch_shapes=[pltpu.VMEM((nseq, H_C, HEAD_DIM, HEAD_DIM), F32),
                        pltpu.VMEM((nseq, C_SHIFT), F32),
                        pltpu.VMEM((W_C, -(-nseq * ch // LANES) * LANES), F32)],
        compiler_params=_cp(("parallel", "arbitrary")),
        name="rwkv_prompt",
    )(pc, *args)


def _route(x1, rw, rb):
    logits = _dot(x1.astype(BF16), rw) + rb
    lane = lax.broadcasted_iota(jnp.int32, logits.shape, 1).astype(F32)
    vals, idxs = [], []
    cur = logits
    for _ in range(TOP_K):
        m = jnp.max(cur, axis=-1, keepdims=True)
        idx = jnp.min(jnp.where(cur == m, lane, float(N_EXPERTS)), axis=-1, keepdims=True)
        vals.append(m)
        idxs.append(idx)
        cur = jnp.where(lane == idx, -jnp.inf, cur)
    ex = [jnp.exp(v - vals[0]) for v in vals]
    tot = ex[0] + ex[1] + ex[2] + ex[3]
    gates = jnp.concatenate([e / tot for e in ex], axis=-1)
    return jnp.concatenate(idxs, axis=-1).astype(jnp.int32), gates


def _outproj_kernel(x_ref, oa_ref, ob_ref, oc_ref, w_ref, g_ref, b_ref, rw_ref, rb_ref, *refs):
    x1_ref, idx_ref, gate_ref = refs[-3:]
    mix = (_dot(oa_ref[...].astype(BF16), w_ref[0:W_A, :])
           + _dot(ob_ref[...].astype(BF16), w_ref[W_A:W_A + W_B, :])
           + _dot(oc_ref[...].astype(BF16), w_ref[W_A + W_B:, :]))
    x1 = _layernorm_rows(ALPHA * x_ref[...] + mix, g_ref[...], b_ref[...])
    x1_ref[...] = x1
    idx, gates = _route(x1, rw_ref[...], rb_ref[...])
    idx_ref[...] = idx
    gate_ref[...] = gates


def _outproj(x, oa, ob, oc, w, g, b, rw, rb, bufs, row0):
    n = oa.shape[0]
    tm = min(ROW_TILE, n)
    assert row0 % tm == 0
    rows = lambda w_: pl.BlockSpec((tm, w_), lambda i: (i, 0))
    shifted = lambda w_: pl.BlockSpec((tm, w_), lambda i: (i + row0 // tm, 0))
    const = lambda a: pl.BlockSpec(a.shape, lambda i: (0,) * a.ndim)
    consts = [w.astype(BF16), g.reshape(1, -1), b.reshape(1, -1), rw.astype(BF16), rb.reshape(1, -1)]
    n_in = 4 + len(consts)
    return pl.pallas_call(
        _outproj_kernel,
        out_shape=[jax.ShapeDtypeStruct(a.shape, a.dtype) for a in bufs],
        grid=(n // tm,),
        in_specs=[rows(D_MODEL), rows(W_A), rows(W_B), rows(W_C)] + [const(a) for a in consts]
        + [pl.BlockSpec(memory_space=pl.ANY)] * len(bufs),
        out_specs=[shifted(D_MODEL), shifted(TOP_K), shifted(TOP_K)],
        input_output_aliases={n_in + j: j for j in range(len(bufs))},
        compiler_params=_cp(("parallel",)),
        name="outproj",
    )(x, oa, ob, oc, *consts, *bufs)


def _dest_kernel(idx_ref, ltri_ref, utri_ref, dest_ref, cnt_ref, cnt_scr, run_scr, start_scr):
    ph = pl.program_id(0)
    i = pl.program_id(1)
    idx = idx_ref[...]
    lane = lax.broadcasted_iota(jnp.int32, (idx.shape[0], N_EXPERTS), 1)
    hot = [idx[:, k:k + 1] == lane for k in range(TOP_K)]
    multi = jnp.zeros(lane.shape, F32)
    for k in range(TOP_K):
        multi = multi + jnp.where(hot[k], 1.0, 0.0)
    tile_cnt = jnp.sum(multi, axis=0, keepdims=True)

    @pl.when(jnp.logical_and(ph == 0, i == 0))
    def _():
        cnt_scr[...] = jnp.zeros_like(cnt_scr)

    @pl.when(ph == 0)
    def _():
        cnt_scr[...] += tile_cnt

    @pl.when(jnp.logical_and(ph == 1, i == 0))
    def _():
        cnt = cnt_scr[...]
        padded = jnp.floor((cnt + (MOE_ROWS - 1)) * (1.0 / MOE_ROWS)) * MOE_ROWS
        start_scr[...] = _dot(jnp.broadcast_to(padded, (8, N_EXPERTS)), utri_ref[...], HI)[0:1, :]
        run_scr[...] = jnp.zeros_like(run_scr)
        cnt_ref[...] = cnt

    @pl.when(ph == 1)
    def _():
        before = _dot(ltri_ref[...], multi.astype(BF16))
        base = start_scr[...] + run_scr[...] + before
        cols = [jnp.sum(jnp.where(hot[k], base, 0.0), axis=1, keepdims=True) for k in range(TOP_K)]
        dest_ref[...] = jnp.concatenate(cols, axis=1).astype(jnp.int32)
        run_scr[...] += tile_cnt


def _route_dest(top_idx):
    n = top_idx.shape[0]
    tile = max(c for c in range(COMBINE_TOK, ROW_TILE + 1, COMBINE_TOK) if n % c == 0)
    nt = n // tile
    ltri = jnp.asarray(np.tril(np.ones((tile, tile), np.float32), -1)).astype(BF16)
    utri = jnp.asarray(np.triu(np.ones((N_EXPERTS, N_EXPERTS), np.float32), 1))
    return pl.pallas_call(
        _dest_kernel,
        out_shape=[jax.ShapeDtypeStruct((n, TOP_K), jnp.int32),
                   jax.ShapeDtypeStruct((1, N_EXPERTS), F32)],
        grid=(2, nt),
        in_specs=[pl.BlockSpec((tile, TOP_K), lambda ph, i: (i, 0)),
                  pl.BlockSpec((tile, tile), lambda ph, i: (0, 0)),
                  pl.BlockSpec((N_EXPERTS, N_EXPERTS), lambda ph, i: (0, 0))],
        out_specs=[pl.BlockSpec((tile, TOP_K), lambda ph, i: (i * ph, 0)),
                   pl.BlockSpec((1, N_EXPERTS), lambda ph, i: (0, 0))],
        scratch_shapes=[pltpu.VMEM((1, N_EXPERTS), F32)] * 3,
        compiler_params=_cp(("arbitrary", "arbitrary")),
        name="moe_dest",
    )(top_idx, ltri, utri)


def _block_experts(counts, n_blk):
    cnt = counts.reshape(N_EXPERTS).astype(jnp.int32)
    pends = jnp.cumsum((cnt + MOE_ROWS - 1) // MOE_ROWS * MOE_ROWS)
    starts = jnp.arange(n_blk, dtype=jnp.int32)[:, None] * MOE_ROWS
    blk_expert = jnp.minimum(jnp.sum((pends[None, :] <= starts).astype(jnp.int32), axis=1), N_EXPERTS - 1)
    last_blk = jnp.maximum(pends // MOE_ROWS - 1, 0).astype(jnp.int32)
    return blk_expert, (pends[-1] // MOE_ROWS).astype(jnp.int32).reshape(1), last_blk


def _dispatch_kernel(last_ref, na_ref, dest_ref, x1_ref, rows_hbm, zbuf, sem, zsem, *, tm, n_blk):
    @pl.when(pl.program_id(0) == 0)
    def _():
        zbuf[...] = jnp.zeros_like(zbuf)
        n_tail = n_blk - na_ref[0]

        def clear(blk):
            return pltpu.make_async_copy(zbuf, rows_hbm.at[pl.ds(blk * MOE_ROWS, MOE_ROWS), :], zsem)

        def start(j, carry):
            clear(jnp.where(j < N_EXPERTS, last_ref[jnp.minimum(j, N_EXPERTS - 1)],
                            na_ref[0] + j - N_EXPERTS)).start()
            return carry

        def wait(j, carry):
            clear(0).wait()
            return carry

        lax.fori_loop(0, N_EXPERTS + n_tail, start, 0)
        lax.fori_loop(0, N_EXPERTS + n_tail, wait, 0)

    for r in range(tm):
        for k in range(TOP_K):
            pltpu.make_async_copy(x1_ref.at[pl.ds(r, 1), :],
                                  rows_hbm.at[pl.ds(dest_ref[0, 0, r * TOP_K + k], 1), :], sem).start()
    for k in range(TOP_K):
        pltpu.make_async_copy(x1_ref, rows_hbm.at[pl.ds(0, tm), :], sem).wait()


def _dispatch(x1, dest, last_blk, n_active, n_blk):
    n = x1.shape[0]
    tm = COMBINE_TOK
    nt = n // tm
    kern = functools.partial(_dispatch_kernel, tm=tm, n_blk=n_blk)
    grid_spec = pltpu.PrefetchScalarGridSpec(
        num_scalar_prefetch=2,
        grid=(nt,),
        in_specs=[pl.BlockSpec((1, 1, TOP_K * tm), lambda i, lb, na: (i, 0, 0), memory_space=pltpu.SMEM),
                  pl.BlockSpec((tm, D_MODEL), lambda i, lb, na: (i, 0))],
        out_specs=pl.BlockSpec(memory_space=pl.ANY),
        scratch_shapes=[pltpu.VMEM((MOE_ROWS, D_MODEL), F32), pltpu.SemaphoreType.DMA,
                        pltpu.SemaphoreType.DMA],
    )
    return pl.pallas_call(
        kern,
        out_shape=jax.ShapeDtypeStruct((n_blk * MOE_ROWS, D_MODEL), F32),
        grid_spec=grid_spec,
        compiler_params=_cp(("arbitrary",)),
        name="moe_dispatch",
    )(last_blk, n_active, dest.reshape(nt, 1, TOP_K * tm), x1)


def _swiglu(h):
    h_glu = jnp.minimum(h[:, :D_FF], SWIGLU_LIMIT)
    h_lin = jnp.clip(h[:, D_FF:], -SWIGLU_LIMIT, SWIGLU_LIMIT)
    return h_glu * _sigmoid(SWIGLU_ALPHA * h_glu) * (h_lin + 1.0)


def _moe_kernel(be_ref, na_ref, x_ref, wup_ref, bup_ref, wdn_ref, bdn_ref, y_ref, wup_b, wdn_b):
    i = pl.program_id(0)
    active = i < na_ref[0]

    @pl.when(active)
    def _():
        changed = jnp.logical_or(i == 0, be_ref[i] != be_ref[jnp.maximum(i - 1, 0)])

        @pl.when(changed)
        def _():
            wup_b[...] = wup_ref[...].astype(BF16)
            wdn_b[...] = wdn_ref[...].astype(BF16)

        h = _dot(x_ref[...].astype(BF16), wup_b[...]) + bup_ref[...]
        y_ref[...] = _dot(_swiglu(h).astype(BF16), wdn_b[...]) + bdn_ref[...]

    @pl.when(jnp.logical_not(active))
    def _():
        y_ref[...] = jnp.zeros_like(y_ref)


def _moe_rows(rows, blk_expert, n_active, w_up, b_up, w_dn, b_dn, layer):
    n_blk = blk_expert.shape[0]
    grid_spec = pltpu.PrefetchScalarGridSpec(
        num_scalar_prefetch=2,
        grid=(n_blk,),
        in_specs=[
            pl.BlockSpec((MOE_ROWS, D_MODEL), lambda i, be, na: (jnp.minimum(i, jnp.maximum(na[0] - 1, 0)), 0)),
            pl.BlockSpec((None, None, D_MODEL, 2 * D_FF), lambda i, be, na: (layer, be[i], 0, 0)),
            pl.BlockSpec((None, None, 1, 2 * D_FF), lambda i, be, na: (layer, be[i], 0, 0)),
            pl.BlockSpec((None, None, D_FF, D_MODEL), lambda i, be, na: (layer, be[i], 0, 0)),
            pl.BlockSpec((None, None, 1, D_MODEL), lambda i, be, na: (layer, be[i], 0, 0)),
        ],
        out_specs=pl.BlockSpec((MOE_ROWS, D_MODEL), lambda i, be, na: (i, 0)),
        scratch_shapes=[pltpu.VMEM((D_MODEL, 2 * D_FF), BF16),
                        pltpu.VMEM((D_FF, D_MODEL), BF16)],
    )
    return pl.pallas_call(
        _moe_kernel,
        out_shape=jax.ShapeDtypeStruct((n_blk * MOE_ROWS, D_MODEL), F32),
        grid_spec=grid_spec,
        compiler_params=_cp(("arbitrary",)),
        name="moe_rows",
    )(blk_expert, n_active, rows,
      w_up, b_up.reshape(DEPTH, N_EXPERTS, 1, 2 * D_FF), w_dn, b_dn.reshape(DEPTH, N_EXPERTS, 1, D_MODEL))


def _combine_kernel(dest_ref, destn_ref, gate_ref, x1_ref, y_hbm, g_ref, b_ref, o_ref, tail_ref, ybuf, sem,
                    *, tm, head_tiles):
    i = pl.program_id(0)
    slot = i % 2
    n_rows = TOP_K * tm

    def gather_tile(dests, s):
        for r in range(n_rows):
            pltpu.make_async_copy(y_hbm.at[pl.ds(dests[0, 0, r], 1), :],
                                  ybuf.at[s, pl.ds(r, 1), :], sem.at[s]).start()

    @pl.when(i == 0)
    def _():
        gather_tile(dest_ref, 0)

    for s in range(2):
        @pl.when(jnp.logical_and(i + 1 < pl.num_programs(0), slot == 1 - s))
        def _(s=s):
            gather_tile(destn_ref, s)

    for s in range(2):
        @pl.when(slot == s)
        def _(s=s):
            pltpu.make_async_copy(y_hbm.at[pl.ds(0, n_rows), :], ybuf.at[s], sem.at[s]).wait()
            gates = gate_ref[...]
            moe = jnp.zeros((tm, D_MODEL), F32)
            for k in range(TOP_K):
                moe = moe + gates[:, k:k + 1] * ybuf[s, k * tm:(k + 1) * tm, :]
            res = _layernorm_rows(ALPHA * x1_ref[...] + moe, g_ref[...], b_ref[...])

            @pl.when(i < head_tiles)
            def _():
                o_ref[...] = res

            @pl.when(i >= head_tiles)
            def _():
                tail_ref[...] = res


def _combine(dest, gates, x1, y_rows, g, b, n_head):
    n = x1.shape[0]
    tm = COMBINE_TOK
    nt = n // tm
    head_tiles = n_head // tm
    dest_t = dest.reshape(nt, tm, TOP_K).transpose(0, 2, 1).reshape(nt, 1, TOP_K * tm)
    kern = functools.partial(_combine_kernel, tm=tm, head_tiles=head_tiles)
    return pl.pallas_call(
        kern,
        out_shape=[jax.ShapeDtypeStruct((n_head, D_MODEL), F32),
                   jax.ShapeDtypeStruct((n - n_head, D_MODEL), F32)],
        grid=(nt,),
        in_specs=[pl.BlockSpec((1, 1, TOP_K * tm), lambda i: (i, 0, 0), memory_space=pltpu.SMEM),
                  pl.BlockSpec((1, 1, TOP_K * tm), lambda i: (jnp.minimum(i + 1, nt - 1), 0, 0),
                               memory_space=pltpu.SMEM),
                  pl.BlockSpec((tm, TOP_K), lambda i: (i, 0)),
                  pl.BlockSpec((tm, D_MODEL), lambda i: (i, 0)),
                  pl.BlockSpec(memory_space=pl.ANY),
                  pl.BlockSpec((1, D_MODEL), lambda i: (0, 0)),
                  pl.BlockSpec((1, D_MODEL), lambda i: (0, 0))],
        out_specs=[pl.BlockSpec((tm, D_MODEL), lambda i: (jnp.minimum(i, head_tiles - 1), 0)),
                   pl.BlockSpec((tm, D_MODEL), lambda i: (jnp.maximum(i - head_tiles, 0), 0))],
        scratch_shapes=[pltpu.VMEM((2, TOP_K * tm, D_MODEL), F32), pltpu.SemaphoreType.DMA((2,))],
        compiler_params=_cp(("arbitrary",)),
        name="moe_combine",
    )(dest_t, dest_t, gates, x1, y_rows, g.reshape(1, -1), b.reshape(1, -1))


def _moe(x1, top_idx, gates, p, layer, n_head):
    n = x1.shape[0]
    n_blk = -(-n * TOP_K // MOE_ROWS) + N_EXPERTS
    dest, counts = _route_dest(top_idx)
    blk_expert, n_active, last_blk = _block_experts(counts, n_blk)
    rows = _dispatch(x1, dest, last_blk, n_active, n_blk)
    y_rows = _moe_rows(rows, blk_expert, n_active, p["moe_w_up"], p["moe_b_up"],
                       p["moe_w_down"], p["moe_b_down"], layer)
    return _combine(dest, gates, x1, y_rows, p["ln2_g"][layer], p["ln2_b"][layer], n_head)


def _sample_proj_kernel(x_ref, w_ref, h0_ref, lam_ref, bblk_ref, cblk_ref, d_ref, gw_ref, gb_ref,
                        q_ref, k_ref, v_ref, pc_ref, oa_ref, h_ref):
    xb = x_ref[...].astype(BF16)
    mm = lambda lo, hi: _dot(xb, w_ref[:, lo:hi])
    u = mm(_PROJ_SPLITS[0], _PROJ_SPLITS[1])
    q_ref[...] = mm(_PROJ_SPLITS[1], _PROJ_SPLITS[2])
    k_ref[...] = mm(_PROJ_SPLITS[2], _PROJ_SPLITS[3])
    v_ref[...] = mm(_PROJ_SPLITS[3], _PROJ_SPLITS[4])
    pc_ref[...] = mm(_PROJ_SPLITS[4], _PROJ_SPLITS[5])
    bu = _dot(u.astype(BF16), bblk_ref[...])
    bu_re = bu[:, 0:N_ST]
    bu_im = bu_re + bu[:, N_ST:2 * N_ST]
    lr, li = lam_ref[0:1, :], lam_ref[1:2, :]
    hr0, hi0 = h0_ref[0], h0_ref[1]
    hr = bu_re + (lr * hr0 - li * hi0)
    hi = bu_im + (lr * hi0 + li * hr0)
    h_ref[0] = hr
    h_ref[1] = hi
    y = (_dot((hr + hi).astype(BF16), cblk_ref[0:N_ST, :])
         - _dot(hi.astype(BF16), cblk_ref[N_ST:2 * N_ST, :])) + d_ref[...] * u
    z = _gelu(y)
    oa_ref[...] = z * _sigmoid(_dot(z.astype(BF16), gw_ref[...]) + gb_ref[...])


def _sample_proj(x, w_in, h0, lam2, bblk, cblk, d, gw, gb):
    n = x.shape[0]
    widths = (W_B, W_B, W_B, C_SHIFT, W_A)
    return pl.pallas_call(
        _sample_proj_kernel,
        out_shape=[jax.ShapeDtypeStruct((n, w), F32) for w in widths]
        + [jax.ShapeDtypeStruct((2, n, N_ST), F32)],
        compiler_params=_cp(None),
        name="sample_proj",
    )(x, w_in, h0, lam2, bblk, cblk, d.reshape(1, -1), gw.astype(BF16), gb.reshape(1, -1))


def _sample_mix_kernel(q_ref, kn_ref, vn_ref, pc_ref, sh_ref, kc_ref, vc_ref, s0_ref, lb_ref, lb0_ref,
                       mu_ref, w0_ref, w2_ref, a0_ref, a2_ref, g2_ref, kk_ref, ka_ref,
                       rk_ref, lng_ref, lnb_ref, mseg_ref,
                       ob_ref, oc_ref, s_ref):
    rows = 8
    lane = lax.broadcasted_iota(jnp.int32, (rows, W_B), 1)
    sub = lax.broadcasted_iota(jnp.int32, (rows, W_B), 0)
    hmask = (lane // HEAD_DIM) == sub
    rnd = lambda t: t.astype(BF16).astype(F32)
    q = q_ref[0] * (HEAD_DIM ** -0.5)
    qrows = jnp.where(hmask, jnp.broadcast_to(q, (rows, W_B)), 0.0).astype(BF16)
    kcb = kc_ref[0].astype(BF16)
    vcb = vc_ref[0].astype(BF16)
    knb = rnd(kn_ref[0])
    vnb = rnd(vn_ref[0])
    s_all = _dot(qrows, kcb)
    s_new = jnp.sum(qrows.astype(F32) * knb, axis=-1, keepdims=True)
    outs, lses = [], []
    for br in range(len(BRANCHES)):
        s = s_all + lb_ref[br]
        s0 = s_new + lb0_ref[br]
        m = jnp.maximum(jnp.max(s, axis=-1, keepdims=True), s0)
        lse = m + jnp.log(jnp.sum(jnp.exp(s - m), axis=-1, keepdims=True) + jnp.exp(s0 - m))
        o = _dot_nt(jnp.exp(s - lse).astype(BF16), vcb) + rnd(jnp.exp(s0 - lse)) * vnb
        outs.append(rnd(o))
        lses.append(lse)
    top = jnp.maximum(jnp.maximum(lses[0], lses[1]), lses[2])
    ex = [jnp.exp(t - top) for t in lses]
    tot = ex[0] + ex[1] + ex[2]
    o = rnd(ex[0] / tot) * outs[0] + rnd(ex[1] / tot) * outs[1] + rnd(ex[2] / tot) * outs[2]
    ob_ref[0] = jnp.sum(jnp.where(hmask, o, 0.0), axis=0, keepdims=True)

    prm = (mu_ref[...], w0_ref[...], w2_ref[...], a0_ref[...], a2_ref[...], g2_ref[...],
           kk_ref[...], ka_ref[...], rk_ref[...], lng_ref[...], lnb_ref[...], mseg_ref[...])
    pc = jnp.broadcast_to(pc_ref[0], (rows, C_SHIFT))
    prev = jnp.broadcast_to(sh_ref[0], (rows, C_SHIFT))
    seg_sum = lambda t: _dot(t, mseg_ref[...], HI)
    r, k2, v, kk, a, g, logw = _rwkv_pre(pc, prev, prm, seg_sum)
    w = jnp.exp(logw)
    eye = (lax.broadcasted_iota(jnp.int32, (HEAD_DIM, HEAD_DIM), 0)
           == lax.broadcasted_iota(jnp.int32, (HEAD_DIM, HEAD_DIM), 1))
    ys = []
    for h in range(H_C):
        sl = slice(h * HEAD_DIM, (h + 1) * HEAD_DIM)
        row = lambda t: t[0:1, sl]
        col = lambda t: jnp.sum(jnp.where(eye, jnp.broadcast_to(row(t), (HEAD_DIM, HEAD_DIM)), 0.0),
                                axis=1, keepdims=True)
        s0 = s0_ref[0, h]
        sa = jnp.sum(rnd(s0) * rnd(-row(kk)), axis=1, keepdims=True)
        s1 = s0 * row(w) + sa * (row(kk) * row(a)) + col(v) * row(k2)
        s_ref[0, h] = s1
        y_col = jnp.sum(rnd(s1) * rnd(row(r)), axis=1, keepdims=True)
        ys.append(jnp.sum(jnp.where(eye, jnp.broadcast_to(y_col, (HEAD_DIM, HEAD_DIM)), 0.0),
                          axis=0, keepdims=True))
    y = jnp.broadcast_to(jnp.concatenate(ys, axis=1), (rows, W_C))
    oc_ref[0] = _rwkv_post(y, r, k2, v, g, prm, seg_sum)[0:1, :]


def _sample_mix(q, kn, vn, pc, shift0, k_cache, v_cache, s0, lb_rows, lb0, prm, layer):
    n = q.shape[0]
    l_buf = k_cache.shape[3]
    per_b = lambda w: pl.BlockSpec((1, 1, w), lambda b: (b, 0, 0))
    const = lambda a: pl.BlockSpec(a.shape, lambda b: (0,) * a.ndim)
    r3 = lambda t: t.reshape(n, 1, t.shape[-1])
    cache = pl.BlockSpec((None, 1, W_B, l_buf), lambda b: (layer, b, 0, 0))
    return pl.pallas_call(
        _sample_mix_kernel,
        out_shape=[jax.ShapeDtypeStruct((n, 1, W_B), F32),
                   jax.ShapeDtypeStruct((n, 1, W_C), F32),
                   jax.ShapeDtypeStruct((n, H_C, HEAD_DIM, HEAD_DIM), F32)],
        grid=(n,),
        in_specs=[per_b(W_B), per_b(W_B), per_b(W_B), per_b(C_SHIFT), per_b(C_SHIFT),
                  cache, cache,
                  pl.BlockSpec((1, H_C, HEAD_DIM, HEAD_DIM), lambda b: (b, 0, 0, 0)),
                  const(lb_rows), const(lb0)] + [const(a) for a in prm],
        out_specs=[per_b(W_B), per_b(W_C),
                   pl.BlockSpec((1, H_C, HEAD_DIM, HEAD_DIM), lambda b: (b, 0, 0, 0))],
        compiler_params=_cp(("parallel",)),
        name="sample_mix",
    )(r3(q), r3(kn), r3(vn), r3(pc), r3(shift0), k_cache, v_cache, s0, lb_rows, lb0, *prm)


def _prompt_mixers(x, nb, t, p, l, s5p, rwp, tiles, kv_prev):
    u, q, k_all, v_all, pc = _proj(x, p["w_in"][l].astype(BF16), nb, t, l, kv_prev)
    shp = lambda a: a.reshape(nb, t, a.shape[-1])
    o_a, h_t = _s5_prompt(shp(u), *s5p, p["ssm_d"][l], p["ssm_glu_w"][l], p["ssm_glu_b"][l])
    o_b = _attn_prompt(shp(q), k_all, v_all, tiles, l)
    o_c, s_t = _rwkv_prompt(shp(pc), rwp)
    flat = lambda a: a.reshape(nb * t, a.shape[-1])
    ssm = jnp.transpose(h_t.reshape(2, nb, G_A, N_A), (1, 2, 3, 0))
    return (flat(o_a), flat(o_b), flat(o_c)), (ssm, s_t, shp(pc)[:, t - 1]), (k_all, v_all)


def _sample_bias(rel_bias, l_buf):
    offs = _branch_offsets()
    bias = rel_bias[jnp.asarray(_t5_buckets(offs))]
    rows = []
    for br in range(len(BRANCHES)):
        m = np.arange(1, offs.shape[1])
        pos = l_buf - offs[br][m]
        ok = pos >= 0
        r = jnp.full((H_B, l_buf), NEG_INF, F32).at[:, jnp.asarray(pos[ok])].set(bias[br][jnp.asarray(m[ok])].T)
        rows.append(jnp.concatenate([r, jnp.zeros((8 - H_B, l_buf), F32)], axis=0))
    new = jnp.zeros((len(BRANCHES), 8, 1), F32).at[:, :H_B, 0].set(bias[:, 0])
    return jnp.stack(rows), new


def _sample_mixers(x, p, l, s5p, rwp, sbias, k_cache, v_cache, st_ssm, st_rwkv, st_shift):
    n = x.shape[0]
    lam2, bblk, cblk = s5p
    h0 = jnp.transpose(st_ssm.reshape(n, N_ST, 2), (2, 0, 1))
    q, kn, vn, pc, o_a, h1 = _sample_proj(x, p["w_in"][l].astype(BF16), h0, lam2, bblk, cblk, p["ssm_d"][l],
                                          p["ssm_glu_w"][l], p["ssm_glu_b"][l])
    lb_rows, lb0 = sbias
    o_b, o_c, s1 = _sample_mix(q, kn, vn, pc, st_shift, k_cache, v_cache, st_rwkv, lb_rows, lb0, rwp, l)
    ssm = jnp.transpose(h1.reshape(2, n, G_A, N_A), (1, 2, 3, 0))
    return (o_a, o_b.reshape(n, W_B), o_c.reshape(n, W_C)), (kn, vn, ssm, s1, pc)


def _shift_kernel(c_ref, new_ref, o_ref):
    x = c_ref[...]
    length = x.shape[1]
    lane = lax.broadcasted_iota(jnp.int32, x.shape, 1)
    o_ref[...] = jnp.where(lane == length - 1, new_ref[...], pltpu.roll(x, length - 1, axis=1))


def _shift_cache(cache_t, new):
    rows, length = cache_t.shape
    blk = min(ROW_TILE, rows)
    return pl.pallas_call(
        _shift_kernel,
        out_shape=jax.ShapeDtypeStruct((rows, length), F32),
        grid=(rows // blk,),
        in_specs=[pl.BlockSpec((blk, length), lambda i: (i, 0)),
                  pl.BlockSpec((blk, 1), lambda i: (i, 0))],
        out_specs=pl.BlockSpec((blk, length), lambda i: (i, 0)),
        compiler_params=_cp(("parallel",)),
        name="shift_cache",
    )(cache_t, new)


def kernel(x_prompt, x_sample, cache_attn_k, cache_attn_v, state_ssm, state_rwkv, state_shift, w_in, w_out, ln1_g, ln1_b, ln2_g, ln2_b, ssm_a_re, ssm_a_im, ssm_log_dt, ssm_b_re, ssm_b_im, ssm_c_re, ssm_c_im, ssm_d, ssm_glu_w, ssm_glu_b, rel_bias, rwkv_mu, rwkv_w0, rwkv_w2, rwkv_a0, rwkv_a2, rwkv_g2, rwkv_k_k, rwkv_k_a, rwkv_r_k, rwkv_ln_g, rwkv_ln_b, moe_router_w, moe_router_b, moe_w_up, moe_b_up, moe_w_down, moe_b_down):
    p = dict(w_in=w_in, w_out=w_out, ln1_g=ln1_g, ln1_b=ln1_b, ln2_g=ln2_g, ln2_b=ln2_b,
             ssm_d=ssm_d, ssm_glu_w=ssm_glu_w, ssm_glu_b=ssm_glu_b,
             moe_router_w=moe_router_w, moe_router_b=moe_router_b, moe_w_up=moe_w_up,
             moe_b_up=moe_b_up, moe_w_down=moe_w_down, moe_b_down=moe_b_down)
    nb, t, _ = x_prompt.shape
    ns = x_sample.shape[0]
    l_buf = cache_attn_k.shape[2]
    table = _log_bias_table(rel_bias)
    tiles = _bias_tiles(table, min(t, WIN_MAX) // Q_TILE + 1)
    sbias = _sample_bias(rel_bias, l_buf)
    n_p = nb * t
    n_all = n_p + -(-ns // COMBINE_TOK) * COMBINE_TOK
    xp = x_prompt.reshape(n_p, D_MODEL)
    xs = x_sample.reshape(ns, D_MODEL)
    st_p, st_s = [], []
    time_minor = lambda c: jnp.transpose(c, (0, 1, 3, 4, 2)).reshape(DEPTH, c.shape[1], W_B, c.shape[2])
    time_major = lambda c, n: jnp.transpose(c.reshape(DEPTH, n, H_B, HEAD_DIM, c.shape[-1]), (0, 1, 4, 2, 3))
    kc_t, vc_t = time_minor(cache_attn_k), time_minor(cache_attn_v)
    kv = (jnp.zeros((DEPTH, nb, W_B, t), F32), jnp.zeros((DEPTH, nb, W_B, t), F32))
    for l in range(DEPTH):
        s5p, s5s = _s5_params(ssm_a_re[l], ssm_a_im[l], ssm_log_dt[l], ssm_b_re[l], ssm_b_im[l],
                              ssm_c_re[l], ssm_c_im[l])
        rwp = _rwkv_params(dict(rwkv_mu=rwkv_mu[l], rwkv_w0=rwkv_w0[l], rwkv_w2=rwkv_w2[l],
                                rwkv_a0=rwkv_a0[l], rwkv_a2=rwkv_a2[l], rwkv_g2=rwkv_g2[l],
                                rwkv_k_k=rwkv_k_k[l], rwkv_k_a=rwkv_k_a[l], rwkv_r_k=rwkv_r_k[l],
                                rwkv_ln_g=rwkv_ln_g[l], rwkv_ln_b=rwkv_ln_b[l]))
        mix_p, sp, kv = _prompt_mixers(xp, nb, t, p, l, s5p, rwp, tiles, kv)
        mix_s, ss = _sample_mixers(xs, p, l, s5s, rwp, sbias, kc_t, vc_t,
                                   state_ssm[l], state_rwkv[l], state_shift[l])
        st_p.append(sp)
        st_s.append(ss)
        bufs = (jnp.zeros((n_all, D_MODEL), F32),
                jnp.broadcast_to(jnp.arange(TOP_K, dtype=jnp.int32), (n_all, TOP_K)),
                jnp.zeros((n_all, TOP_K), F32))
        head = (p["w_out"][l], p["ln1_g"][l], p["ln1_b"][l], p["moe_router_w"][l], p["moe_router_b"][l])
        bufs = _outproj(xp, *mix_p, *head, bufs, 0)
        bufs = _outproj(xs, *mix_s, *head, bufs, n_p)
        xp, tail = _moe(*bufs, p, l, n_p)
        xs = tail[:ns]
    p_ssm, p_rwkv, p_shift = (jnp.stack(z) for z in zip(*st_p))
    kn, vn, s_ssm, s_rwkv, s_shift = (jnp.stack(z) for z in zip(*st_s))
    keep_p = min(WIN_MAX, t)
    p_k, p_v = (time_major(a[..., t - keep_p:], nb) for a in kv)
    keep = min(WIN_MAX, l_buf + 1)
    drop = l_buf + 1 - keep
    if drop == 1:
        advance = lambda c_t, new: _shift_cache(c_t.reshape(-1, l_buf), new.reshape(-1, 1)).reshape(c_t.shape)
    else:
        advance = lambda c_t, new: jnp.concatenate([c_t[..., drop:], new[..., None]], axis=-1)
    s_k = time_major(advance(kc_t, kn), ns)
    s_v = time_major(advance(vc_t, vn), ns)
    return (xp.reshape(nb, t, D_MODEL), xs.reshape(ns, 1, D_MODEL),
            p_k, p_v, p_ssm, p_rwkv, p_shift, s_k, s_v, s_ssm, s_rwkv, s_shift)
```

```python
import functools
import math

import numpy as np
import jax
import jax.numpy as jnp
from jax import lax
from jax.experimental import pallas as pl
from jax.experimental.pallas import tpu as pltpu

F32 = jnp.float32
BF16 = jnp.bfloat16
HI = lax.Precision.HIGHEST

D_MODEL = 1024
DEPTH = 2
HEAD_DIM = 64
W_A = 256
C_GRP = 16
G_A = 16
N_A = 64
N_ST = G_A * N_A
W_B = 384
H_B = 6
BRANCHES = ((128, 1), (512, 4), (2048, 16))
WIN_MAX = 2048
N_BUCKETS = 32
MAX_DIST = WIN_MAX
NEG_INF = -1e30
W_C = 384
H_C = 6
R_W = 32
R_A = 32
R_G = 64
C_SHIFT = 3 * W_C + R_W + R_A + R_G
N_IN = W_A + 3 * W_B + C_SHIFT
GN_EPS = 64e-5
N_EXPERTS = 32
TOP_K = 4
D_FF = D_MODEL
SWIGLU_ALPHA = 1.702
SWIGLU_LIMIT = 7.0
ALPHA = (2 * DEPTH) ** 0.25
LN_EPS = 1e-5

LANES = 128
Q_TILE = 128
RWKV_CHUNK = 64
RWKV_SEQS = 2
S5_CHUNK = 64
ROW_TILE = 512
MOE_ROWS = 256
COMBINE_TOK = 128
VMEM_LIMIT = 56 * 1024 * 1024


def _cp(sem, vmem=VMEM_LIMIT):
    return pltpu.CompilerParams(dimension_semantics=sem, vmem_limit_bytes=vmem)


def _dot(a, b, precision=None):
    return jnp.dot(a, b, preferred_element_type=F32, precision=precision)


def _dot_nt(a, b, precision=None):
    return lax.dot_general(a, b, (((1,), (1,)), ((), ())),
                           preferred_element_type=F32, precision=precision)


def _dot_tn(a, b, precision=None):
    return lax.dot_general(a, b, (((0,), (0,)), ((), ())),
                           preferred_element_type=F32, precision=precision)


def _layernorm_rows(x, g, b):
    mu = jnp.mean(x, axis=-1, keepdims=True)
    d = x - mu
    var = jnp.mean(d * d, axis=-1, keepdims=True)
    return d * lax.rsqrt(var + LN_EPS) * g + b


def _sigmoid(x):
    return 1.0 / (1.0 + jnp.exp(-x))


def _softplus(x):
    return jnp.maximum(x, 0.0) + jnp.log(1.0 + jnp.exp(-jnp.abs(x)))


def _gelu(x):
    c = math.sqrt(2.0 / math.pi)
    return 0.5 * x * (1.0 + jnp.tanh(c * (x + 0.044715 * (x * x * x))))


_PROJ_SPLITS = (0, W_A, W_A + W_B, W_A + 2 * W_B, W_A + 3 * W_B, N_IN)


def _proj_kernel(x_ref, w_ref, *refs):
    u_ref, q_ref, k_ref, v_ref, pc_ref = refs[-5:]
    xb = x_ref[...].astype(BF16)
    outs = ((u_ref, False), (q_ref, False), (k_ref, True), (v_ref, True), (pc_ref, False))
    for (o_ref, time_minor), lo, hi in zip(outs, _PROJ_SPLITS[:-1], _PROJ_SPLITS[1:]):
        y = _dot(xb, w_ref[:, lo:hi])
        o_ref[...] = jnp.transpose(y) if time_minor else y


def _proj(x, w_bf16, nb, t, layer, kv_prev):
    n = nb * t
    per_seq = t // ROW_TILE
    widths = [hi - lo for lo, hi in zip(_PROJ_SPLITS[:-1], _PROJ_SPLITS[1:])]
    rows = lambda w: pl.BlockSpec((ROW_TILE, w), lambda i: (i, 0))
    stacked = pl.BlockSpec((None, None, W_B, ROW_TILE), lambda i: (layer, i // per_seq, 0, i % per_seq))
    flat = lambda w: jax.ShapeDtypeStruct((n, w), F32)
    kv = jax.ShapeDtypeStruct((DEPTH, nb, W_B, t), F32)
    prev = list(kv_prev)
    return pl.pallas_call(
        _proj_kernel,
        out_shape=[flat(widths[0]), flat(widths[1]), kv, kv, flat(widths[4])],
        grid=(n // ROW_TILE,),
        in_specs=[pl.BlockSpec((ROW_TILE, D_MODEL), lambda i: (i, 0)),
                  pl.BlockSpec((D_MODEL, N_IN), lambda i: (0, 0))]
        + [pl.BlockSpec(memory_space=pl.ANY)] * len(prev),
        out_specs=[rows(widths[0]), rows(widths[1]), stacked, stacked, rows(widths[4])],
        input_output_aliases={2: 2, 3: 3},
        compiler_params=_cp(("parallel",)),
        name="proj",
    )(x, w_bf16, *prev)


def _s5_kernel(u_ref, lam_ref, bblk_ref, cblk_ref, d_ref, gw_ref, gb_ref,
               o_ref, ht_ref, utm, hs, ytm, h_scr, *, nb, lt):
    c = pl.program_id(0)

    @pl.when(c == 0)
    def _():
        h_scr[...] = jnp.zeros_like(h_scr)

    n_half = W_A // LANES
    for b in range(nb):
        for j in range(n_half):
            utm[j, pl.ds(b, lt, stride=nb), :] = u_ref[b, :, j * LANES:(j + 1) * LANES]
    u_all = jnp.concatenate([utm[j] for j in range(n_half)], axis=1)
    hs[...] = _dot(u_all.astype(BF16), bblk_ref[...])
    lr = jnp.broadcast_to(lam_ref[0:1, :], (nb, N_ST))
    li = jnp.broadcast_to(lam_ref[1:2, :], (nb, N_ST))

    def body(t, carry):
        hr, hi = carry
        row = pl.multiple_of(t * nb, nb)
        br = hs[pl.ds(row, nb), 0:N_ST]
        bi = hs[pl.ds(row, nb), N_ST:2 * N_ST]
        nr = lr * hr - li * hi + br
        ni = lr * hi + li * hr + bi
        hs[pl.ds(row, nb), 0:N_ST] = nr
        hs[pl.ds(row, nb), N_ST:2 * N_ST] = ni
        return nr, ni

    hr, hi = lax.fori_loop(0, lt, body, (h_scr[0], h_scr[1]), unroll=2)
    h_scr[0] = hr
    h_scr[1] = hi
    ht_ref[0] = hr
    ht_ref[1] = hi
    y = _dot(hs[...].astype(BF16), cblk_ref[...]) + d_ref[...] * u_all
    z = _gelu(y)
    gl = _dot(z.astype(BF16), gw_ref[...]) + gb_ref[...]
    res = z * _sigmoid(gl)
    for j in range(n_half):
        ytm[j] = res[:, j * LANES:(j + 1) * LANES]
    for b in range(nb):
        for j in range(n_half):
            o_ref[b, :, j * LANES:(j + 1) * LANES] = ytm[j, pl.ds(b, lt, stride=nb), :]


def _s5_params(a_re, a_im, log_dt, b_re, b_im, c_re, c_im):
    lam = lax.complex(a_re, a_im)
    dt = jnp.exp(log_dt)[:, None]
    lam_bar = jnp.exp(lam * dt)
    b_bar = ((lam_bar - 1.0) / lam)[..., None] * lax.complex(b_re, b_im)
    eye = jnp.eye(G_A, dtype=F32)

    def blk_in(m):
        return jnp.einsum("gnc,gh->gchn", m, eye).reshape(W_A, N_ST)

    def blk_out(m):
        return jnp.einsum("gcn,gh->gnhc", m, eye).reshape(N_ST, W_A)

    bblk = jnp.concatenate([blk_in(b_bar.real), blk_in(b_bar.imag)], axis=1)
    cblk = jnp.concatenate([blk_out(c_re), blk_out(-c_im)], axis=0)
    lam2 = jnp.stack([lam_bar.real.reshape(N_ST), lam_bar.imag.reshape(N_ST)])
    bblk3 = jnp.concatenate([blk_in(b_bar.real), blk_in(b_bar.imag - b_bar.real)], axis=1)
    cblk3 = jnp.concatenate([blk_out(c_re), blk_out(c_re + c_im)], axis=0)
    return (lam2, bblk, cblk), (lam2, bblk3.astype(BF16), cblk3.astype(BF16))


def _s5_prompt(u, lam2, bblk, cblk, d, gw, gb):
    nb, t, _ = u.shape
    lt = min(S5_CHUNK, t)
    kern = functools.partial(_s5_kernel, nb=nb, lt=lt)
    const = lambda shape: pl.BlockSpec(shape, lambda c: (0,) * len(shape))
    return pl.pallas_call(
        kern,
        out_shape=[jax.ShapeDtypeStruct((nb, t, W_A), F32),
                   jax.ShapeDtypeStruct((2, nb, N_ST), F32)],
        grid=(t // lt,),
        in_specs=[pl.BlockSpec((nb, lt, W_A), lambda c: (0, c, 0)),
                  const((2, N_ST)), const((W_A, 2 * N_ST)), const((2 * N_ST, W_A)),
                  const((1, W_A)), const((W_A, W_A)), const((1, W_A))],
        out_specs=[pl.BlockSpec((nb, lt, W_A), lambda c: (0, c, 0)),
                   const((2, nb, N_ST))],
        scratch_shapes=[pltpu.VMEM((W_A // LANES, nb * lt, LANES), F32),
                        pltpu.VMEM((nb * lt, 2 * N_ST), F32),
                        pltpu.VMEM((W_A // LANES, nb * lt, LANES), F32),
                        pltpu.VMEM((2, nb, N_ST), F32)],
        compiler_params=_cp(("arbitrary",)),
        name="s5_prompt",
    )(u, lam2, bblk.astype(BF16), cblk.astype(BF16), d.reshape(1, W_A),
      gw.astype(BF16), gb.reshape(1, W_A))


def _branch_offsets():
    return np.stack([np.arange(w // d + 1) * d for (w, d) in BRANCHES]).astype(np.int32)


def _t5_buckets(dist):
    max_exact = N_BUCKETS // 2
    d = np.maximum(dist, 1).astype(np.float32)
    large = max_exact + (np.log(d / max_exact) / np.log(MAX_DIST / max_exact)
                         * (N_BUCKETS - max_exact)).astype(np.int32)
    return np.where(dist < max_exact, dist, np.minimum(large, N_BUCKETS - 1)).astype(np.int32)


def _log_bias_table(rel_bias):
    offs = _branch_offsets()
    bias = rel_bias[jnp.asarray(_t5_buckets(offs))]
    table = jnp.full((WIN_MAX + 1, H_B), -jnp.inf, F32)
    for br in range(len(BRANCHES)):
        idx = jnp.asarray(offs[br])
        table = table.at[idx].set(jnp.logaddexp(table[idx], bias[br]))
    return jnp.maximum(table, NEG_INF)


def _bias_tiles(table, n_diff):
    per = 2 * Q_TILE
    m = np.arange(per)
    dist = np.arange(n_diff)[:, None] * Q_TILE + np.where(m < Q_TILE, -m, per - m)[None, :]
    ok = (dist >= 0) & (dist <= WIN_MAX) & (m != Q_TILE)[None, :]
    v = jnp.where(jnp.asarray(ok)[..., None], table[jnp.asarray(np.clip(dist, 0, WIN_MAX))], NEG_INF)
    v = jnp.transpose(v, (2, 0, 1))
    flat = jnp.broadcast_to(v[:, :, None, :], (H_B, n_diff, Q_TILE, per)).reshape(H_B, n_diff, Q_TILE * per)
    tiles = flat[:, :, :Q_TILE * (per - 1)].reshape(H_B, n_diff, Q_TILE, per - 1)[..., :Q_TILE]
    tiles = tiles.reshape(H_B // 2, 2, n_diff, Q_TILE, Q_TILE)
    return jnp.transpose(tiles, (0, 2, 1, 3, 4)).reshape(H_B // 2, n_diff, 2 * Q_TILE, Q_TILE)


def _attn_kernel(q_ref, k_ref, v_ref, bias_ref, o_ref, kb, vb, *, nq):
    kb[...] = k_ref[...].astype(BF16)
    vb[...] = v_ref[...].astype(BF16)
    lo = lax.broadcasted_iota(jnp.int32, (Q_TILE, LANES), 1) < HEAD_DIM
    for qi in range(nq):
        rows = slice(qi * Q_TILE, (qi + 1) * Q_TILE)
        q = q_ref[0, rows, :] * (HEAD_DIM ** -0.5)
        q2 = jnp.concatenate([jnp.where(lo, q, 0.0), jnp.where(lo, 0.0, q)], axis=0).astype(BF16)
        nk = (qi + 1) * Q_TILE
        bias = jnp.concatenate([bias_ref[qi - kj] for kj in range(qi + 1)], axis=1)
        s = _dot(q2, kb[:, 0:nk]) + bias
        m = jnp.max(s, axis=-1, keepdims=True)
        p = jnp.exp(s - m)
        l = jnp.sum(p, axis=-1, keepdims=True)
        o = _dot_nt(p.astype(BF16), vb[:, 0:nk]) / l
        o_ref[0, rows, :] = jnp.where(lo, o[0:Q_TILE], o[Q_TILE:2 * Q_TILE])


def _attn_prompt(q, k_all, v_all, tiles, layer):
    nb, t, _ = q.shape
    n_diff = tiles.shape[1]
    hp = W_B // LANES
    kern = functools.partial(_attn_kernel, nq=t // Q_TILE)
    seq = pl.BlockSpec((1, t, LANES), lambda b, p: (b, 0, p))
    kv = pl.BlockSpec((None, None, LANES, t), lambda b, p: (layer, b, p, 0))
    return pl.pallas_call(
        kern,
        out_shape=jax.ShapeDtypeStruct((nb, t, W_B), F32),
        grid=(nb, hp),
        in_specs=[seq, kv, kv,
                  pl.BlockSpec((None, n_diff, 2 * Q_TILE, Q_TILE), lambda b, p: (p, 0, 0, 0))],
        out_specs=seq,
        scratch_shapes=[pltpu.VMEM((LANES, t), BF16), pltpu.VMEM((LANES, t), BF16)],
        compiler_params=_cp(("parallel", "parallel")),
        name="attn_prompt",
    )(q, k_all, v_all, tiles)


def _split_dot(x, m_bf16, parts):
    acc = None
    rem = x
    for i in range(parts):
        piece = rem.astype(BF16)
        term = _dot(piece, m_bf16)
        acc = term if acc is None else acc + term
        if i + 1 < parts:
            rem = rem - piece.astype(F32)
    return acc


def _split_dot_left(m_bf16, x, parts):
    acc = None
    rem = x
    for i in range(parts):
        piece = rem.astype(BF16)
        term = _dot(m_bf16, piece)
        acc = term if acc is None else acc + term
        if i + 1 < parts:
            rem = rem - piece.astype(F32)
    return acc


def _rwkv_pre(pc, prev, prm, seg_sum):
    (mu, w0, w2p, a0, a2p, g2p, k_k, k_a, r_k, ln_g, ln_b, mseg) = prm
    xs = pc + (prev - pc) * mu
    r = xs[:, 0:W_C]
    k = xs[:, W_C:2 * W_C]
    v = xs[:, 2 * W_C:3 * W_C]
    tail = xs[:, 3 * W_C:C_SHIFT]
    w_raw = -_softplus(-(w0 + _dot(jnp.tanh(tail).astype(BF16), w2p))) - 0.5
    a = _sigmoid(a0 + _dot(tail.astype(BF16), a2p))
    g = _dot(_sigmoid(tail).astype(BF16), g2p)
    kk = k * k_k
    nrm = jnp.sqrt(seg_sum(kk * kk))
    kk = kk / jnp.maximum(nrm, 1e-12)
    k2 = k * (1.0 + (a - 1.0) * k_a)
    logw = -jnp.exp(w_raw)
    return r, k2, v, kk, a, g, logw


def _rwkv_post(y, r, k2, v, g, prm, seg_sum):
    (mu, w0, w2p, a0, a2p, g2p, k_k, k_a, r_k, ln_g, ln_b, mseg) = prm
    mean = seg_sum(y) * (1.0 / HEAD_DIM)
    d = y - mean
    var = seg_sum(d * d) * (1.0 / HEAD_DIM)
    yn = d * lax.rsqrt(var + GN_EPS) * ln_g + ln_b
    bonus = seg_sum(r * k2 * r_k) * v
    return (yn + bonus) * g


def _rwkv_kernel(pc_ref, mu_ref, w0_ref, w2_ref, a0_ref, a2_ref, g2_ref, kk_ref, ka_ref,
                 rk_ref, lng_ref, lnb_ref, mseg_ref, ltri_ref,
                 o_ref, s_ref, s_scr, prev_scr, yt_scr, *, ch, nseq):
    c = pl.program_id(1)
    rows = nseq * ch

    @pl.when(c == 0)
    def _():
        s_scr[...] = jnp.zeros_like(s_scr)
        prev_scr[...] = jnp.zeros_like(prev_scr)
        yt_scr[...] = jnp.zeros_like(yt_scr)

    prm = (mu_ref[...], w0_ref[...], w2_ref[...], a0_ref[...], a2_ref[...], g2_ref[...],
           kk_ref[...], ka_ref[...], rk_ref[...], lng_ref[...], lnb_ref[...], None)
    mseg = mseg_ref[...]
    seg_sum = lambda t: _split_dot(t, mseg, 2)
    pc = pc_ref[...].reshape(rows, C_SHIFT)
    rowi = lax.broadcasted_iota(jnp.int32, (rows, C_SHIFT), 0)
    prev = pltpu.roll(pc, 1, axis=0)
    for b in range(nseq):
        prev = jnp.where(rowi == b * ch, prev_scr[b:b + 1, :], prev)
        prev_scr[b:b + 1, :] = pc[(b + 1) * ch - 1:(b + 1) * ch, :]
    r, k2, v, kk, a, g, logw = _rwkv_pre(pc, prev, prm, seg_sum)

    cs = _split_dot_left(ltri_ref[...], logw, 3)
    g_in = jnp.exp(cs)
    g_ex = jnp.exp(cs - logw)
    g_inv = jnp.exp(-cs)
    al = (-kk * g_ex).astype(BF16)
    rt = (r * g_in).astype(BF16)
    bh = kk * a * g_inv
    kh = k2 * g_inv
    vb = v.astype(BF16)

    si = lax.broadcasted_iota(jnp.int32, (2 * ch, 2 * ch), 0) % ch
    ti = lax.broadcasted_iota(jnp.int32, (2 * ch, 2 * ch), 1)
    keep = si + jnp.where(ti < ch, 0, ch - 1) < ti
    n_sq = int(math.log2(ch))
    units = [(b, h) for b in range(nseq) for h in range(H_C)]
    idx = range(len(units))
    cut = lambda t, u: t[units[u][0] * ch:(units[u][0] + 1) * ch,
                         units[u][1] * HEAD_DIM:(units[u][1] + 1) * HEAD_DIM]
    g_end = [cut(g_in, u)[ch - 1:ch, :] for u in idx]
    al_u = [cut(al, u) for u in idx]
    rt_u = [cut(rt, u) for u in idx]
    v_u = [cut(vb, u) for u in idx]
    bh_u = [cut(bh, u) for u in idx]
    kh_u = [cut(kh, u) for u in idx]
    bc = [bh_u[u] * g_end[u] for u in idx]
    kc = [kh_u[u] * g_end[u] for u in idx]
    s0 = [s_scr[b, h] for (b, h) in units]
    s0b = [t.astype(BF16) for t in s0]
    bk = [jnp.concatenate([bh_u[u], kh_u[u]], axis=0).astype(BF16) for u in idx]
    ar = [jnp.concatenate([al_u[u], rt_u[u]], axis=0) for u in idx]
    gm = [jnp.where(keep, _dot_nt(bk[u], ar[u]), 0.0) for u in idx]
    ps = [_dot_nt(s0b[u], ar[u]) for u in idx]
    a_t = [gm[u][0:ch, 0:ch] for u in idx]
    vv = [_dot_tn(v_u[u], jnp.concatenate([gm[u][ch:2 * ch, :], kc[u]], axis=1).astype(BF16))
          for u in idx]
    x_t = [ps[u][:, 0:ch] + vv[u][:, 0:ch] for u in idx]
    for it in range(n_sq):
        a_b = [a_t[u].astype(BF16) for u in idx]
        if it + 1 < n_sq:
            both = [_dot(jnp.concatenate([x_t[u], a_t[u]], axis=0).astype(BF16), a_b[u]) for u in idx]
            x_t = [x_t[u] + both[u][0:HEAD_DIM] for u in idx]
            a_t = [both[u][HEAD_DIM:HEAD_DIM + ch] for u in idx]
        else:
            step = [_dot(x_t[u].astype(BF16), a_b[u]) for u in idx]
            x_t = [x_t[u] + step[u] for u in idx]
    x_b = [t.astype(BF16) for t in x_t]
    xx = [_dot(x_b[u], jnp.concatenate([gm[u][0:ch, :], bc[u]], axis=1).astype(BF16))
          for u in idx]
    for u, (b, h) in enumerate(units):
        s_new = s0[u] * g_end[u] + xx[u][:, 2 * ch:] + vv[u][:, 2 * ch:]
        y_sum = ps[u] + xx[u][:, 0:2 * ch] + vv[u][:, 0:2 * ch]
        yt_scr[h * HEAD_DIM:(h + 1) * HEAD_DIM, b * ch:(b + 1) * ch] = y_sum[:, ch:2 * ch]
        s_scr[b, h] = s_new
        s_ref[b, h] = s_new

    y = jnp.transpose(yt_scr[...])[0:rows, :]
    o_ref[...] = _rwkv_post(y, r, k2, v, g, prm, seg_sum).reshape(nseq, ch, W_C)


def _rwkv_params(p):
    pad = lambda w, lo: jnp.zeros((R_W + R_A + R_G, W_C), BF16).at[lo:lo + w.shape[0]].set(w.astype(BF16))
    seg = np.arange(W_C) // HEAD_DIM
    mseg = jnp.asarray((seg[:, None] == seg[None, :]).astype(np.float32))
    row = lambda t: t.reshape(1, -1)
    return (row(p["rwkv_mu"]), row(p["rwkv_w0"]), pad(p["rwkv_w2"], 0), row(p["rwkv_a0"]),
            pad(p["rwkv_a2"], R_W), pad(p["rwkv_g2"], R_W + R_A), row(p["rwkv_k_k"]),
            row(p["rwkv_k_a"]), row(p["rwkv_r_k"]), row(p["rwkv_ln_g"]), row(p["rwkv_ln_b"]), mseg)


def _rwkv_prompt(pc, prm):
    nb, t, _ = pc.shape
    ch = min(RWKV_CHUNK, t)
    nseq = max(1, min(RWKV_SEQS * LANES // ch, nb))
    assert nb % nseq == 0
    ltri = np.kron(np.eye(nseq, dtype=np.float32), np.tril(np.ones((ch, ch), np.float32)))
    ltri = jnp.asarray(ltri).astype(BF16)
    kern = functools.partial(_rwkv_kernel, ch=ch, nseq=nseq)
    const = lambda a: pl.BlockSpec(a.shape, lambda b, c: (0,) * a.ndim)
    args = list(prm[:-1]) + [prm[-1].astype(BF16), ltri]
    return pl.pallas_call(
        kern,
        out_shape=[jax.ShapeDtypeStruct((nb, t, W_C), F32),
                   jax.ShapeDtypeStruct((nb, H_C, HEAD_DIM, HEAD_DIM), F32)],
        grid=(nb // nseq, t // ch),
        in_specs=[pl.BlockSpec((nseq, ch, C_SHIFT), lambda b, c: (b, c, 0))] + [const(a) for a in args],
        out_specs=[pl.BlockSpec((nseq, ch, W_C), lambda b, c: (b, c, 0)),
                   pl.BlockSpec((nseq, H_C, HEAD_DIM, HEAD_DIM), lambda b, c: (b, 0, 0, 0))],
        scratch_shapes=[pltpu.VMEM((nseq, H_C, HEAD_DIM, HEAD_DIM), F32),
                        pltpu.VMEM((nseq, C_SHIFT), F32),
                        pltpu.VMEM((W_C, -(-nseq * ch // LANES) * LANES), F32)],
        compiler_params=_cp(("parallel", "arbitrary")),
        name="rwkv_prompt",
    )(pc, *args)


def _route(x1, rw, rb):
    logits = _dot(x1.astype(BF16), rw) + rb
    lane = lax.broadcasted_iota(jnp.int32, logits.shape, 1).astype(F32)
    vals, idxs = [], []
    cur = logits
    for _ in range(TOP_K):
        m = jnp.max(cur, axis=-1, keepdims=True)
        idx = jnp.min(jnp.where(cur == m, lane, float(N_EXPERTS)), axis=-1, keepdims=True)
        vals.append(m)
        idxs.append(idx)
        cur = jnp.where(lane == idx, -jnp.inf, cur)
    ex = [jnp.exp(v - vals[0]) for v in vals]
    tot = ex[0] + ex[1] + ex[2] + ex[3]
    gates = jnp.concatenate([e / tot for e in ex], axis=-1)
    return jnp.concatenate(idxs, axis=-1).astype(jnp.int32), gates


def _outproj_kernel(x_ref, oa_ref, ob_ref, oc_ref, w_ref, g_ref, b_ref, rw_ref, rb_ref, *refs):
    x1_ref, idx_ref, gate_ref = refs[-3:]
    mix = (_dot(oa_ref[...].astype(BF16), w_ref[0:W_A, :])
           + _dot(ob_ref[...].astype(BF16), w_ref[W_A:W_A + W_B, :])
           + _dot(oc_ref[...].astype(BF16), w_ref[W_A + W_B:, :]))
    x1 = _layernorm_rows(ALPHA * x_ref[...] + mix, g_ref[...], b_ref[...])
    x1_ref[...] = x1
    idx, gates = _route(x1, rw_ref[...], rb_ref[...])
    idx_ref[...] = idx
    gate_ref[...] = gates


def _outproj(x, oa, ob, oc, w, g, b, rw, rb, bufs, row0):
    n = oa.shape[0]
    tm = min(ROW_TILE, n)
    assert row0 % tm == 0
    rows = lambda w_: pl.BlockSpec((tm, w_), lambda i: (i, 0))
    shifted = lambda w_: pl.BlockSpec((tm, w_), lambda i: (i + row0 // tm, 0))
    const = lambda a: pl.BlockSpec(a.shape, lambda i: (0,) * a.ndim)
    consts = [w.astype(BF16), g.reshape(1, -1), b.reshape(1, -1), rw.astype(BF16), rb.reshape(1, -1)]
    n_in = 4 + len(consts)
    return pl.pallas_call(
        _outproj_kernel,
        out_shape=[jax.ShapeDtypeStruct(a.shape, a.dtype) for a in bufs],
        grid=(n // tm,),
        in_specs=[rows(D_MODEL), rows(W_A), rows(W_B), rows(W_C)] + [const(a) for a in consts]
        + [pl.BlockSpec(memory_space=pl.ANY)] * len(bufs),
        out_specs=[shifted(D_MODEL), shifted(TOP_K), shifted(TOP_K)],
        input_output_aliases={n_in + j: j for j in range(len(bufs))},
        compiler_params=_cp(("parallel",)),
        name="outproj",
    )(x, oa, ob, oc, *consts, *bufs)


def _dest_kernel(idx_ref, ltri_ref, utri_ref, dest_ref, cnt_ref, cnt_scr, run_scr, start_scr):
    ph = pl.program_id(0)
    i = pl.program_id(1)
    idx = idx_ref[...]
    lane = lax.broadcasted_iota(jnp.int32, (idx.shape[0], N_EXPERTS), 1)
    hot = [idx[:, k:k + 1] == lane for k in range(TOP_K)]
    multi = jnp.zeros(lane.shape, F32)
    for k in range(TOP_K):
        multi = multi + jnp.where(hot[k], 1.0, 0.0)
    tile_cnt = jnp.sum(multi, axis=0, keepdims=True)

    @pl.when(jnp.logical_and(ph == 0, i == 0))
    def _():
        cnt_scr[...] = jnp.zeros_like(cnt_scr)

    @pl.when(ph == 0)
    def _():
        cnt_scr[...] += tile_cnt

    @pl.when(jnp.logical_and(ph == 1, i == 0))
    def _():
        cnt = cnt_scr[...]
        padded = jnp.floor((cnt + (MOE_ROWS - 1)) * (1.0 / MOE_ROWS)) * MOE_ROWS
        start_scr[...] = _dot(jnp.broadcast_to(padded, (8, N_EXPERTS)), utri_ref[...], HI)[0:1, :]
        run_scr[...] = jnp.zeros_like(run_scr)
        cnt_ref[...] = cnt

    @pl.when(ph == 1)
    def _():
        before = _dot(ltri_ref[...], multi.astype(BF16))
        base = start_scr[...] + run_scr[...] + before
        cols = [jnp.sum(jnp.where(hot[k], base, 0.0), axis=1, keepdims=True) for k in range(TOP_K)]
        dest_ref[...] = jnp.concatenate(cols, axis=1).astype(jnp.int32)
        run_scr[...] += tile_cnt


def _route_dest(top_idx):
    n = top_idx.shape[0]
    tile = max(c for c in range(COMBINE_TOK, ROW_TILE + 1, COMBINE_TOK) if n % c == 0)
    nt = n // tile
    ltri = jnp.asarray(np.tril(np.ones((tile, tile), np.float32), -1)).astype(BF16)
    utri = jnp.asarray(np.triu(np.ones((N_EXPERTS, N_EXPERTS), np.float32), 1))
    return pl.pallas_call(
        _dest_kernel,
        out_shape=[jax.ShapeDtypeStruct((n, TOP_K), jnp.int32),
                   jax.ShapeDtypeStruct((1, N_EXPERTS), F32)],
        grid=(2, nt),
        in_specs=[pl.BlockSpec((tile, TOP_K), lambda ph, i: (i, 0)),
                  pl.BlockSpec((tile, tile), lambda ph, i: (0, 0)),
                  pl.BlockSpec((N_EXPERTS, N_EXPERTS), lambda ph, i: (0, 0))],
        out_specs=[pl.BlockSpec((tile, TOP_K), lambda ph, i: (i * ph, 0)),
                   pl.BlockSpec((1, N_EXPERTS), lambda ph, i: (0, 0))],
        scratch_shapes=[pltpu.VMEM((1, N_EXPERTS), F32)] * 3,
        compiler_params=_cp(("arbitrary", "arbitrary")),
        name="moe_dest",
    )(top_idx, ltri, utri)


def _block_experts(counts, n_blk):
    cnt = counts.reshape(N_EXPERTS).astype(jnp.int32)
    pends = jnp.cumsum((cnt + MOE_ROWS - 1) // MOE_ROWS * MOE_ROWS)
    starts = jnp.arange(n_blk, dtype=jnp.int32)[:, None] * MOE_ROWS
    blk_expert = jnp.minimum(jnp.sum((pends[None, :] <= starts).astype(jnp.int32), axis=1), N_EXPERTS - 1)
    last_blk = jnp.maximum(pends // MOE_ROWS - 1, 0).astype(jnp.int32)
    return blk_expert, (pends[-1] // MOE_ROWS).astype(jnp.int32).reshape(1), last_blk


def _dispatch_kernel(last_ref, na_ref, dest_ref, x1_ref, rows_hbm, zbuf, sem, zsem, *, tm, n_blk):
    @pl.when(pl.program_id(0) == 0)
    def _():
        zbuf[...] = jnp.zeros_like(zbuf)
        n_tail = n_blk - na_ref[0]

        def clear(blk):
            return pltpu.make_async_copy(zbuf, rows_hbm.at[pl.ds(blk * MOE_ROWS, MOE_ROWS), :], zsem)

        def start(j, carry):
            clear(jnp.where(j < N_EXPERTS, last_ref[jnp.minimum(j, N_EXPERTS - 1)],
                            na_ref[0] + j - N_EXPERTS)).start()
            return carry

        def wait(j, carry):
            clear(0).wait()
            return carry

        lax.fori_loop(0, N_EXPERTS + n_tail, start, 0)
        lax.fori_loop(0, N_EXPERTS + n_tail, wait, 0)

    for r in range(tm):
        for k in range(TOP_K):
            pltpu.make_async_copy(x1_ref.at[pl.ds(r, 1), :],
                                  rows_hbm.at[pl.ds(dest_ref[0, 0, r * TOP_K + k], 1), :], sem).start()
    for k in range(TOP_K):
        pltpu.make_async_copy(x1_ref, rows_hbm.at[pl.ds(0, tm), :], sem).wait()


def _dispatch(x1, dest, last_blk, n_active, n_blk):
    n = x1.shape[0]
    tm = COMBINE_TOK
    nt = n // tm
    kern = functools.partial(_dispatch_kernel, tm=tm, n_blk=n_blk)
    grid_spec = pltpu.PrefetchScalarGridSpec(
        num_scalar_prefetch=2,
        grid=(nt,),
        in_specs=[pl.BlockSpec((1, 1, TOP_K * tm), lambda i, lb, na: (i, 0, 0), memory_space=pltpu.SMEM),
                  pl.BlockSpec((tm, D_MODEL), lambda i, lb, na: (i, 0))],
        out_specs=pl.BlockSpec(memory_space=pl.ANY),
        scratch_shapes=[pltpu.VMEM((MOE_ROWS, D_MODEL), F32), pltpu.SemaphoreType.DMA,
                        pltpu.SemaphoreType.DMA],
    )
    return pl.pallas_call(
        kern,
        out_shape=jax.ShapeDtypeStruct((n_blk * MOE_ROWS, D_MODEL), F32),
        grid_spec=grid_spec,
        compiler_params=_cp(("arbitrary",)),
        name="moe_dispatch",
    )(last_blk, n_active, dest.reshape(nt, 1, TOP_K * tm), x1)


def _swiglu(h):
    h_glu = jnp.minimum(h[:, :D_FF], SWIGLU_LIMIT)
    h_lin = jnp.clip(h[:, D_FF:], -SWIGLU_LIMIT, SWIGLU_LIMIT)
    return h_glu * _sigmoid(SWIGLU_ALPHA * h_glu) * (h_lin + 1.0)


def _moe_kernel(be_ref, na_ref, par_ref, nxt_ref, x_ref, wup_hbm, bup_ref, wdn_hbm, bdn_ref, y_ref,
                wup_f, wdn_f, wup_b, wdn_b, sem, *, layer):
    i = pl.program_id(0)
    active = i < na_ref[0]

    def fetch(expert, s):
        return (pltpu.make_async_copy(wup_hbm.at[layer, expert], wup_f.at[s], sem.at[s]),
                pltpu.make_async_copy(wdn_hbm.at[layer, expert], wdn_f.at[s], sem.at[s]))

    @pl.when(active)
    def _():
        changed = jnp.logical_or(i == 0, be_ref[i] != be_ref[jnp.maximum(i - 1, 0)])

        @pl.when(i == 0)
        def _():
            for c in fetch(be_ref[0], 0):
                c.start()

        for s in range(2):
            @pl.when(jnp.logical_and(changed, par_ref[i] == s))
            def _(s=s):
                for c in fetch(0, s):
                    c.wait()
                wup_b[...] = wup_f[s].astype(BF16)
                wdn_b[...] = wdn_f[s].astype(BF16)

                @pl.when(nxt_ref[i] >= 0)
                def _():
                    for c in fetch(nxt_ref[i], 1 - s):
                        c.start()

        h = _dot(x_ref[...].astype(BF16), wup_b[...]) + bup_ref[...]
        y_ref[...] = _dot(_swiglu(h).astype(BF16), wdn_b[...]) + bdn_ref[...]

    @pl.when(jnp.logical_not(active))
    def _():
        y_ref[...] = jnp.zeros_like(y_ref)


def _run_schedule(blk_expert, n_active):
    n_blk = blk_expert.shape[0]
    pos = jnp.arange(n_blk, dtype=jnp.int32)
    first = jnp.concatenate([jnp.ones((1,), bool), blk_expert[1:] != blk_expert[:-1]]) & (pos < n_active[0])
    parity = ((jnp.cumsum(first.astype(jnp.int32)) - 1) % 2).astype(jnp.int32)
    later = jnp.where(first, pos, n_blk)
    nxt_pos = lax.cummin(jnp.concatenate([later[1:], jnp.full((1,), n_blk, jnp.int32)]), axis=0, reverse=True)
    nxt = jnp.where(nxt_pos < n_blk, blk_expert[jnp.minimum(nxt_pos, n_blk - 1)], -1).astype(jnp.int32)
    return parity, nxt


def _moe_rows(rows, blk_expert, n_active, w_up, b_up, w_dn, b_dn, layer):
    n_blk = blk_expert.shape[0]
    parity, nxt = _run_schedule(blk_expert, n_active)
    grid_spec = pltpu.PrefetchScalarGridSpec(
        num_scalar_prefetch=4,
        grid=(n_blk,),
        in_specs=[
            pl.BlockSpec((MOE_ROWS, D_MODEL), lambda i, be, na, *_: (jnp.minimum(i, jnp.maximum(na[0] - 1, 0)), 0)),
            pl.BlockSpec(memory_space=pl.ANY),
            pl.BlockSpec((None, None, 1, 2 * D_FF), lambda i, be, na, *_: (layer, be[i], 0, 0)),
            pl.BlockSpec(memory_space=pl.ANY),
            pl.BlockSpec((None, None, 1, D_MODEL), lambda i, be, na, *_: (layer, be[i], 0, 0)),
        ],
        out_specs=pl.BlockSpec((MOE_ROWS, D_MODEL), lambda i, be, na, *_: (i, 0)),
        scratch_shapes=[pltpu.VMEM((2, D_MODEL, 2 * D_FF), F32),
                        pltpu.VMEM((2, D_FF, D_MODEL), F32),
                        pltpu.VMEM((D_MODEL, 2 * D_FF), BF16),
                        pltpu.VMEM((D_FF, D_MODEL), BF16),
                        pltpu.SemaphoreType.DMA((2,))],
    )
    return pl.pallas_call(
        functools.partial(_moe_kernel, layer=layer),
        out_shape=jax.ShapeDtypeStruct((n_blk * MOE_ROWS, D_MODEL), F32),
        grid_spec=grid_spec,
        compiler_params=_cp(("arbitrary",)),
        name="moe_rows",
    )(blk_expert, n_active, parity, nxt, rows,
      w_up, b_up.reshape(DEPTH, N_EXPERTS, 1, 2 * D_FF), w_dn, b_dn.reshape(DEPTH, N_EXPERTS, 1, D_MODEL))


def _combine_kernel(dest_ref, destn_ref, gate_ref, x1_ref, y_hbm, g_ref, b_ref, o_ref, tail_ref, ybuf, sem,
                    *, tm, head_tiles):
    i = pl.program_id(0)
    slot = i % 2
    n_rows = TOP_K * tm

    def gather_tile(dests, s):
        for r in range(n_rows):
            pltpu.make_async_copy(y_hbm.at[pl.ds(dests[0, 0, r], 1), :],
                                  ybuf.at[s, pl.ds(r, 1), :], sem.at[s]).start()

    @pl.when(i == 0)
    def _():
        gather_tile(dest_ref, 0)

    for s in range(2):
        @pl.when(jnp.logical_and(i + 1 < pl.num_programs(0), slot == 1 - s))
        def _(s=s):
            gather_tile(destn_ref, s)

    for s in range(2):
        @pl.when(slot == s)
        def _(s=s):
            pltpu.make_async_copy(y_hbm.at[pl.ds(0, n_rows), :], ybuf.at[s], sem.at[s]).wait()
            gates = gate_ref[...]
            moe = jnp.zeros((tm, D_MODEL), F32)
            for k in range(TOP_K):
                moe = moe + gates[:, k:k + 1] * ybuf[s, k * tm:(k + 1) * tm, :]
            res = _layernorm_rows(ALPHA * x1_ref[...] + moe, g_ref[...], b_ref[...])

            @pl.when(i < head_tiles)
            def _():
                o_ref[...] = res

            @pl.when(i >= head_tiles)
            def _():
                tail_ref[...] = res


def _combine(dest, gates, x1, y_rows, g, b, n_head):
    n = x1.shape[0]
    tm = COMBINE_TOK
    nt = n // tm
    head_tiles = n_head // tm
    dest_t = dest.reshape(nt, tm, TOP_K).transpose(0, 2, 1).reshape(nt, 1, TOP_K * tm)
    kern = functools.partial(_combine_kernel, tm=tm, head_tiles=head_tiles)
    return pl.pallas_call(
        kern,
        out_shape=[jax.ShapeDtypeStruct((n_head, D_MODEL), F32),
                   jax.ShapeDtypeStruct((n - n_head, D_MODEL), F32)],
        grid=(nt,),
        in_specs=[pl.BlockSpec((1, 1, TOP_K * tm), lambda i: (i, 0, 0), memory_space=pltpu.SMEM),
                  pl.BlockSpec((1, 1, TOP_K * tm), lambda i: (jnp.minimum(i + 1, nt - 1), 0, 0),
                               memory_space=pltpu.SMEM),
                  pl.BlockSpec((tm, TOP_K), lambda i: (i, 0)),
                  pl.BlockSpec((tm, D_MODEL), lambda i: (i, 0)),
                  pl.BlockSpec(memory_space=pl.ANY),
                  pl.BlockSpec((1, D_MODEL), lambda i: (0, 0)),
                  pl.BlockSpec((1, D_MODEL), lambda i: (0, 0))],
        out_specs=[pl.BlockSpec((tm, D_MODEL), lambda i: (jnp.minimum(i, head_tiles - 1), 0)),
                   pl.BlockSpec((tm, D_MODEL), lambda i: (jnp.maximum(i - head_tiles, 0), 0))],
        scratch_shapes=[pltpu.VMEM((2, TOP_K * tm, D_MODEL), F32), pltpu.SemaphoreType.DMA((2,))],
        compiler_params=_cp(("arbitrary",)),
        name="moe_combine",
    )(dest_t, dest_t, gates, x1, y_rows, g.reshape(1, -1), b.reshape(1, -1))


def _moe(x1, top_idx, gates, p, layer, n_head):
    n = x1.shape[0]
    n_blk = -(-n * TOP_K // MOE_ROWS) + N_EXPERTS
    dest, counts = _route_dest(top_idx)
    blk_expert, n_active, last_blk = _block_experts(counts, n_blk)
    rows = _dispatch(x1, dest, last_blk, n_active, n_blk)
    y_rows = _moe_rows(rows, blk_expert, n_active, p["moe_w_up"], p["moe_b_up"],
                       p["moe_w_down"], p["moe_b_down"], layer)
    return _combine(dest, gates, x1, y_rows, p["ln2_g"][layer], p["ln2_b"][layer], n_head)


def _sample_proj_kernel(x_ref, w_ref, h0_ref, lam_ref, bblk_ref, cblk_ref, d_ref, gw_ref, gb_ref,
                        q_ref, k_ref, v_ref, pc_ref, oa_ref, h_ref):
    xb = x_ref[...].astype(BF16)
    mm = lambda lo, hi: _dot(xb, w_ref[:, lo:hi])
    u = mm(_PROJ_SPLITS[0], _PROJ_SPLITS[1])
    q_ref[...] = mm(_PROJ_SPLITS[1], _PROJ_SPLITS[2])
    k_ref[...] = mm(_PROJ_SPLITS[2], _PROJ_SPLITS[3])
    v_ref[...] = mm(_PROJ_SPLITS[3], _PROJ_SPLITS[4])
    pc_ref[...] = mm(_PROJ_SPLITS[4], _PROJ_SPLITS[5])
    bu = _dot(u.astype(BF16), bblk_ref[...])
    bu_re = bu[:, 0:N_ST]
    bu_im = bu_re + bu[:, N_ST:2 * N_ST]
    lr, li = lam_ref[0:1, :], lam_ref[1:2, :]
    hr0, hi0 = h0_ref[0], h0_ref[1]
    hr = bu_re + (lr * hr0 - li * hi0)
    hi = bu_im + (lr * hi0 + li * hr0)
    h_ref[0] = hr
    h_ref[1] = hi
    y = (_dot((hr + hi).astype(BF16), cblk_ref[0:N_ST, :])
         - _dot(hi.astype(BF16), cblk_ref[N_ST:2 * N_ST, :])) + d_ref[...] * u
    z = _gelu(y)
    oa_ref[...] = z * _sigmoid(_dot(z.astype(BF16), gw_ref[...]) + gb_ref[...])


def _sample_proj(x, w_in, h0, lam2, bblk, cblk, d, gw, gb):
    n = x.shape[0]
    widths = (W_B, W_B, W_B, C_SHIFT, W_A)
    return pl.pallas_call(
        _sample_proj_kernel,
        out_shape=[jax.ShapeDtypeStruct((n, w), F32) for w in widths]
        + [jax.ShapeDtypeStruct((2, n, N_ST), F32)],
        compiler_params=_cp(None),
        name="sample_proj",
    )(x, w_in, h0, lam2, bblk, cblk, d.reshape(1, -1), gw.astype(BF16), gb.reshape(1, -1))


def _sample_mix_kernel(q_ref, kn_ref, vn_ref, pc_ref, sh_ref, kc_ref, vc_ref, s0_ref, lb_ref, lb0_ref,
                       mu_ref, w0_ref, w2_ref, a0_ref, a2_ref, g2_ref, kk_ref, ka_ref,
                       rk_ref, lng_ref, lnb_ref, mseg_ref,
                       ob_ref, oc_ref, s_ref):
    rows = 8
    lane = lax.broadcasted_iota(jnp.int32, (rows, W_B), 1)
    sub = lax.broadcasted_iota(jnp.int32, (rows, W_B), 0)
    hmask = (lane // HEAD_DIM) == sub
    rnd = lambda t: t.astype(BF16).astype(F32)
    q = q_ref[0] * (HEAD_DIM ** -0.5)
    qrows = jnp.where(hmask, jnp.broadcast_to(q, (rows, W_B)), 0.0).astype(BF16)
    kcb = kc_ref[0].astype(BF16)
    vcb = vc_ref[0].astype(BF16)
    knb = rnd(kn_ref[0])
    vnb = rnd(vn_ref[0])
    s_all = _dot(qrows, kcb)
    s_new = jnp.sum(qrows.astype(F32) * knb, axis=-1, keepdims=True)
    outs, lses = [], []
    for br in range(len(BRANCHES)):
        s = s_all + lb_ref[br]
        s0 = s_new + lb0_ref[br]
        m = jnp.maximum(jnp.max(s, axis=-1, keepdims=True), s0)
        lse = m + jnp.log(jnp.sum(jnp.exp(s - m), axis=-1, keepdims=True) + jnp.exp(s0 - m))
        o = _dot_nt(jnp.exp(s - lse).astype(BF16), vcb) + rnd(jnp.exp(s0 - lse)) * vnb
        outs.append(rnd(o))
        lses.append(lse)
    top = jnp.maximum(jnp.maximum(lses[0], lses[1]), lses[2])
    ex = [jnp.exp(t - top) for t in lses]
    tot = ex[0] + ex[1] + ex[2]
    o = rnd(ex[0] / tot) * outs[0] + rnd(ex[1] / tot) * outs[1] + rnd(ex[2] / tot) * outs[2]
    ob_ref[0] = jnp.sum(jnp.where(hmask, o, 0.0), axis=0, keepdims=True)

    prm = (mu_ref[...], w0_ref[...], w2_ref[...], a0_ref[...], a2_ref[...], g2_ref[...],
           kk_ref[...], ka_ref[...], rk_ref[...], lng_ref[...], lnb_ref[...], mseg_ref[...])
    pc = jnp.broadcast_to(pc_ref[0], (rows, C_SHIFT))
    prev = jnp.broadcast_to(sh_ref[0], (rows, C_SHIFT))
    seg_sum = lambda t: _dot(t, mseg_ref[...], HI)
    r, k2, v, kk, a, g, logw = _rwkv_pre(pc, prev, prm, seg_sum)
    w = jnp.exp(logw)
    eye = (lax.broadcasted_iota(jnp.int32, (HEAD_DIM, HEAD_DIM), 0)
           == lax.broadcasted_iota(jnp.int32, (HEAD_DIM, HEAD_DIM), 1))
    ys = []
    for h in range(H_C):
        sl = slice(h * HEAD_DIM, (h + 1) * HEAD_DIM)
        row = lambda t: t[0:1, sl]
        col = lambda t: jnp.sum(jnp.where(eye, jnp.broadcast_to(row(t), (HEAD_DIM, HEAD_DIM)), 0.0),
                                axis=1, keepdims=True)
        s0 = s0_ref[0, h]
        sa = jnp.sum(rnd(s0) * rnd(-row(kk)), axis=1, keepdims=True)
        s1 = s0 * row(w) + sa * (row(kk) * row(a)) + col(v) * row(k2)
        s_ref[0, h] = s1
        y_col = jnp.sum(rnd(s1) * rnd(row(r)), axis=1, keepdims=True)
        ys.append(jnp.sum(jnp.where(eye, jnp.broadcast_to(y_col, (HEAD_DIM, HEAD_DIM)), 0.0),
                          axis=0, keepdims=True))
    y = jnp.broadcast_to(jnp.concatenate(ys, axis=1), (rows, W_C))
    oc_ref[0] = _rwkv_post(y, r, k2, v, g, prm, seg_sum)[0:1, :]


def _sample_mix(q, kn, vn, pc, shift0, k_cache, v_cache, s0, lb_rows, lb0, prm, layer):
    n = q.shape[0]
    l_buf = k_cache.shape[3]
    per_b = lambda w: pl.BlockSpec((1, 1, w), lambda b: (b, 0, 0))
    const = lambda a: pl.BlockSpec(a.shape, lambda b: (0,) * a.ndim)
    r3 = lambda t: t.reshape(n, 1, t.shape[-1])
    cache = pl.BlockSpec((None, 1, W_B, l_buf), lambda b: (layer, b, 0, 0))
    return pl.pallas_call(
        _sample_mix_kernel,
        out_shape=[jax.ShapeDtypeStruct((n, 1, W_B), F32),
                   jax.ShapeDtypeStruct((n, 1, W_C), F32),
                   jax.ShapeDtypeStruct((n, H_C, HEAD_DIM, HEAD_DIM), F32)],
        grid=(n,),
        in_specs=[per_b(W_B), per_b(W_B), per_b(W_B), per_b(C_SHIFT), per_b(C_SHIFT),
                  cache, cache,
                  pl.BlockSpec((1, H_C, HEAD_DIM, HEAD_DIM), lambda b: (b, 0, 0, 0)),
                  const(lb_rows), const(lb0)] + [const(a) for a in prm],
        out_specs=[per_b(W_B), per_b(W_C),
                   pl.BlockSpec((1, H_C, HEAD_DIM, HEAD_DIM), lambda b: (b, 0, 0, 0))],
        compiler_params=_cp(("parallel",)),
        name="sample_mix",
    )(r3(q), r3(kn), r3(vn), r3(pc), r3(shift0), k_cache, v_cache, s0, lb_rows, lb0, *prm)


def _prompt_mixers(x, nb, t, p, l, s5p, rwp, tiles, kv_prev):
    u, q, k_all, v_all, pc = _proj(x, p["w_in"][l].astype(BF16), nb, t, l, kv_prev)
    shp = lambda a: a.reshape(nb, t, a.shape[-1])
    o_a, h_t = _s5_prompt(shp(u), *s5p, p["ssm_d"][l], p["ssm_glu_w"][l], p["ssm_glu_b"][l])
    o_b = _attn_prompt(shp(q), k_all, v_all, tiles, l)
    o_c, s_t = _rwkv_prompt(shp(pc), rwp)
    flat = lambda a: a.reshape(nb * t, a.shape[-1])
    ssm = jnp.transpose(h_t.reshape(2, nb, G_A, N_A), (1, 2, 3, 0))
    return (flat(o_a), flat(o_b), flat(o_c)), (ssm, s_t, shp(pc)[:, t - 1]), (k_all, v_all)


def _sample_bias(rel_bias, l_buf):
    offs = _branch_offsets()
    bias = rel_bias[jnp.asarray(_t5_buckets(offs))]
    rows = []
    for br in range(len(BRANCHES)):
        m = np.arange(1, offs.shape[1])
        pos = l_buf - offs[br][m]
        ok = pos >= 0
        r = jnp.full((H_B, l_buf), NEG_INF, F32).at[:, jnp.asarray(pos[ok])].set(bias[br][jnp.asarray(m[ok])].T)
        rows.append(jnp.concatenate([r, jnp.zeros((8 - H_B, l_buf), F32)], axis=0))
    new = jnp.zeros((len(BRANCHES), 8, 1), F32).at[:, :H_B, 0].set(bias[:, 0])
    return jnp.stack(rows), new


def _sample_mixers(x, p, l, s5p, rwp, sbias, k_cache, v_cache, st_ssm, st_rwkv, st_shift):
    n = x.shape[0]
    lam2, bblk, cblk = s5p
    h0 = jnp.transpose(st_ssm.reshape(n, N_ST, 2), (2, 0, 1))
    q, kn, vn, pc, o_a, h1 = _sample_proj(x, p["w_in"][l].astype(BF16), h0, lam2, bblk, cblk, p["ssm_d"][l],
                                          p["ssm_glu_w"][l], p["ssm_glu_b"][l])
    lb_rows, lb0 = sbias
    o_b, o_c, s1 = _sample_mix(q, kn, vn, pc, st_shift, k_cache, v_cache, st_rwkv, lb_rows, lb0, rwp, l)
    ssm = jnp.transpose(h1.reshape(2, n, G_A, N_A), (1, 2, 3, 0))
    return (o_a, o_b.reshape(n, W_B), o_c.reshape(n, W_C)), (kn, vn, ssm, s1, pc)


def _shift_kernel(c_ref, new_ref, o_ref):
    x = c_ref[...]
    length = x.shape[1]
    lane = lax.broadcasted_iota(jnp.int32, x.shape, 1)
    o_ref[...] = jnp.where(lane == length - 1, new_ref[...], pltpu.roll(x, length - 1, axis=1))


def _shift_cache(cache_t, new):
    rows, length = cache_t.shape
    blk = min(ROW_TILE, rows)
    return pl.pallas_call(
        _shift_kernel,
        out_shape=jax.ShapeDtypeStruct((rows, length), F32),
        grid=(rows // blk,),
        in_specs=[pl.BlockSpec((blk, length), lambda i: (i, 0)),
                  pl.BlockSpec((blk, 1), lambda i: (i, 0))],
        out_specs=pl.BlockSpec((blk, length), lambda i: (i, 0)),
        compiler_params=_cp(("parallel",)),
        name="shift_cache",
    )(cache_t, new)


def kernel(x_prompt, x_sample, cache_attn_k, cache_attn_v, state_ssm, state_rwkv, state_shift, w_in, w_out, ln1_g, ln1_b, ln2_g, ln2_b, ssm_a_re, ssm_a_im, ssm_log_dt, ssm_b_re, ssm_b_im, ssm_c_re, ssm_c_im, ssm_d, ssm_glu_w, ssm_glu_b, rel_bias, rwkv_mu, rwkv_w0, rwkv_w2, rwkv_a0, rwkv_a2, rwkv_g2, rwkv_k_k, rwkv_k_a, rwkv_r_k, rwkv_ln_g, rwkv_ln_b, moe_router_w, moe_router_b, moe_w_up, moe_b_up, moe_w_down, moe_b_down):
    p = dict(w_in=w_in, w_out=w_out, ln1_g=ln1_g, ln1_b=ln1_b, ln2_g=ln2_g, ln2_b=ln2_b,
             ssm_d=ssm_d, ssm_glu_w=ssm_glu_w, ssm_glu_b=ssm_glu_b,
             moe_router_w=moe_router_w, moe_router_b=moe_router_b, moe_w_up=moe_w_up,
             moe_b_up=moe_b_up, moe_w_down=moe_w_down, moe_b_down=moe_b_down)
    nb, t, _ = x_prompt.shape
    ns = x_sample.shape[0]
    l_buf = cache_attn_k.shape[2]
    table = _log_bias_table(rel_bias)
    tiles = _bias_tiles(table, min(t, WIN_MAX) // Q_TILE + 1)
    sbias = _sample_bias(rel_bias, l_buf)
    n_p = nb * t
    n_all = n_p + -(-ns // COMBINE_TOK) * COMBINE_TOK
    xp = x_prompt.reshape(n_p, D_MODEL)
    xs = x_sample.reshape(ns, D_MODEL)
    st_p, st_s = [], []
    time_minor = lambda c: jnp.transpose(c, (0, 1, 3, 4, 2)).reshape(DEPTH, c.shape[1], W_B, c.shape[2])
    time_major = lambda c, n: jnp.transpose(c.reshape(DEPTH, n, H_B, HEAD_DIM, c.shape[-1]), (0, 1, 4, 2, 3))
    kc_t, vc_t = time_minor(cache_attn_k), time_minor(cache_attn_v)
    kv = (jnp.zeros((DEPTH, nb, W_B, t), F32), jnp.zeros((DEPTH, nb, W_B, t), F32))
    for l in range(DEPTH):
        s5p, s5s = _s5_params(ssm_a_re[l], ssm_a_im[l], ssm_log_dt[l], ssm_b_re[l], ssm_b_im[l],
                              ssm_c_re[l], ssm_c_im[l])
        rwp = _rwkv_params(dict(rwkv_mu=rwkv_mu[l], rwkv_w0=rwkv_w0[l], rwkv_w2=rwkv_w2[l],
                                rwkv_a0=rwkv_a0[l], rwkv_a2=rwkv_a2[l], rwkv_g2=rwkv_g2[l],
                                rwkv_k_k=rwkv_k_k[l], rwkv_k_a=rwkv_k_a[l], rwkv_r_k=rwkv_r_k[l],
                                rwkv_ln_g=rwkv_ln_g[l], rwkv_ln_b=rwkv_ln_b[l]))
        mix_p, sp, kv = _prompt_mixers(xp, nb, t, p, l, s5p, rwp, tiles, kv)
        mix_s, ss = _sample_mixers(xs, p, l, s5s, rwp, sbias, kc_t, vc_t,
                                   state_ssm[l], state_rwkv[l], state_shift[l])
        st_p.append(sp)
        st_s.append(ss)
        bufs = (jnp.zeros((n_all, D_MODEL), F32),
                jnp.broadcast_to(jnp.arange(TOP_K, dtype=jnp.int32), (n_all, TOP_K)),
                jnp.zeros((n_all, TOP_K), F32))
        head = (p["w_out"][l], p["ln1_g"][l], p["ln1_b"][l], p["moe_router_w"][l], p["moe_router_b"][l])
        bufs = _outproj(xp, *mix_p, *head, bufs, 0)
        bufs = _outproj(xs, *mix_s, *head, bufs, n_p)
        xp, tail = _moe(*bufs, p, l, n_p)
        xs = tail[:ns]
    p_ssm, p_rwkv, p_shift = (jnp.stack(z) for z in zip(*st_p))
    kn, vn, s_ssm, s_rwkv, s_shift = (jnp.stack(z) for z in zip(*st_s))
    keep_p = min(WIN_MAX, t)
    p_k, p_v = (time_major(a[..., t - keep_p:], nb) for a in kv)
    keep = min(WIN_MAX, l_buf + 1)
    drop = l_buf + 1 - keep
    if drop == 1:
        advance = lambda c_t, new: _shift_cache(c_t.reshape(-1, l_buf), new.reshape(-1, 1)).reshape(c_t.shape)
    else:
        advance = lambda c_t, new: jnp.concatenate([c_t[..., drop:], new[..., None]], axis=-1)
    s_k = time_major(advance(kc_t, kn), ns)
    s_v = time_major(advance(vc_t, vn), ns)
    return (xp.reshape(nb, t, D_MODEL), xs.reshape(ns, 1, D_MODEL),
            p_k, p_v, p_ssm, p_rwkv, p_shift, s_k, s_v, s_ssm, s_rwkv, s_shift)
```

```python
import functools
import math

import numpy as np
import jax
import jax.numpy as jnp
from jax import lax
from jax.experimental import pallas as pl
from jax.experimental.pallas import tpu as pltpu

F32 = jnp.float32
BF16 = jnp.bfloat16
HI = lax.Precision.HIGHEST

D_MODEL = 1024
DEPTH = 2
HEAD_DIM = 64
W_A = 256
C_GRP = 16
G_A = 16
N_A = 64
N_ST = G_A * N_A
W_B = 384
H_B = 6
BRANCHES = ((128, 1), (512, 4), (2048, 16))
WIN_MAX = 2048
N_BUCKETS = 32
MAX_DIST = WIN_MAX
NEG_INF = -1e30
W_C = 384
H_C = 6
R_W = 32
R_A = 32
R_G = 64
C_SHIFT = 3 * W_C + R_W + R_A + R_G
N_IN = W_A + 3 * W_B + C_SHIFT
GN_EPS = 64e-5
N_EXPERTS = 32
TOP_K = 4
D_FF = D_MODEL
SWIGLU_ALPHA = 1.702
SWIGLU_LIMIT = 7.0
ALPHA = (2 * DEPTH) ** 0.25
LN_EPS = 1e-5

LANES = 128
Q_TILE = 128
RWKV_CHUNK = 64
RWKV_SEQS = 2
S5_CHUNK = 64
ROW_TILE = 512
MOE_ROWS = 256
COMBINE_TOK = 128
VMEM_LIMIT = 56 * 1024 * 1024


def _cp(sem, vmem=VMEM_LIMIT):
    return pltpu.CompilerParams(dimension_semantics=sem, vmem_limit_bytes=vmem)


def _dot(a, b, precision=None):
    return jnp.dot(a, b, preferred_element_type=F32, precision=precision)


def _dot_nt(a, b, precision=None):
    return lax.dot_general(a, b, (((1,), (1,)), ((), ())),
                           preferred_element_type=F32, precision=precision)


def _dot_tn(a, b, precision=None):
    return lax.dot_general(a, b, (((0,), (0,)), ((), ())),
                           preferred_element_type=F32, precision=precision)


def _layernorm_rows(x, g, b):
    mu = jnp.mean(x, axis=-1, keepdims=True)
    d = x - mu
    var = jnp.mean(d * d, axis=-1, keepdims=True)
    return d * lax.rsqrt(var + LN_EPS) * g + b


def _sigmoid(x):
    return 1.0 / (1.0 + jnp.exp(-x))


def _softplus(x):
    return jnp.maximum(x, 0.0) + jnp.log(1.0 + jnp.exp(-jnp.abs(x)))


def _gelu(x):
    c = math.sqrt(2.0 / math.pi)
    return 0.5 * x * (1.0 + jnp.tanh(c * (x + 0.044715 * (x * x * x))))


_PROJ_SPLITS = (0, W_A, W_A + W_B, W_A + 2 * W_B, W_A + 3 * W_B, N_IN)


def _proj_kernel(x_ref, w_ref, *refs):
    u_ref, q_ref, k_ref, v_ref, pc_ref = refs[-5:]
    xb = x_ref[...].astype(BF16)
    outs = ((u_ref, False), (q_ref, False), (k_ref, True), (v_ref, True), (pc_ref, False))
    for (o_ref, time_minor), lo, hi in zip(outs, _PROJ_SPLITS[:-1], _PROJ_SPLITS[1:]):
        y = _dot(xb, w_ref[:, lo:hi])
        o_ref[...] = jnp.transpose(y) if time_minor else y


def _proj(x, w_bf16, nb, t, layer, kv_prev):
    n = nb * t
    per_seq = t // ROW_TILE
    widths = [hi - lo for lo, hi in zip(_PROJ_SPLITS[:-1], _PROJ_SPLITS[1:])]
    rows = lambda w: pl.BlockSpec((ROW_TILE, w), lambda i: (i, 0))
    stacked = pl.BlockSpec((None, None, W_B, ROW_TILE), lambda i: (layer, i // per_seq, 0, i % per_seq))
    flat = lambda w: jax.ShapeDtypeStruct((n, w), F32)
    kv = jax.ShapeDtypeStruct((DEPTH, nb, W_B, t), F32)
    prev = list(kv_prev)
    return pl.pallas_call(
        _proj_kernel,
        out_shape=[flat(widths[0]), flat(widths[1]), kv, kv, flat(widths[4])],
        grid=(n // ROW_TILE,),
        in_specs=[pl.BlockSpec((ROW_TILE, D_MODEL), lambda i: (i, 0)),
                  pl.BlockSpec((D_MODEL, N_IN), lambda i: (0, 0))]
        + [pl.BlockSpec(memory_space=pl.ANY)] * len(prev),
        out_specs=[rows(widths[0]), rows(widths[1]), stacked, stacked, rows(widths[4])],
        input_output_aliases={2: 2, 3: 3},
        compiler_params=_cp(("parallel",)),
        name="proj",
    )(x, w_bf16, *prev)


def _s5_kernel(u_ref, lam_ref, bblk_ref, cblk_ref, d_ref, gw_ref, gb_ref,
               o_ref, ht_ref, utm, hs, ytm, h_scr, *, nb, lt):
    c = pl.program_id(0)

    @pl.when(c == 0)
    def _():
        h_scr[...] = jnp.zeros_like(h_scr)

    n_half = W_A // LANES
    for b in range(nb):
        for j in range(n_half):
            utm[j, pl.ds(b, lt, stride=nb), :] = u_ref[b, :, j * LANES:(j + 1) * LANES]
    u_all = jnp.concatenate([utm[j] for j in range(n_half)], axis=1)
    hs[...] = _dot(u_all.astype(BF16), bblk_ref[...])
    lr = jnp.broadcast_to(lam_ref[0:1, :], (nb, N_ST))
    li = jnp.broadcast_to(lam_ref[1:2, :], (nb, N_ST))

    def body(t, carry):
        hr, hi = carry
        row = pl.multiple_of(t * nb, nb)
        br = hs[pl.ds(row, nb), 0:N_ST]
        bi = hs[pl.ds(row, nb), N_ST:2 * N_ST]
        nr = lr * hr - li * hi + br
        ni = lr * hi + li * hr + bi
        hs[pl.ds(row, nb), 0:N_ST] = nr
        hs[pl.ds(row, nb), N_ST:2 * N_ST] = ni
        return nr, ni

    hr, hi = lax.fori_loop(0, lt, body, (h_scr[0], h_scr[1]), unroll=2)
    h_scr[0] = hr
    h_scr[1] = hi
    ht_ref[0] = hr
    ht_ref[1] = hi
    y = _dot(hs[...].astype(BF16), cblk_ref[...]) + d_ref[...] * u_all
    z = _gelu(y)
    gl = _dot(z.astype(BF16), gw_ref[...]) + gb_ref[...]
    res = z * _sigmoid(gl)
    for j in range(n_half):
        ytm[j] = res[:, j * LANES:(j + 1) * LANES]
    for b in range(nb):
        for j in range(n_half):
            o_ref[b, :, j * LANES:(j + 1) * LANES] = ytm[j, pl.ds(b, lt, stride=nb), :]


def _s5_params(a_re, a_im, log_dt, b_re, b_im, c_re, c_im):
    lam = lax.complex(a_re, a_im)
    dt = jnp.exp(log_dt)[:, None]
    lam_bar = jnp.exp(lam * dt)
    b_bar = ((lam_bar - 1.0) / lam)[..., None] * lax.complex(b_re, b_im)
    eye = jnp.eye(G_A, dtype=F32)

    def blk_in(m):
        return jnp.einsum("gnc,gh->gchn", m, eye).reshape(W_A, N_ST)

    def blk_out(m):
        return jnp.einsum("gcn,gh->gnhc", m, eye).reshape(N_ST, W_A)

    bblk = jnp.concatenate([blk_in(b_bar.real), blk_in(b_bar.imag)], axis=1)
    cblk = jnp.concatenate([blk_out(c_re), blk_out(-c_im)], axis=0)
    lam2 = jnp.stack([lam_bar.real.reshape(N_ST), lam_bar.imag.reshape(N_ST)])
    bblk3 = jnp.concatenate([blk_in(b_bar.real), blk_in(b_bar.imag - b_bar.real)], axis=1)
    cblk3 = jnp.concatenate([blk_out(c_re), blk_out(c_re + c_im)], axis=0)
    return (lam2, bblk, cblk), (lam2, bblk3.astype(BF16), cblk3.astype(BF16))


def _s5_prompt(u, lam2, bblk, cblk, d, gw, gb):
    nb, t, _ = u.shape
    lt = min(S5_CHUNK, t)
    kern = functools.partial(_s5_kernel, nb=nb, lt=lt)
    const = lambda shape: pl.BlockSpec(shape, lambda c: (0,) * len(shape))
    return pl.pallas_call(
        kern,
        out_shape=[jax.ShapeDtypeStruct((nb, t, W_A), F32),
                   jax.ShapeDtypeStruct((2, nb, N_ST), F32)],
        grid=(t // lt,),
        in_specs=[pl.BlockSpec((nb, lt, W_A), lambda c: (0, c, 0)),
                  const((2, N_ST)), const((W_A, 2 * N_ST)), const((2 * N_ST, W_A)),
                  const((1, W_A)), const((W_A, W_A)), const((1, W_A))],
        out_specs=[pl.BlockSpec((nb, lt, W_A), lambda c: (0, c, 0)),
                   const((2, nb, N_ST))],
        scratch_shapes=[pltpu.VMEM((W_A // LANES, nb * lt, LANES), F32),
                        pltpu.VMEM((nb * lt, 2 * N_ST), F32),
                        pltpu.VMEM((W_A // LANES, nb * lt, LANES), F32),
                        pltpu.VMEM((2, nb, N_ST), F32)],
        compiler_params=_cp(("arbitrary",)),
        name="s5_prompt",
    )(u, lam2, bblk.astype(BF16), cblk.astype(BF16), d.reshape(1, W_A),
      gw.astype(BF16), gb.reshape(1, W_A))


def _branch_offsets():
    return np.stack([np.arange(w // d + 1) * d for (w, d) in BRANCHES]).astype(np.int32)


def _t5_buckets(dist):
    max_exact = N_BUCKETS // 2
    d = np.maximum(dist, 1).astype(np.float32)
    large = max_exact + (np.log(d / max_exact) / np.log(MAX_DIST / max_exact)
                         * (N_BUCKETS - max_exact)).astype(np.int32)
    return np.where(dist < max_exact, dist, np.minimum(large, N_BUCKETS - 1)).astype(np.int32)


def _log_bias_table(rel_bias):
    offs = _branch_offsets()
    bias = rel_bias[jnp.asarray(_t5_buckets(offs))]
    table = jnp.full((WIN_MAX + 1, H_B), -jnp.inf, F32)
    for br in range(len(BRANCHES)):
        idx = jnp.asarray(offs[br])
        table = table.at[idx].set(jnp.logaddexp(table[idx], bias[br]))
    return jnp.maximum(table, NEG_INF)


def _bias_tiles(table, n_diff):
    per = 2 * Q_TILE
    m = np.arange(per)
    dist = np.arange(n_diff)[:, None] * Q_TILE + np.where(m < Q_TILE, -m, per - m)[None, :]
    ok = (dist >= 0) & (dist <= WIN_MAX) & (m != Q_TILE)[None, :]
    v = jnp.where(jnp.asarray(ok)[..., None], table[jnp.asarray(np.clip(dist, 0, WIN_MAX))], NEG_INF)
    v = jnp.transpose(v, (2, 0, 1))
    flat = jnp.broadcast_to(v[:, :, None, :], (H_B, n_diff, Q_TILE, per)).reshape(H_B, n_diff, Q_TILE * per)
    tiles = flat[:, :, :Q_TILE * (per - 1)].reshape(H_B, n_diff, Q_TILE, per - 1)[..., :Q_TILE]
    tiles = tiles.reshape(H_B // 2, 2, n_diff, Q_TILE, Q_TILE)
    return jnp.transpose(tiles, (0, 2, 1, 3, 4)).reshape(H_B // 2, n_diff, 2 * Q_TILE, Q_TILE)


def _attn_kernel(q_ref, k_ref, v_ref, bias_ref, o_ref, kb, vb, *, nq):
    kb[...] = k_ref[...].astype(BF16)
    vb[...] = v_ref[...].astype(BF16)
    lo = lax.broadcasted_iota(jnp.int32, (Q_TILE, LANES), 1) < HEAD_DIM
    for qi in range(nq):
        rows = slice(qi * Q_TILE, (qi + 1) * Q_TILE)
        q = q_ref[0, rows, :] * (HEAD_DIM ** -0.5)
        q2 = jnp.concatenate([jnp.where(lo, q, 0.0), jnp.where(lo, 0.0, q)], axis=0).astype(BF16)
        nk = (qi + 1) * Q_TILE
        bias = jnp.concatenate([bias_ref[qi - kj] for kj in range(qi + 1)], axis=1)
        s = _dot(q2, kb[:, 0:nk]) + bias
        m = jnp.max(s, axis=-1, keepdims=True)
        p = jnp.exp(s - m)
        l = jnp.sum(p, axis=-1, keepdims=True)
        o = _dot_nt(p.astype(BF16), vb[:, 0:nk]) / l
        o_ref[0, rows, :] = jnp.where(lo, o[0:Q_TILE], o[Q_TILE:2 * Q_TILE])


def _attn_prompt(q, k_all, v_all, tiles, layer):
    nb, t, _ = q.shape
    n_diff = tiles.shape[1]
    hp = W_B // LANES
    kern = functools.partial(_attn_kernel, nq=t // Q_TILE)
    seq = pl.BlockSpec((1, t, LANES), lambda b, p: (b, 0, p))
    kv = pl.BlockSpec((None, None, LANES, t), lambda b, p: (layer, b, p, 0))
    return pl.pallas_call(
        kern,
        out_shape=jax.ShapeDtypeStruct((nb, t, W_B), F32),
        grid=(nb, hp),
        in_specs=[seq, kv, kv,
                  pl.BlockSpec((None, n_diff, 2 * Q_TILE, Q_TILE), lambda b, p: (p, 0, 0, 0))],
        out_specs=seq,
        scratch_shapes=[pltpu.VMEM((LANES, t), BF16), pltpu.VMEM((LANES, t), BF16)],
        compiler_params=_cp(("parallel", "parallel")),
        name="attn_prompt",
    )(q, k_all, v_all, tiles)


def _split_dot(x, m_bf16, parts):
    acc = None
    rem = x
    for i in range(parts):
        piece = rem.astype(BF16)
        term = _dot(piece, m_bf16)
        acc = term if acc is None else acc + term
        if i + 1 < parts:
            rem = rem - piece.astype(F32)
    return acc


def _split_dot_left(m_bf16, x, parts):
    acc = None
    rem = x
    for i in range(parts):
        piece = rem.astype(BF16)
        term = _dot(m_bf16, piece)
        acc = term if acc is None else acc + term
        if i + 1 < parts:
            rem = rem - piece.astype(F32)
    return acc


def _rwkv_pre(pc, prev, prm, seg_sum):
    (mu, w0, w2p, a0, a2p, g2p, k_k, k_a, r_k, ln_g, ln_b, mseg) = prm
    xs = pc + (prev - pc) * mu
    r = xs[:, 0:W_C]
    k = xs[:, W_C:2 * W_C]
    v = xs[:, 2 * W_C:3 * W_C]
    tail = xs[:, 3 * W_C:C_SHIFT]
    w_raw = -_softplus(-(w0 + _dot(jnp.tanh(tail).astype(BF16), w2p))) - 0.5
    a = _sigmoid(a0 + _dot(tail.astype(BF16), a2p))
    g = _dot(_sigmoid(tail).astype(BF16), g2p)
    kk = k * k_k
    nrm = jnp.sqrt(seg_sum(kk * kk))
    kk = kk / jnp.maximum(nrm, 1e-12)
    k2 = k * (1.0 + (a - 1.0) * k_a)
    logw = -jnp.exp(w_raw)
    return r, k2, v, kk, a, g, logw


def _rwkv_post(y, r, k2, v, g, prm, seg_sum):
    (mu, w0, w2p, a0, a2p, g2p, k_k, k_a, r_k, ln_g, ln_b, mseg) = prm
    mean = seg_sum(y) * (1.0 / HEAD_DIM)
    d = y - mean
    var = seg_sum(d * d) * (1.0 / HEAD_DIM)
    yn = d * lax.rsqrt(var + GN_EPS) * ln_g + ln_b
    bonus = seg_sum(r * k2 * r_k) * v
    return (yn + bonus) * g


def _rwkv_kernel(pc_ref, mu_ref, w0_ref, w2_ref, a0_ref, a2_ref, g2_ref, kk_ref, ka_ref,
                 rk_ref, lng_ref, lnb_ref, mseg_ref, ltri_ref,
                 o_ref, s_ref, s_scr, prev_scr, yt_scr, *, ch, nseq):
    c = pl.program_id(1)
    rows = nseq * ch

    @pl.when(c == 0)
    def _():
        s_scr[...] = jnp.zeros_like(s_scr)
        prev_scr[...] = jnp.zeros_like(prev_scr)
        yt_scr[...] = jnp.zeros_like(yt_scr)

    prm = (mu_ref[...], w0_ref[...], w2_ref[...], a0_ref[...], a2_ref[...], g2_ref[...],
           kk_ref[...], ka_ref[...], rk_ref[...], lng_ref[...], lnb_ref[...], None)
    mseg = mseg_ref[...]
    seg_sum = lambda t: _split_dot(t, mseg, 2)
    pc = pc_ref[...].reshape(rows, C_SHIFT)
    rowi = lax.broadcasted_iota(jnp.int32, (rows, C_SHIFT), 0)
    prev = pltpu.roll(pc, 1, axis=0)
    for b in range(nseq):
        prev = jnp.where(rowi == b * ch, prev_scr[b:b + 1, :], prev)
        prev_scr[b:b + 1, :] = pc[(b + 1) * ch - 1:(b + 1) * ch, :]
    r, k2, v, kk, a, g, logw = _rwkv_pre(pc, prev, prm, seg_sum)

    cs = _split_dot_left(ltri_ref[...], logw, 3)
    g_in = jnp.exp(cs)
    g_ex = jnp.exp(cs - logw)
    g_inv = jnp.exp(-cs)
    al = (-kk * g_ex).astype(BF16)
    rt = (r * g_in).astype(BF16)
    bh = kk * a * g_inv
    kh = k2 * g_inv
    vb = v.astype(BF16)

    si = lax.broadcasted_iota(jnp.int32, (2 * ch, 2 * ch), 0) % ch
    ti = lax.broadcasted_iota(jnp.int32, (2 * ch, 2 * ch), 1)
    keep = si + jnp.where(ti < ch, 0, ch - 1) < ti
    n_sq = int(math.log2(ch))
    units = [(b, h) for b in range(nseq) for h in range(H_C)]
    idx = range(len(units))
    cut = lambda t, u: t[units[u][0] * ch:(units[u][0] + 1) * ch,
                         units[u][1] * HEAD_DIM:(units[u][1] + 1) * HEAD_DIM]
    g_end = [cut(g_in, u)[ch - 1:ch, :] for u in idx]
    al_u = [cut(al, u) for u in idx]
    rt_u = [cut(rt, u) for u in idx]
    v_u = [cut(vb, u) for u in idx]
    bh_u = [cut(bh, u) for u in idx]
    kh_u = [cut(kh, u) for u in idx]
    bc = [bh_u[u] * g_end[u] for u in idx]
    kc = [kh_u[u] * g_end[u] for u in idx]
    s0 = [s_scr[b, h] for (b, h) in units]
    s0b = [t.astype(BF16) for t in s0]
    bk = [jnp.concatenate([bh_u[u], kh_u[u]], axis=0).astype(BF16) for u in idx]
    ar = [jnp.concatenate([al_u[u], rt_u[u]], axis=0) for u in idx]
    gm = [jnp.where(keep, _dot_nt(bk[u], ar[u]), 0.0) for u in idx]
    ps = [_dot_nt(s0b[u], ar[u]) for u in idx]
    a_t = [gm[u][0:ch, 0:ch] for u in idx]
    vv = [_dot_tn(v_u[u], jnp.concatenate([gm[u][ch:2 * ch, :], kc[u]], axis=1).astype(BF16))
          for u in idx]
    x_t = [ps[u][:, 0:ch] + vv[u][:, 0:ch] for u in idx]
    for it in range(n_sq):
        a_b = [a_t[u].astype(BF16) for u in idx]
        if it + 1 < n_sq:
            both = [_dot(jnp.concatenate([x_t[u], a_t[u]], axis=0).astype(BF16), a_b[u]) for u in idx]
            x_t = [x_t[u] + both[u][0:HEAD_DIM] for u in idx]
            a_t = [both[u][HEAD_DIM:HEAD_DIM + ch] for u in idx]
        else:
            step = [_dot(x_t[u].astype(BF16), a_b[u]) for u in idx]
            x_t = [x_t[u] + step[u] for u in idx]
    x_b = [t.astype(BF16) for t in x_t]
    xx = [_dot(x_b[u], jnp.concatenate([gm[u][0:ch, :], bc[u]], axis=1).astype(BF16))
          for u in idx]
    for u, (b, h) in enumerate(units):
        s_new = s0[u] * g_end[u] + xx[u][:, 2 * ch:] + vv[u][:, 2 * ch:]
        y_sum = ps[u] + xx[u][:, 0:2 * ch] + vv[u][:, 0:2 * ch]
        yt_scr[h * HEAD_DIM:(h + 1) * HEAD_DIM, b * ch:(b + 1) * ch] = y_sum[:, ch:2 * ch]
        s_scr[b, h] = s_new
        s_ref[b, h] = s_new

    y = jnp.transpose(yt_scr[...])[0:rows, :]
    o_ref[...] = _rwkv_post(y, r, k2, v, g, prm, seg_sum).reshape(nseq, ch, W_C)


def _rwkv_params(p):
    pad = lambda w, lo: jnp.zeros((R_W + R_A + R_G, W_C), BF16).at[lo:lo + w.shape[0]].set(w.astype(BF16))
    seg = np.arange(W_C) // HEAD_DIM
    mseg = jnp.asarray((seg[:, None] == seg[None, :]).astype(np.float32))
    row = lambda t: t.reshape(1, -1)
    return (row(p["rwkv_mu"]), row(p["rwkv_w0"]), pad(p["rwkv_w2"], 0), row(p["rwkv_a0"]),
            pad(p["rwkv_a2"], R_W), pad(p["rwkv_g2"], R_W + R_A), row(p["rwkv_k_k"]),
            row(p["rwkv_k_a"]), row(p["rwkv_r_k"]), row(p["rwkv_ln_g"]), row(p["rwkv_ln_b"]), mseg)


def _rwkv_prompt(pc, prm):
    nb, t, _ = pc.shape
    ch = min(RWKV_CHUNK, t)
    nseq = max(1, min(RWKV_SEQS * LANES // ch, nb))
    assert nb % nseq == 0
    ltri = np.kron(np.eye(nseq, dtype=np.float32), np.tril(np.ones((ch, ch), np.float32)))
    ltri = jnp.asarray(ltri).astype(BF16)
    kern = functools.partial(_rwkv_kernel, ch=ch, nseq=nseq)
    const = lambda a: pl.BlockSpec(a.shape, lambda b, c: (0,) * a.ndim)
    args = list(prm[:-1]) + [prm[-1].astype(BF16), ltri]
    return pl.pallas_call(
        kern,
        out_shape=[jax.ShapeDtypeStruct((nb, t, W_C), F32),
                   jax.ShapeDtypeStruct((nb, H_C, HEAD_DIM, HEAD_DIM), F32)],
        grid=(nb // nseq, t // ch),
        in_specs=[pl.BlockSpec((nseq, ch, C_SHIFT), lambda b, c: (b, c, 0))] + [const(a) for a in args],
        out_specs=[pl.BlockSpec((nseq, ch, W_C), lambda b, c: (b, c, 0)),
                   pl.BlockSpec((nseq, H_C, HEAD_DIM, HEAD_DIM), lambda b, c: (b, 0, 0, 0))],
        scratch_shapes=[pltpu.VMEM((nseq, H_C, HEAD_DIM, HEAD_DIM), F32),
                        pltpu.VMEM((nseq, C_SHIFT), F32),
                        pltpu.VMEM((W_C, -(-nseq * ch // LANES) * LANES), F32)],
        compiler_params=_cp(("parallel", "arbitrary")),
        name="rwkv_prompt",
    )(pc, *args)


def _route(x1, rw, rb):
    logits = _dot(x1.astype(BF16), rw) + rb
    lane = lax.broadcasted_iota(jnp.int32, logits.shape, 1).astype(F32)
    vals, idxs = [], []
    cur = logits
    for _ in range(TOP_K):
        m = jnp.max(cur, axis=-1, keepdims=True)
        idx = jnp.min(jnp.where(cur == m, lane, float(N_EXPERTS)), axis=-1, keepdims=True)
        vals.append(m)
        idxs.append(idx)
        cur = jnp.where(lane == idx, -jnp.inf, cur)
    ex = [jnp.exp(v - vals[0]) for v in vals]
    tot = ex[0] + ex[1] + ex[2] + ex[3]
    gates = jnp.concatenate([e / tot for e in ex], axis=-1)
    return jnp.concatenate(idxs, axis=-1).astype(jnp.int32), gates


def _outproj_kernel(x_ref, oa_ref, ob_ref, oc_ref, w_ref, g_ref, b_ref, rw_ref, rb_ref, *refs):
    x1_ref, idx_ref, gate_ref = refs[-3:]
    mix = (_dot(oa_ref[...].astype(BF16), w_ref[0:W_A, :])
           + _dot(ob_ref[...].astype(BF16), w_ref[W_A:W_A + W_B, :])
           + _dot(oc_ref[...].astype(BF16), w_ref[W_A + W_B:, :]))
    x1 = _layernorm_rows(ALPHA * x_ref[...] + mix, g_ref[...], b_ref[...])
    x1_ref[...] = x1
    idx, gates = _route(x1, rw_ref[...], rb_ref[...])
    idx_ref[...] = idx
    gate_ref[...] = gates


def _outproj(x, oa, ob, oc, w, g, b, rw, rb, bufs, row0):
    n = oa.shape[0]
    tm = min(ROW_TILE, n)
    assert row0 % tm == 0
    rows = lambda w_: pl.BlockSpec((tm, w_), lambda i: (i, 0))
    shifted = lambda w_: pl.BlockSpec((tm, w_), lambda i: (i + row0 // tm, 0))
    const = lambda a: pl.BlockSpec(a.shape, lambda i: (0,) * a.ndim)
    consts = [w.astype(BF16), g.reshape(1, -1), b.reshape(1, -1), rw.astype(BF16), rb.reshape(1, -1)]
    n_in = 4 + len(consts)
    return pl.pallas_call(
        _outproj_kernel,
        out_shape=[jax.ShapeDtypeStruct(a.shape, a.dtype) for a in bufs],
        grid=(n // tm,),
        in_specs=[rows(D_MODEL), rows(W_A), rows(W_B), rows(W_C)] + [const(a) for a in consts]
        + [pl.BlockSpec(memory_space=pl.ANY)] * len(bufs),
        out_specs=[shifted(D_MODEL), shifted(TOP_K), shifted(TOP_K)],
        input_output_aliases={n_in + j: j for j in range(len(bufs))},
        compiler_params=_cp(("parallel",)),
        name="outproj",
    )(x, oa, ob, oc, *consts, *bufs)


def _dest_kernel(idx_ref, ltri_ref, utri_ref, dest_ref, cnt_ref, cnt_scr, run_scr, start_scr):
    ph = pl.program_id(0)
    i = pl.program_id(1)
    idx = idx_ref[...]
    lane = lax.broadcasted_iota(jnp.int32, (idx.shape[0], N_EXPERTS), 1)
    hot = [idx[:, k:k + 1] == lane for k in range(TOP_K)]
    multi = jnp.zeros(lane.shape, F32)
    for k in range(TOP_K):
        multi = multi + jnp.where(hot[k], 1.0, 0.0)
    tile_cnt = jnp.sum(multi, axis=0, keepdims=True)

    @pl.when(jnp.logical_and(ph == 0, i == 0))
    def _():
        cnt_scr[...] = jnp.zeros_like(cnt_scr)

    @pl.when(ph == 0)
    def _():
        cnt_scr[...] += tile_cnt

    @pl.when(jnp.logical_and(ph == 1, i == 0))
    def _():
        cnt = cnt_scr[...]
        padded = jnp.floor((cnt + (MOE_ROWS - 1)) * (1.0 / MOE_ROWS)) * MOE_ROWS
        start_scr[...] = _dot(jnp.broadcast_to(padded, (8, N_EXPERTS)), utri_ref[...], HI)[0:1, :]
        run_scr[...] = jnp.zeros_like(run_scr)
        cnt_ref[...] = cnt

    @pl.when(ph == 1)
    def _():
        before = _dot(ltri_ref[...], multi.astype(BF16))
        base = start_scr[...] + run_scr[...] + before
        cols = [jnp.sum(jnp.where(hot[k], base, 0.0), axis=1, keepdims=True) for k in range(TOP_K)]
        dest_ref[...] = jnp.concatenate(cols, axis=1).astype(jnp.int32)
        run_scr[...] += tile_cnt


def _route_dest(top_idx):
    n = top_idx.shape[0]
    tile = max(c for c in range(COMBINE_TOK, ROW_TILE + 1, COMBINE_TOK) if n % c == 0)
    nt = n // tile
    ltri = jnp.asarray(np.tril(np.ones((tile, tile), np.float32), -1)).astype(BF16)
    utri = jnp.asarray(np.triu(np.ones((N_EXPERTS, N_EXPERTS), np.float32), 1))
    return pl.pallas_call(
        _dest_kernel,
        out_shape=[jax.ShapeDtypeStruct((n, TOP_K), jnp.int32),
                   jax.ShapeDtypeStruct((1, N_EXPERTS), F32)],
        grid=(2, nt),
        in_specs=[pl.BlockSpec((tile, TOP_K), lambda ph, i: (i, 0)),
                  pl.BlockSpec((tile, tile), lambda ph, i: (0, 0)),
                  pl.BlockSpec((N_EXPERTS, N_EXPERTS), lambda ph, i: (0, 0))],
        out_specs=[pl.BlockSpec((tile, TOP_K), lambda ph, i: (i * ph, 0)),
                   pl.BlockSpec((1, N_EXPERTS), lambda ph, i: (0, 0))],
        scratch_shapes=[pltpu.VMEM((1, N_EXPERTS), F32)] * 3,
        compiler_params=_cp(("arbitrary", "arbitrary")),
        name="moe_dest",
    )(top_idx, ltri, utri)


def _block_experts(counts, n_blk):
    cnt = counts.reshape(N_EXPERTS).astype(jnp.int32)
    pends = jnp.cumsum((cnt + MOE_ROWS - 1) // MOE_ROWS * MOE_ROWS)
    starts = jnp.arange(n_blk, dtype=jnp.int32)[:, None] * MOE_ROWS
    blk_expert = jnp.minimum(jnp.sum((pends[None, :] <= starts).astype(jnp.int32), axis=1), N_EXPERTS - 1)
    last_blk = jnp.maximum(pends // MOE_ROWS - 1, 0).astype(jnp.int32)
    return blk_expert, (pends[-1] // MOE_ROWS).astype(jnp.int32).reshape(1), last_blk


def _dispatch_kernel(last_ref, na_ref, dest_ref, x1_ref, rows_hbm, zbuf, sem, zsem, *, tm, n_blk):
    @pl.when(pl.program_id(0) == 0)
    def _():
        zbuf[...] = jnp.zeros_like(zbuf)
        n_tail = n_blk - na_ref[0]

        def clear(blk):
            return pltpu.make_async_copy(zbuf, rows_hbm.at[pl.ds(blk * MOE_ROWS, MOE_ROWS), :], zsem)

        def start(j, carry):
            clear(jnp.where(j < N_EXPERTS, last_ref[jnp.minimum(j, N_EXPERTS - 1)],
                            na_ref[0] + j - N_EXPERTS)).start()
            return carry

        def wait(j, carry):
            clear(0).wait()
            return carry

        lax.fori_loop(0, N_EXPERTS + n_tail, start, 0)
        lax.fori_loop(0, N_EXPERTS + n_tail, wait, 0)

    for r in range(tm):
        for k in range(TOP_K):
            pltpu.make_async_copy(x1_ref.at[pl.ds(r, 1), :],
                                  rows_hbm.at[pl.ds(dest_ref[0, 0, r * TOP_K + k], 1), :], sem).start()
    for k in range(TOP_K):
        pltpu.make_async_copy(x1_ref, rows_hbm.at[pl.ds(0, tm), :], sem).wait()


def _dispatch(x1, dest, last_blk, n_active, n_blk):
    n = x1.shape[0]
    tm = COMBINE_TOK
    nt = n // tm
    kern = functools.partial(_dispatch_kernel, tm=tm, n_blk=n_blk)
    grid_spec = pltpu.PrefetchScalarGridSpec(
        num_scalar_prefetch=2,
        grid=(nt,),
        in_specs=[pl.BlockSpec((1, 1, TOP_K * tm), lambda i, lb, na: (i, 0, 0), memory_space=pltpu.SMEM),
                  pl.BlockSpec((tm, D_MODEL), lambda i, lb, na: (i, 0))],
        out_specs=pl.BlockSpec(memory_space=pl.ANY),
        scratch_shapes=[pltpu.VMEM((MOE_ROWS, D_MODEL), F32), pltpu.SemaphoreType.DMA,
                        pltpu.SemaphoreType.DMA],
    )
    return pl.pallas_call(
        kern,
        out_shape=jax.ShapeDtypeStruct((n_blk * MOE_ROWS, D_MODEL), F32),
        grid_spec=grid_spec,
        compiler_params=_cp(("arbitrary",)),
        name="moe_dispatch",
    )(last_blk, n_active, dest.reshape(nt, 1, TOP_K * tm), x1)


def _swiglu(h):
    h_glu = jnp.minimum(h[:, :D_FF], SWIGLU_LIMIT)
    h_lin = jnp.clip(h[:, D_FF:], -SWIGLU_LIMIT, SWIGLU_LIMIT)
    return h_glu * _sigmoid(SWIGLU_ALPHA * h_glu) * (h_lin + 1.0)


def _moe_kernel(be_ref, na_ref, par_ref, nxt_ref, x_ref, wup_hbm, bup_ref, wdn_hbm, bdn_ref, y_ref,
                wup_f, wdn_f, wup_b, wdn_b, sem, *, layer):
    i = pl.program_id(0)
    active = i < na_ref[0]

    def fetch(expert, s):
        return (pltpu.make_async_copy(wup_hbm.at[layer, expert], wup_f.at[s], sem.at[s]),
                pltpu.make_async_copy(wdn_hbm.at[layer, expert], wdn_f.at[s], sem.at[s]))

    @pl.when(active)
    def _():
        changed = jnp.logical_or(i == 0, be_ref[i] != be_ref[jnp.maximum(i - 1, 0)])

        @pl.when(i == 0)
        def _():
            for c in fetch(be_ref[0], 0):
                c.start()

        for s in range(2):
            @pl.when(jnp.logical_and(changed, par_ref[i] == s))
            def _(s=s):
                for c in fetch(0, s):
                    c.wait()
                wup_b[...] = wup_f[s].astype(BF16)
                wdn_b[...] = wdn_f[s].astype(BF16)

                @pl.when(nxt_ref[i] >= 0)
                def _():
                    for c in fetch(nxt_ref[i], 1 - s):
                        c.start()

        h = _dot(x_ref[...].astype(BF16), wup_b[...]) + bup_ref[...]
        y_ref[...] = _dot(_swiglu(h).astype(BF16), wdn_b[...]) + bdn_ref[...]

    @pl.when(jnp.logical_not(active))
    def _():
        y_ref[...] = jnp.zeros_like(y_ref)


def _run_schedule(blk_expert, n_active):
    n_blk = blk_expert.shape[0]
    pos = jnp.arange(n_blk, dtype=jnp.int32)
    first = jnp.concatenate([jnp.ones((1,), bool), blk_expert[1:] != blk_expert[:-1]]) & (pos < n_active[0])
    parity = ((jnp.cumsum(first.astype(jnp.int32)) - 1) % 2).astype(jnp.int32)
    later = jnp.where(first, pos, n_blk)
    nxt_pos = lax.cummin(jnp.concatenate([later[1:], jnp.full((1,), n_blk, jnp.int32)]), axis=0, reverse=True)
    nxt = jnp.where(nxt_pos < n_blk, blk_expert[jnp.minimum(nxt_pos, n_blk - 1)], -1).astype(jnp.int32)
    return parity, nxt


def _moe_rows(rows, blk_expert, n_active, w_up, b_up, w_dn, b_dn, layer):
    n_blk = blk_expert.shape[0]
    parity, nxt = _run_schedule(blk_expert, n_active)
    grid_spec = pltpu.PrefetchScalarGridSpec(
        num_scalar_prefetch=4,
        grid=(n_blk,),
        in_specs=[
            pl.BlockSpec((MOE_ROWS, D_MODEL), lambda i, be, na, *_: (jnp.minimum(i, jnp.maximum(na[0] - 1, 0)), 0)),
            pl.BlockSpec(memory_space=pl.ANY),
            pl.BlockSpec((None, None, 1, 2 * D_FF), lambda i, be, na, *_: (layer, be[i], 0, 0)),
            pl.BlockSpec(memory_space=pl.ANY),
            pl.BlockSpec((None, None, 1, D_MODEL), lambda i, be, na, *_: (layer, be[i], 0, 0)),
        ],
        out_specs=pl.BlockSpec((MOE_ROWS, D_MODEL), lambda i, be, na, *_: (i, 0)),
        scratch_shapes=[pltpu.VMEM((2, D_MODEL, 2 * D_FF), F32),
                        pltpu.VMEM((2, D_FF, D_MODEL), F32),
                        pltpu.VMEM((D_MODEL, 2 * D_FF), BF16),
                        pltpu.VMEM((D_FF, D_MODEL), BF16),
                        pltpu.SemaphoreType.DMA((2,))],
    )
    return pl.pallas_call(
        functools.partial(_moe_kernel, layer=layer),
        out_shape=jax.ShapeDtypeStruct((n_blk * MOE_ROWS, D_MODEL), F32),
        grid_spec=grid_spec,
        compiler_params=_cp(("arbitrary",)),
        name="moe_rows",
    )(blk_expert, n_active, parity, nxt, rows,
      w_up, b_up.reshape(DEPTH, N_EXPERTS, 1, 2 * D_FF), w_dn, b_dn.reshape(DEPTH, N_EXPERTS, 1, D_MODEL))


def _combine_kernel(dest_ref, destn_ref, gate_ref, x1_ref, y_hbm, g_ref, b_ref, o_ref, tail_ref, ybuf, sem,
                    *, tm, head_tiles):
    i = pl.program_id(0)
    slot = i % 2
    n_rows = TOP_K * tm

    def gather_tile(dests, s):
        for r in range(n_rows):
            pltpu.make_async_copy(y_hbm.at[pl.ds(dests[0, 0, r], 1), :],
                                  ybuf.at[s, pl.ds(r, 1), :], sem.at[s]).start()

    @pl.when(i == 0)
    def _():
        gather_tile(dest_ref, 0)

    for s in range(2):
        @pl.when(jnp.logical_and(i + 1 < pl.num_programs(0), slot == 1 - s))
        def _(s=s):
            gather_tile(destn_ref, s)

    for s in range(2):
        @pl.when(slot == s)
        def _(s=s):
            pltpu.make_async_copy(y_hbm.at[pl.ds(0, n_rows), :], ybuf.at[s], sem.at[s]).wait()
            gates = gate_ref[...]
            moe = jnp.zeros((tm, D_MODEL), F32)
            for k in range(TOP_K):
                moe = moe + gates[:, k:k + 1] * ybuf[s, k * tm:(k + 1) * tm, :]
            res = _layernorm_rows(ALPHA * x1_ref[...] + moe, g_ref[...], b_ref[...])

            @pl.when(i < head_tiles)
            def _():
                o_ref[...] = res

            @pl.when(i >= head_tiles)
            def _():
                tail_ref[...] = res


def _combine(dest, gates, x1, y_rows, g, b, n_head):
    n = x1.shape[0]
    tm = COMBINE_TOK
    nt = n // tm
    head_tiles = n_head // tm
    dest_t = dest.reshape(nt, tm, TOP_K).transpose(0, 2, 1).reshape(nt, 1, TOP_K * tm)
    kern = functools.partial(_combine_kernel, tm=tm, head_tiles=head_tiles)
    return pl.pallas_call(
        kern,
        out_shape=[jax.ShapeDtypeStruct((n_head, D_MODEL), F32),
                   jax.ShapeDtypeStruct((n - n_head, D_MODEL), F32)],
        grid=(nt,),
        in_specs=[pl.BlockSpec((1, 1, TOP_K * tm), lambda i: (i, 0, 0), memory_space=pltpu.SMEM),
                  pl.BlockSpec((1, 1, TOP_K * tm), lambda i: (jnp.minimum(i + 1, nt - 1), 0, 0),
                               memory_space=pltpu.SMEM),
                  pl.BlockSpec((tm, TOP_K), lambda i: (i, 0)),
                  pl.BlockSpec((tm, D_MODEL), lambda i: (i, 0)),
                  pl.BlockSpec(memory_space=pl.ANY),
                  pl.BlockSpec((1, D_MODEL), lambda i: (0, 0)),
                  pl.BlockSpec((1, D_MODEL), lambda i: (0, 0))],
        out_specs=[pl.BlockSpec((tm, D_MODEL), lambda i: (jnp.minimum(i, head_tiles - 1), 0)),
                   pl.BlockSpec((tm, D_MODEL), lambda i: (jnp.maximum(i - head_tiles, 0), 0))],
        scratch_shapes=[pltpu.VMEM((2, TOP_K * tm, D_MODEL), F32), pltpu.SemaphoreType.DMA((2,))],
        compiler_params=_cp(("arbitrary",)),
        name="moe_combine",
    )(dest_t, dest_t, gates, x1, y_rows, g.reshape(1, -1), b.reshape(1, -1))


def _moe(x1, top_idx, gates, p, layer, n_head):
    n = x1.shape[0]
    n_blk = -(-n * TOP_K // MOE_ROWS) + N_EXPERTS
    dest, counts = _route_dest(top_idx)
    blk_expert, n_active, last_blk = _block_experts(counts, n_blk)
    rows = _dispatch(x1, dest, last_blk, n_active, n_blk)
    y_rows = _moe_rows(rows, blk_expert, n_active, p["moe_w_up"], p["moe_b_up"],
                       p["moe_w_down"], p["moe_b_down"], layer)
    return _combine(dest, gates, x1, y_rows, p["ln2_g"][layer], p["ln2_b"][layer], n_head)


def _sample_proj_kernel(x_ref, w_ref, h0_ref, lam_ref, bblk_ref, cblk_ref, d_ref, gw_ref, gb_ref,
                        q_ref, k_ref, v_ref, pc_ref, oa_ref, h_ref):
    xb = x_ref[...].astype(BF16)
    mm = lambda lo, hi: _dot(xb, w_ref[:, lo:hi])
    u = mm(_PROJ_SPLITS[0], _PROJ_SPLITS[1])
    q_ref[...] = mm(_PROJ_SPLITS[1], _PROJ_SPLITS[2])
    k_ref[...] = mm(_PROJ_SPLITS[2], _PROJ_SPLITS[3])
    v_ref[...] = mm(_PROJ_SPLITS[3], _PROJ_SPLITS[4])
    pc_ref[...] = mm(_PROJ_SPLITS[4], _PROJ_SPLITS[5])
    bu = _dot(u.astype(BF16), bblk_ref[...])
    bu_re = bu[:, 0:N_ST]
    bu_im = bu_re + bu[:, N_ST:2 * N_ST]
    lr, li = lam_ref[0:1, :], lam_ref[1:2, :]
    hr0, hi0 = h0_ref[0], h0_ref[1]
    hr = bu_re + (lr * hr0 - li * hi0)
    hi = bu_im + (lr * hi0 + li * hr0)
    h_ref[0] = hr
    h_ref[1] = hi
    y = (_dot((hr + hi).astype(BF16), cblk_ref[0:N_ST, :])
         - _dot(hi.astype(BF16), cblk_ref[N_ST:2 * N_ST, :])) + d_ref[...] * u
    z = _gelu(y)
    oa_ref[...] = z * _sigmoid(_dot(z.astype(BF16), gw_ref[...]) + gb_ref[...])


def _sample_proj(x, w_in, h0, lam2, bblk, cblk, d, gw, gb):
    n = x.shape[0]
    widths = (W_B, W_B, W_B, C_SHIFT, W_A)
    return pl.pallas_call(
        _sample_proj_kernel,
        out_shape=[jax.ShapeDtypeStruct((n, w), F32) for w in widths]
        + [jax.ShapeDtypeStruct((2, n, N_ST), F32)],
        compiler_params=_cp(None),
        name="sample_proj",
    )(x, w_in, h0, lam2, bblk, cblk, d.reshape(1, -1), gw.astype(BF16), gb.reshape(1, -1))


def _sample_mix_kernel(q_ref, kn_ref, vn_ref, pc_ref, sh_ref, kc_ref, vc_ref, s0_ref, lb_ref, lb0_ref,
                       mu_ref, w0_ref, w2_ref, a0_ref, a2_ref, g2_ref, kk_ref, ka_ref,
                       rk_ref, lng_ref, lnb_ref, mseg_ref,
                       ob_ref, oc_ref, s_ref):
    rows = 8
    lane = lax.broadcasted_iota(jnp.int32, (rows, W_B), 1)
    sub = lax.broadcasted_iota(jnp.int32, (rows, W_B), 0)
    hmask = (lane // HEAD_DIM) == sub
    rnd = lambda t: t.astype(BF16).astype(F32)
    q = q_ref[0] * (HEAD_DIM ** -0.5)
    qrows = jnp.where(hmask, jnp.broadcast_to(q, (rows, W_B)), 0.0).astype(BF16)
    kcb = kc_ref[0].astype(BF16)
    vcb = vc_ref[0].astype(BF16)
    knb = rnd(kn_ref[0])
    vnb = rnd(vn_ref[0])
    s_all = _dot(qrows, kcb)
    s_new = jnp.sum(qrows.astype(F32) * knb, axis=-1, keepdims=True)
    probs, p_new, lses = [], [], []
    for br in range(len(BRANCHES)):
        s = s_all + lb_ref[br]
        s0 = s_new + lb0_ref[br]
        m = jnp.maximum(jnp.max(s, axis=-1, keepdims=True), s0)
        lse = m + jnp.log(jnp.sum(jnp.exp(s - m), axis=-1, keepdims=True) + jnp.exp(s0 - m))
        probs.append(jnp.exp(s - lse))
        p_new.append(rnd(jnp.exp(s0 - lse)))
        lses.append(lse)
    pv = _dot_nt(jnp.concatenate(probs, axis=0).astype(BF16), vcb)
    outs = [rnd(pv[br * rows:(br + 1) * rows] + p_new[br] * vnb) for br in range(len(BRANCHES))]
    top = jnp.maximum(jnp.maximum(lses[0], lses[1]), lses[2])
    ex = [jnp.exp(t - top) for t in lses]
    tot = ex[0] + ex[1] + ex[2]
    o = rnd(ex[0] / tot) * outs[0] + rnd(ex[1] / tot) * outs[1] + rnd(ex[2] / tot) * outs[2]
    ob_ref[0] = jnp.sum(jnp.where(hmask, o, 0.0), axis=0, keepdims=True)

    prm = (mu_ref[...], w0_ref[...], w2_ref[...], a0_ref[...], a2_ref[...], g2_ref[...],
           kk_ref[...], ka_ref[...], rk_ref[...], lng_ref[...], lnb_ref[...], mseg_ref[...])
    pc = jnp.broadcast_to(pc_ref[0], (rows, C_SHIFT))
    prev = jnp.broadcast_to(sh_ref[0], (rows, C_SHIFT))
    head_of_lane = lax.broadcasted_iota(jnp.int32, (rows, W_C), 1) // HEAD_DIM

    def seg_sum(t):
        out = jnp.zeros_like(t)
        for h in range(H_C):
            m = head_of_lane == h
            out = jnp.where(m, jnp.sum(jnp.where(m, t, 0.0), axis=1, keepdims=True), out)
        return out
    r, k2, v, kk, a, g, logw = _rwkv_pre(pc, prev, prm, seg_sum)
    w = jnp.exp(logw)
    eye = (lax.broadcasted_iota(jnp.int32, (HEAD_DIM, HEAD_DIM), 0)
           == lax.broadcasted_iota(jnp.int32, (HEAD_DIM, HEAD_DIM), 1))
    heads = range(H_C)
    row = lambda t, h: t[0:1, h * HEAD_DIM:(h + 1) * HEAD_DIM]
    col = lambda t, h: jnp.sum(jnp.where(eye, jnp.broadcast_to(row(t, h), (HEAD_DIM, HEAD_DIM)), 0.0),
                               axis=1, keepdims=True)
    s0 = [s0_ref[0, h] for h in heads]
    sa = [jnp.sum(rnd(s0[h]) * rnd(-row(kk, h)), axis=1, keepdims=True) for h in heads]
    v_col = [col(v, h) for h in heads]
    s1 = [s0[h] * row(w, h) + sa[h] * (row(kk, h) * row(a, h)) + v_col[h] * row(k2, h) for h in heads]
    y_col = [jnp.sum(rnd(s1[h]) * rnd(row(r, h)), axis=1, keepdims=True) for h in heads]
    ys = [jnp.sum(jnp.where(eye, jnp.broadcast_to(y_col[h], (HEAD_DIM, HEAD_DIM)), 0.0), axis=0, keepdims=True)
          for h in heads]
    for h in heads:
        s_ref[0, h] = s1[h]
    y = jnp.broadcast_to(jnp.concatenate(ys, axis=1), (rows, W_C))
    oc_ref[0] = _rwkv_post(y, r, k2, v, g, prm, seg_sum)[0:1, :]


def _sample_mix(q, kn, vn, pc, shift0, k_cache, v_cache, s0, lb_rows, lb0, prm, layer):
    n = q.shape[0]
    l_buf = k_cache.shape[3]
    per_b = lambda w: pl.BlockSpec((1, 1, w), lambda b: (b, 0, 0))
    const = lambda a: pl.BlockSpec(a.shape, lambda b: (0,) * a.ndim)
    r3 = lambda t: t.reshape(n, 1, t.shape[-1])
    cache = pl.BlockSpec((None, 1, W_B, l_buf), lambda b: (layer, b, 0, 0))
    return pl.pallas_call(
        _sample_mix_kernel,
        out_shape=[jax.ShapeDtypeStruct((n, 1, W_B), F32),
                   jax.ShapeDtypeStruct((n, 1, W_C), F32),
                   jax.ShapeDtypeStruct((n, H_C, HEAD_DIM, HEAD_DIM), F32)],
        grid=(n,),
        in_specs=[per_b(W_B), per_b(W_B), per_b(W_B), per_b(C_SHIFT), per_b(C_SHIFT),
                  cache, cache,
                  pl.BlockSpec((1, H_C, HEAD_DIM, HEAD_DIM), lambda b: (b, 0, 0, 0)),
                  const(lb_rows), const(lb0)] + [const(a) for a in prm],
        out_specs=[per_b(W_B), per_b(W_C),
                   pl.BlockSpec((1, H_C, HEAD_DIM, HEAD_DIM), lambda b: (b, 0, 0, 0))],
        compiler_params=_cp(("parallel",)),
        name="sample_mix",
    )(r3(q), r3(kn), r3(vn), r3(pc), r3(shift0), k_cache, v_cache, s0, lb_rows, lb0, *prm)


def _prompt_mixers(x, nb, t, p, l, s5p, rwp, tiles, kv_prev):
    u, q, k_all, v_all, pc = _proj(x, p["w_in"][l].astype(BF16), nb, t, l, kv_prev)
    shp = lambda a: a.reshape(nb, t, a.shape[-1])
    o_a, h_t = _s5_prompt(shp(u), *s5p, p["ssm_d"][l], p["ssm_glu_w"][l], p["ssm_glu_b"][l])
    o_b = _attn_prompt(shp(q), k_all, v_all, tiles, l)
    o_c, s_t = _rwkv_prompt(shp(pc), rwp)
    flat = lambda a: a.reshape(nb * t, a.shape[-1])
    ssm = jnp.transpose(h_t.reshape(2, nb, G_A, N_A), (1, 2, 3, 0))
    return (flat(o_a), flat(o_b), flat(o_c)), (ssm, s_t, shp(pc)[:, t - 1]), (k_all, v_all)


def _sample_bias(rel_bias, l_buf):
    offs = _branch_offsets()
    bias = rel_bias[jnp.asarray(_t5_buckets(offs))]
    rows = []
    for br in range(len(BRANCHES)):
        m = np.arange(1, offs.shape[1])
        pos = l_buf - offs[br][m]
        ok = pos >= 0
        r = jnp.full((H_B, l_buf), NEG_INF, F32).at[:, jnp.asarray(pos[ok])].set(bias[br][jnp.asarray(m[ok])].T)
        rows.append(jnp.concatenate([r, jnp.zeros((8 - H_B, l_buf), F32)], axis=0))
    new = jnp.zeros((len(BRANCHES), 8, 1), F32).at[:, :H_B, 0].set(bias[:, 0])
    return jnp.stack(rows), new


def _sample_mixers(x, p, l, s5p, rwp, sbias, k_cache, v_cache, st_ssm, st_rwkv, st_shift):
    n = x.shape[0]
    lam2, bblk, cblk = s5p
    h0 = jnp.transpose(st_ssm.reshape(n, N_ST, 2), (2, 0, 1))
    q, kn, vn, pc, o_a, h1 = _sample_proj(x, p["w_in"][l].astype(BF16), h0, lam2, bblk, cblk, p["ssm_d"][l],
                                          p["ssm_glu_w"][l], p["ssm_glu_b"][l])
    lb_rows, lb0 = sbias
    o_b, o_c, s1 = _sample_mix(q, kn, vn, pc, st_shift, k_cache, v_cache, st_rwkv, lb_rows, lb0, rwp, l)
    ssm = jnp.transpose(h1.reshape(2, n, G_A, N_A), (1, 2, 3, 0))
    return (o_a, o_b.reshape(n, W_B), o_c.reshape(n, W_C)), (kn, vn, ssm, s1, pc)


def _shift_kernel(c_ref, new_ref, o_ref):
    x = c_ref[...]
    length = x.shape[1]
    lane = lax.broadcasted_iota(jnp.int32, x.shape, 1)
    o_ref[...] = jnp.where(lane == length - 1, new_ref[...], pltpu.roll(x, length - 1, axis=1))


def _shift_cache(cache_t, new):
    rows, length = cache_t.shape
    blk = min(ROW_TILE, rows)
    return pl.pallas_call(
        _shift_kernel,
        out_shape=jax.ShapeDtypeStruct((rows, length), F32),
        grid=(rows // blk,),
        in_specs=[pl.BlockSpec((blk, length), lambda i: (i, 0)),
                  pl.BlockSpec((blk, 1), lambda i: (i, 0))],
        out_specs=pl.BlockSpec((blk, length), lambda i: (i, 0)),
        compiler_params=_cp(("parallel",)),
        name="shift_cache",
    )(cache_t, new)


def kernel(x_prompt, x_sample, cache_attn_k, cache_attn_v, state_ssm, state_rwkv, state_shift, w_in, w_out, ln1_g, ln1_b, ln2_g, ln2_b, ssm_a_re, ssm_a_im, ssm_log_dt, ssm_b_re, ssm_b_im, ssm_c_re, ssm_c_im, ssm_d, ssm_glu_w, ssm_glu_b, rel_bias, rwkv_mu, rwkv_w0, rwkv_w2, rwkv_a0, rwkv_a2, rwkv_g2, rwkv_k_k, rwkv_k_a, rwkv_r_k, rwkv_ln_g, rwkv_ln_b, moe_router_w, moe_router_b, moe_w_up, moe_b_up, moe_w_down, moe_b_down):
    p = dict(w_in=w_in, w_out=w_out, ln1_g=ln1_g, ln1_b=ln1_b, ln2_g=ln2_g, ln2_b=ln2_b,
             ssm_d=ssm_d, ssm_glu_w=ssm_glu_w, ssm_glu_b=ssm_glu_b,
             moe_router_w=moe_router_w, moe_router_b=moe_router_b, moe_w_up=moe_w_up,
             moe_b_up=moe_b_up, moe_w_down=moe_w_down, moe_b_down=moe_b_down)
    nb, t, _ = x_prompt.shape
    ns = x_sample.shape[0]
    l_buf = cache_attn_k.shape[2]
    table = _log_bias_table(rel_bias)
    tiles = _bias_tiles(table, min(t, WIN_MAX) // Q_TILE + 1)
    sbias = _sample_bias(rel_bias, l_buf)
    n_p = nb * t
    n_all = n_p + -(-ns // COMBINE_TOK) * COMBINE_TOK
    xp = x_prompt.reshape(n_p, D_MODEL)
    xs = x_sample.reshape(ns, D_MODEL)
    st_p, st_s = [], []
    time_minor = lambda c: jnp.transpose(c, (0, 1, 3, 4, 2)).reshape(DEPTH, c.shape[1], W_B, c.shape[2])
    time_major = lambda c, n: jnp.transpose(c.reshape(DEPTH, n, H_B, HEAD_DIM, c.shape[-1]), (0, 1, 4, 2, 3))
    kc_t, vc_t = time_minor(cache_attn_k), time_minor(cache_attn_v)
    kv = (jnp.zeros((DEPTH, nb, W_B, t), F32), jnp.zeros((DEPTH, nb, W_B, t), F32))
    for l in range(DEPTH):
        s5p, s5s = _s5_params(ssm_a_re[l], ssm_a_im[l], ssm_log_dt[l], ssm_b_re[l], ssm_b_im[l],
                              ssm_c_re[l], ssm_c_im[l])
        rwp = _rwkv_params(dict(rwkv_mu=rwkv_mu[l], rwkv_w0=rwkv_w0[l], rwkv_w2=rwkv_w2[l],
                                rwkv_a0=rwkv_a0[l], rwkv_a2=rwkv_a2[l], rwkv_g2=rwkv_g2[l],
                                rwkv_k_k=rwkv_k_k[l], rwkv_k_a=rwkv_k_a[l], rwkv_r_k=rwkv_r_k[l],
                                rwkv_ln_g=rwkv_ln_g[l], rwkv_ln_b=rwkv_ln_b[l]))
        mix_p, sp, kv = _prompt_mixers(xp, nb, t, p, l, s5p, rwp, tiles, kv)
        mix_s, ss = _sample_mixers(xs, p, l, s5s, rwp, sbias, kc_t, vc_t,
                                   state_ssm[l], state_rwkv[l], state_shift[l])
        st_p.append(sp)
        st_s.append(ss)
        bufs = (jnp.zeros((n_all, D_MODEL), F32),
                jnp.broadcast_to(jnp.arange(TOP_K, dtype=jnp.int32), (n_all, TOP_K)),
                jnp.zeros((n_all, TOP_K), F32))
        head = (p["w_out"][l], p["ln1_g"][l], p["ln1_b"][l], p["moe_router_w"][l], p["moe_router_b"][l])
        bufs = _outproj(xp, *mix_p, *head, bufs, 0)
        bufs = _outproj(xs, *mix_s, *head, bufs, n_p)
        xp, tail = _moe(*bufs, p, l, n_p)
        xs = tail[:ns]
    p_ssm, p_rwkv, p_shift = (jnp.stack(z) for z in zip(*st_p))
    kn, vn, s_ssm, s_rwkv, s_shift = (jnp.stack(z) for z in zip(*st_s))
    keep_p = min(WIN_MAX, t)
    p_k, p_v = (time_major(a[..., t - keep_p:], nb) for a in kv)
    keep = min(WIN_MAX, l_buf + 1)
    drop = l_buf + 1 - keep
    if drop == 1:
        advance = lambda c_t, new: _shift_cache(c_t.reshape(-1, l_buf), new.reshape(-1, 1)).reshape(c_t.shape)
    else:
        advance = lambda c_t, new: jnp.concatenate([c_t[..., drop:], new[..., None]], axis=-1)
    s_k = time_major(advance(kc_t, kn), ns)
    s_v = time_major(advance(vc_t, vn), ns)
    return (xp.reshape(nb, t, D_MODEL), xs.reshape(ns, 1, D_MODEL),
            p_k, p_v, p_ssm, p_rwkv, p_shift, s_k, s_v, s_ssm, s_rwkv, s_shift)
```

```python
import functools
import math

import numpy as np
import jax
import jax.numpy as jnp
from jax import lax
from jax.experimental import pallas as pl
from jax.experimental.pallas import tpu as pltpu

F32 = jnp.float32
BF16 = jnp.bfloat16
HI = lax.Precision.HIGHEST

D_MODEL = 1024
DEPTH = 2
HEAD_DIM = 64
W_A = 256
C_GRP = 16
G_A = 16
N_A = 64
N_ST = G_A * N_A
W_B = 384
H_B = 6
BRANCHES = ((128, 1), (512, 4), (2048, 16))
WIN_MAX = 2048
N_BUCKETS = 32
MAX_DIST = WIN_MAX
NEG_INF = -1e30
W_C = 384
H_C = 6
R_W = 32
R_A = 32
R_G = 64
C_SHIFT = 3 * W_C + R_W + R_A + R_G
N_IN = W_A + 3 * W_B + C_SHIFT
GN_EPS = 64e-5
N_EXPERTS = 32
TOP_K = 4
D_FF = D_MODEL
SWIGLU_ALPHA = 1.702
SWIGLU_LIMIT = 7.0
ALPHA = (2 * DEPTH) ** 0.25
LN_EPS = 1e-5

LANES = 128
Q_TILE = 128
RWKV_CHUNK = 64
RWKV_SEQS = 2
S5_CHUNK = 64
ROW_TILE = 512
MOE_ROWS = 256
COMBINE_TOK = 128
VMEM_LIMIT = 56 * 1024 * 1024


def _cp(sem, vmem=VMEM_LIMIT):
    return pltpu.CompilerParams(dimension_semantics=sem, vmem_limit_bytes=vmem)


def _dot(a, b, precision=None):
    return jnp.dot(a, b, preferred_element_type=F32, precision=precision)


def _dot_nt(a, b, precision=None):
    return lax.dot_general(a, b, (((1,), (1,)), ((), ())),
                           preferred_element_type=F32, precision=precision)


def _dot_tn(a, b, precision=None):
    return lax.dot_general(a, b, (((0,), (0,)), ((), ())),
                           preferred_element_type=F32, precision=precision)


def _layernorm_rows(x, g, b):
    mu = jnp.mean(x, axis=-1, keepdims=True)
    d = x - mu
    var = jnp.mean(d * d, axis=-1, keepdims=True)
    return d * lax.rsqrt(var + LN_EPS) * g + b


def _sigmoid(x):
    return 1.0 / (1.0 + jnp.exp(-x))


def _softplus(x):
    return jnp.maximum(x, 0.0) + jnp.log(1.0 + jnp.exp(-jnp.abs(x)))


def _gelu(x):
    c = math.sqrt(2.0 / math.pi)
    return 0.5 * x * (1.0 + jnp.tanh(c * (x + 0.044715 * (x * x * x))))


_PROJ_SPLITS = (0, W_A, W_A + W_B, W_A + 2 * W_B, W_A + 3 * W_B, N_IN)


def _proj_kernel(x_ref, w_ref, *refs):
    u_ref, q_ref, k_ref, v_ref, pc_ref = refs[-5:]
    xb = x_ref[...].astype(BF16)
    outs = ((u_ref, False), (q_ref, False), (k_ref, True), (v_ref, True), (pc_ref, False))
    for (o_ref, time_minor), lo, hi in zip(outs, _PROJ_SPLITS[:-1], _PROJ_SPLITS[1:]):
        y = _dot(xb, w_ref[:, lo:hi])
        o_ref[...] = jnp.transpose(y) if time_minor else y


def _proj(x, w_bf16, nb, t, layer, kv_prev):
    n = nb * t
    per_seq = t // ROW_TILE
    widths = [hi - lo for lo, hi in zip(_PROJ_SPLITS[:-1], _PROJ_SPLITS[1:])]
    rows = lambda w: pl.BlockSpec((ROW_TILE, w), lambda i: (i, 0))
    stacked = pl.BlockSpec((None, None, W_B, ROW_TILE), lambda i: (layer, i // per_seq, 0, i % per_seq))
    flat = lambda w: jax.ShapeDtypeStruct((n, w), F32)
    kv = jax.ShapeDtypeStruct((DEPTH, nb, W_B, t), F32)
    prev = list(kv_prev)
    return pl.pallas_call(
        _proj_kernel,
        out_shape=[flat(widths[0]), flat(widths[1]), kv, kv, flat(widths[4])],
        grid=(n // ROW_TILE,),
        in_specs=[pl.BlockSpec((ROW_TILE, D_MODEL), lambda i: (i, 0)),
                  pl.BlockSpec((D_MODEL, N_IN), lambda i: (0, 0))]
        + [pl.BlockSpec(memory_space=pl.ANY)] * len(prev),
        out_specs=[rows(widths[0]), rows(widths[1]), stacked, stacked, rows(widths[4])],
        input_output_aliases={2: 2, 3: 3},
        compiler_params=_cp(("parallel",)),
        name="proj",
    )(x, w_bf16, *prev)


def _s5_kernel(u_ref, lam_ref, bblk_ref, cblk_ref, d_ref, gw_ref, gb_ref,
               o_ref, ht_ref, utm, hs, ytm, h_scr, *, nb, lt):
    c = pl.program_id(0)

    @pl.when(c == 0)
    def _():
        h_scr[...] = jnp.zeros_like(h_scr)

    n_half = W_A // LANES
    for b in range(nb):
        for j in range(n_half):
            utm[j, pl.ds(b, lt, stride=nb), :] = u_ref[b, :, j * LANES:(j + 1) * LANES]
    u_all = jnp.concatenate([utm[j] for j in range(n_half)], axis=1)
    hs[...] = _dot(u_all.astype(BF16), bblk_ref[...])
    lr = jnp.broadcast_to(lam_ref[0:1, :], (nb, N_ST))
    li = jnp.broadcast_to(lam_ref[1:2, :], (nb, N_ST))

    def body(t, carry):
        hr, hi = carry
        row = pl.multiple_of(t * nb, nb)
        br = hs[pl.ds(row, nb), 0:N_ST]
        bi = hs[pl.ds(row, nb), N_ST:2 * N_ST]
        nr = lr * hr - li * hi + br
        ni = lr * hi + li * hr + bi
        hs[pl.ds(row, nb), 0:N_ST] = nr
        hs[pl.ds(row, nb), N_ST:2 * N_ST] = ni
        return nr, ni

    hr, hi = lax.fori_loop(0, lt, body, (h_scr[0], h_scr[1]), unroll=2)
    h_scr[0] = hr
    h_scr[1] = hi
    ht_ref[0] = hr
    ht_ref[1] = hi
    y = _dot(hs[...].astype(BF16), cblk_ref[...]) + d_ref[...] * u_all
    z = _gelu(y)
    gl = _dot(z.astype(BF16), gw_ref[...]) + gb_ref[...]
    res = z * _sigmoid(gl)
    for j in range(n_half):
        ytm[j] = res[:, j * LANES:(j + 1) * LANES]
    for b in range(nb):
        for j in range(n_half):
            o_ref[b, :, j * LANES:(j + 1) * LANES] = ytm[j, pl.ds(b, lt, stride=nb), :]


def _s5_params(a_re, a_im, log_dt, b_re, b_im, c_re, c_im):
    lam = lax.complex(a_re, a_im)
    dt = jnp.exp(log_dt)[:, None]
    lam_bar = jnp.exp(lam * dt)
    b_bar = ((lam_bar - 1.0) / lam)[..., None] * lax.complex(b_re, b_im)
    eye = jnp.eye(G_A, dtype=F32)

    def blk_in(m):
        return jnp.einsum("gnc,gh->gchn", m, eye).reshape(W_A, N_ST)

    def blk_out(m):
        return jnp.einsum("gcn,gh->gnhc", m, eye).reshape(N_ST, W_A)

    bblk = jnp.concatenate([blk_in(b_bar.real), blk_in(b_bar.imag)], axis=1)
    cblk = jnp.concatenate([blk_out(c_re), blk_out(-c_im)], axis=0)
    lam2 = jnp.stack([lam_bar.real.reshape(N_ST), lam_bar.imag.reshape(N_ST)])
    bblk3 = jnp.concatenate([blk_in(b_bar.real), blk_in(b_bar.imag - b_bar.real)], axis=1)
    cblk3 = jnp.concatenate([blk_out(c_re), blk_out(c_re + c_im)], axis=0)
    return (lam2, bblk, cblk), (lam2, bblk3.astype(BF16), cblk3.astype(BF16))


def _s5_prompt(u, lam2, bblk, cblk, d, gw, gb):
    nb, t, _ = u.shape
    lt = min(S5_CHUNK, t)
    kern = functools.partial(_s5_kernel, nb=nb, lt=lt)
    const = lambda shape: pl.BlockSpec(shape, lambda c: (0,) * len(shape))
    return pl.pallas_call(
        kern,
        out_shape=[jax.ShapeDtypeStruct((nb, t, W_A), F32),
                   jax.ShapeDtypeStruct((2, nb, N_ST), F32)],
        grid=(t // lt,),
        in_specs=[pl.BlockSpec((nb, lt, W_A), lambda c: (0, c, 0)),
                  const((2, N_ST)), const((W_A, 2 * N_ST)), const((2 * N_ST, W_A)),
                  const((1, W_A)), const((W_A, W_A)), const((1, W_A))],
        out_specs=[pl.BlockSpec((nb, lt, W_A), lambda c: (0, c, 0)),
                   const((2, nb, N_ST))],
        scratch_shapes=[pltpu.VMEM((W_A // LANES, nb * lt, LANES), F32),
                        pltpu.VMEM((nb * lt, 2 * N_ST), F32),
                        pltpu.VMEM((W_A // LANES, nb * lt, LANES), F32),
                        pltpu.VMEM((2, nb, N_ST), F32)],
        compiler_params=_cp(("arbitrary",)),
        name="s5_prompt",
    )(u, lam2, bblk.astype(BF16), cblk.astype(BF16), d.reshape(1, W_A),
      gw.astype(BF16), gb.reshape(1, W_A))


def _branch_offsets():
    return np.stack([np.arange(w // d + 1) * d for (w, d) in BRANCHES]).astype(np.int32)


def _t5_buckets(dist):
    max_exact = N_BUCKETS // 2
    d = np.maximum(dist, 1).astype(np.float32)
    large = max_exact + (np.log(d / max_exact) / np.log(MAX_DIST / max_exact)
                         * (N_BUCKETS - max_exact)).astype(np.int32)
    return np.where(dist < max_exact, dist, np.minimum(large, N_BUCKETS - 1)).astype(np.int32)


def _log_bias_table(rel_bias):
    offs = _branch_offsets()
    bias = rel_bias[jnp.asarray(_t5_buckets(offs))]
    table = jnp.full((WIN_MAX + 1, H_B), -jnp.inf, F32)
    for br in range(len(BRANCHES)):
        idx = jnp.asarray(offs[br])
        table = table.at[idx].set(jnp.logaddexp(table[idx], bias[br]))
    return jnp.maximum(table, NEG_INF)


def _bias_tiles(table, n_diff):
    per = 2 * Q_TILE
    m = np.arange(per)
    dist = np.arange(n_diff)[:, None] * Q_TILE + np.where(m < Q_TILE, -m, per - m)[None, :]
    ok = (dist >= 0) & (dist <= WIN_MAX) & (m != Q_TILE)[None, :]
    v = jnp.where(jnp.asarray(ok)[..., None], table[jnp.asarray(np.clip(dist, 0, WIN_MAX))], NEG_INF)
    v = jnp.transpose(v, (2, 0, 1))
    flat = jnp.broadcast_to(v[:, :, None, :], (H_B, n_diff, Q_TILE, per)).reshape(H_B, n_diff, Q_TILE * per)
    tiles = flat[:, :, :Q_TILE * (per - 1)].reshape(H_B, n_diff, Q_TILE, per - 1)[..., :Q_TILE]
    tiles = tiles.reshape(H_B // 2, 2, n_diff, Q_TILE, Q_TILE)
    return jnp.transpose(tiles, (0, 2, 1, 3, 4)).reshape(H_B // 2, n_diff, 2 * Q_TILE, Q_TILE)


def _attn_kernel(q_ref, k_ref, v_ref, bias_ref, o_ref, kb, vb, *, nq):
    kb[...] = k_ref[...].astype(BF16)
    vb[...] = v_ref[...].astype(BF16)
    lo = lax.broadcasted_iota(jnp.int32, (Q_TILE, LANES), 1) < HEAD_DIM
    for qi in range(nq):
        rows = slice(qi * Q_TILE, (qi + 1) * Q_TILE)
        q = q_ref[0, rows, :] * (HEAD_DIM ** -0.5)
        q2 = jnp.concatenate([jnp.where(lo, q, 0.0), jnp.where(lo, 0.0, q)], axis=0).astype(BF16)
        nk = (qi + 1) * Q_TILE
        bias = jnp.concatenate([bias_ref[qi - kj] for kj in range(qi + 1)], axis=1)
        s = _dot(q2, kb[:, 0:nk]) + bias
        m = jnp.max(s, axis=-1, keepdims=True)
        p = jnp.exp(s - m)
        l = jnp.sum(p, axis=-1, keepdims=True)
        o = _dot_nt(p.astype(BF16), vb[:, 0:nk]) / l
        o_ref[0, rows, :] = jnp.where(lo, o[0:Q_TILE], o[Q_TILE:2 * Q_TILE])


def _attn_prompt(q, k_all, v_all, tiles, layer):
    nb, t, _ = q.shape
    n_diff = tiles.shape[1]
    hp = W_B // LANES
    kern = functools.partial(_attn_kernel, nq=t // Q_TILE)
    seq = pl.BlockSpec((1, t, LANES), lambda b, p: (b, 0, p))
    kv = pl.BlockSpec((None, None, LANES, t), lambda b, p: (layer, b, p, 0))
    return pl.pallas_call(
        kern,
        out_shape=jax.ShapeDtypeStruct((nb, t, W_B), F32),
        grid=(nb, hp),
        in_specs=[seq, kv, kv,
                  pl.BlockSpec((None, n_diff, 2 * Q_TILE, Q_TILE), lambda b, p: (p, 0, 0, 0))],
        out_specs=seq,
        scratch_shapes=[pltpu.VMEM((LANES, t), BF16), pltpu.VMEM((LANES, t), BF16)],
        compiler_params=_cp(("parallel", "parallel")),
        name="attn_prompt",
    )(q, k_all, v_all, tiles)


def _split_dot(x, m_bf16, parts):
    acc = None
    rem = x
    for i in range(parts):
        piece = rem.astype(BF16)
        term = _dot(piece, m_bf16)
        acc = term if acc is None else acc + term
        if i + 1 < parts:
            rem = rem - piece.astype(F32)
    return acc


def _split_dot_left(m_bf16, x, parts):
    acc = None
    rem = x
    for i in range(parts):
        piece = rem.astype(BF16)
        term = _dot(m_bf16, piece)
        acc = term if acc is None else acc + term
        if i + 1 < parts:
            rem = rem - piece.astype(F32)
    return acc


def _rwkv_pre(pc, prev, prm, seg_sum):
    (mu, w0, w2p, a0, a2p, g2p, k_k, k_a, r_k, ln_g, ln_b, mseg) = prm
    xs = pc + (prev - pc) * mu
    r = xs[:, 0:W_C]
    k = xs[:, W_C:2 * W_C]
    v = xs[:, 2 * W_C:3 * W_C]
    tail = xs[:, 3 * W_C:C_SHIFT]
    w_raw = -_softplus(-(w0 + _dot(jnp.tanh(tail).astype(BF16), w2p))) - 0.5
    a = _sigmoid(a0 + _dot(tail.astype(BF16), a2p))
    g = _dot(_sigmoid(tail).astype(BF16), g2p)
    kk = k * k_k
    nrm = jnp.sqrt(seg_sum(kk * kk))
    kk = kk / jnp.maximum(nrm, 1e-12)
    k2 = k * (1.0 + (a - 1.0) * k_a)
    logw = -jnp.exp(w_raw)
    return r, k2, v, kk, a, g, logw


def _rwkv_post(y, r, k2, v, g, prm, seg_sum):
    (mu, w0, w2p, a0, a2p, g2p, k_k, k_a, r_k, ln_g, ln_b, mseg) = prm
    mean = seg_sum(y) * (1.0 / HEAD_DIM)
    d = y - mean
    var = seg_sum(d * d) * (1.0 / HEAD_DIM)
    yn = d * lax.rsqrt(var + GN_EPS) * ln_g + ln_b
    bonus = seg_sum(r * k2 * r_k) * v
    return (yn + bonus) * g


def _rwkv_kernel(pc_ref, mu_ref, w0_ref, w2_ref, a0_ref, a2_ref, g2_ref, kk_ref, ka_ref,
                 rk_ref, lng_ref, lnb_ref, mseg_ref, ltri_ref,
                 o_ref, s_ref, s_scr, prev_scr, yt_scr, *, ch, nseq):
    c = pl.program_id(1)
    rows = nseq * ch

    @pl.when(c == 0)
    def _():
        s_scr[...] = jnp.zeros_like(s_scr)
        prev_scr[...] = jnp.zeros_like(prev_scr)
        yt_scr[...] = jnp.zeros_like(yt_scr)

    prm = (mu_ref[...], w0_ref[...], w2_ref[...], a0_ref[...], a2_ref[...], g2_ref[...],
           kk_ref[...], ka_ref[...], rk_ref[...], lng_ref[...], lnb_ref[...], None)
    mseg = mseg_ref[...]
    seg_sum = lambda t: _split_dot(t, mseg, 2)
    pc = pc_ref[...].reshape(rows, C_SHIFT)
    rowi = lax.broadcasted_iota(jnp.int32, (rows, C_SHIFT), 0)
    prev = pltpu.roll(pc, 1, axis=0)
    for b in range(nseq):
        prev = jnp.where(rowi == b * ch, prev_scr[b:b + 1, :], prev)
        prev_scr[b:b + 1, :] = pc[(b + 1) * ch - 1:(b + 1) * ch, :]
    r, k2, v, kk, a, g, logw = _rwkv_pre(pc, prev, prm, seg_sum)

    cs = _split_dot_left(ltri_ref[...], logw, 3)
    g_in = jnp.exp(cs)
    g_ex = jnp.exp(cs - logw)
    g_inv = jnp.exp(-cs)
    al = (-kk * g_ex).astype(BF16)
    rt = (r * g_in).astype(BF16)
    bh = kk * a * g_inv
    kh = k2 * g_inv
    vb = v.astype(BF16)

    si = lax.broadcasted_iota(jnp.int32, (2 * ch, 2 * ch), 0) % ch
    ti = lax.broadcasted_iota(jnp.int32, (2 * ch, 2 * ch), 1)
    keep = si + jnp.where(ti < ch, 0, ch - 1) < ti
    n_sq = int(math.log2(ch))
    units = [(b, h) for b in range(nseq) for h in range(H_C)]
    idx = range(len(units))
    cut = lambda t, u: t[units[u][0] * ch:(units[u][0] + 1) * ch,
                         units[u][1] * HEAD_DIM:(units[u][1] + 1) * HEAD_DIM]
    g_end = [cut(g_in, u)[ch - 1:ch, :] for u in idx]
    al_u = [cut(al, u) for u in idx]
    rt_u = [cut(rt, u) for u in idx]
    v_u = [cut(vb, u) for u in idx]
    bh_u = [cut(bh, u) for u in idx]
    kh_u = [cut(kh, u) for u in idx]
    bc = [bh_u[u] * g_end[u] for u in idx]
    kc = [kh_u[u] * g_end[u] for u in idx]
    s0 = [s_scr[b, h] for (b, h) in units]
    s0b = [t.astype(BF16) for t in s0]
    bk = [jnp.concatenate([bh_u[u], kh_u[u]], axis=0).astype(BF16) for u in idx]
    ar = [jnp.concatenate([al_u[u], rt_u[u]], axis=0) for u in idx]
    gm = [jnp.where(keep, _dot_nt(bk[u], ar[u]), 0.0) for u in idx]
    ps = [_dot_nt(s0b[u], ar[u]) for u in idx]
    a_t = [gm[u][0:ch, 0:ch] for u in idx]
    vv = [_dot_tn(v_u[u], jnp.concatenate([gm[u][ch:2 * ch, :], kc[u]], axis=1).astype(BF16))
          for u in idx]
    x_t = [ps[u][:, 0:ch] + vv[u][:, 0:ch] for u in idx]
    for it in range(n_sq):
        a_b = [a_t[u].astype(BF16) for u in idx]
        if it + 1 < n_sq:
            both = [_dot(jnp.concatenate([x_t[u], a_t[u]], axis=0).astype(BF16), a_b[u]) for u in idx]
            x_t = [x_t[u] + both[u][0:HEAD_DIM] for u in idx]
            a_t = [both[u][HEAD_DIM:HEAD_DIM + ch] for u in idx]
        else:
            step = [_dot(x_t[u].astype(BF16), a_b[u]) for u in idx]
            x_t = [x_t[u] + step[u] for u in idx]
    x_b = [t.astype(BF16) for t in x_t]
    xx = [_dot(x_b[u], jnp.concatenate([gm[u][0:ch, :], bc[u]], axis=1).astype(BF16))
          for u in idx]
    for u, (b, h) in enumerate(units):
        s_new = s0[u] * g_end[u] + xx[u][:, 2 * ch:] + vv[u][:, 2 * ch:]
        y_sum = ps[u] + xx[u][:, 0:2 * ch] + vv[u][:, 0:2 * ch]
        yt_scr[h * HEAD_DIM:(h + 1) * HEAD_DIM, b * ch:(b + 1) * ch] = y_sum[:, ch:2 * ch]
        s_scr[b, h] = s_new
        s_ref[b, h] = s_new

    y = jnp.transpose(yt_scr[...])[0:rows, :]
    o_ref[...] = _rwkv_post(y, r, k2, v, g, prm, seg_sum).reshape(nseq, ch, W_C)


def _rwkv_params(p):
    pad = lambda w, lo: jnp.zeros((R_W + R_A + R_G, W_C), BF16).at[lo:lo + w.shape[0]].set(w.astype(BF16))
    seg = np.arange(W_C) // HEAD_DIM
    mseg = jnp.asarray((seg[:, None] == seg[None, :]).astype(np.float32))
    row = lambda t: t.reshape(1, -1)
    return (row(p["rwkv_mu"]), row(p["rwkv_w0"]), pad(p["rwkv_w2"], 0), row(p["rwkv_a0"]),
            pad(p["rwkv_a2"], R_W), pad(p["rwkv_g2"], R_W + R_A), row(p["rwkv_k_k"]),
            row(p["rwkv_k_a"]), row(p["rwkv_r_k"]), row(p["rwkv_ln_g"]), row(p["rwkv_ln_b"]), mseg)


def _rwkv_prompt(pc, prm):
    nb, t, _ = pc.shape
    ch = min(RWKV_CHUNK, t)
    nseq = max(1, min(RWKV_SEQS * LANES // ch, nb))
    assert nb % nseq == 0
    ltri = np.kron(np.eye(nseq, dtype=np.float32), np.tril(np.ones((ch, ch), np.float32)))
    ltri = jnp.asarray(ltri).astype(BF16)
    kern = functools.partial(_rwkv_kernel, ch=ch, nseq=nseq)
    const = lambda a: pl.BlockSpec(a.shape, lambda b, c: (0,) * a.ndim)
    args = list(prm[:-1]) + [prm[-1].astype(BF16), ltri]
    return pl.pallas_call(
        kern,
        out_shape=[jax.ShapeDtypeStruct((nb, t, W_C), F32),
                   jax.ShapeDtypeStruct((nb, H_C, HEAD_DIM, HEAD_DIM), F32)],
        grid=(nb // nseq, t // ch),
        in_specs=[pl.BlockSpec((nseq, ch, C_SHIFT), lambda b, c: (b, c, 0))] + [const(a) for a in args],
        out_specs=[pl.BlockSpec((nseq, ch, W_C), lambda b, c: (b, c, 0)),
                   pl.BlockSpec((nseq, H_C, HEAD_DIM, HEAD_DIM), lambda b, c: (b, 0, 0, 0))],
        scratch_shapes=[pltpu.VMEM((nseq, H_C, HEAD_DIM, HEAD_DIM), F32),
                        pltpu.VMEM((nseq, C_SHIFT), F32),
                        pltpu.VMEM((W_C, -(-nseq * ch // LANES) * LANES), F32)],
        compiler_params=_cp(("parallel", "arbitrary")),
        name="rwkv_prompt",
    )(pc, *args)


def _route(x1, rw, rb):
    logits = _dot(x1.astype(BF16), rw) + rb
    lane = lax.broadcasted_iota(jnp.int32, logits.shape, 1).astype(F32)
    vals, idxs = [], []
    cur = logits
    for _ in range(TOP_K):
        m = jnp.max(cur, axis=-1, keepdims=True)
        idx = jnp.min(jnp.where(cur == m, lane, float(N_EXPERTS)), axis=-1, keepdims=True)
        vals.append(m)
        idxs.append(idx)
        cur = jnp.where(lane == idx, -jnp.inf, cur)
    ex = [jnp.exp(v - vals[0]) for v in vals]
    tot = ex[0] + ex[1] + ex[2] + ex[3]
    gates = jnp.concatenate([e / tot for e in ex], axis=-1)
    return jnp.concatenate(idxs, axis=-1).astype(jnp.int32), gates


def _outproj_kernel(x_ref, oa_ref, ob_ref, oc_ref, w_ref, g_ref, b_ref, rw_ref, rb_ref, *refs):
    x1_ref, idx_ref, gate_ref = refs[-3:]
    mix = (_dot(oa_ref[...].astype(BF16), w_ref[0:W_A, :])
           + _dot(ob_ref[...].astype(BF16), w_ref[W_A:W_A + W_B, :])
           + _dot(oc_ref[...].astype(BF16), w_ref[W_A + W_B:, :]))
    x1 = _layernorm_rows(ALPHA * x_ref[...] + mix, g_ref[...], b_ref[...])
    x1_ref[...] = x1
    idx, gates = _route(x1, rw_ref[...], rb_ref[...])
    idx_ref[...] = idx
    gate_ref[...] = gates


def _outproj(x, oa, ob, oc, w, g, b, rw, rb, bufs, row0):
    n = oa.shape[0]
    tm = min(ROW_TILE, n)
    assert row0 % tm == 0
    rows = lambda w_: pl.BlockSpec((tm, w_), lambda i: (i, 0))
    shifted = lambda w_: pl.BlockSpec((tm, w_), lambda i: (i + row0 // tm, 0))
    const = lambda a: pl.BlockSpec(a.shape, lambda i: (0,) * a.ndim)
    consts = [w.astype(BF16), g.reshape(1, -1), b.reshape(1, -1), rw.astype(BF16), rb.reshape(1, -1)]
    n_in = 4 + len(consts)
    return pl.pallas_call(
        _outproj_kernel,
        out_shape=[jax.ShapeDtypeStruct(a.shape, a.dtype) for a in bufs],
        grid=(n // tm,),
        in_specs=[rows(D_MODEL), rows(W_A), rows(W_B), rows(W_C)] + [const(a) for a in consts]
        + [pl.BlockSpec(memory_space=pl.ANY)] * len(bufs),
        out_specs=[shifted(D_MODEL), shifted(TOP_K), shifted(TOP_K)],
        input_output_aliases={n_in + j: j for j in range(len(bufs))},
        compiler_params=_cp(("parallel",)),
        name="outproj",
    )(x, oa, ob, oc, *consts, *bufs)


def _dest_kernel(idx_ref, ltri_ref, utri_ref, dest_ref, cnt_ref, cnt_scr, run_scr, start_scr):
    ph = pl.program_id(0)
    i = pl.program_id(1)
    idx = idx_ref[...]
    lane = lax.broadcasted_iota(jnp.int32, (idx.shape[0], N_EXPERTS), 1)
    hot = [idx[:, k:k + 1] == lane for k in range(TOP_K)]
    multi = jnp.zeros(lane.shape, F32)
    for k in range(TOP_K):
        multi = multi + jnp.where(hot[k], 1.0, 0.0)
    tile_cnt = jnp.sum(multi, axis=0, keepdims=True)

    @pl.when(jnp.logical_and(ph == 0, i == 0))
    def _():
        cnt_scr[...] = jnp.zeros_like(cnt_scr)

    @pl.when(ph == 0)
    def _():
        cnt_scr[...] += tile_cnt

    @pl.when(jnp.logical_and(ph == 1, i == 0))
    def _():
        cnt = cnt_scr[...]
        padded = jnp.floor((cnt + (MOE_ROWS - 1)) * (1.0 / MOE_ROWS)) * MOE_ROWS
        start_scr[...] = _dot(jnp.broadcast_to(padded, (8, N_EXPERTS)), utri_ref[...], HI)[0:1, :]
        run_scr[...] = jnp.zeros_like(run_scr)
        cnt_ref[...] = cnt

    @pl.when(ph == 1)
    def _():
        before = _dot(ltri_ref[...], multi.astype(BF16))
        base = start_scr[...] + run_scr[...] + before
        cols = [jnp.sum(jnp.where(hot[k], base, 0.0), axis=1, keepdims=True) for k in range(TOP_K)]
        dest_ref[...] = jnp.concatenate(cols, axis=1).astype(jnp.int32)
        run_scr[...] += tile_cnt


def _route_dest(top_idx):
    n = top_idx.shape[0]
    tile = max(c for c in range(COMBINE_TOK, ROW_TILE + 1, COMBINE_TOK) if n % c == 0)
    nt = n // tile
    ltri = jnp.asarray(np.tril(np.ones((tile, tile), np.float32), -1)).astype(BF16)
    utri = jnp.asarray(np.triu(np.ones((N_EXPERTS, N_EXPERTS), np.float32), 1))
    return pl.pallas_call(
        _dest_kernel,
        out_shape=[jax.ShapeDtypeStruct((n, TOP_K), jnp.int32),
                   jax.ShapeDtypeStruct((1, N_EXPERTS), F32)],
        grid=(2, nt),
        in_specs=[pl.BlockSpec((tile, TOP_K), lambda ph, i: (i, 0)),
                  pl.BlockSpec((tile, tile), lambda ph, i: (0, 0)),
                  pl.BlockSpec((N_EXPERTS, N_EXPERTS), lambda ph, i: (0, 0))],
        out_specs=[pl.BlockSpec((tile, TOP_K), lambda ph, i: (i * ph, 0)),
                   pl.BlockSpec((1, N_EXPERTS), lambda ph, i: (0, 0))],
        scratch_shapes=[pltpu.VMEM((1, N_EXPERTS), F32)] * 3,
        compiler_params=_cp(("arbitrary", "arbitrary")),
        name="moe_dest",
    )(top_idx, ltri, utri)


def _block_experts(counts, n_blk):
    cnt = counts.reshape(N_EXPERTS).astype(jnp.int32)
    pends = jnp.cumsum((cnt + MOE_ROWS - 1) // MOE_ROWS * MOE_ROWS)
    starts = jnp.arange(n_blk, dtype=jnp.int32)[:, None] * MOE_ROWS
    blk_expert = jnp.minimum(jnp.sum((pends[None, :] <= starts).astype(jnp.int32), axis=1), N_EXPERTS - 1)
    last_blk = jnp.maximum(pends // MOE_ROWS - 1, 0).astype(jnp.int32)
    return blk_expert, (pends[-1] // MOE_ROWS).astype(jnp.int32).reshape(1), last_blk


def _dispatch_kernel(last_ref, na_ref, dest_ref, x1_ref, rows_hbm, zbuf, xbuf, sem, zsem, *, tm, n_blk):
    @pl.when(pl.program_id(0) == 0)
    def _():
        zbuf[...] = jnp.zeros_like(zbuf)
        n_tail = n_blk - na_ref[0]

        def clear(blk):
            return pltpu.make_async_copy(zbuf, rows_hbm.at[pl.ds(blk * MOE_ROWS, MOE_ROWS), :], zsem)

        def start(j, carry):
            clear(jnp.where(j < N_EXPERTS, last_ref[jnp.minimum(j, N_EXPERTS - 1)],
                            na_ref[0] + j - N_EXPERTS)).start()
            return carry

        def wait(j, carry):
            clear(0).wait()
            return carry

        lax.fori_loop(0, N_EXPERTS + n_tail, start, 0)
        lax.fori_loop(0, N_EXPERTS + n_tail, wait, 0)

    i = pl.program_id(0)

    def drain(s):
        for k in range(TOP_K):
            pltpu.make_async_copy(xbuf.at[s], rows_hbm.at[pl.ds(0, tm), :], sem.at[s]).wait()

    for s in range(2):
        @pl.when(i % 2 == s)
        def _(s=s):
            @pl.when(i >= 2)
            def _():
                drain(s)

            xbuf[s] = x1_ref[...]
            for r in range(tm):
                for k in range(TOP_K):
                    pltpu.make_async_copy(xbuf.at[s, pl.ds(r, 1), :],
                                          rows_hbm.at[pl.ds(dest_ref[0, 0, r * TOP_K + k], 1), :],
                                          sem.at[s]).start()

            @pl.when(i == pl.num_programs(0) - 1)
            def _():
                drain(s)

                @pl.when(i >= 1)
                def _():
                    drain(1 - s)


def _dispatch(x1, dest, last_blk, n_active, n_blk):
    n = x1.shape[0]
    tm = COMBINE_TOK
    nt = n // tm
    kern = functools.partial(_dispatch_kernel, tm=tm, n_blk=n_blk)
    grid_spec = pltpu.PrefetchScalarGridSpec(
        num_scalar_prefetch=2,
        grid=(nt,),
        in_specs=[pl.BlockSpec((1, 1, TOP_K * tm), lambda i, lb, na: (i, 0, 0), memory_space=pltpu.SMEM),
                  pl.BlockSpec((tm, D_MODEL), lambda i, lb, na: (i, 0))],
        out_specs=pl.BlockSpec(memory_space=pl.ANY),
        scratch_shapes=[pltpu.VMEM((MOE_ROWS, D_MODEL), F32), pltpu.VMEM((2, tm, D_MODEL), F32),
                        pltpu.SemaphoreType.DMA((2,)), pltpu.SemaphoreType.DMA],
    )
    return pl.pallas_call(
        kern,
        out_shape=jax.ShapeDtypeStruct((n_blk * MOE_ROWS, D_MODEL), F32),
        grid_spec=grid_spec,
        compiler_params=_cp(("arbitrary",)),
        name="moe_dispatch",
    )(last_blk, n_active, dest.reshape(nt, 1, TOP_K * tm), x1)


def _swiglu(h):
    h_glu = jnp.minimum(h[:, :D_FF], SWIGLU_LIMIT)
    h_lin = jnp.clip(h[:, D_FF:], -SWIGLU_LIMIT, SWIGLU_LIMIT)
    return h_glu * _sigmoid(SWIGLU_ALPHA * h_glu) * (h_lin + 1.0)


def _moe_kernel(be_ref, na_ref, par_ref, nxt_ref, x_ref, wup_hbm, bup_ref, wdn_hbm, bdn_ref, y_ref,
                wup_f, wdn_f, wup_b, wdn_b, sem, *, layer):
    i = pl.program_id(0)
    active = i < na_ref[0]

    def fetch(expert, s):
        return (pltpu.make_async_copy(wup_hbm.at[layer, expert], wup_f.at[s], sem.at[s]),
                pltpu.make_async_copy(wdn_hbm.at[layer, expert], wdn_f.at[s], sem.at[s]))

    @pl.when(active)
    def _():
        changed = jnp.logical_or(i == 0, be_ref[i] != be_ref[jnp.maximum(i - 1, 0)])

        @pl.when(i == 0)
        def _():
            for c in fetch(be_ref[0], 0):
                c.start()

        for s in range(2):
            @pl.when(jnp.logical_and(changed, par_ref[i] == s))
            def _(s=s):
                for c in fetch(0, s):
                    c.wait()
                wup_b[...] = wup_f[s].astype(BF16)
                wdn_b[...] = wdn_f[s].astype(BF16)

                @pl.when(nxt_ref[i] >= 0)
                def _():
                    for c in fetch(nxt_ref[i], 1 - s):
                        c.start()

        h = _dot(x_ref[...].astype(BF16), wup_b[...]) + bup_ref[...]
        y_ref[...] = _dot(_swiglu(h).astype(BF16), wdn_b[...]) + bdn_ref[...]

    @pl.when(jnp.logical_not(active))
    def _():
        y_ref[...] = jnp.zeros_like(y_ref)


def _run_schedule(blk_expert, n_active):
    n_blk = blk_expert.shape[0]
    pos = jnp.arange(n_blk, dtype=jnp.int32)
    first = jnp.concatenate([jnp.ones((1,), bool), blk_expert[1:] != blk_expert[:-1]]) & (pos < n_active[0])
    parity = ((jnp.cumsum(first.astype(jnp.int32)) - 1) % 2).astype(jnp.int32)
    later = jnp.where(first, pos, n_blk)
    nxt_pos = lax.cummin(jnp.concatenate([later[1:], jnp.full((1,), n_blk, jnp.int32)]), axis=0, reverse=True)
    nxt = jnp.where(nxt_pos < n_blk, blk_expert[jnp.minimum(nxt_pos, n_blk - 1)], -1).astype(jnp.int32)
    return parity, nxt


def _moe_rows(rows, blk_expert, n_active, w_up, b_up, w_dn, b_dn, layer):
    n_blk = blk_expert.shape[0]
    parity, nxt = _run_schedule(blk_expert, n_active)
    grid_spec = pltpu.PrefetchScalarGridSpec(
        num_scalar_prefetch=4,
        grid=(n_blk,),
        in_specs=[
            pl.BlockSpec((MOE_ROWS, D_MODEL), lambda i, be, na, *_: (jnp.minimum(i, jnp.maximum(na[0] - 1, 0)), 0)),
            pl.BlockSpec(memory_space=pl.ANY),
            pl.BlockSpec((None, None, 1, 2 * D_FF), lambda i, be, na, *_: (layer, be[i], 0, 0)),
            pl.BlockSpec(memory_space=pl.ANY),
            pl.BlockSpec((None, None, 1, D_MODEL), lambda i, be, na, *_: (layer, be[i], 0, 0)),
        ],
        out_specs=pl.BlockSpec((MOE_ROWS, D_MODEL), lambda i, be, na, *_: (i, 0)),
        scratch_shapes=[pltpu.VMEM((2, D_MODEL, 2 * D_FF), F32),
                        pltpu.VMEM((2, D_FF, D_MODEL), F32),
                        pltpu.VMEM((D_MODEL, 2 * D_FF), BF16),
                        pltpu.VMEM((D_FF, D_MODEL), BF16),
                        pltpu.SemaphoreType.DMA((2,))],
    )
    return pl.pallas_call(
        functools.partial(_moe_kernel, layer=layer),
        out_shape=jax.ShapeDtypeStruct((n_blk * MOE_ROWS, D_MODEL), F32),
        grid_spec=grid_spec,
        compiler_params=_cp(("arbitrary",)),
        name="moe_rows",
    )(blk_expert, n_active, parity, nxt, rows,
      w_up, b_up.reshape(DEPTH, N_EXPERTS, 1, 2 * D_FF), w_dn, b_dn.reshape(DEPTH, N_EXPERTS, 1, D_MODEL))


def _combine_kernel(dest_ref, destn_ref, gate_ref, x1_ref, y_hbm, g_ref, b_ref, o_ref, tail_ref, ybuf, sem,
                    *, tm, head_tiles):
    i = pl.program_id(0)
    slot = i % 2
    n_rows = TOP_K * tm

    def gather_tile(dests, s):
        for r in range(n_rows):
            pltpu.make_async_copy(y_hbm.at[pl.ds(dests[0, 0, r], 1), :],
                                  ybuf.at[s, pl.ds(r, 1), :], sem.at[s]).start()

    @pl.when(i == 0)
    def _():
        gather_tile(dest_ref, 0)

    for s in range(2):
        @pl.when(jnp.logical_and(i + 1 < pl.num_programs(0), slot == 1 - s))
        def _(s=s):
            gather_tile(destn_ref, s)

    for s in range(2):
        @pl.when(slot == s)
        def _(s=s):
            pltpu.make_async_copy(y_hbm.at[pl.ds(0, n_rows), :], ybuf.at[s], sem.at[s]).wait()
            gates = gate_ref[...]
            moe = jnp.zeros((tm, D_MODEL), F32)
            for k in range(TOP_K):
                moe = moe + gates[:, k:k + 1] * ybuf[s, k * tm:(k + 1) * tm, :]
            res = _layernorm_rows(ALPHA * x1_ref[...] + moe, g_ref[...], b_ref[...])

            @pl.when(i < head_tiles)
            def _():
                o_ref[...] = res

            @pl.when(i >= head_tiles)
            def _():
                tail_ref[...] = res


def _combine(dest, gates, x1, y_rows, g, b, n_head):
    n = x1.shape[0]
    tm = COMBINE_TOK
    nt = n // tm
    head_tiles = n_head // tm
    dest_t = dest.reshape(nt, tm, TOP_K).transpose(0, 2, 1).reshape(nt, 1, TOP_K * tm)
    kern = functools.partial(_combine_kernel, tm=tm, head_tiles=head_tiles)
    return pl.pallas_call(
        kern,
        out_shape=[jax.ShapeDtypeStruct((n_head, D_MODEL), F32),
                   jax.ShapeDtypeStruct((n - n_head, D_MODEL), F32)],
        grid=(nt,),
        in_specs=[pl.BlockSpec((1, 1, TOP_K * tm), lambda i: (i, 0, 0), memory_space=pltpu.SMEM),
                  pl.BlockSpec((1, 1, TOP_K * tm), lambda i: (jnp.minimum(i + 1, nt - 1), 0, 0),
                               memory_space=pltpu.SMEM),
                  pl.BlockSpec((tm, TOP_K), lambda i: (i, 0)),
                  pl.BlockSpec((tm, D_MODEL), lambda i: (i, 0)),
                  pl.BlockSpec(memory_space=pl.ANY),
                  pl.BlockSpec((1, D_MODEL), lambda i: (0, 0)),
                  pl.BlockSpec((1, D_MODEL), lambda i: (0, 0))],
        out_specs=[pl.BlockSpec((tm, D_MODEL), lambda i: (jnp.minimum(i, head_tiles - 1), 0)),
                   pl.BlockSpec((tm, D_MODEL), lambda i: (jnp.maximum(i - head_tiles, 0), 0))],
        scratch_shapes=[pltpu.VMEM((2, TOP_K * tm, D_MODEL), F32), pltpu.SemaphoreType.DMA((2,))],
        compiler_params=_cp(("arbitrary",)),
        name="moe_combine",
    )(dest_t, dest_t, gates, x1, y_rows, g.reshape(1, -1), b.reshape(1, -1))


def _moe(x1, top_idx, gates, p, layer, n_head):
    n = x1.shape[0]
    n_blk = -(-n * TOP_K // MOE_ROWS) + N_EXPERTS
    dest, counts = _route_dest(top_idx)
    blk_expert, n_active, last_blk = _block_experts(counts, n_blk)
    rows = _dispatch(x1, dest, last_blk, n_active, n_blk)
    y_rows = _moe_rows(rows, blk_expert, n_active, p["moe_w_up"], p["moe_b_up"],
                       p["moe_w_down"], p["moe_b_down"], layer)
    return _combine(dest, gates, x1, y_rows, p["ln2_g"][layer], p["ln2_b"][layer], n_head)


def _sample_proj_kernel(x_ref, w_ref, h0_ref, lam_ref, bblk_ref, cblk_ref, d_ref, gw_ref, gb_ref,
                        q_ref, k_ref, v_ref, pc_ref, oa_ref, h_ref):
    xb = x_ref[...].astype(BF16)
    mm = lambda lo, hi: _dot(xb, w_ref[:, lo:hi])
    u = mm(_PROJ_SPLITS[0], _PROJ_SPLITS[1])
    q_ref[...] = mm(_PROJ_SPLITS[1], _PROJ_SPLITS[2])
    k_ref[...] = mm(_PROJ_SPLITS[2], _PROJ_SPLITS[3])
    v_ref[...] = mm(_PROJ_SPLITS[3], _PROJ_SPLITS[4])
    pc_ref[...] = mm(_PROJ_SPLITS[4], _PROJ_SPLITS[5])
    bu = _dot(u.astype(BF16), bblk_ref[...])
    bu_re = bu[:, 0:N_ST]
    bu_im = bu_re + bu[:, N_ST:2 * N_ST]
    lr, li = lam_ref[0:1, :], lam_ref[1:2, :]
    hr0, hi0 = h0_ref[0], h0_ref[1]
    hr = bu_re + (lr * hr0 - li * hi0)
    hi = bu_im + (lr * hi0 + li * hr0)
    h_ref[0] = hr
    h_ref[1] = hi
    y = (_dot((hr + hi).astype(BF16), cblk_ref[0:N_ST, :])
         - _dot(hi.astype(BF16), cblk_ref[N_ST:2 * N_ST, :])) + d_ref[...] * u
    z = _gelu(y)
    oa_ref[...] = z * _sigmoid(_dot(z.astype(BF16), gw_ref[...]) + gb_ref[...])


def _sample_proj(x, w_in, h0, lam2, bblk, cblk, d, gw, gb):
    n = x.shape[0]
    widths = (W_B, W_B, W_B, C_SHIFT, W_A)
    return pl.pallas_call(
        _sample_proj_kernel,
        out_shape=[jax.ShapeDtypeStruct((n, w), F32) for w in widths]
        + [jax.ShapeDtypeStruct((2, n, N_ST), F32)],
        compiler_params=_cp(None),
        name="sample_proj",
    )(x, w_in, h0, lam2, bblk, cblk, d.reshape(1, -1), gw.astype(BF16), gb.reshape(1, -1))


def _sample_mix_kernel(q_ref, kn_ref, vn_ref, pc_ref, sh_ref, kc_ref, vc_ref, s0_ref, lb_ref, lb0_ref,
                       mu_ref, w0_ref, w2_ref, a0_ref, a2_ref, g2_ref, kk_ref, ka_ref,
                       rk_ref, lng_ref, lnb_ref, mseg_ref,
                       ob_ref, oc_ref, s_ref):
    rows = 8
    lane = lax.broadcasted_iota(jnp.int32, (rows, W_B), 1)
    sub = lax.broadcasted_iota(jnp.int32, (rows, W_B), 0)
    hmask = (lane // HEAD_DIM) == sub
    rnd = lambda t: t.astype(BF16).astype(F32)
    q = q_ref[0] * (HEAD_DIM ** -0.5)
    qrows = jnp.where(hmask, jnp.broadcast_to(q, (rows, W_B)), 0.0).astype(BF16)
    kcb = kc_ref[0].astype(BF16)
    vcb = vc_ref[0].astype(BF16)
    knb = rnd(kn_ref[0])
    vnb = rnd(vn_ref[0])
    s_all = _dot(qrows, kcb)
    s_new = jnp.sum(qrows.astype(F32) * knb, axis=-1, keepdims=True)
    probs, p_new, lses = [], [], []
    for br in range(len(BRANCHES)):
        s = s_all + lb_ref[br]
        s0 = s_new + lb0_ref[br]
        m = jnp.maximum(jnp.max(s, axis=-1, keepdims=True), s0)
        lse = m + jnp.log(jnp.sum(jnp.exp(s - m), axis=-1, keepdims=True) + jnp.exp(s0 - m))
        probs.append(jnp.exp(s - lse))
        p_new.append(rnd(jnp.exp(s0 - lse)))
        lses.append(lse)
    pv = _dot_nt(jnp.concatenate(probs, axis=0).astype(BF16), vcb)
    outs = [rnd(pv[br * rows:(br + 1) * rows] + p_new[br] * vnb) for br in range(len(BRANCHES))]
    top = jnp.maximum(jnp.maximum(lses[0], lses[1]), lses[2])
    ex = [jnp.exp(t - top) for t in lses]
    tot = ex[0] + ex[1] + ex[2]
    o = rnd(ex[0] / tot) * outs[0] + rnd(ex[1] / tot) * outs[1] + rnd(ex[2] / tot) * outs[2]
    ob_ref[0] = jnp.sum(jnp.where(hmask, o, 0.0), axis=0, keepdims=True)

    prm = (mu_ref[...], w0_ref[...], w2_ref[...], a0_ref[...], a2_ref[...], g2_ref[...],
           kk_ref[...], ka_ref[...], rk_ref[...], lng_ref[...], lnb_ref[...], mseg_ref[...])
    pc = jnp.broadcast_to(pc_ref[0], (rows, C_SHIFT))
    prev = jnp.broadcast_to(sh_ref[0], (rows, C_SHIFT))
    head_of_lane = lax.broadcasted_iota(jnp.int32, (rows, W_C), 1) // HEAD_DIM

    def seg_sum(t):
        out = jnp.zeros_like(t)
        for h in range(H_C):
            m = head_of_lane == h
            out = jnp.where(m, jnp.sum(jnp.where(m, t, 0.0), axis=1, keepdims=True), out)
        return out
    r, k2, v, kk, a, g, logw = _rwkv_pre(pc, prev, prm, seg_sum)
    w = jnp.exp(logw)
    eye = (lax.broadcasted_iota(jnp.int32, (HEAD_DIM, HEAD_DIM), 0)
           == lax.broadcasted_iota(jnp.int32, (HEAD_DIM, HEAD_DIM), 1))
    heads = range(H_C)
    row = lambda t, h: t[0:1, h * HEAD_DIM:(h + 1) * HEAD_DIM]
    col = lambda t, h: jnp.sum(jnp.where(eye, jnp.broadcast_to(row(t, h), (HEAD_DIM, HEAD_DIM)), 0.0),
                               axis=1, keepdims=True)
    s0 = [s0_ref[0, h] for h in heads]
    sa = [jnp.sum(rnd(s0[h]) * rnd(-row(kk, h)), axis=1, keepdims=True) for h in heads]
    v_col = [col(v, h) for h in heads]
    s1 = [s0[h] * row(w, h) + sa[h] * (row(kk, h) * row(a, h)) + v_col[h] * row(k2, h) for h in heads]
    y_col = [jnp.sum(rnd(s1[h]) * rnd(row(r, h)), axis=1, keepdims=True) for h in heads]
    ys = [jnp.sum(jnp.where(eye, jnp.broadcast_to(y_col[h], (HEAD_DIM, HEAD_DIM)), 0.0), axis=0, keepdims=True)
          for h in heads]
    for h in heads:
        s_ref[0, h] = s1[h]
    y = jnp.broadcast_to(jnp.concatenate(ys, axis=1), (rows, W_C))
    oc_ref[0] = _rwkv_post(y, r, k2, v, g, prm, seg_sum)[0:1, :]


def _sample_mix(q, kn, vn, pc, shift0, k_cache, v_cache, s0, lb_rows, lb0, prm, layer):
    n = q.shape[0]
    l_buf = k_cache.shape[3]
    per_b = lambda w: pl.BlockSpec((1, 1, w), lambda b: (b, 0, 0))
    const = lambda a: pl.BlockSpec(a.shape, lambda b: (0,) * a.ndim)
    r3 = lambda t: t.reshape(n, 1, t.shape[-1])
    cache = pl.BlockSpec((None, 1, W_B, l_buf), lambda b: (layer, b, 0, 0))
    return pl.pallas_call(
        _sample_mix_kernel,
        out_shape=[jax.ShapeDtypeStruct((n, 1, W_B), F32),
                   jax.ShapeDtypeStruct((n, 1, W_C), F32),
                   jax.ShapeDtypeStruct((n, H_C, HEAD_DIM, HEAD_DIM), F32)],
        grid=(n,),
        in_specs=[per_b(W_B), per_b(W_B), per_b(W_B), per_b(C_SHIFT), per_b(C_SHIFT),
                  cache, cache,
                  pl.BlockSpec((1, H_C, HEAD_DIM, HEAD_DIM), lambda b: (b, 0, 0, 0)),
                  const(lb_rows), const(lb0)] + [const(a) for a in prm],
        out_specs=[per_b(W_B), per_b(W_C),
                   pl.BlockSpec((1, H_C, HEAD_DIM, HEAD_DIM), lambda b: (b, 0, 0, 0))],
        compiler_params=_cp(("parallel",)),
        name="sample_mix",
    )(r3(q), r3(kn), r3(vn), r3(pc), r3(shift0), k_cache, v_cache, s0, lb_rows, lb0, *prm)


def _prompt_mixers(x, nb, t, p, l, s5p, rwp, tiles, kv_prev):
    u, q, k_all, v_all, pc = _proj(x, p["w_in"][l].astype(BF16), nb, t, l, kv_prev)
    shp = lambda a: a.reshape(nb, t, a.shape[-1])
    o_a, h_t = _s5_prompt(shp(u), *s5p, p["ssm_d"][l], p["ssm_glu_w"][l], p["ssm_glu_b"][l])
    o_b = _attn_prompt(shp(q), k_all, v_all, tiles, l)
    o_c, s_t = _rwkv_prompt(shp(pc), rwp)
    flat = lambda a: a.reshape(nb * t, a.shape[-1])
    ssm = jnp.transpose(h_t.reshape(2, nb, G_A, N_A), (1, 2, 3, 0))
    return (flat(o_a), flat(o_b), flat(o_c)), (ssm, s_t, shp(pc)[:, t - 1]), (k_all, v_all)


def _sample_bias(rel_bias, l_buf):
    offs = _branch_offsets()
    bias = rel_bias[jnp.asarray(_t5_buckets(offs))]
    rows = []
    for br in range(len(BRANCHES)):
        m = np.arange(1, offs.shape[1])
        pos = l_buf - offs[br][m]
        ok = pos >= 0
        r = jnp.full((H_B, l_buf), NEG_INF, F32).at[:, jnp.asarray(pos[ok])].set(bias[br][jnp.asarray(m[ok])].T)
        rows.append(jnp.concatenate([r, jnp.zeros((8 - H_B, l_buf), F32)], axis=0))
    new = jnp.zeros((len(BRANCHES), 8, 1), F32).at[:, :H_B, 0].set(bias[:, 0])
    return jnp.stack(rows), new


def _sample_mixers(x, p, l, s5p, rwp, sbias, k_cache, v_cache, st_ssm, st_rwkv, st_shift):
    n = x.shape[0]
    lam2, bblk, cblk = s5p
    h0 = jnp.transpose(st_ssm.reshape(n, N_ST, 2), (2, 0, 1))
    q, kn, vn, pc, o_a, h1 = _sample_proj(x, p["w_in"][l].astype(BF16), h0, lam2, bblk, cblk, p["ssm_d"][l],
                                          p["ssm_glu_w"][l], p["ssm_glu_b"][l])
    lb_rows, lb0 = sbias
    o_b, o_c, s1 = _sample_mix(q, kn, vn, pc, st_shift, k_cache, v_cache, st_rwkv, lb_rows, lb0, rwp, l)
    ssm = jnp.transpose(h1.reshape(2, n, G_A, N_A), (1, 2, 3, 0))
    return (o_a, o_b.reshape(n, W_B), o_c.reshape(n, W_C)), (kn, vn, ssm, s1, pc)


def _shift_kernel(c_ref, new_ref, o_ref):
    x = c_ref[...]
    length = x.shape[1]
    lane = lax.broadcasted_iota(jnp.int32, x.shape, 1)
    o_ref[...] = jnp.where(lane == length - 1, new_ref[...], pltpu.roll(x, length - 1, axis=1))


def _shift_cache(cache_t, new):
    rows, length = cache_t.shape
    blk = min(ROW_TILE, rows)
    return pl.pallas_call(
        _shift_kernel,
        out_shape=jax.ShapeDtypeStruct((rows, length), F32),
        grid=(rows // blk,),
        in_specs=[pl.BlockSpec((blk, length), lambda i: (i, 0)),
                  pl.BlockSpec((blk, 1), lambda i: (i, 0))],
        out_specs=pl.BlockSpec((blk, length), lambda i: (i, 0)),
        compiler_params=_cp(("parallel",)),
        name="shift_cache",
    )(cache_t, new)


def kernel(x_prompt, x_sample, cache_attn_k, cache_attn_v, state_ssm, state_rwkv, state_shift, w_in, w_out, ln1_g, ln1_b, ln2_g, ln2_b, ssm_a_re, ssm_a_im, ssm_log_dt, ssm_b_re, ssm_b_im, ssm_c_re, ssm_c_im, ssm_d, ssm_glu_w, ssm_glu_b, rel_bias, rwkv_mu, rwkv_w0, rwkv_w2, rwkv_a0, rwkv_a2, rwkv_g2, rwkv_k_k, rwkv_k_a, rwkv_r_k, rwkv_ln_g, rwkv_ln_b, moe_router_w, moe_router_b, moe_w_up, moe_b_up, moe_w_down, moe_b_down):
    p = dict(w_in=w_in, w_out=w_out, ln1_g=ln1_g, ln1_b=ln1_b, ln2_g=ln2_g, ln2_b=ln2_b,
             ssm_d=ssm_d, ssm_glu_w=ssm_glu_w, ssm_glu_b=ssm_glu_b,
             moe_router_w=moe_router_w, moe_router_b=moe_router_b, moe_w_up=moe_w_up,
             moe_b_up=moe_b_up, moe_w_down=moe_w_down, moe_b_down=moe_b_down)
    nb, t, _ = x_prompt.shape
    ns = x_sample.shape[0]
    l_buf = cache_attn_k.shape[2]
    table = _log_bias_table(rel_bias)
    tiles = _bias_tiles(table, min(t, WIN_MAX) // Q_TILE + 1)
    sbias = _sample_bias(rel_bias, l_buf)
    n_p = nb * t
    n_all = n_p + -(-ns // COMBINE_TOK) * COMBINE_TOK
    xp = x_prompt.reshape(n_p, D_MODEL)
    xs = x_sample.reshape(ns, D_MODEL)
    st_p, st_s = [], []
    time_minor = lambda c: jnp.transpose(c, (0, 1, 3, 4, 2)).reshape(DEPTH, c.shape[1], W_B, c.shape[2])
    time_major = lambda c, n: jnp.transpose(c.reshape(DEPTH, n, H_B, HEAD_DIM, c.shape[-1]), (0, 1, 4, 2, 3))
    kc_t, vc_t = time_minor(cache_attn_k), time_minor(cache_attn_v)
    kv = (jnp.zeros((DEPTH, nb, W_B, t), F32), jnp.zeros((DEPTH, nb, W_B, t), F32))
    for l in range(DEPTH):
        s5p, s5s = _s5_params(ssm_a_re[l], ssm_a_im[l], ssm_log_dt[l], ssm_b_re[l], ssm_b_im[l],
                              ssm_c_re[l], ssm_c_im[l])
        rwp = _rwkv_params(dict(rwkv_mu=rwkv_mu[l], rwkv_w0=rwkv_w0[l], rwkv_w2=rwkv_w2[l],
                                rwkv_a0=rwkv_a0[l], rwkv_a2=rwkv_a2[l], rwkv_g2=rwkv_g2[l],
                                rwkv_k_k=rwkv_k_k[l], rwkv_k_a=rwkv_k_a[l], rwkv_r_k=rwkv_r_k[l],
                                rwkv_ln_g=rwkv_ln_g[l], rwkv_ln_b=rwkv_ln_b[l]))
        mix_p, sp, kv = _prompt_mixers(xp, nb, t, p, l, s5p, rwp, tiles, kv)
        mix_s, ss = _sample_mixers(xs, p, l, s5s, rwp, sbias, kc_t, vc_t,
                                   state_ssm[l], state_rwkv[l], state_shift[l])
        st_p.append(sp)
        st_s.append(ss)
        bufs = (jnp.zeros((n_all, D_MODEL), F32),
                jnp.broadcast_to(jnp.arange(TOP_K, dtype=jnp.int32), (n_all, TOP_K)),
                jnp.zeros((n_all, TOP_K), F32))
        head = (p["w_out"][l], p["ln1_g"][l], p["ln1_b"][l], p["moe_router_w"][l], p["moe_router_b"][l])
        bufs = _outproj(xp, *mix_p, *head, bufs, 0)
        bufs = _outproj(xs, *mix_s, *head, bufs, n_p)
        xp, tail = _moe(*bufs, p, l, n_p)
        xs = tail[:ns]
    p_ssm, p_rwkv, p_shift = (jnp.stack(z) for z in zip(*st_p))
    kn, vn, s_ssm, s_rwkv, s_shift = (jnp.stack(z) for z in zip(*st_s))
    keep_p = min(WIN_MAX, t)
    p_k, p_v = (time_major(a[..., t - keep_p:], nb) for a in kv)
    keep = min(WIN_MAX, l_buf + 1)
    drop = l_buf + 1 - keep
    if drop == 1:
        advance = lambda c_t, new: _shift_cache(c_t.reshape(-1, l_buf), new.reshape(-1, 1)).reshape(c_t.shape)
    else:
        advance = lambda c_t, new: jnp.concatenate([c_t[..., drop:], new[..., None]], axis=-1)
    s_k = time_major(advance(kc_t, kn), ns)
    s_v = time_major(advance(vc_t, vn), ns)
    return (xp.reshape(nb, t, D_MODEL), xs.reshape(ns, 1, D_MODEL),
            p_k, p_v, p_ssm, p_rwkv, p_shift, s_k, s_v, s_ssm, s_rwkv, s_shift)
```

```python
import functools
import math

import numpy as np
import jax
import jax.numpy as jnp
from jax import lax
from jax.experimental import pallas as pl
from jax.experimental.pallas import tpu as pltpu

F32 = jnp.float32
BF16 = jnp.bfloat16
HI = lax.Precision.HIGHEST

D_MODEL = 1024
DEPTH = 2
HEAD_DIM = 64
W_A = 256
C_GRP = 16
G_A = 16
N_A = 64
N_ST = G_A * N_A
W_B = 384
H_B = 6
BRANCHES = ((128, 1), (512, 4), (2048, 16))
WIN_MAX = 2048
N_BUCKETS = 32
MAX_DIST = WIN_MAX
NEG_INF = -1e30
W_C = 384
H_C = 6
R_W = 32
R_A = 32
R_G = 64
C_SHIFT = 3 * W_C + R_W + R_A + R_G
N_IN = W_A + 3 * W_B + C_SHIFT
GN_EPS = 64e-5
N_EXPERTS = 32
TOP_K = 4
D_FF = D_MODEL
SWIGLU_ALPHA = 1.702
SWIGLU_LIMIT = 7.0
ALPHA = (2 * DEPTH) ** 0.25
LN_EPS = 1e-5

LANES = 128
Q_TILE = 128
RWKV_CHUNK = 64
RWKV_SEQS = 2
S5_CHUNK = 64
ROW_TILE = 512
MOE_ROWS = 256
COMBINE_TOK = 128
VMEM_LIMIT = 56 * 1024 * 1024


def _cp(sem, vmem=VMEM_LIMIT):
    return pltpu.CompilerParams(dimension_semantics=sem, vmem_limit_bytes=vmem)


def _dot(a, b, precision=None):
    return jnp.dot(a, b, preferred_element_type=F32, precision=precision)


def _dot_nt(a, b, precision=None):
    return lax.dot_general(a, b, (((1,), (1,)), ((), ())),
                           preferred_element_type=F32, precision=precision)


def _dot_tn(a, b, precision=None):
    return lax.dot_general(a, b, (((0,), (0,)), ((), ())),
                           preferred_element_type=F32, precision=precision)


def _layernorm_rows(x, g, b):
    mu = jnp.mean(x, axis=-1, keepdims=True)
    d = x - mu
    var = jnp.mean(d * d, axis=-1, keepdims=True)
    return d * lax.rsqrt(var + LN_EPS) * g + b


def _sigmoid(x):
    return 1.0 / (1.0 + jnp.exp(-x))


def _softplus(x):
    return jnp.maximum(x, 0.0) + jnp.log(1.0 + jnp.exp(-jnp.abs(x)))


def _gelu(x):
    c = math.sqrt(2.0 / math.pi)
    return 0.5 * x * (1.0 + jnp.tanh(c * (x + 0.044715 * (x * x * x))))


_PROJ_SPLITS = (0, W_A, W_A + W_B, W_A + 2 * W_B, W_A + 3 * W_B, N_IN)


def _proj_kernel(x_ref, w_ref, *refs):
    u_ref, q_ref, k_ref, v_ref, pc_ref = refs[-5:]
    xb = x_ref[...].astype(BF16)
    outs = ((u_ref, False), (q_ref, False), (k_ref, True), (v_ref, True), (pc_ref, False))
    for (o_ref, time_minor), lo, hi in zip(outs, _PROJ_SPLITS[:-1], _PROJ_SPLITS[1:]):
        y = _dot(xb, w_ref[:, lo:hi])
        o_ref[...] = jnp.transpose(y) if time_minor else y


def _proj(x, w_bf16, nb, t, layer, kv_prev):
    n = nb * t
    per_seq = t // ROW_TILE
    widths = [hi - lo for lo, hi in zip(_PROJ_SPLITS[:-1], _PROJ_SPLITS[1:])]
    rows = lambda w: pl.BlockSpec((ROW_TILE, w), lambda i: (i, 0))
    stacked = pl.BlockSpec((None, None, W_B, ROW_TILE), lambda i: (layer, i // per_seq, 0, i % per_seq))
    flat = lambda w: jax.ShapeDtypeStruct((n, w), F32)
    kv = jax.ShapeDtypeStruct((DEPTH, nb, W_B, t), F32)
    prev = list(kv_prev)
    return pl.pallas_call(
        _proj_kernel,
        out_shape=[flat(widths[0]), flat(widths[1]), kv, kv, flat(widths[4])],
        grid=(n // ROW_TILE,),
        in_specs=[pl.BlockSpec((ROW_TILE, D_MODEL), lambda i: (i, 0)),
                  pl.BlockSpec((D_MODEL, N_IN), lambda i: (0, 0))]
        + [pl.BlockSpec(memory_space=pl.ANY)] * len(prev),
        out_specs=[rows(widths[0]), rows(widths[1]), stacked, stacked, rows(widths[4])],
        input_output_aliases={2: 2, 3: 3},
        compiler_params=_cp(("parallel",)),
        name="proj",
    )(x, w_bf16, *prev)


def _s5_kernel(u_ref, lam_ref, bblk_ref, cblk_ref, d_ref, gw_ref, gb_ref,
               o_ref, ht_ref, utm, hs, ytm, h_scr, *, nb, lt):
    c = pl.program_id(0)

    @pl.when(c == 0)
    def _():
        h_scr[...] = jnp.zeros_like(h_scr)

    n_half = W_A // LANES
    for b in range(nb):
        for j in range(n_half):
            utm[j, pl.ds(b, lt, stride=nb), :] = u_ref[b, :, j * LANES:(j + 1) * LANES]
    u_all = jnp.concatenate([utm[j] for j in range(n_half)], axis=1)
    hs[...] = _dot(u_all.astype(BF16), bblk_ref[...])
    lr = jnp.broadcast_to(lam_ref[0:1, :], (nb, N_ST))
    li = jnp.broadcast_to(lam_ref[1:2, :], (nb, N_ST))

    def body(t, carry):
        hr, hi = carry
        row = pl.multiple_of(t * nb, nb)
        br = hs[pl.ds(row, nb), 0:N_ST]
        bi = hs[pl.ds(row, nb), N_ST:2 * N_ST]
        nr = lr * hr - li * hi + br
        ni = lr * hi + li * hr + bi
        hs[pl.ds(row, nb), 0:N_ST] = nr
        hs[pl.ds(row, nb), N_ST:2 * N_ST] = ni
        return nr, ni

    hr, hi = lax.fori_loop(0, lt, body, (h_scr[0], h_scr[1]), unroll=2)
    h_scr[0] = hr
    h_scr[1] = hi
    ht_ref[0] = hr
    ht_ref[1] = hi
    y = _dot(hs[...].astype(BF16), cblk_ref[...]) + d_ref[...] * u_all
    z = _gelu(y)
    gl = _dot(z.astype(BF16), gw_ref[...]) + gb_ref[...]
    res = z * _sigmoid(gl)
    for j in range(n_half):
        ytm[j] = res[:, j * LANES:(j + 1) * LANES]
    for b in range(nb):
        for j in range(n_half):
            o_ref[b, :, j * LANES:(j + 1) * LANES] = ytm[j, pl.ds(b, lt, stride=nb), :]


def _s5_params(a_re, a_im, log_dt, b_re, b_im, c_re, c_im):
    lam = lax.complex(a_re, a_im)
    dt = jnp.exp(log_dt)[:, None]
    lam_bar = jnp.exp(lam * dt)
    b_bar = ((lam_bar - 1.0) / lam)[..., None] * lax.complex(b_re, b_im)
    eye = jnp.eye(G_A, dtype=F32)

    def blk_in(m):
        return jnp.einsum("gnc,gh->gchn", m, eye).reshape(W_A, N_ST)

    def blk_out(m):
        return jnp.einsum("gcn,gh->gnhc", m, eye).reshape(N_ST, W_A)

    bblk = jnp.concatenate([blk_in(b_bar.real), blk_in(b_bar.imag)], axis=1)
    cblk = jnp.concatenate([blk_out(c_re), blk_out(-c_im)], axis=0)
    lam2 = jnp.stack([lam_bar.real.reshape(N_ST), lam_bar.imag.reshape(N_ST)])
    bblk3 = jnp.concatenate([blk_in(b_bar.real), blk_in(b_bar.imag - b_bar.real)], axis=1)
    cblk3 = jnp.concatenate([blk_out(c_re), blk_out(c_re + c_im)], axis=0)
    return (lam2, bblk, cblk), (lam2, bblk3.astype(BF16), cblk3.astype(BF16))


def _s5_prompt(u, lam2, bblk, cblk, d, gw, gb):
    nb, t, _ = u.shape
    lt = min(S5_CHUNK, t)
    kern = functools.partial(_s5_kernel, nb=nb, lt=lt)
    const = lambda shape: pl.BlockSpec(shape, lambda c: (0,) * len(shape))
    return pl.pallas_call(
        kern,
        out_shape=[jax.ShapeDtypeStruct((nb, t, W_A), F32),
                   jax.ShapeDtypeStruct((2, nb, N_ST), F32)],
        grid=(t // lt,),
        in_specs=[pl.BlockSpec((nb, lt, W_A), lambda c: (0, c, 0)),
                  const((2, N_ST)), const((W_A, 2 * N_ST)), const((2 * N_ST, W_A)),
                  const((1, W_A)), const((W_A, W_A)), const((1, W_A))],
        out_specs=[pl.BlockSpec((nb, lt, W_A), lambda c: (0, c, 0)),
                   const((2, nb, N_ST))],
        scratch_shapes=[pltpu.VMEM((W_A // LANES, nb * lt, LANES), F32),
                        pltpu.VMEM((nb * lt, 2 * N_ST), F32),
                        pltpu.VMEM((W_A // LANES, nb * lt, LANES), F32),
                        pltpu.VMEM((2, nb, N_ST), F32)],
        compiler_params=_cp(("arbitrary",)),
        name="s5_prompt",
    )(u, lam2, bblk.astype(BF16), cblk.astype(BF16), d.reshape(1, W_A),
      gw.astype(BF16), gb.reshape(1, W_A))


def _branch_offsets():
    return np.stack([np.arange(w // d + 1) * d for (w, d) in BRANCHES]).astype(np.int32)


def _t5_buckets(dist):
    max_exact = N_BUCKETS // 2
    d = np.maximum(dist, 1).astype(np.float32)
    large = max_exact + (np.log(d / max_exact) / np.log(MAX_DIST / max_exact)
                         * (N_BUCKETS - max_exact)).astype(np.int32)
    return np.where(dist < max_exact, dist, np.minimum(large, N_BUCKETS - 1)).astype(np.int32)


def _log_bias_table(rel_bias):
    offs = _branch_offsets()
    bias = rel_bias[jnp.asarray(_t5_buckets(offs))]
    table = jnp.full((WIN_MAX + 1, H_B), -jnp.inf, F32)
    for br in range(len(BRANCHES)):
        idx = jnp.asarray(offs[br])
        table = table.at[idx].set(jnp.logaddexp(table[idx], bias[br]))
    return jnp.maximum(table, NEG_INF)


def _bias_tiles(table, n_diff):
    per = 2 * Q_TILE
    m = np.arange(per)
    dist = np.arange(n_diff)[:, None] * Q_TILE + np.where(m < Q_TILE, -m, per - m)[None, :]
    ok = (dist >= 0) & (dist <= WIN_MAX) & (m != Q_TILE)[None, :]
    v = jnp.where(jnp.asarray(ok)[..., None], table[jnp.asarray(np.clip(dist, 0, WIN_MAX))], NEG_INF)
    v = jnp.transpose(v, (2, 0, 1))
    flat = jnp.broadcast_to(v[:, :, None, :], (H_B, n_diff, Q_TILE, per)).reshape(H_B, n_diff, Q_TILE * per)
    tiles = flat[:, :, :Q_TILE * (per - 1)].reshape(H_B, n_diff, Q_TILE, per - 1)[..., :Q_TILE]
    tiles = tiles.reshape(H_B // 2, 2, n_diff, Q_TILE, Q_TILE)
    return jnp.transpose(tiles, (0, 2, 1, 3, 4)).reshape(H_B // 2, n_diff, 2 * Q_TILE, Q_TILE)


def _attn_kernel(q_ref, k_ref, v_ref, bias_ref, o_ref, kb, vb, *, nq):
    kb[...] = k_ref[...].astype(BF16)
    vb[...] = v_ref[...].astype(BF16)
    lo = lax.broadcasted_iota(jnp.int32, (Q_TILE, LANES), 1) < HEAD_DIM
    for qi in range(nq):
        rows = slice(qi * Q_TILE, (qi + 1) * Q_TILE)
        q = q_ref[0, rows, :] * (HEAD_DIM ** -0.5)
        q2 = jnp.concatenate([jnp.where(lo, q, 0.0), jnp.where(lo, 0.0, q)], axis=0).astype(BF16)
        nk = (qi + 1) * Q_TILE
        bias = jnp.concatenate([bias_ref[qi - kj] for kj in range(qi + 1)], axis=1)
        s = _dot(q2, kb[:, 0:nk]) + bias
        m = jnp.max(s, axis=-1, keepdims=True)
        p = jnp.exp(s - m)
        l = jnp.sum(p, axis=-1, keepdims=True)
        o = _dot_nt(p.astype(BF16), vb[:, 0:nk]) / l
        o_ref[0, rows, :] = jnp.where(lo, o[0:Q_TILE], o[Q_TILE:2 * Q_TILE])


def _attn_prompt(q, k_all, v_all, tiles, layer):
    nb, t, _ = q.shape
    n_diff = tiles.shape[1]
    hp = W_B // LANES
    kern = functools.partial(_attn_kernel, nq=t // Q_TILE)
    seq = pl.BlockSpec((1, t, LANES), lambda b, p: (b, 0, p))
    kv = pl.BlockSpec((None, None, LANES, t), lambda b, p: (layer, b, p, 0))
    return pl.pallas_call(
        kern,
        out_shape=jax.ShapeDtypeStruct((nb, t, W_B), F32),
        grid=(nb, hp),
        in_specs=[seq, kv, kv,
                  pl.BlockSpec((None, n_diff, 2 * Q_TILE, Q_TILE), lambda b, p: (p, 0, 0, 0))],
        out_specs=seq,
        scratch_shapes=[pltpu.VMEM((LANES, t), BF16), pltpu.VMEM((LANES, t), BF16)],
        compiler_params=_cp(("parallel", "parallel")),
        name="attn_prompt",
    )(q, k_all, v_all, tiles)


def _split_dot(x, m_bf16, parts):
    acc = None
    rem = x
    for i in range(parts):
        piece = rem.astype(BF16)
        term = _dot(piece, m_bf16)
        acc = term if acc is None else acc + term
        if i + 1 < parts:
            rem = rem - piece.astype(F32)
    return acc


def _split_dot_left(m_bf16, x, parts):
    acc = None
    rem = x
    for i in range(parts):
        piece = rem.astype(BF16)
        term = _dot(m_bf16, piece)
        acc = term if acc is None else acc + term
        if i + 1 < parts:
            rem = rem - piece.astype(F32)
    return acc


def _rwkv_pre(pc, prev, prm, seg_sum):
    (mu, w0, w2p, a0, a2p, g2p, k_k, k_a, r_k, ln_g, ln_b, mseg) = prm
    xs = pc + (prev - pc) * mu
    r = xs[:, 0:W_C]
    k = xs[:, W_C:2 * W_C]
    v = xs[:, 2 * W_C:3 * W_C]
    tail = xs[:, 3 * W_C:C_SHIFT]
    w_raw = -_softplus(-(w0 + _dot(jnp.tanh(tail).astype(BF16), w2p))) - 0.5
    a = _sigmoid(a0 + _dot(tail.astype(BF16), a2p))
    g = _dot(_sigmoid(tail).astype(BF16), g2p)
    kk = k * k_k
    nrm = jnp.sqrt(seg_sum(kk * kk))
    kk = kk / jnp.maximum(nrm, 1e-12)
    k2 = k * (1.0 + (a - 1.0) * k_a)
    logw = -jnp.exp(w_raw)
    return r, k2, v, kk, a, g, logw


def _rwkv_post(y, r, k2, v, g, prm, seg_sum):
    (mu, w0, w2p, a0, a2p, g2p, k_k, k_a, r_k, ln_g, ln_b, mseg) = prm
    mean = seg_sum(y) * (1.0 / HEAD_DIM)
    d = y - mean
    var = seg_sum(d * d) * (1.0 / HEAD_DIM)
    yn = d * lax.rsqrt(var + GN_EPS) * ln_g + ln_b
    bonus = seg_sum(r * k2 * r_k) * v
    return (yn + bonus) * g


def _rwkv_kernel(pc_ref, mu_ref, w0_ref, w2_ref, a0_ref, a2_ref, g2_ref, kk_ref, ka_ref,
                 rk_ref, lng_ref, lnb_ref, mseg_ref, ltri_ref,
                 o_ref, s_ref, s_scr, prev_scr, yt_scr, *, ch, nseq):
    c = pl.program_id(1)
    rows = nseq * ch

    @pl.when(c == 0)
    def _():
        s_scr[...] = jnp.zeros_like(s_scr)
        prev_scr[...] = jnp.zeros_like(prev_scr)
        yt_scr[...] = jnp.zeros_like(yt_scr)

    prm = (mu_ref[...], w0_ref[...], w2_ref[...], a0_ref[...], a2_ref[...], g2_ref[...],
           kk_ref[...], ka_ref[...], rk_ref[...], lng_ref[...], lnb_ref[...], None)
    mseg = mseg_ref[...]
    seg_sum = lambda t: _split_dot(t, mseg, 2)
    pc = pc_ref[...].reshape(rows, C_SHIFT)
    rowi = lax.broadcasted_iota(jnp.int32, (rows, C_SHIFT), 0)
    prev = pltpu.roll(pc, 1, axis=0)
    for b in range(nseq):
        prev = jnp.where(rowi == b * ch, prev_scr[b:b + 1, :], prev)
        prev_scr[b:b + 1, :] = pc[(b + 1) * ch - 1:(b + 1) * ch, :]
    r, k2, v, kk, a, g, logw = _rwkv_pre(pc, prev, prm, seg_sum)

    cs = _split_dot_left(ltri_ref[...], logw, 3)
    g_in = jnp.exp(cs)
    g_ex = jnp.exp(cs - logw)
    g_inv = jnp.exp(-cs)
    al = (-kk * g_ex).astype(BF16)
    rt = (r * g_in).astype(BF16)
    bh = kk * a * g_inv
    kh = k2 * g_inv
    vb = v.astype(BF16)

    si = lax.broadcasted_iota(jnp.int32, (2 * ch, 2 * ch), 0) % ch
    ti = lax.broadcasted_iota(jnp.int32, (2 * ch, 2 * ch), 1)
    keep = si + jnp.where(ti < ch, 0, ch - 1) < ti
    n_sq = int(math.log2(ch))
    units = [(b, h) for b in range(nseq) for h in range(H_C)]
    idx = range(len(units))
    cut = lambda t, u: t[units[u][0] * ch:(units[u][0] + 1) * ch,
                         units[u][1] * HEAD_DIM:(units[u][1] + 1) * HEAD_DIM]
    g_end = [cut(g_in, u)[ch - 1:ch, :] for u in idx]
    al_u = [cut(al, u) for u in idx]
    rt_u = [cut(rt, u) for u in idx]
    v_u = [cut(vb, u) for u in idx]
    bh_u = [cut(bh, u) for u in idx]
    kh_u = [cut(kh, u) for u in idx]
    bc = [bh_u[u] * g_end[u] for u in idx]
    kc = [kh_u[u] * g_end[u] for u in idx]
    s0 = [s_scr[b, h] for (b, h) in units]
    s0b = [t.astype(BF16) for t in s0]
    bk = [jnp.concatenate([bh_u[u], kh_u[u]], axis=0).astype(BF16) for u in idx]
    ar = [jnp.concatenate([al_u[u], rt_u[u]], axis=0) for u in idx]
    gm = [jnp.where(keep, _dot_nt(bk[u], ar[u]), 0.0) for u in idx]
    ps = [_dot_nt(s0b[u], ar[u]) for u in idx]
    a_t = [gm[u][0:ch, 0:ch] for u in idx]
    vv = [_dot_tn(v_u[u], jnp.concatenate([gm[u][ch:2 * ch, :], kc[u]], axis=1).astype(BF16))
          for u in idx]
    x_t = [ps[u][:, 0:ch] + vv[u][:, 0:ch] for u in idx]
    for it in range(n_sq):
        a_b = [a_t[u].astype(BF16) for u in idx]
        if it + 1 < n_sq:
            both = [_dot(jnp.concatenate([x_t[u], a_t[u]], axis=0).astype(BF16), a_b[u]) for u in idx]
            x_t = [x_t[u] + both[u][0:HEAD_DIM] for u in idx]
            a_t = [both[u][HEAD_DIM:HEAD_DIM + ch] for u in idx]
        else:
            step = [_dot(x_t[u].astype(BF16), a_b[u]) for u in idx]
            x_t = [x_t[u] + step[u] for u in idx]
    x_b = [t.astype(BF16) for t in x_t]
    xx = [_dot(x_b[u], jnp.concatenate([gm[u][0:ch, :], bc[u]], axis=1).astype(BF16))
          for u in idx]
    for u, (b, h) in enumerate(units):
        s_new = s0[u] * g_end[u] + xx[u][:, 2 * ch:] + vv[u][:, 2 * ch:]
        y_sum = ps[u] + xx[u][:, 0:2 * ch] + vv[u][:, 0:2 * ch]
        yt_scr[h * HEAD_DIM:(h + 1) * HEAD_DIM, b * ch:(b + 1) * ch] = y_sum[:, ch:2 * ch]
        s_scr[b, h] = s_new
        s_ref[b, h] = s_new

    y = jnp.transpose(yt_scr[...])[0:rows, :]
    o_ref[...] = _rwkv_post(y, r, k2, v, g, prm, seg_sum).reshape(nseq, ch, W_C)


def _rwkv_params(p):
    pad = lambda w, lo: jnp.zeros((R_W + R_A + R_G, W_C), BF16).at[lo:lo + w.shape[0]].set(w.astype(BF16))
    seg = np.arange(W_C) // HEAD_DIM
    mseg = jnp.asarray((seg[:, None] == seg[None, :]).astype(np.float32))
    row = lambda t: t.reshape(1, -1)
    return (row(p["rwkv_mu"]), row(p["rwkv_w0"]), pad(p["rwkv_w2"], 0), row(p["rwkv_a0"]),
            pad(p["rwkv_a2"], R_W), pad(p["rwkv_g2"], R_W + R_A), row(p["rwkv_k_k"]),
            row(p["rwkv_k_a"]), row(p["rwkv_r_k"]), row(p["rwkv_ln_g"]), row(p["rwkv_ln_b"]), mseg)


def _rwkv_prompt(pc, prm):
    nb, t, _ = pc.shape
    ch = min(RWKV_CHUNK, t)
    nseq = max(1, min(RWKV_SEQS * LANES // ch, nb))
    assert nb % nseq == 0
    ltri = np.kron(np.eye(nseq, dtype=np.float32), np.tril(np.ones((ch, ch), np.float32)))
    ltri = jnp.asarray(ltri).astype(BF16)
    kern = functools.partial(_rwkv_kernel, ch=ch, nseq=nseq)
    const = lambda a: pl.BlockSpec(a.shape, lambda b, c: (0,) * a.ndim)
    args = list(prm[:-1]) + [prm[-1].astype(BF16), ltri]
    return pl.pallas_call(
        kern,
        out_shape=[jax.ShapeDtypeStruct((nb, t, W_C), F32),
                   jax.ShapeDtypeStruct((nb, H_C, HEAD_DIM, HEAD_DIM), F32)],
        grid=(nb // nseq, t // ch),
        in_specs=[pl.BlockSpec((nseq, ch, C_SHIFT), lambda b, c: (b, c, 0))] + [const(a) for a in args],
        out_specs=[pl.BlockSpec((nseq, ch, W_C), lambda b, c: (b, c, 0)),
                   pl.BlockSpec((nseq, H_C, HEAD_DIM, HEAD_DIM), lambda b, c: (b, 0, 0, 0))],
        scratch_shapes=[pltpu.VMEM((nseq, H_C, HEAD_DIM, HEAD_DIM), F32),
                        pltpu.VMEM((nseq, C_SHIFT), F32),
                        pltpu.VMEM((W_C, -(-nseq * ch // LANES) * LANES), F32)],
        compiler_params=_cp(("parallel", "arbitrary")),
        name="rwkv_prompt",
    )(pc, *args)


def _route(x1, rw, rb):
    logits = _dot(x1.astype(BF16), rw) + rb
    lane = lax.broadcasted_iota(jnp.int32, logits.shape, 1).astype(F32)
    vals, idxs = [], []
    cur = logits
    for _ in range(TOP_K):
        m = jnp.max(cur, axis=-1, keepdims=True)
        idx = jnp.min(jnp.where(cur == m, lane, float(N_EXPERTS)), axis=-1, keepdims=True)
        vals.append(m)
        idxs.append(idx)
        cur = jnp.where(lane == idx, -jnp.inf, cur)
    ex = [jnp.exp(v - vals[0]) for v in vals]
    tot = ex[0] + ex[1] + ex[2] + ex[3]
    gates = jnp.concatenate([e / tot for e in ex], axis=-1)
    return jnp.concatenate(idxs, axis=-1).astype(jnp.int32), gates


def _outproj_kernel(x_ref, oa_ref, ob_ref, oc_ref, w_ref, g_ref, b_ref, rw_ref, rb_ref, *refs):
    x1_ref, idx_ref, gate_ref = refs[-3:]
    mix = (_dot(oa_ref[...].astype(BF16), w_ref[0:W_A, :])
           + _dot(ob_ref[...].astype(BF16), w_ref[W_A:W_A + W_B, :])
           + _dot(oc_ref[...].astype(BF16), w_ref[W_A + W_B:, :]))
    x1 = _layernorm_rows(ALPHA * x_ref[...] + mix, g_ref[...], b_ref[...])
    x1_ref[...] = x1
    idx, gates = _route(x1, rw_ref[...], rb_ref[...])
    idx_ref[...] = idx
    gate_ref[...] = gates


def _outproj(x, oa, ob, oc, w, g, b, rw, rb, bufs, row0):
    n = oa.shape[0]
    tm = min(ROW_TILE, n)
    assert row0 % tm == 0
    rows = lambda w_: pl.BlockSpec((tm, w_), lambda i: (i, 0))
    shifted = lambda w_: pl.BlockSpec((tm, w_), lambda i: (i + row0 // tm, 0))
    const = lambda a: pl.BlockSpec(a.shape, lambda i: (0,) * a.ndim)
    consts = [w.astype(BF16), g.reshape(1, -1), b.reshape(1, -1), rw.astype(BF16), rb.reshape(1, -1)]
    n_in = 4 + len(consts)
    return pl.pallas_call(
        _outproj_kernel,
        out_shape=[jax.ShapeDtypeStruct(a.shape, a.dtype) for a in bufs],
        grid=(n // tm,),
        in_specs=[rows(D_MODEL), rows(W_A), rows(W_B), rows(W_C)] + [const(a) for a in consts]
        + [pl.BlockSpec(memory_space=pl.ANY)] * len(bufs),
        out_specs=[shifted(D_MODEL), shifted(TOP_K), shifted(TOP_K)],
        input_output_aliases={n_in + j: j for j in range(len(bufs))},
        compiler_params=_cp(("parallel",)),
        name="outproj",
    )(x, oa, ob, oc, *consts, *bufs)


def _dest_kernel(idx_ref, ltri_ref, utri_ref, dest_ref, cnt_ref, cnt_scr, run_scr, start_scr):
    ph = pl.program_id(0)
    i = pl.program_id(1)
    idx = idx_ref[...]
    lane = lax.broadcasted_iota(jnp.int32, (idx.shape[0], N_EXPERTS), 1)
    hot = [idx[:, k:k + 1] == lane for k in range(TOP_K)]
    multi = jnp.zeros(lane.shape, F32)
    for k in range(TOP_K):
        multi = multi + jnp.where(hot[k], 1.0, 0.0)
    tile_cnt = jnp.sum(multi, axis=0, keepdims=True)

    @pl.when(jnp.logical_and(ph == 0, i == 0))
    def _():
        cnt_scr[...] = jnp.zeros_like(cnt_scr)

    @pl.when(ph == 0)
    def _():
        cnt_scr[...] += tile_cnt

    @pl.when(jnp.logical_and(ph == 1, i == 0))
    def _():
        cnt = cnt_scr[...]
        padded = jnp.floor((cnt + (MOE_ROWS - 1)) * (1.0 / MOE_ROWS)) * MOE_ROWS
        start_scr[...] = _dot(jnp.broadcast_to(padded, (8, N_EXPERTS)), utri_ref[...], HI)[0:1, :]
        run_scr[...] = jnp.zeros_like(run_scr)
        cnt_ref[...] = cnt

    @pl.when(ph == 1)
    def _():
        before = _dot(ltri_ref[...], multi.astype(BF16))
        base = start_scr[...] + run_scr[...] + before
        cols = [jnp.sum(jnp.where(hot[k], base, 0.0), axis=1, keepdims=True) for k in range(TOP_K)]
        dest_ref[...] = jnp.concatenate(cols, axis=1).astype(jnp.int32)
        run_scr[...] += tile_cnt


def _route_dest(top_idx):
    n = top_idx.shape[0]
    tile = max(c for c in range(COMBINE_TOK, ROW_TILE + 1, COMBINE_TOK) if n % c == 0)
    nt = n // tile
    ltri = jnp.asarray(np.tril(np.ones((tile, tile), np.float32), -1)).astype(BF16)
    utri = jnp.asarray(np.triu(np.ones((N_EXPERTS, N_EXPERTS), np.float32), 1))
    return pl.pallas_call(
        _dest_kernel,
        out_shape=[jax.ShapeDtypeStruct((n, TOP_K), jnp.int32),
                   jax.ShapeDtypeStruct((1, N_EXPERTS), F32)],
        grid=(2, nt),
        in_specs=[pl.BlockSpec((tile, TOP_K), lambda ph, i: (i, 0)),
                  pl.BlockSpec((tile, tile), lambda ph, i: (0, 0)),
                  pl.BlockSpec((N_EXPERTS, N_EXPERTS), lambda ph, i: (0, 0))],
        out_specs=[pl.BlockSpec((tile, TOP_K), lambda ph, i: (i * ph, 0)),
                   pl.BlockSpec((1, N_EXPERTS), lambda ph, i: (0, 0))],
        scratch_shapes=[pltpu.VMEM((1, N_EXPERTS), F32)] * 3,
        compiler_params=_cp(("arbitrary", "arbitrary")),
        name="moe_dest",
    )(top_idx, ltri, utri)


def _block_experts(counts, n_blk):
    cnt = counts.reshape(N_EXPERTS).astype(jnp.int32)
    pends = jnp.cumsum((cnt + MOE_ROWS - 1) // MOE_ROWS * MOE_ROWS)
    starts = jnp.arange(n_blk, dtype=jnp.int32)[:, None] * MOE_ROWS
    blk_expert = jnp.minimum(jnp.sum((pends[None, :] <= starts).astype(jnp.int32), axis=1), N_EXPERTS - 1)
    last_blk = jnp.maximum(pends // MOE_ROWS - 1, 0).astype(jnp.int32)
    return blk_expert, (pends[-1] // MOE_ROWS).astype(jnp.int32).reshape(1), last_blk


def _dispatch_kernel(last_ref, na_ref, dest_ref, x1_ref, rows_hbm, zbuf, xbuf, sem, zsem, *, tm, n_blk):
    @pl.when(pl.program_id(0) == 0)
    def _():
        zbuf[...] = jnp.zeros_like(zbuf)
        n_tail = n_blk - na_ref[0]

        def clear(blk):
            return pltpu.make_async_copy(zbuf, rows_hbm.at[pl.ds(blk * MOE_ROWS, MOE_ROWS), :], zsem)

        def start(j, carry):
            clear(jnp.where(j < N_EXPERTS, last_ref[jnp.minimum(j, N_EXPERTS - 1)],
                            na_ref[0] + j - N_EXPERTS)).start()
            return carry

        def wait(j, carry):
            clear(0).wait()
            return carry

        lax.fori_loop(0, N_EXPERTS + n_tail, start, 0)
        lax.fori_loop(0, N_EXPERTS + n_tail, wait, 0)

    i = pl.program_id(0)

    def drain(s):
        for k in range(TOP_K):
            pltpu.make_async_copy(xbuf.at[s], rows_hbm.at[pl.ds(0, tm), :], sem.at[s]).wait()

    for s in range(2):
        @pl.when(i % 2 == s)
        def _(s=s):
            @pl.when(i >= 2)
            def _():
                drain(s)

            xbuf[s] = x1_ref[...]
            for r in range(tm):
                for k in range(TOP_K):
                    pltpu.make_async_copy(xbuf.at[s, pl.ds(r, 1), :],
                                          rows_hbm.at[pl.ds(dest_ref[0, 0, r * TOP_K + k], 1), :],
                                          sem.at[s]).start(priority=k % 2)

            @pl.when(i == pl.num_programs(0) - 1)
            def _():
                drain(s)

                @pl.when(i >= 1)
                def _():
                    drain(1 - s)


def _dispatch(x1, dest, last_blk, n_active, n_blk):
    n = x1.shape[0]
    tm = COMBINE_TOK
    nt = n // tm
    kern = functools.partial(_dispatch_kernel, tm=tm, n_blk=n_blk)
    grid_spec = pltpu.PrefetchScalarGridSpec(
        num_scalar_prefetch=2,
        grid=(nt,),
        in_specs=[pl.BlockSpec((1, 1, TOP_K * tm), lambda i, lb, na: (i, 0, 0), memory_space=pltpu.SMEM),
                  pl.BlockSpec((tm, D_MODEL), lambda i, lb, na: (i, 0))],
        out_specs=pl.BlockSpec(memory_space=pl.ANY),
        scratch_shapes=[pltpu.VMEM((MOE_ROWS, D_MODEL), F32), pltpu.VMEM((2, tm, D_MODEL), F32),
                        pltpu.SemaphoreType.DMA((2,)), pltpu.SemaphoreType.DMA],
    )
    return pl.pallas_call(
        kern,
        out_shape=jax.ShapeDtypeStruct((n_blk * MOE_ROWS, D_MODEL), F32),
        grid_spec=grid_spec,
        compiler_params=_cp(("arbitrary",)),
        name="moe_dispatch",
    )(last_blk, n_active, dest.reshape(nt, 1, TOP_K * tm), x1)


def _swiglu(h):
    h_glu = jnp.minimum(h[:, :D_FF], SWIGLU_LIMIT)
    h_lin = jnp.clip(h[:, D_FF:], -SWIGLU_LIMIT, SWIGLU_LIMIT)
    return h_glu * _sigmoid(SWIGLU_ALPHA * h_glu) * (h_lin + 1.0)


def _moe_kernel(be_ref, na_ref, par_ref, nxt_ref, x_ref, wup_hbm, bup_ref, wdn_hbm, bdn_ref, y_ref,
                wup_f, wdn_f, wup_b, wdn_b, sem, *, layer):
    i = pl.program_id(0)
    active = i < na_ref[0]

    def fetch(expert, s):
        return (pltpu.make_async_copy(wup_hbm.at[layer, expert], wup_f.at[s], sem.at[s]),
                pltpu.make_async_copy(wdn_hbm.at[layer, expert], wdn_f.at[s], sem.at[s]))

    @pl.when(active)
    def _():
        changed = jnp.logical_or(i == 0, be_ref[i] != be_ref[jnp.maximum(i - 1, 0)])

        @pl.when(i == 0)
        def _():
            for c in fetch(be_ref[0], 0):
                c.start()

        for s in range(2):
            @pl.when(jnp.logical_and(changed, par_ref[i] == s))
            def _(s=s):
                for c in fetch(0, s):
                    c.wait()
                wup_b[...] = wup_f[s].astype(BF16)
                wdn_b[...] = wdn_f[s].astype(BF16)

                @pl.when(nxt_ref[i] >= 0)
                def _():
                    for c in fetch(nxt_ref[i], 1 - s):
                        c.start()

        h = _dot(x_ref[...].astype(BF16), wup_b[...]) + bup_ref[...]
        y_ref[...] = _dot(_swiglu(h).astype(BF16), wdn_b[...]) + bdn_ref[...]

    @pl.when(jnp.logical_not(active))
    def _():
        y_ref[...] = jnp.zeros_like(y_ref)


def _run_schedule(blk_expert, n_active):
    n_blk = blk_expert.shape[0]
    pos = jnp.arange(n_blk, dtype=jnp.int32)
    first = jnp.concatenate([jnp.ones((1,), bool), blk_expert[1:] != blk_expert[:-1]]) & (pos < n_active[0])
    parity = ((jnp.cumsum(first.astype(jnp.int32)) - 1) % 2).astype(jnp.int32)
    later = jnp.where(first, pos, n_blk)
    nxt_pos = lax.cummin(jnp.concatenate([later[1:], jnp.full((1,), n_blk, jnp.int32)]), axis=0, reverse=True)
    nxt = jnp.where(nxt_pos < n_blk, blk_expert[jnp.minimum(nxt_pos, n_blk - 1)], -1).astype(jnp.int32)
    return parity, nxt


def _moe_rows(rows, blk_expert, n_active, w_up, b_up, w_dn, b_dn, layer):
    n_blk = blk_expert.shape[0]
    parity, nxt = _run_schedule(blk_expert, n_active)
    grid_spec = pltpu.PrefetchScalarGridSpec(
        num_scalar_prefetch=4,
        grid=(n_blk,),
        in_specs=[
            pl.BlockSpec((MOE_ROWS, D_MODEL), lambda i, be, na, *_: (jnp.minimum(i, jnp.maximum(na[0] - 1, 0)), 0)),
            pl.BlockSpec(memory_space=pl.ANY),
            pl.BlockSpec((None, None, 1, 2 * D_FF), lambda i, be, na, *_: (layer, be[i], 0, 0)),
            pl.BlockSpec(memory_space=pl.ANY),
            pl.BlockSpec((None, None, 1, D_MODEL), lambda i, be, na, *_: (layer, be[i], 0, 0)),
        ],
        out_specs=pl.BlockSpec((MOE_ROWS, D_MODEL), lambda i, be, na, *_: (i, 0)),
        scratch_shapes=[pltpu.VMEM((2, D_MODEL, 2 * D_FF), F32),
                        pltpu.VMEM((2, D_FF, D_MODEL), F32),
                        pltpu.VMEM((D_MODEL, 2 * D_FF), BF16),
                        pltpu.VMEM((D_FF, D_MODEL), BF16),
                        pltpu.SemaphoreType.DMA((2,))],
    )
    return pl.pallas_call(
        functools.partial(_moe_kernel, layer=layer),
        out_shape=jax.ShapeDtypeStruct((n_blk * MOE_ROWS, D_MODEL), F32),
        grid_spec=grid_spec,
        compiler_params=_cp(("arbitrary",)),
        name="moe_rows",
    )(blk_expert, n_active, parity, nxt, rows,
      w_up, b_up.reshape(DEPTH, N_EXPERTS, 1, 2 * D_FF), w_dn, b_dn.reshape(DEPTH, N_EXPERTS, 1, D_MODEL))


def _combine_kernel(dest_ref, destn_ref, gate_ref, x1_ref, y_hbm, g_ref, b_ref, o_ref, tail_ref, ybuf, sem,
                    *, tm, head_tiles):
    i = pl.program_id(0)
    slot = i % 2
    n_rows = TOP_K * tm

    def gather_tile(dests, s):
        for r in range(n_rows):
            pltpu.make_async_copy(y_hbm.at[pl.ds(dests[0, 0, r], 1), :],
                                  ybuf.at[s, pl.ds(r, 1), :], sem.at[s]).start(priority=r % 2)

    @pl.when(i == 0)
    def _():
        gather_tile(dest_ref, 0)

    for s in range(2):
        @pl.when(jnp.logical_and(i + 1 < pl.num_programs(0), slot == 1 - s))
        def _(s=s):
            gather_tile(destn_ref, s)

    for s in range(2):
        @pl.when(slot == s)
        def _(s=s):
            pltpu.make_async_copy(y_hbm.at[pl.ds(0, n_rows), :], ybuf.at[s], sem.at[s]).wait()
            gates = gate_ref[...]
            moe = jnp.zeros((tm, D_MODEL), F32)
            for k in range(TOP_K):
                moe = moe + gates[:, k:k + 1] * ybuf[s, k * tm:(k + 1) * tm, :]
            res = _layernorm_rows(ALPHA * x1_ref[...] + moe, g_ref[...], b_ref[...])

            @pl.when(i < head_tiles)
            def _():
                o_ref[...] = res

            @pl.when(i >= head_tiles)
            def _():
                tail_ref[...] = res


def _combine(dest, gates, x1, y_rows, g, b, n_head):
    n = x1.shape[0]
    tm = COMBINE_TOK
    nt = n // tm
    head_tiles = n_head // tm
    dest_t = dest.reshape(nt, tm, TOP_K).transpose(0, 2, 1).reshape(nt, 1, TOP_K * tm)
    kern = functools.partial(_combine_kernel, tm=tm, head_tiles=head_tiles)
    return pl.pallas_call(
        kern,
        out_shape=[jax.ShapeDtypeStruct((n_head, D_MODEL), F32),
                   jax.ShapeDtypeStruct((n - n_head, D_MODEL), F32)],
        grid=(nt,),
        in_specs=[pl.BlockSpec((1, 1, TOP_K * tm), lambda i: (i, 0, 0), memory_space=pltpu.SMEM),
                  pl.BlockSpec((1, 1, TOP_K * tm), lambda i: (jnp.minimum(i + 1, nt - 1), 0, 0),
                               memory_space=pltpu.SMEM),
                  pl.BlockSpec((tm, TOP_K), lambda i: (i, 0)),
                  pl.BlockSpec((tm, D_MODEL), lambda i: (i, 0)),
                  pl.BlockSpec(memory_space=pl.ANY),
                  pl.BlockSpec((1, D_MODEL), lambda i: (0, 0)),
                  pl.BlockSpec((1, D_MODEL), lambda i: (0, 0))],
        out_specs=[pl.BlockSpec((tm, D_MODEL), lambda i: (jnp.minimum(i, head_tiles - 1), 0)),
                   pl.BlockSpec((tm, D_MODEL), lambda i: (jnp.maximum(i - head_tiles, 0), 0))],
        scratch_shapes=[pltpu.VMEM((2, TOP_K * tm, D_MODEL), F32), pltpu.SemaphoreType.DMA((2,))],
        compiler_params=_cp(("arbitrary",)),
        name="moe_combine",
    )(dest_t, dest_t, gates, x1, y_rows, g.reshape(1, -1), b.reshape(1, -1))


def _moe(x1, top_idx, gates, p, layer, n_head):
    n = x1.shape[0]
    n_blk = -(-n * TOP_K // MOE_ROWS) + N_EXPERTS
    dest, counts = _route_dest(top_idx)
    blk_expert, n_active, last_blk = _block_experts(counts, n_blk)
    rows = _dispatch(x1, dest, last_blk, n_active, n_blk)
    y_rows = _moe_rows(rows, blk_expert, n_active, p["moe_w_up"], p["moe_b_up"],
                       p["moe_w_down"], p["moe_b_down"], layer)
    return _combine(dest, gates, x1, y_rows, p["ln2_g"][layer], p["ln2_b"][layer], n_head)


def _sample_proj_kernel(x_ref, w_ref, h0_ref, lam_ref, bblk_ref, cblk_ref, d_ref, gw_ref, gb_ref,
                        q_ref, k_ref, v_ref, pc_ref, oa_ref, h_ref):
    xb = x_ref[...].astype(BF16)
    mm = lambda lo, hi: _dot(xb, w_ref[:, lo:hi])
    u = mm(_PROJ_SPLITS[0], _PROJ_SPLITS[1])
    q_ref[...] = mm(_PROJ_SPLITS[1], _PROJ_SPLITS[2])
    k_ref[...] = mm(_PROJ_SPLITS[2], _PROJ_SPLITS[3])
    v_ref[...] = mm(_PROJ_SPLITS[3], _PROJ_SPLITS[4])
    pc_ref[...] = mm(_PROJ_SPLITS[4], _PROJ_SPLITS[5])
    bu = _dot(u.astype(BF16), bblk_ref[...])
    bu_re = bu[:, 0:N_ST]
    bu_im = bu_re + bu[:, N_ST:2 * N_ST]
    lr, li = lam_ref[0:1, :], lam_ref[1:2, :]
    hr0, hi0 = h0_ref[0], h0_ref[1]
    hr = bu_re + (lr * hr0 - li * hi0)
    hi = bu_im + (lr * hi0 + li * hr0)
    h_ref[0] = hr
    h_ref[1] = hi
    y = (_dot((hr + hi).astype(BF16), cblk_ref[0:N_ST, :])
         - _dot(hi.astype(BF16), cblk_ref[N_ST:2 * N_ST, :])) + d_ref[...] * u
    z = _gelu(y)
    oa_ref[...] = z * _sigmoid(_dot(z.astype(BF16), gw_ref[...]) + gb_ref[...])


def _sample_proj(x, w_in, h0, lam2, bblk, cblk, d, gw, gb):
    n = x.shape[0]
    widths = (W_B, W_B, W_B, C_SHIFT, W_A)
    return pl.pallas_call(
        _sample_proj_kernel,
        out_shape=[jax.ShapeDtypeStruct((n, w), F32) for w in widths]
        + [jax.ShapeDtypeStruct((2, n, N_ST), F32)],
        compiler_params=_cp(None),
        name="sample_proj",
    )(x, w_in, h0, lam2, bblk, cblk, d.reshape(1, -1), gw.astype(BF16), gb.reshape(1, -1))


def _sample_mix_kernel(q_ref, kn_ref, vn_ref, pc_ref, sh_ref, kc_ref, vc_ref, s0_ref, lb_ref, lb0_ref,
                       mu_ref, w0_ref, w2_ref, a0_ref, a2_ref, g2_ref, kk_ref, ka_ref,
                       rk_ref, lng_ref, lnb_ref, mseg_ref,
                       ob_ref, oc_ref, s_ref):
    rows = 8
    lane = lax.broadcasted_iota(jnp.int32, (rows, W_B), 1)
    sub = lax.broadcasted_iota(jnp.int32, (rows, W_B), 0)
    hmask = (lane // HEAD_DIM) == sub
    rnd = lambda t: t.astype(BF16).astype(F32)
    q = q_ref[0] * (HEAD_DIM ** -0.5)
    qrows = jnp.where(hmask, jnp.broadcast_to(q, (rows, W_B)), 0.0).astype(BF16)
    kcb = kc_ref[0].astype(BF16)
    vcb = vc_ref[0].astype(BF16)
    knb = rnd(kn_ref[0])
    vnb = rnd(vn_ref[0])
    s_all = _dot(qrows, kcb)
    s_new = jnp.sum(qrows.astype(F32) * knb, axis=-1, keepdims=True)
    probs, p_new, lses = [], [], []
    for br in range(len(BRANCHES)):
        s = s_all + lb_ref[br]
        s0 = s_new + lb0_ref[br]
        m = jnp.maximum(jnp.max(s, axis=-1, keepdims=True), s0)
        lse = m + jnp.log(jnp.sum(jnp.exp(s - m), axis=-1, keepdims=True) + jnp.exp(s0 - m))
        probs.append(jnp.exp(s - lse))
        p_new.append(rnd(jnp.exp(s0 - lse)))
        lses.append(lse)
    pv = _dot_nt(jnp.concatenate(probs, axis=0).astype(BF16), vcb)
    outs = [rnd(pv[br * rows:(br + 1) * rows] + p_new[br] * vnb) for br in range(len(BRANCHES))]
    top = jnp.maximum(jnp.maximum(lses[0], lses[1]), lses[2])
    ex = [jnp.exp(t - top) for t in lses]
    tot = ex[0] + ex[1] + ex[2]
    o = rnd(ex[0] / tot) * outs[0] + rnd(ex[1] / tot) * outs[1] + rnd(ex[2] / tot) * outs[2]
    ob_ref[0] = jnp.sum(jnp.where(hmask, o, 0.0), axis=0, keepdims=True)

    prm = (mu_ref[...], w0_ref[...], w2_ref[...], a0_ref[...], a2_ref[...], g2_ref[...],
           kk_ref[...], ka_ref[...], rk_ref[...], lng_ref[...], lnb_ref[...], mseg_ref[...])
    pc = jnp.broadcast_to(pc_ref[0], (rows, C_SHIFT))
    prev = jnp.broadcast_to(sh_ref[0], (rows, C_SHIFT))
    head_of_lane = lax.broadcasted_iota(jnp.int32, (rows, W_C), 1) // HEAD_DIM

    def seg_sum(t):
        out = jnp.zeros_like(t)
        for h in range(H_C):
            m = head_of_lane == h
            out = jnp.where(m, jnp.sum(jnp.where(m, t, 0.0), axis=1, keepdims=True), out)
        return out
    r, k2, v, kk, a, g, logw = _rwkv_pre(pc, prev, prm, seg_sum)
    w = jnp.exp(logw)
    eye = (lax.broadcasted_iota(jnp.int32, (HEAD_DIM, HEAD_DIM), 0)
           == lax.broadcasted_iota(jnp.int32, (HEAD_DIM, HEAD_DIM), 1))
    heads = range(H_C)
    row = lambda t, h: t[0:1, h * HEAD_DIM:(h + 1) * HEAD_DIM]
    col = lambda t, h: jnp.sum(jnp.where(eye, jnp.broadcast_to(row(t, h), (HEAD_DIM, HEAD_DIM)), 0.0),
                               axis=1, keepdims=True)
    s0 = [s0_ref[0, h] for h in heads]
    sa = [jnp.sum(rnd(s0[h]) * rnd(-row(kk, h)), axis=1, keepdims=True) for h in heads]
    v_col = [col(v, h) for h in heads]
    s1 = [s0[h] * row(w, h) + sa[h] * (row(kk, h) * row(a, h)) + v_col[h] * row(k2, h) for h in heads]
    y_col = [jnp.sum(rnd(s1[h]) * rnd(row(r, h)), axis=1, keepdims=True) for h in heads]
    ys = [jnp.sum(jnp.where(eye, jnp.broadcast_to(y_col[h], (HEAD_DIM, HEAD_DIM)), 0.0), axis=0, keepdims=True)
          for h in heads]
    for h in heads:
        s_ref[0, h] = s1[h]
    y = jnp.broadcast_to(jnp.concatenate(ys, axis=1), (rows, W_C))
    oc_ref[0] = _rwkv_post(y, r, k2, v, g, prm, seg_sum)[0:1, :]


def _sample_mix(q, kn, vn, pc, shift0, k_cache, v_cache, s0, lb_rows, lb0, prm, layer):
    n = q.shape[0]
    l_buf = k_cache.shape[3]
    per_b = lambda w: pl.BlockSpec((1, 1, w), lambda b: (b, 0, 0))
    const = lambda a: pl.BlockSpec(a.shape, lambda b: (0,) * a.ndim)
    r3 = lambda t: t.reshape(n, 1, t.shape[-1])
    cache = pl.BlockSpec((None, 1, W_B, l_buf), lambda b: (layer, b, 0, 0))
    return pl.pallas_call(
        _sample_mix_kernel,
        out_shape=[jax.ShapeDtypeStruct((n, 1, W_B), F32),
                   jax.ShapeDtypeStruct((n, 1, W_C), F32),
                   jax.ShapeDtypeStruct((n, H_C, HEAD_DIM, HEAD_DIM), F32)],
        grid=(n,),
        in_specs=[per_b(W_B), per_b(W_B), per_b(W_B), per_b(C_SHIFT), per_b(C_SHIFT),
                  cache, cache,
                  pl.BlockSpec((1, H_C, HEAD_DIM, HEAD_DIM), lambda b: (b, 0, 0, 0)),
                  const(lb_rows), const(lb0)] + [const(a) for a in prm],
        out_specs=[per_b(W_B), per_b(W_C),
                   pl.BlockSpec((1, H_C, HEAD_DIM, HEAD_DIM), lambda b: (b, 0, 0, 0))],
        compiler_params=_cp(("parallel",)),
        name="sample_mix",
    )(r3(q), r3(kn), r3(vn), r3(pc), r3(shift0), k_cache, v_cache, s0, lb_rows, lb0, *prm)


def _prompt_mixers(x, nb, t, p, l, s5p, rwp, tiles, kv_prev):
    u, q, k_all, v_all, pc = _proj(x, p["w_in"][l].astype(BF16), nb, t, l, kv_prev)
    shp = lambda a: a.reshape(nb, t, a.shape[-1])
    o_a, h_t = _s5_prompt(shp(u), *s5p, p["ssm_d"][l], p["ssm_glu_w"][l], p["ssm_glu_b"][l])
    o_b = _attn_prompt(shp(q), k_all, v_all, tiles, l)
    o_c, s_t = _rwkv_prompt(shp(pc), rwp)
    flat = lambda a: a.reshape(nb * t, a.shape[-1])
    ssm = jnp.transpose(h_t.reshape(2, nb, G_A, N_A), (1, 2, 3, 0))
    return (flat(o_a), flat(o_b), flat(o_c)), (ssm, s_t, shp(pc)[:, t - 1]), (k_all, v_all)


def _sample_bias(rel_bias, l_buf):
    offs = _branch_offsets()
    bias = rel_bias[jnp.asarray(_t5_buckets(offs))]
    rows = []
    for br in range(len(BRANCHES)):
        m = np.arange(1, offs.shape[1])
        pos = l_buf - offs[br][m]
        ok = pos >= 0
        r = jnp.full((H_B, l_buf), NEG_INF, F32).at[:, jnp.asarray(pos[ok])].set(bias[br][jnp.asarray(m[ok])].T)
        rows.append(jnp.concatenate([r, jnp.zeros((8 - H_B, l_buf), F32)], axis=0))
    new = jnp.zeros((len(BRANCHES), 8, 1), F32).at[:, :H_B, 0].set(bias[:, 0])
    return jnp.stack(rows), new


def _sample_mixers(x, p, l, s5p, rwp, sbias, k_cache, v_cache, st_ssm, st_rwkv, st_shift):
    n = x.shape[0]
    lam2, bblk, cblk = s5p
    h0 = jnp.transpose(st_ssm.reshape(n, N_ST, 2), (2, 0, 1))
    q, kn, vn, pc, o_a, h1 = _sample_proj(x, p["w_in"][l].astype(BF16), h0, lam2, bblk, cblk, p["ssm_d"][l],
                                          p["ssm_glu_w"][l], p["ssm_glu_b"][l])
    lb_rows, lb0 = sbias
    o_b, o_c, s1 = _sample_mix(q, kn, vn, pc, st_shift, k_cache, v_cache, st_rwkv, lb_rows, lb0, rwp, l)
    ssm = jnp.transpose(h1.reshape(2, n, G_A, N_A), (1, 2, 3, 0))
    return (o_a, o_b.reshape(n, W_B), o_c.reshape(n, W_C)), (kn, vn, ssm, s1, pc)


def _shift_kernel(c_ref, new_ref, o_ref):
    x = c_ref[...]
    length = x.shape[1]
    lane = lax.broadcasted_iota(jnp.int32, x.shape, 1)
    o_ref[...] = jnp.where(lane == length - 1, new_ref[...], pltpu.roll(x, length - 1, axis=1))


def _shift_cache(cache_t, new):
    rows, length = cache_t.shape
    blk = min(ROW_TILE, rows)
    return pl.pallas_call(
        _shift_kernel,
        out_shape=jax.ShapeDtypeStruct((rows, length), F32),
        grid=(rows // blk,),
        in_specs=[pl.BlockSpec((blk, length), lambda i: (i, 0)),
                  pl.BlockSpec((blk, 1), lambda i: (i, 0))],
        out_specs=pl.BlockSpec((blk, length), lambda i: (i, 0)),
        compiler_params=_cp(("parallel",)),
        name="shift_cache",
    )(cache_t, new)


def kernel(x_prompt, x_sample, cache_attn_k, cache_attn_v, state_ssm, state_rwkv, state_shift, w_in, w_out, ln1_g, ln1_b, ln2_g, ln2_b, ssm_a_re, ssm_a_im, ssm_log_dt, ssm_b_re, ssm_b_im, ssm_c_re, ssm_c_im, ssm_d, ssm_glu_w, ssm_glu_b, rel_bias, rwkv_mu, rwkv_w0, rwkv_w2, rwkv_a0, rwkv_a2, rwkv_g2, rwkv_k_k, rwkv_k_a, rwkv_r_k, rwkv_ln_g, rwkv_ln_b, moe_router_w, moe_router_b, moe_w_up, moe_b_up, moe_w_down, moe_b_down):
    p = dict(w_in=w_in, w_out=w_out, ln1_g=ln1_g, ln1_b=ln1_b, ln2_g=ln2_g, ln2_b=ln2_b,
             ssm_d=ssm_d, ssm_glu_w=ssm_glu_w, ssm_glu_b=ssm_glu_b,
             moe_router_w=moe_router_w, moe_router_b=moe_router_b, moe_w_up=moe_w_up,
             moe_b_up=moe_b_up, moe_w_down=moe_w_down, moe_b_down=moe_b_down)
    nb, t, _ = x_prompt.shape
    ns = x_sample.shape[0]
    l_buf = cache_attn_k.shape[2]
    table = _log_bias_table(rel_bias)
    tiles = _bias_tiles(table, min(t, WIN_MAX) // Q_TILE + 1)
    sbias = _sample_bias(rel_bias, l_buf)
    n_p = nb * t
    n_all = n_p + -(-ns // COMBINE_TOK) * COMBINE_TOK
    xp = x_prompt.reshape(n_p, D_MODEL)
    xs = x_sample.reshape(ns, D_MODEL)
    st_p, st_s = [], []
    time_minor = lambda c: jnp.transpose(c, (0, 1, 3, 4, 2)).reshape(DEPTH, c.shape[1], W_B, c.shape[2])
    time_major = lambda c, n: jnp.transpose(c.reshape(DEPTH, n, H_B, HEAD_DIM, c.shape[-1]), (0, 1, 4, 2, 3))
    kc_t, vc_t = time_minor(cache_attn_k), time_minor(cache_attn_v)
    kv = (jnp.zeros((DEPTH, nb, W_B, t), F32), jnp.zeros((DEPTH, nb, W_B, t), F32))
    for l in range(DEPTH):
        s5p, s5s = _s5_params(ssm_a_re[l], ssm_a_im[l], ssm_log_dt[l], ssm_b_re[l], ssm_b_im[l],
                              ssm_c_re[l], ssm_c_im[l])
        rwp = _rwkv_params(dict(rwkv_mu=rwkv_mu[l], rwkv_w0=rwkv_w0[l], rwkv_w2=rwkv_w2[l],
                                rwkv_a0=rwkv_a0[l], rwkv_a2=rwkv_a2[l], rwkv_g2=rwkv_g2[l],
                                rwkv_k_k=rwkv_k_k[l], rwkv_k_a=rwkv_k_a[l], rwkv_r_k=rwkv_r_k[l],
                                rwkv_ln_g=rwkv_ln_g[l], rwkv_ln_b=rwkv_ln_b[l]))
        mix_p, sp, kv = _prompt_mixers(xp, nb, t, p, l, s5p, rwp, tiles, kv)
        mix_s, ss = _sample_mixers(xs, p, l, s5s, rwp, sbias, kc_t, vc_t,
                                   state_ssm[l], state_rwkv[l], state_shift[l])
        st_p.append(sp)
        st_s.append(ss)
        bufs = (jnp.zeros((n_all, D_MODEL), F32),
                jnp.broadcast_to(jnp.arange(TOP_K, dtype=jnp.int32), (n_all, TOP_K)),
                jnp.zeros((n_all, TOP_K), F32))
        head = (p["w_out"][l], p["ln1_g"][l], p["ln1_b"][l], p["moe_router_w"][l], p["moe_router_b"][l])
        bufs = _outproj(xp, *mix_p, *head, bufs, 0)
        bufs = _outproj(xs, *mix_s, *head, bufs, n_p)
        xp, tail = _moe(*bufs, p, l, n_p)
        xs = tail[:ns]
    p_ssm, p_rwkv, p_shift = (jnp.stack(z) for z in zip(*st_p))
    kn, vn, s_ssm, s_rwkv, s_shift = (jnp.stack(z) for z in zip(*st_s))
    keep_p = min(WIN_MAX, t)
    p_k, p_v = (time_major(a[..., t - keep_p:], nb) for a in kv)
    keep = min(WIN_MAX, l_buf + 1)
    drop = l_buf + 1 - keep
    if drop == 1:
        advance = lambda c_t, new: _shift_cache(c_t.reshape(-1, l_buf), new.reshape(-1, 1)).reshape(c_t.shape)
    else:
        advance = lambda c_t, new: jnp.concatenate([c_t[..., drop:], new[..., None]], axis=-1)
    s_k = time_major(advance(kc_t, kn), ns)
    s_v = time_major(advance(vc_t, vn), ns)
    return (xp.reshape(nb, t, D_MODEL), xs.reshape(ns, 1, D_MODEL),
            p_k, p_v, p_ssm, p_rwkv, p_shift, s_k, s_v, s_ssm, s_rwkv, s_shift)
```
